```python
import jax, jax.numpy as jnp
from jax import lax
import numpy as np

D_MODEL = 2048
BATCH = 8
SEQ = 4096
DEPTH = 2

N_MIXERS = 2
N_GMLP_LAYERS = (DEPTH + 1) // 2
N_SWA_LAYERS = DEPTH // 2
D_FF = 5632
FFN_RESIDUAL_WEIGHT = 0.5
PLE_DIM = 256
RMS_EPS = 1e-6
LN_EPS = 1e-5
CHUNK = 128
GMLP_WIDTH = 2 * D_MODEL
GMLP_GROUPS = 16
GMLP_GROUP_DIM = GMLP_WIDTH // GMLP_GROUPS
N_Q_HEADS = 32
N_KV_HEADS = 4
HEAD_DIM = 64
Q_PER_KV = N_Q_HEADS // N_KV_HEADS
WINDOW = 128
ROPE_THETA = 500000.0
ROPE_DIM = HEAD_DIM // 4

kernel_name = "hybrid_gmlp_swa_sink_macaron"


def rms_norm(x, g):
    xf = x.astype(jnp.float32)
    y = xf * lax.rsqrt(jnp.mean(xf * xf, axis=-1, keepdims=True) + RMS_EPS)
    return (y * g.astype(jnp.float32)).astype(x.dtype)


def layer_norm(x, g, b):
    xf = x.astype(jnp.float32)
    mu = jnp.mean(xf, axis=-1, keepdims=True)
    var = jnp.mean(jnp.square(xf - mu), axis=-1, keepdims=True)
    y = (xf - mu) * lax.rsqrt(var + LN_EPS)
    return (y * g.astype(jnp.float32) + b.astype(jnp.float32)).astype(x.dtype)


def swiglu(h, w1, w3, w2):
    return (jax.nn.silu(h @ w1) * (h @ w3)) @ w2


def gmlp_chunk_mixer(h, w_in, ln_g, ln_b, w_s, b_s, w_out):
    B, S, _ = h.shape
    z = jax.nn.gelu(h @ w_in, approximate=False)
    u, v = jnp.split(z, 2, axis=-1)
    v = layer_norm(v, ln_g, ln_b)
    v = v.reshape(B, S // CHUNK, CHUNK, GMLP_GROUPS, GMLP_GROUP_DIM)
    causal = jnp.tril(jnp.ones((CHUNK, CHUNK), dtype=bool))
    w = jnp.where(causal[None], w_s, jnp.zeros((), w_s.dtype))
    s = jnp.einsum('gts,bcsgd->bctgd', w, v) + b_s.T[:, :, None]
    gated = u * s.reshape(B, S, GMLP_WIDTH)
    return gated @ w_out


def rope_tables(S):
    inv_freq = ROPE_THETA ** (-jnp.arange(0, ROPE_DIM, 2, dtype=jnp.float32) / ROPE_DIM)
    ang = jnp.arange(S, dtype=jnp.float32)[:, None] * inv_freq[None, :]
    return jnp.cos(ang), jnp.sin(ang)


def apply_partial_rope(x, cos, sin):
    half = ROPE_DIM // 2
    c = cos[:, None, :].astype(x.dtype)
    s = sin[:, None, :].astype(x.dtype)
    x1 = x[..., :half]
    x2 = x[..., half:ROPE_DIM]
    return jnp.concatenate([x1 * c - x2 * s, x2 * c + x1 * s, x[..., ROPE_DIM:]], axis=-1)


def swa_sink_attention(h, wq, bq, wk, bk, wv, bv, sinks, wo, bo):
    B, S, _ = h.shape
    NB = S // WINDOW
    q = (h @ wq + bq).reshape(B, S, N_Q_HEADS, HEAD_DIM)
    k = (h @ wk + bk).reshape(B, S, N_KV_HEADS, HEAD_DIM)
    v = (h @ wv + bv).reshape(B, S, N_KV_HEADS, HEAD_DIM)
    cos, sin = rope_tables(S)
    q = apply_partial_rope(q, cos, sin)
    k = apply_partial_rope(k, cos, sin)
    q = q.reshape(B, NB, WINDOW, N_KV_HEADS, Q_PER_KV, HEAD_DIM)
    k = k.reshape(B, NB, WINDOW, N_KV_HEADS, HEAD_DIM)
    v = v.reshape(B, NB, WINDOW, N_KV_HEADS, HEAD_DIM)
    pad = ((0, 0), (1, 0), (0, 0), (0, 0), (0, 0))
    kb = jnp.concatenate([jnp.pad(k, pad)[:, :-1], k], axis=2)
    vb = jnp.concatenate([jnp.pad(v, pad)[:, :-1], v], axis=2)
    scores = jnp.einsum('bnqkgd,bnskd->bnkgqs', q, kb).astype(jnp.float32) * (HEAD_DIM ** -0.5)
    qpos = jnp.arange(WINDOW)[:, None] + WINDOW
    kpos = jnp.arange(2 * WINDOW)[None, :]
    diff = qpos - kpos
    band = (diff >= 0) & (diff < WINDOW)
    has_prev = (jnp.arange(NB) > 0)[:, None, None]
    valid = jnp.where(has_prev, band[None], (band & (kpos >= WINDOW))[None])
    scores = jnp.where(valid[None, :, None, None], scores, -jnp.inf)
    sink = sinks.astype(jnp.float32).reshape(N_KV_HEADS, Q_PER_KV)[None, None, :, :, None, None]
    m = jnp.maximum(jnp.max(scores, axis=-1, keepdims=True), sink)
    e = jnp.exp(scores - m)
    denom = jnp.sum(e, axis=-1, keepdims=True) + jnp.exp(sink - m)
    probs = (e / denom).astype(vb.dtype)
    o = jnp.einsum('bnkgqs,bnskd->bnqkgd', probs, vb).reshape(B, S, N_Q_HEADS * HEAD_DIM)
    return o @ wo + bo


def per_layer_embedding(x, p_i, g, w_gate, w_proj):
    gate = jax.nn.sigmoid(rms_norm(x, g) @ w_gate)
    return gate * (p_i @ w_proj)


def _fwd_setup_inputs(seed: int = 0) -> dict:
    key = jax.random.key(seed)
    ks = iter(jax.random.split(key, 40))
    f32 = jnp.float32

    def dense(shape, scale=1.0):
        return jax.random.normal(next(ks), shape, f32) * (scale * shape[-2] ** -0.5)

    def gain(shape):
        return 1.0 + 0.02 * jax.random.normal(next(ks), shape, f32)

    def bias(shape, s=0.02):
        return s * jax.random.normal(next(ks), shape, f32)

    out_scale = (2.0 * DEPTH) ** -0.5
    NA, NBL = N_GMLP_LAYERS, N_SWA_LAYERS
    QW = N_Q_HEADS * HEAD_DIM
    KW = N_KV_HEADS * HEAD_DIM
    return {
        "x": jax.random.normal(next(ks), (BATCH, SEQ, D_MODEL), f32),
        "p": jax.random.normal(next(ks), (DEPTH, BATCH, SEQ, PLE_DIM), f32),
        "ffn1_norm": gain((DEPTH, D_MODEL)),
        "ffn1_w1": dense((DEPTH, D_MODEL, D_FF)),
        "ffn1_w3": dense((DEPTH, D_MODEL, D_FF)),
        "ffn1_w2": dense((DEPTH, D_FF, D_MODEL), out_scale),
        "mix_norm": gain((DEPTH, D_MODEL)),
        "ffn2_norm": gain((DEPTH, D_MODEL)),
        "ffn2_w1": dense((DEPTH, D_MODEL, D_FF)),
        "ffn2_w3": dense((DEPTH, D_MODEL, D_FF)),
        "ffn2_w2": dense((DEPTH, D_FF, D_MODEL), out_scale),
        "ple_norm": gain((DEPTH, D_MODEL)),
        "ple_w_gate": dense((DEPTH, D_MODEL, D_MODEL)),
        "ple_w_proj": dense((DEPTH, PLE_DIM, D_MODEL), out_scale),
        "gmlp_w_in": dense((NA, D_MODEL, 2 * GMLP_WIDTH)),
        "gmlp_ln_g": gain((NA, GMLP_WIDTH)),
        "gmlp_ln_b": bias((NA, GMLP_WIDTH)),
        "gmlp_w_s": dense((NA, GMLP_GROUPS, CHUNK, CHUNK), 0.5),
        "gmlp_b_s": 1.0 + bias((NA, GMLP_GROUPS, CHUNK)),
        "gmlp_w_out": dense((NA, GMLP_WIDTH, D_MODEL), out_scale),
        "swa_wq": dense((NBL, D_MODEL, QW)),
        "swa_bq": bias((NBL, QW)),
        "swa_wk": dense((NBL, D_MODEL, KW)),
        "swa_bk": bias((NBL, KW)),
        "swa_wv": dense((NBL, D_MODEL, KW)),
        "swa_bv": bias((NBL, KW)),
        "swa_sinks": 0.5 * jax.random.normal(next(ks), (NBL, N_Q_HEADS), f32),
        "swa_wo": dense((NBL, QW, D_MODEL), out_scale),
        "swa_bo": bias((NBL, D_MODEL)),
        "final_norm": gain((D_MODEL,)),
    }


def _fwd_reference(x, p, ffn1_norm, ffn1_w1, ffn1_w3, ffn1_w2, mix_norm,
              ffn2_norm, ffn2_w1, ffn2_w3, ffn2_w2,
              ple_norm, ple_w_gate, ple_w_proj,
              gmlp_w_in, gmlp_ln_g, gmlp_ln_b, gmlp_w_s, gmlp_b_s, gmlp_w_out,
              swa_wq, swa_bq, swa_wk, swa_bk, swa_wv, swa_bv, swa_sinks, swa_wo, swa_bo,
              final_norm):
    for i in range(DEPTH):
        x = x + FFN_RESIDUAL_WEIGHT * swiglu(rms_norm(x, ffn1_norm[i]), ffn1_w1[i], ffn1_w3[i], ffn1_w2[i])
        h = rms_norm(x, mix_norm[i])
        j = i // N_MIXERS
        if i % N_MIXERS == 0:
            x = x + gmlp_chunk_mixer(h, gmlp_w_in[j], gmlp_ln_g[j], gmlp_ln_b[j],
                                     gmlp_w_s[j], gmlp_b_s[j], gmlp_w_out[j])
        else:
            x = x + swa_sink_attention(h, swa_wq[j], swa_bq[j], swa_wk[j], swa_bk[j],
                                       swa_wv[j], swa_bv[j], swa_sinks[j], swa_wo[j], swa_bo[j])
        x = x + FFN_RESIDUAL_WEIGHT * swiglu(rms_norm(x, ffn2_norm[i]), ffn2_w1[i], ffn2_w3[i], ffn2_w2[i])
        x = x + per_layer_embedding(x, p[i], ple_norm[i], ple_w_gate[i], ple_w_proj[i])
    return rms_norm(x, final_norm)


import jax as _jax
import jax.numpy as _jnp

TWIN_FORMAT = 'train_step'
FWD_PARAMS = ['x', 'p', 'ffn1_norm', 'ffn1_w1', 'ffn1_w3', 'ffn1_w2', 'mix_norm', 'ffn2_norm', 'ffn2_w1', 'ffn2_w3', 'ffn2_w2', 'ple_norm', 'ple_w_gate', 'ple_w_proj', 'gmlp_w_in', 'gmlp_ln_g', 'gmlp_ln_b', 'gmlp_w_s', 'gmlp_b_s', 'gmlp_w_out', 'swa_wq', 'swa_bq', 'swa_wk', 'swa_bk', 'swa_wv', 'swa_bv', 'swa_sinks', 'swa_wo', 'swa_bo', 'final_norm']
TWIN_WEIGHTS = ['ffn1_norm', 'ffn1_w1', 'ffn1_w3', 'ffn1_w2', 'mix_norm', 'ffn2_norm', 'ffn2_w1', 'ffn2_w3', 'ffn2_w2', 'ple_norm', 'ple_w_gate', 'ple_w_proj', 'gmlp_w_in', 'gmlp_ln_g', 'gmlp_ln_b', 'gmlp_w_s', 'gmlp_b_s', 'gmlp_w_out', 'swa_wq', 'swa_bq', 'swa_wk', 'swa_bk', 'swa_wv', 'swa_bv', 'swa_sinks', 'swa_wo', 'swa_bo', 'final_norm']
TWIN_DIFF_INPUT = 'x'
TWIN_INPUTS = ['x', 'p', 'ffn1_norm', 'ffn1_w1', 'ffn1_w3', 'ffn1_w2', 'mix_norm', 'ffn2_norm', 'ffn2_w1', 'ffn2_w3', 'ffn2_w2', 'ple_norm', 'ple_w_gate', 'ple_w_proj', 'gmlp_w_in', 'gmlp_ln_g', 'gmlp_ln_b', 'gmlp_w_s', 'gmlp_b_s', 'gmlp_w_out', 'swa_wq', 'swa_bq', 'swa_wk', 'swa_bk', 'swa_wv', 'swa_bv', 'swa_sinks', 'swa_wo', 'swa_bo', 'final_norm', 'loss_target', 'm_ffn1_norm', 'm_ffn1_w1', 'm_ffn1_w3', 'm_ffn1_w2', 'm_mix_norm', 'm_ffn2_norm', 'm_ffn2_w1', 'm_ffn2_w3', 'm_ffn2_w2', 'm_ple_norm', 'm_ple_w_gate', 'm_ple_w_proj', 'm_gmlp_w_in', 'm_gmlp_ln_g', 'm_gmlp_ln_b', 'm_gmlp_w_s', 'm_gmlp_b_s', 'm_gmlp_w_out', 'm_swa_wq', 'm_swa_bq', 'm_swa_wk', 'm_swa_bk', 'm_swa_wv', 'm_swa_bv', 'm_swa_sinks', 'm_swa_wo', 'm_swa_bo', 'm_final_norm', 'v_ffn1_norm', 'v_ffn1_w1', 'v_ffn1_w3', 'v_ffn1_w2', 'v_mix_norm', 'v_ffn2_norm', 'v_ffn2_w1', 'v_ffn2_w3', 'v_ffn2_w2', 'v_ple_norm', 'v_ple_w_gate', 'v_ple_w_proj', 'v_gmlp_w_in', 'v_gmlp_ln_g', 'v_gmlp_ln_b', 'v_gmlp_w_s', 'v_gmlp_b_s', 'v_gmlp_w_out', 'v_swa_wq', 'v_swa_bq', 'v_swa_wk', 'v_swa_bk', 'v_swa_wv', 'v_swa_bv', 'v_swa_sinks', 'v_swa_wo', 'v_swa_bo', 'v_final_norm']
TWIN_OUTPUTS = ['loss', 'grad_x', 'grad_ffn1_norm', 'grad_ffn1_w1', 'grad_ffn1_w3', 'grad_ffn1_w2', 'grad_mix_norm', 'grad_ffn2_norm', 'grad_ffn2_w1', 'grad_ffn2_w3', 'grad_ffn2_w2', 'grad_ple_norm', 'grad_ple_w_gate', 'grad_ple_w_proj', 'grad_gmlp_w_in', 'grad_gmlp_ln_g', 'grad_gmlp_ln_b', 'grad_gmlp_w_s', 'grad_gmlp_b_s', 'grad_gmlp_w_out', 'grad_swa_wq', 'grad_swa_bq', 'grad_swa_wk', 'grad_swa_bk', 'grad_swa_wv', 'grad_swa_bv', 'grad_swa_sinks', 'grad_swa_wo', 'grad_swa_bo', 'grad_final_norm', 'delta_ffn1_norm', 'delta_ffn1_w1', 'delta_ffn1_w3', 'delta_ffn1_w2', 'delta_mix_norm', 'delta_ffn2_norm', 'delta_ffn2_w1', 'delta_ffn2_w3', 'delta_ffn2_w2', 'delta_ple_norm', 'delta_ple_w_gate', 'delta_ple_w_proj', 'delta_gmlp_w_in', 'delta_gmlp_ln_g', 'delta_gmlp_ln_b', 'delta_gmlp_w_s', 'delta_gmlp_b_s', 'delta_gmlp_w_out', 'delta_swa_wq', 'delta_swa_bq', 'delta_swa_wk', 'delta_swa_bk', 'delta_swa_wv', 'delta_swa_bv', 'delta_swa_sinks', 'delta_swa_wo', 'delta_swa_bo', 'delta_final_norm', 'new_m_ffn1_norm', 'new_m_ffn1_w1', 'new_m_ffn1_w3', 'new_m_ffn1_w2', 'new_m_mix_norm', 'new_m_ffn2_norm', 'new_m_ffn2_w1', 'new_m_ffn2_w3', 'new_m_ffn2_w2', 'new_m_ple_norm', 'new_m_ple_w_gate', 'new_m_ple_w_proj', 'new_m_gmlp_w_in', 'new_m_gmlp_ln_g', 'new_m_gmlp_ln_b', 'new_m_gmlp_w_s', 'new_m_gmlp_b_s', 'new_m_gmlp_w_out', 'new_m_swa_wq', 'new_m_swa_bq', 'new_m_swa_wk', 'new_m_swa_bk', 'new_m_swa_wv', 'new_m_swa_bv', 'new_m_swa_sinks', 'new_m_swa_wo', 'new_m_swa_bo', 'new_m_final_norm', 'new_v_ffn1_norm', 'new_v_ffn1_w1', 'new_v_ffn1_w3', 'new_v_ffn1_w2', 'new_v_mix_norm', 'new_v_ffn2_norm', 'new_v_ffn2_w1', 'new_v_ffn2_w3', 'new_v_ffn2_w2', 'new_v_ple_norm', 'new_v_ple_w_gate', 'new_v_ple_w_proj', 'new_v_gmlp_w_in', 'new_v_gmlp_ln_g', 'new_v_gmlp_ln_b', 'new_v_gmlp_w_s', 'new_v_gmlp_b_s', 'new_v_gmlp_w_out', 'new_v_swa_wq', 'new_v_swa_bq', 'new_v_swa_wk', 'new_v_swa_bk', 'new_v_swa_wv', 'new_v_swa_bv', 'new_v_swa_sinks', 'new_v_swa_wo', 'new_v_swa_bo', 'new_v_final_norm']
TWIN_LEAF_KINDS = {'loss': 'loss', 'grad_x': 'grad_x', 'grad_ffn1_norm': 'grad_w', 'grad_ffn1_w1': 'grad_w', 'grad_ffn1_w3': 'grad_w', 'grad_ffn1_w2': 'grad_w', 'grad_mix_norm': 'grad_w', 'grad_ffn2_norm': 'grad_w', 'grad_ffn2_w1': 'grad_w', 'grad_ffn2_w3': 'grad_w', 'grad_ffn2_w2': 'grad_w', 'grad_ple_norm': 'grad_w', 'grad_ple_w_gate': 'grad_w', 'grad_ple_w_proj': 'grad_w', 'grad_gmlp_w_in': 'grad_w', 'grad_gmlp_ln_g': 'grad_w', 'grad_gmlp_ln_b': 'grad_w', 'grad_gmlp_w_s': 'grad_w', 'grad_gmlp_b_s': 'grad_w', 'grad_gmlp_w_out': 'grad_w', 'grad_swa_wq': 'grad_w', 'grad_swa_bq': 'grad_w', 'grad_swa_wk': 'grad_w', 'grad_swa_bk': 'grad_w', 'grad_swa_wv': 'grad_w', 'grad_swa_bv': 'grad_w', 'grad_swa_sinks': 'grad_w', 'grad_swa_wo': 'grad_w', 'grad_swa_bo': 'grad_w', 'grad_final_norm': 'grad_w', 'delta_ffn1_norm': 'delta_w', 'delta_ffn1_w1': 'delta_w', 'delta_ffn1_w3': 'delta_w', 'delta_ffn1_w2': 'delta_w', 'delta_mix_norm': 'delta_w', 'delta_ffn2_norm': 'delta_w', 'delta_ffn2_w1': 'delta_w', 'delta_ffn2_w3': 'delta_w', 'delta_ffn2_w2': 'delta_w', 'delta_ple_norm': 'delta_w', 'delta_ple_w_gate': 'delta_w', 'delta_ple_w_proj': 'delta_w', 'delta_gmlp_w_in': 'delta_w', 'delta_gmlp_ln_g': 'delta_w', 'delta_gmlp_ln_b': 'delta_w', 'delta_gmlp_w_s': 'delta_w', 'delta_gmlp_b_s': 'delta_w', 'delta_gmlp_w_out': 'delta_w', 'delta_swa_wq': 'delta_w', 'delta_swa_bq': 'delta_w', 'delta_swa_wk': 'delta_w', 'delta_swa_bk': 'delta_w', 'delta_swa_wv': 'delta_w', 'delta_swa_bv': 'delta_w', 'delta_swa_sinks': 'delta_w', 'delta_swa_wo': 'delta_w', 'delta_swa_bo': 'delta_w', 'delta_final_norm': 'delta_w', 'new_m_ffn1_norm': 'new_m', 'new_m_ffn1_w1': 'new_m', 'new_m_ffn1_w3': 'new_m', 'new_m_ffn1_w2': 'new_m', 'new_m_mix_norm': 'new_m', 'new_m_ffn2_norm': 'new_m', 'new_m_ffn2_w1': 'new_m', 'new_m_ffn2_w3': 'new_m', 'new_m_ffn2_w2': 'new_m', 'new_m_ple_norm': 'new_m', 'new_m_ple_w_gate': 'new_m', 'new_m_ple_w_proj': 'new_m', 'new_m_gmlp_w_in': 'new_m', 'new_m_gmlp_ln_g': 'new_m', 'new_m_gmlp_ln_b': 'new_m', 'new_m_gmlp_w_s': 'new_m', 'new_m_gmlp_b_s': 'new_m', 'new_m_gmlp_w_out': 'new_m', 'new_m_swa_wq': 'new_m', 'new_m_swa_bq': 'new_m', 'new_m_swa_wk': 'new_m', 'new_m_swa_bk': 'new_m', 'new_m_swa_wv': 'new_m', 'new_m_swa_bv': 'new_m', 'new_m_swa_sinks': 'new_m', 'new_m_swa_wo': 'new_m', 'new_m_swa_bo': 'new_m', 'new_m_final_norm': 'new_m', 'new_v_ffn1_norm': 'new_v', 'new_v_ffn1_w1': 'new_v', 'new_v_ffn1_w3': 'new_v', 'new_v_ffn1_w2': 'new_v', 'new_v_mix_norm': 'new_v', 'new_v_ffn2_norm': 'new_v', 'new_v_ffn2_w1': 'new_v', 'new_v_ffn2_w3': 'new_v', 'new_v_ffn2_w2': 'new_v', 'new_v_ple_norm': 'new_v', 'new_v_ple_w_gate': 'new_v', 'new_v_ple_w_proj': 'new_v', 'new_v_gmlp_w_in': 'new_v', 'new_v_gmlp_ln_g': 'new_v', 'new_v_gmlp_ln_b': 'new_v', 'new_v_gmlp_w_s': 'new_v', 'new_v_gmlp_b_s': 'new_v', 'new_v_gmlp_w_out': 'new_v', 'new_v_swa_wq': 'new_v', 'new_v_swa_bq': 'new_v', 'new_v_swa_wk': 'new_v', 'new_v_swa_bk': 'new_v', 'new_v_swa_wv': 'new_v', 'new_v_swa_bv': 'new_v', 'new_v_swa_sinks': 'new_v', 'new_v_swa_wo': 'new_v', 'new_v_swa_bo': 'new_v', 'new_v_final_norm': 'new_v'}


def _forward(args):
    return _fwd_reference(*[args[k] for k in FWD_PARAMS])


def _output_shape():
    def fwd():
        inp = _fwd_setup_inputs(0)
        return _fwd_reference(*[inp[k] for k in FWD_PARAMS])
    out = _jax.eval_shape(fwd)
    return out.shape, out.dtype

N_MICROBATCH = 1
ADAM_LR = 0.001
ADAM_B1 = 0.9
ADAM_B2 = 0.999
ADAM_EPS = 1e-08
ADAM_WD = 0.01
ADAM_STEP = 10
PER_EXAMPLE_BATCH_AXIS = {'x': 0, 'p': 1, 'loss_target': 0}
SHARED_INPUTS = []
_WEIGHT_DTYPES = {'ffn1_norm': _jnp.float32, 'ffn1_w1': _jnp.float32, 'ffn1_w3': _jnp.float32, 'ffn1_w2': _jnp.float32, 'mix_norm': _jnp.float32, 'ffn2_norm': _jnp.float32, 'ffn2_w1': _jnp.float32, 'ffn2_w3': _jnp.float32, 'ffn2_w2': _jnp.float32, 'ple_norm': _jnp.float32, 'ple_w_gate': _jnp.float32, 'ple_w_proj': _jnp.float32, 'gmlp_w_in': _jnp.float32, 'gmlp_ln_g': _jnp.float32, 'gmlp_ln_b': _jnp.float32, 'gmlp_w_s': _jnp.float32, 'gmlp_b_s': _jnp.float32, 'gmlp_w_out': _jnp.float32, 'swa_wq': _jnp.float32, 'swa_bq': _jnp.float32, 'swa_wk': _jnp.float32, 'swa_bk': _jnp.float32, 'swa_wv': _jnp.float32, 'swa_bv': _jnp.float32, 'swa_sinks': _jnp.float32, 'swa_wo': _jnp.float32, 'swa_bo': _jnp.float32, 'final_norm': _jnp.float32}
MOMENT_SCALE = {'ffn1_norm': 1.800138e-02, 'ffn1_w1': 7.687494e-03, 'ffn1_w3': 7.445787e-03, 'ffn1_w2': 2.468698e-02, 'mix_norm': 2.407051e-02, 'ffn2_norm': 1.709153e-02, 'ffn2_w1': 7.246550e-03, 'ffn2_w3': 7.022760e-03, 'ffn2_w2': 2.332275e-02, 'ple_norm': 8.262283e-03, 'ple_w_gate': 8.282577e-03, 'ple_w_proj': 4.203570e-02, 'gmlp_w_in': 1.557004e-02, 'gmlp_ln_g': 7.006377e-03, 'gmlp_ln_b': 6.730864e-03, 'gmlp_w_s': 1.876802e-02, 'gmlp_b_s': 2.670000e-02, 'gmlp_w_out': 6.004840e-02, 'swa_wq': 6.781276e-03, 'swa_bq': 6.765816e-03, 'swa_wk': 1.940904e-02, 'swa_bk': 6.241859e-03, 'swa_wv': 2.567146e-02, 'swa_bv': 1.573748e-01, 'swa_sinks': 4.915187e-03, 'swa_wo': 1.859179e-02, 'swa_bo': 9.803109e-02, 'final_norm': 1.599436e+01}


def _to_microbatches(a, axis):
    t = _jnp.moveaxis(a, axis, 0)
    t = t.reshape((N_MICROBATCH, t.shape[0] // N_MICROBATCH) + t.shape[1:])
    return _jnp.moveaxis(t, 1, axis + 1)


def setup_inputs(seed: int = 0) -> dict:
    inp = _fwd_setup_inputs(seed)
    key = _jax.random.fold_in(_jax.random.key(seed), 7919)
    shape, _ = _output_shape()
    out = dict(inp)
    out["loss_target"] = _jax.random.normal(_jax.random.fold_in(key, 0), shape, _jnp.float32)
    for i, name in enumerate(TWIN_WEIGHTS):
        w = inp[name].astype(_jnp.float32)
        if MOMENT_SCALE is None:
            s = _jnp.sqrt(_jnp.mean(_jnp.square(w)) + 1e-30)
        else:
            s = MOMENT_SCALE[name]
        km, kv = _jax.random.split(_jax.random.fold_in(key, i + 1))
        out[name] = w
        out["m_" + name] = s * _jax.random.normal(km, w.shape, _jnp.float32)
        out["v_" + name] = (s * s) * _jax.random.uniform(kv, w.shape, _jnp.float32, 0.5, 1.5)
    if N_MICROBATCH > 1:
        for name, axis in PER_EXAMPLE_BATCH_AXIS.items():
            out[name] = _to_microbatches(out[name], axis)
    return {'x': out['x'], 'p': out['p'], 'ffn1_norm': out['ffn1_norm'], 'ffn1_w1': out['ffn1_w1'], 'ffn1_w3': out['ffn1_w3'], 'ffn1_w2': out['ffn1_w2'], 'mix_norm': out['mix_norm'], 'ffn2_norm': out['ffn2_norm'], 'ffn2_w1': out['ffn2_w1'], 'ffn2_w3': out['ffn2_w3'], 'ffn2_w2': out['ffn2_w2'], 'ple_norm': out['ple_norm'], 'ple_w_gate': out['ple_w_gate'], 'ple_w_proj': out['ple_w_proj'], 'gmlp_w_in': out['gmlp_w_in'], 'gmlp_ln_g': out['gmlp_ln_g'], 'gmlp_ln_b': out['gmlp_ln_b'], 'gmlp_w_s': out['gmlp_w_s'], 'gmlp_b_s': out['gmlp_b_s'], 'gmlp_w_out': out['gmlp_w_out'], 'swa_wq': out['swa_wq'], 'swa_bq': out['swa_bq'], 'swa_wk': out['swa_wk'], 'swa_bk': out['swa_bk'], 'swa_wv': out['swa_wv'], 'swa_bv': out['swa_bv'], 'swa_sinks': out['swa_sinks'], 'swa_wo': out['swa_wo'], 'swa_bo': out['swa_bo'], 'final_norm': out['final_norm'], 'loss_target': out['loss_target'], 'm_ffn1_norm': out['m_ffn1_norm'], 'm_ffn1_w1': out['m_ffn1_w1'], 'm_ffn1_w3': out['m_ffn1_w3'], 'm_ffn1_w2': out['m_ffn1_w2'], 'm_mix_norm': out['m_mix_norm'], 'm_ffn2_norm': out['m_ffn2_norm'], 'm_ffn2_w1': out['m_ffn2_w1'], 'm_ffn2_w3': out['m_ffn2_w3'], 'm_ffn2_w2': out['m_ffn2_w2'], 'm_ple_norm': out['m_ple_norm'], 'm_ple_w_gate': out['m_ple_w_gate'], 'm_ple_w_proj': out['m_ple_w_proj'], 'm_gmlp_w_in': out['m_gmlp_w_in'], 'm_gmlp_ln_g': out['m_gmlp_ln_g'], 'm_gmlp_ln_b': out['m_gmlp_ln_b'], 'm_gmlp_w_s': out['m_gmlp_w_s'], 'm_gmlp_b_s': out['m_gmlp_b_s'], 'm_gmlp_w_out': out['m_gmlp_w_out'], 'm_swa_wq': out['m_swa_wq'], 'm_swa_bq': out['m_swa_bq'], 'm_swa_wk': out['m_swa_wk'], 'm_swa_bk': out['m_swa_bk'], 'm_swa_wv': out['m_swa_wv'], 'm_swa_bv': out['m_swa_bv'], 'm_swa_sinks': out['m_swa_sinks'], 'm_swa_wo': out['m_swa_wo'], 'm_swa_bo': out['m_swa_bo'], 'm_final_norm': out['m_final_norm'], 'v_ffn1_norm': out['v_ffn1_norm'], 'v_ffn1_w1': out['v_ffn1_w1'], 'v_ffn1_w3': out['v_ffn1_w3'], 'v_ffn1_w2': out['v_ffn1_w2'], 'v_mix_norm': out['v_mix_norm'], 'v_ffn2_norm': out['v_ffn2_norm'], 'v_ffn2_w1': out['v_ffn2_w1'], 'v_ffn2_w3': out['v_ffn2_w3'], 'v_ffn2_w2': out['v_ffn2_w2'], 'v_ple_norm': out['v_ple_norm'], 'v_ple_w_gate': out['v_ple_w_gate'], 'v_ple_w_proj': out['v_ple_w_proj'], 'v_gmlp_w_in': out['v_gmlp_w_in'], 'v_gmlp_ln_g': out['v_gmlp_ln_g'], 'v_gmlp_ln_b': out['v_gmlp_ln_b'], 'v_gmlp_w_s': out['v_gmlp_w_s'], 'v_gmlp_b_s': out['v_gmlp_b_s'], 'v_gmlp_w_out': out['v_gmlp_w_out'], 'v_swa_wq': out['v_swa_wq'], 'v_swa_bq': out['v_swa_bq'], 'v_swa_wk': out['v_swa_wk'], 'v_swa_bk': out['v_swa_bk'], 'v_swa_wv': out['v_swa_wv'], 'v_swa_bv': out['v_swa_bv'], 'v_swa_sinks': out['v_swa_sinks'], 'v_swa_wo': out['v_swa_wo'], 'v_swa_bo': out['v_swa_bo'], 'v_final_norm': out['v_final_norm']}


def _loss(weights, diff, rest, loss_target):
    with _jax.named_scope("forward"):
        args = {**rest, TWIN_DIFF_INPUT: diff, **{k: w.astype(_WEIGHT_DTYPES[k]) for k, w in weights.items()}}
        y = _forward(args)
    with _jax.named_scope("loss_head"):
        err = _jnp.square(y.astype(_jnp.float32) - loss_target)
        return 0.5 * _jnp.sum(_jnp.mean(err, axis=-1)) if err.ndim else 0.5 * err


def _adamw(w, g, m, v):
    m = ADAM_B1 * m + (1.0 - ADAM_B1) * g
    v = ADAM_B2 * v + (1.0 - ADAM_B2) * _jnp.square(g)
    m_hat = m / (1.0 - ADAM_B1 ** ADAM_STEP)
    v_hat = v / (1.0 - ADAM_B2 ** ADAM_STEP)
    delta = -ADAM_LR * (m_hat / (_jnp.sqrt(v_hat) + ADAM_EPS) + ADAM_WD * w)
    return delta, m, v


def reference(x, p, ffn1_norm, ffn1_w1, ffn1_w3, ffn1_w2, mix_norm, ffn2_norm, ffn2_w1, ffn2_w3, ffn2_w2, ple_norm, ple_w_gate, ple_w_proj, gmlp_w_in, gmlp_ln_g, gmlp_ln_b, gmlp_w_s, gmlp_b_s, gmlp_w_out, swa_wq, swa_bq, swa_wk, swa_bk, swa_wv, swa_bv, swa_sinks, swa_wo, swa_bo, final_norm, loss_target, m_ffn1_norm, m_ffn1_w1, m_ffn1_w3, m_ffn1_w2, m_mix_norm, m_ffn2_norm, m_ffn2_w1, m_ffn2_w3, m_ffn2_w2, m_ple_norm, m_ple_w_gate, m_ple_w_proj, m_gmlp_w_in, m_gmlp_ln_g, m_gmlp_ln_b, m_gmlp_w_s, m_gmlp_b_s, m_gmlp_w_out, m_swa_wq, m_swa_bq, m_swa_wk, m_swa_bk, m_swa_wv, m_swa_bv, m_swa_sinks, m_swa_wo, m_swa_bo, m_final_norm, v_ffn1_norm, v_ffn1_w1, v_ffn1_w3, v_ffn1_w2, v_mix_norm, v_ffn2_norm, v_ffn2_w1, v_ffn2_w3, v_ffn2_w2, v_ple_norm, v_ple_w_gate, v_ple_w_proj, v_gmlp_w_in, v_gmlp_ln_g, v_gmlp_ln_b, v_gmlp_w_s, v_gmlp_b_s, v_gmlp_w_out, v_swa_wq, v_swa_bq, v_swa_wk, v_swa_bk, v_swa_wv, v_swa_bv, v_swa_sinks, v_swa_wo, v_swa_bo, v_final_norm):
    given = dict(x=x, p=p, ffn1_norm=ffn1_norm, ffn1_w1=ffn1_w1, ffn1_w3=ffn1_w3, ffn1_w2=ffn1_w2, mix_norm=mix_norm, ffn2_norm=ffn2_norm, ffn2_w1=ffn2_w1, ffn2_w3=ffn2_w3, ffn2_w2=ffn2_w2, ple_norm=ple_norm, ple_w_gate=ple_w_gate, ple_w_proj=ple_w_proj, gmlp_w_in=gmlp_w_in, gmlp_ln_g=gmlp_ln_g, gmlp_ln_b=gmlp_ln_b, gmlp_w_s=gmlp_w_s, gmlp_b_s=gmlp_b_s, gmlp_w_out=gmlp_w_out, swa_wq=swa_wq, swa_bq=swa_bq, swa_wk=swa_wk, swa_bk=swa_bk, swa_wv=swa_wv, swa_bv=swa_bv, swa_sinks=swa_sinks, swa_wo=swa_wo, swa_bo=swa_bo, final_norm=final_norm, loss_target=loss_target, m_ffn1_norm=m_ffn1_norm, m_ffn1_w1=m_ffn1_w1, m_ffn1_w3=m_ffn1_w3, m_ffn1_w2=m_ffn1_w2, m_mix_norm=m_mix_norm, m_ffn2_norm=m_ffn2_norm, m_ffn2_w1=m_ffn2_w1, m_ffn2_w3=m_ffn2_w3, m_ffn2_w2=m_ffn2_w2, m_ple_norm=m_ple_norm, m_ple_w_gate=m_ple_w_gate, m_ple_w_proj=m_ple_w_proj, m_gmlp_w_in=m_gmlp_w_in, m_gmlp_ln_g=m_gmlp_ln_g, m_gmlp_ln_b=m_gmlp_ln_b, m_gmlp_w_s=m_gmlp_w_s, m_gmlp_b_s=m_gmlp_b_s, m_gmlp_w_out=m_gmlp_w_out, m_swa_wq=m_swa_wq, m_swa_bq=m_swa_bq, m_swa_wk=m_swa_wk, m_swa_bk=m_swa_bk, m_swa_wv=m_swa_wv, m_swa_bv=m_swa_bv, m_swa_sinks=m_swa_sinks, m_swa_wo=m_swa_wo, m_swa_bo=m_swa_bo, m_final_norm=m_final_norm, v_ffn1_norm=v_ffn1_norm, v_ffn1_w1=v_ffn1_w1, v_ffn1_w3=v_ffn1_w3, v_ffn1_w2=v_ffn1_w2, v_mix_norm=v_mix_norm, v_ffn2_norm=v_ffn2_norm, v_ffn2_w1=v_ffn2_w1, v_ffn2_w3=v_ffn2_w3, v_ffn2_w2=v_ffn2_w2, v_ple_norm=v_ple_norm, v_ple_w_gate=v_ple_w_gate, v_ple_w_proj=v_ple_w_proj, v_gmlp_w_in=v_gmlp_w_in, v_gmlp_ln_g=v_gmlp_ln_g, v_gmlp_ln_b=v_gmlp_ln_b, v_gmlp_w_s=v_gmlp_w_s, v_gmlp_b_s=v_gmlp_b_s, v_gmlp_w_out=v_gmlp_w_out, v_swa_wq=v_swa_wq, v_swa_bq=v_swa_bq, v_swa_wk=v_swa_wk, v_swa_bk=v_swa_bk, v_swa_wv=v_swa_wv, v_swa_bv=v_swa_bv, v_swa_sinks=v_swa_sinks, v_swa_wo=v_swa_wo, v_swa_bo=v_swa_bo, v_final_norm=v_final_norm)
    weights = {n: given[n] for n in TWIN_WEIGHTS}
    shared = {n: given[n] for n in SHARED_INPUTS}
    per_example = {n: given[n] for n in ['x', 'p']}
    grad_fn = _jax.value_and_grad(_loss, argnums=(0, 1))

    def one_microbatch(ex, loss_target):
        ex = dict(ex)
        diff = ex.pop(TWIN_DIFF_INPUT)
        return grad_fn(weights, diff, {**shared, **ex}, loss_target)

    if N_MICROBATCH == 1:
        loss, (grad_w, grad_x) = one_microbatch(per_example, given["loss_target"])
    else:
        def body(carry, xs):
            loss_sum, grad_sum = carry
            l_k, (gw_k, gx_k) = one_microbatch(xs[0], xs[1])
            with _jax.named_scope("update"):
                return (loss_sum + l_k, _jax.tree.map(_jnp.add, grad_sum, gw_k)), gx_k

        init = (_jnp.zeros((), _jnp.float32), _jax.tree.map(_jnp.zeros_like, weights))
        (loss, grad_w), grad_x = _jax.lax.scan(body, init, (per_example, given["loss_target"]))
    with _jax.named_scope("update"):
        delta_w, new_m, new_v = {}, {}, {}
        for n in TWIN_WEIGHTS:
            delta_w[n], new_m[n], new_v[n] = _adamw(weights[n], grad_w[n], given["m_" + n], given["v_" + n])
    return (loss, grad_x, *[grad_w[n] for n in TWIN_WEIGHTS], *[delta_w[n] for n in TWIN_WEIGHTS],
            *[new_m[n] for n in TWIN_WEIGHTS], *[new_v[n] for n in TWIN_WEIGHTS])
```

```python
import functools

import jax
import jax.numpy as jnp
from jax import lax
from jax.experimental import pallas as pl
from jax.experimental.pallas import tpu as pltpu

F32 = jnp.float32
BF16 = jnp.bfloat16

RMS_EPS = 1e-6
LN_EPS = 1e-5
FFN_RESIDUAL_WEIGHT = 0.5
HEAD_DIM = 64
ROPE_DIM = 16
ROPE_THETA = 500000.0
ADAM_LR = 0.001
ADAM_B1 = 0.9
ADAM_B2 = 0.999
ADAM_EPS = 1e-08
ADAM_WD = 0.01
ADAM_STEP = 10
N_CHIPS = 4
N_DEV = 8
LANES = 128
VMEM_LIMIT_BYTES = 56 * 1024 * 1024
MESH_ID = pl.DeviceIdType.MESH

_DIMS = {
    "nn": (((1,), (0,)), ((), ())),
    "nt": (((1,), (1,)), ((), ())),
    "tn": (((0,), (0,)), ((), ())),
}


def _params(n_axes):
    return pltpu.CompilerParams(dimension_semantics=("arbitrary",) * n_axes, vmem_limit_bytes=VMEM_LIMIT_BYTES)


def _mm(name, grid, pairs, extras, outs, epilogue, acc_shapes, order="ij"):
    ni, nj, nk = grid
    if order == "ij":
        pgrid = (ni, nj, nk)
        ijk = lambda g0, g1, g2: (g0, g1, g2)
    else:
        pgrid = (nj, ni, nk)
        ijk = lambda g0, g1, g2: (g1, g0, g2)
    in_specs, args = [], []
    for a, ablk, amap, b, bblk, bmap, _, _, _ in pairs:
        in_specs.append(pl.BlockSpec(ablk, lambda g0, g1, g2, m=amap: m(*ijk(g0, g1, g2))))
        in_specs.append(pl.BlockSpec(bblk, lambda g0, g1, g2, m=bmap: m(*ijk(g0, g1, g2))))
        args += [a, b]
    for e, eblk, emap in extras:
        in_specs.append(pl.BlockSpec(eblk, lambda g0, g1, g2, m=emap: m(*ijk(g0, g1, g2)[:2])))
        args.append(e)
    out_specs = [pl.BlockSpec(oblk, lambda g0, g1, g2, m=omap: m(*ijk(g0, g1, g2)[:2])) for _, _, oblk, omap in outs]
    out_shape = [jax.ShapeDtypeStruct(s, d) for s, d, _, _ in outs]
    n_p, n_e, n_o = len(pairs), len(extras), len(outs)

    def body(*refs):
        p_refs = refs[: 2 * n_p]
        e_refs = refs[2 * n_p: 2 * n_p + n_e]
        o_refs = refs[2 * n_p + n_e: 2 * n_p + n_e + n_o]
        accs = refs[2 * n_p + n_e + n_o:]
        k = pl.program_id(2)

        @pl.when(k == 0)
        def _():
            for acc in accs:
                acc[...] = jnp.zeros(acc.shape, F32)

        for idx in range(n_p):
            dims, ai, k0 = pairs[idx][6], pairs[idx][7], pairs[idx][8]

            def step(idx=idx, dims=dims, ai=ai):
                a = p_refs[2 * idx][...].astype(BF16)
                b = p_refs[2 * idx + 1][...].astype(BF16)
                accs[ai][...] += lax.dot_general(a, b, _DIMS[dims], preferred_element_type=F32)

            if k0:
                pl.when(k == 0)(step)
            else:
                step()

        @pl.when(k == nk - 1)
        def _():
            vals = epilogue([acc[...] for acc in accs], [e[...] for e in e_refs])
            for o, v in zip(o_refs, vals):
                o[...] = v.astype(o.dtype)

    res = pl.pallas_call(
        body, name=name, grid=pgrid, in_specs=in_specs, out_specs=out_specs, out_shape=out_shape,
        scratch_shapes=[pltpu.VMEM(s, F32) for s in acc_shapes], compiler_params=_params(3),
    )(*args)
    return res


def _col_nn(r0, tk, tn, wc):
    assert r0 % tk == 0 and wc % tn == 0
    npc = wc // tn
    return (None, tk, tn), lambda i, j, k: (j // npc, r0 // tk + k, j % npc)


def _row_nn(r0, tk, tn, rc):
    assert r0 % tk == 0 and rc % tk == 0
    kpc = rc // tk
    return (None, tk, tn), lambda i, j, k: (k // kpc, r0 // tk + k % kpc, j)


def _col_nt(r0, tn, tk, wc):
    assert r0 % tn == 0 and wc % tk == 0
    kpc = wc // tk
    return (None, tn, tk), lambda i, j, k: (k // kpc, r0 // tn + j, k % kpc)


def _row_nt(r0, tn, tk, rc):
    assert r0 % tn == 0 and rc % tn == 0
    npc = rc // tn
    return (None, tn, tk), lambda i, j, k: (j // npc, r0 // tn + j % npc, k)


def _out_col(tm, tn, wc):
    npc = wc // tn
    return (None, tm, tn), lambda i, j: (j // npc, i, j % npc)


def _out_row(tm, tn, rc):
    mpc = rc // tm
    return (None, tm, tn), lambda i, j: (i // mpc, i % mpc, j)


def _ik(i, j, k):
    return (i, k)


def _ki(i, j, k):
    return (k, i)


def _kj(i, j, k):
    return (k, j)


def _i0(i, j, k):
    return (i, 0)


def _ij(i, j):
    return (i, j)


def _0j(i, j):
    return (0, j)


def _i0e(i, j):
    return (i, 0)


def _rows(name, n, ins, outs, body, scratch=()):
    in_specs, args = [], []
    for item in ins:
        if len(item) == 2:
            in_specs.append(item[1])
        else:
            in_specs.append(pl.BlockSpec(item[1], item[2]))
        args.append(item[0])
    out_specs = [pl.BlockSpec(blk, m) for _, _, blk, m in outs]
    out_shape = [jax.ShapeDtypeStruct(s, d) for s, d, _, _ in outs]
    return pl.pallas_call(
        body, name=name, grid=(n,), in_specs=in_specs, out_specs=out_specs, out_shape=out_shape,
        scratch_shapes=list(scratch), compiler_params=_params(1),
    )(*args)


def _row(arr, tm):
    return (arr, (tm, arr.shape[1]), lambda i: (i, 0))


def _full(arr):
    nd = arr.ndim
    return (arr, arr.shape, lambda i: (0,) * nd)


def _row_out(shape, dtype, tm):
    return (shape, dtype, (tm, shape[1]), lambda i: (i, 0))


def _acc_out(shape):
    nd = len(shape)
    return (shape, F32, shape, lambda i: (0,) * nd)


def _tm(t):
    return 512 if t >= 1024 else t // 2


def _first(i, refs):
    @pl.when(i == 0)
    def _():
        for r in refs:
            r[...] = jnp.zeros(r.shape, r.dtype)


def _rms_fwd(x, g, tag):
    t, d = x.shape
    tm = _tm(t)

    def body(x_ref, g_ref, h_ref):
        xv = x_ref[...]
        r = lax.rsqrt(jnp.mean(xv * xv, axis=-1, keepdims=True) + RMS_EPS)
        h_ref[...] = (xv * r * g_ref[...]).astype(BF16)

    return _rows(f"rms_fwd_{tag}", t // tm, [_row(x, tm), _full(g)], [_row_out((t, d), BF16, tm)], body)[0]


def _rms_bwd(dh, x, g, dx_out, tag):
    t, d = x.shape
    tm = _tm(t)

    def body(dh_ref, x_ref, g_ref, dxo_ref, dx_ref, dg_ref):
        i = pl.program_id(0)
        _first(i, [dg_ref])
        xv = x_ref[...]
        r = lax.rsqrt(jnp.mean(xv * xv, axis=-1, keepdims=True) + RMS_EPS)
        xh = xv * r
        dhv = dh_ref[...]
        dxh = dhv * g_ref[...]
        dx_ref[...] = dxo_ref[...] + r * (dxh - xh * jnp.mean(dxh * xh, axis=-1, keepdims=True))
        dg_ref[...] += jnp.sum(dhv * xh, axis=0, keepdims=True)

    return _rows(f"rms_bwd_{tag}", t // tm, [_row(dh, tm), _row(x, tm), _full(g), _row(dx_out, tm)],
                 [_row_out((t, d), F32, tm), _acc_out((1, d))], body)


def _loss_head(x, g, target):
    t, d = x.shape
    tm = _tm(t)

    def body(x_ref, g_ref, t_ref, dx_ref, dg_ref, loss_ref):
        i = pl.program_id(0)
        _first(i, [dg_ref, loss_ref])
        xv = x_ref[...]
        gv = g_ref[...]
        r = lax.rsqrt(jnp.mean(xv * xv, axis=-1, keepdims=True) + RMS_EPS)
        xh = xv * r
        err = xh * gv - t_ref[...]
        loss_ref[...] += jnp.full((1, LANES), 0.5, F32) * jnp.sum(jnp.mean(err * err, axis=-1, keepdims=True))
        dy = err * (1.0 / d)
        dxh = dy * gv
        dx_ref[...] = r * (dxh - xh * jnp.mean(dxh * xh, axis=-1, keepdims=True))
        dg_ref[...] += jnp.sum(dy * xh, axis=0, keepdims=True)

    return _rows("loss_head", t // tm, [_row(x, tm), _full(g), _row(target, tm)],
                 [_row_out((t, d), F32, tm), _acc_out((1, d)), _acc_out((1, LANES))], body)


def _colsum(arr, tag):
    t, w = arr.shape
    tm = _tm(t)

    def body(a_ref, o_ref):
        _first(pl.program_id(0), [o_ref])
        o_ref[...] += jnp.sum(a_ref[...].astype(F32), axis=0, keepdims=True)

    return _rows(f"colsum_{tag}", t // tm, [_row(arr, tm)], [_acc_out((1, w))], body)[0]


def _ffn_fwd(x, gn, f_a, f_b, o_w1, o_w3, o_w2, tag):
    t, d = x.shape
    fc = f_a.shape[2]
    f = N_CHIPS * fc
    tm = _tm(t)
    h = _rms_fwd(x, gn, f"ffn_{tag}")
    w1blk, w1map = _col_nn(o_w1, d, fc, fc)
    w3blk, w3map = _col_nn(o_w3, d, fc, fc)
    tma = min(tm, 256)

    def ep_ab(accs, ex):
        a, b = accs
        return a, b, a * jax.nn.sigmoid(a) * b

    a, b, act = _mm(
        f"ffn_ab_{tag}", (t // tma, N_CHIPS, 1),
        [(h, (tma, d), _ik, f_a, w1blk, w1map, "nn", 0, False), (h, (tma, d), _ik, f_a, w3blk, w3map, "nn", 1, False)],
        [], [((t, f), BF16, (tma, fc), _ij)] * 3, ep_ab, [(tma, fc)] * 2, order="ji")
    tn = min(d, 1024)
    w2blk, w2map = _row_nn(o_w2, fc, tn, fc)
    x_new = _mm(
        f"ffn_out_{tag}", (t // tm, d // tn, N_CHIPS),
        [(act, (tm, fc), _ik, f_b, w2blk, w2map, "nn", 0, False)],
        [(x, (tm, tn), _ij)], [((t, d), F32, (tm, tn), _ij)],
        lambda accs, ex: [ex[0] + FFN_RESIDUAL_WEIGHT * accs[0]], [(tm, tn)])[0]
    return x_new, (x, h, a, b, act)


def _ffn_bwd(dx_out, saved, gn, f_a, f_b, o_w1, o_w3, o_w2, tag):
    x, h, a, b, act = saved
    t, d = x.shape
    fc = f_a.shape[2]
    f = N_CHIPS * fc
    tm = _tm(t)
    tma = min(tm, 256)
    w2blk, w2map = _row_nt(o_w2, fc, d, fc)

    def ep_dg(accs, ex):
        dg = FFN_RESIDUAL_WEIGHT * accs[0]
        av, bv = ex[0].astype(F32), ex[1].astype(F32)
        sig = jax.nn.sigmoid(av)
        return dg * bv * sig * (1.0 + av * (1.0 - sig)), dg * av * sig

    da, db = _mm(
        f"ffn_dg_{tag}", (t // tma, N_CHIPS, 1),
        [(dx_out, (tma, d), _ik, f_b, w2blk, w2map, "nt", 0, False)],
        [(a, (tma, fc), _ij), (b, (tma, fc), _ij)], [((t, f), BF16, (tma, fc), _ij)] * 2, ep_dg, [(tma, fc)], order="ji")

    tmd = min(d, 512)
    tk = min(t, 512)
    oblk, omap = _out_col(tmd, fc, fc)
    dw1, dw3 = _mm(
        f"ffn_dw13_{tag}", (d // tmd, N_CHIPS, t // tk),
        [(h, (tk, tmd), _ki, da, (tk, fc), _kj, "tn", 0, False), (h, (tk, tmd), _ki, db, (tk, fc), _kj, "tn", 1, False)],
        [], [((N_CHIPS, d, fc), BF16, oblk, omap)] * 2, lambda accs, ex: accs, [(tmd, fc)] * 2)
    tn = min(d, 1024)
    tk2 = min(t, 256)
    dw2 = _mm(
        f"ffn_dw2_{tag}", (N_CHIPS, d // tn, t // tk2),
        [(act, (tk2, fc), _ki, dx_out, (tk2, tn), _kj, "tn", 0, False)],
        [], [((N_CHIPS, fc, d), BF16, (None, fc, tn), lambda i, j: (i, 0, j))],
        lambda accs, ex: [FFN_RESIDUAL_WEIGHT * accs[0]], [(fc, tn)])[0]
    w1blk, w1map = _col_nt(o_w1, tn, fc, fc)
    w3blk, w3map = _col_nt(o_w3, tn, fc, fc)
    dh = _mm(
        f"ffn_dh_{tag}", (t // tm, d // tn, N_CHIPS),
        [(da, (tm, fc), _ik, f_a, w1blk, w1map, "nt", 0, False), (db, (tm, fc), _ik, f_a, w3blk, w3map, "nt", 0, False)],
        [], [((t, d), F32, (tm, tn), _ij)], lambda accs, ex: accs, [(tm, tn)])[0]
    dx, dgn = _rms_bwd(dh, x, gn, dx_out, f"ffn_{tag}")
    return dx, dgn, dw1, dw3, dw2


def _ple_fwd(x, p_i, gn, f_b, f_d, o_gate, o_proj, tag):
    t, d = x.shape
    pdim = p_i.shape[1]
    d4 = d // N_CHIPS
    tm = _tm(t)
    hp = _rms_fwd(x, gn, f"ple_{tag}")
    gblk, gmap = _row_nn(o_gate, d4, d4, d4)
    assert o_proj % pdim == 0

    def ep(accs, ex):
        s = jax.nn.sigmoid(accs[0])
        return ex[0] + s * accs[1], s, accs[1]

    x_new, s, pp = _mm(
        f"ple_fwd_{tag}", (t // tm, N_CHIPS, N_CHIPS),
        [(hp, (tm, d4), _ik, f_b, gblk, gmap, "nn", 0, False),
         (p_i, (tm, pdim), _i0, f_d, (None, pdim, d4), lambda i, j, k: (j, o_proj // pdim, 0), "nn", 1, True)],
        [(x, (tm, d4), _ij)], [((t, d), F32, (tm, d4), _ij), ((t, d), BF16, (tm, d4), _ij), ((t, d), BF16, (tm, d4), _ij)],
        ep, [(tm, d4)] * 2)
    return x_new, (x, hp, s, pp)


def _ple_bwd(dx_out, saved, p_i, gn, f_b, o_gate, tag):
    x, hp, s, pp = saved
    t, d = x.shape
    pdim = p_i.shape[1]
    d4 = d // N_CHIPS
    tm = _tm(t)

    def body(dx_ref, s_ref, pp_ref, dpp_ref, dgp_ref):
        dxv = dx_ref[...]
        sv = s_ref[...].astype(F32)
        ppv = pp_ref[...].astype(F32)
        dpp_ref[...] = (dxv * sv).astype(BF16)
        dgp_ref[...] = (dxv * ppv * sv * (1.0 - sv)).astype(BF16)

    dpp, dgp = _rows(f"ple_ew_{tag}", t // tm, [_row(dx_out, tm), _row(s, tm), _row(pp, tm)],
                     [_row_out((t, d), BF16, tm)] * 2, body)
    tk = min(t, 512)
    dwp = _mm(
        f"ple_dwp_{tag}", (1, N_CHIPS, t // tk),
        [(p_i, (tk, pdim), lambda i, j, k: (k, 0), dpp, (tk, d4), _kj, "tn", 0, False)],
        [], [((N_CHIPS, pdim, d4), BF16, (None, pdim, d4), lambda i, j: (j, 0, 0))], lambda accs, ex: accs, [(pdim, d4)])[0]
    tn = min(d, 1024)
    dwg = _mm(
        f"ple_dwg_{tag}", (N_CHIPS, d // tn, t // tk),
        [(hp, (tk, d4), _ki, dgp, (tk, tn), _kj, "tn", 0, False)],
        [], [((N_CHIPS, d4, d), BF16, (None, d4, tn), lambda i, j: (i, 0, j))], lambda accs, ex: accs, [(d4, tn)])[0]
    gblk, gmap = _row_nt(o_gate, d4, d, d4)
    dhp = _mm(
        f"ple_dh_{tag}", (t // tm, N_CHIPS, 1),
        [(dgp, (tm, d), _ik, f_b, gblk, gmap, "nt", 0, False)],
        [], [((t, d), F32, (tm, d4), _ij)], lambda accs, ex: accs, [(tm, d4)], order="ji")[0]
    dx, dgn = _rms_bwd(dhp, x, gn, dx_out, f"ple_{tag}")
    return dx, dgn, dwg, dwp


_SQRT_HALF = 0.7071067811865476
_INV_SQRT_2PI = 0.3989422804014327


def _gelu(z):
    return z * (lax.erf(z * _SQRT_HALF) + 1.0) * 0.5


def _gelu_grad(z):
    return 0.5 * (1.0 + lax.erf(z * _SQRT_HALF)) + z * (_INV_SQRT_2PI * jnp.exp(-0.5 * z * z))


def _causal_bf16(w):
    c = w.shape[0]
    keep = lax.broadcasted_iota(jnp.int32, (c, c), 0) >= lax.broadcasted_iota(jnp.int32, (c, c), 1)
    return jnp.where(keep, w, 0.0).astype(BF16), keep


def _gmlp_gate_fwd(z_pre, ln_g, ln_b, w_s, b_st, tag):
    t, w2 = z_pre.shape
    gw = w2 // 2
    n_g, chunk, _ = w_s.shape
    gd = gw // n_g

    def body(z_ref, g_ref, b_ref, ws_ref, bs_ref, o_ref):
        z = z_ref[...]
        u = _gelu(z[:, :gw])
        zv = _gelu(z[:, gw:])
        mu = jnp.mean(zv, axis=-1, keepdims=True)
        cen = zv - mu
        rstd = lax.rsqrt(jnp.mean(cen * cen, axis=-1, keepdims=True) + LN_EPS)
        vln = (cen * rstd * g_ref[...] + b_ref[...]).astype(BF16)
        bst = bs_ref[...]
        for g in range(n_g):
            wm, _ = _causal_bf16(ws_ref[g])
            sl = slice(g * gd, (g + 1) * gd)
            s = jnp.dot(wm, vln[:, sl], preferred_element_type=F32) + bst[:, g:g + 1]
            o_ref[:, sl] = (u[:, sl] * s).astype(BF16)

    return _rows(f"gmlp_gate_{tag}", t // chunk, [_row(z_pre, chunk), _full(ln_g), _full(ln_b), _full(w_s), _full(b_st)],
                 [_row_out((t, gw), BF16, chunk)], body)[0]


def _gmlp_gate_bwd(z_pre, dgated, ln_g, ln_b, w_s, b_st, tag):
    t, w2 = z_pre.shape
    gw = w2 // 2
    n_g, chunk, _ = w_s.shape
    gd = gw // n_g

    def body(z_ref, dgt_ref, g_ref, b_ref, ws_ref, bs_ref, dz_ref, dws_ref, dbs_ref, dlg_ref, dlb_ref, dv_scr):
        _first(pl.program_id(0), [dws_ref, dbs_ref, dlg_ref, dlb_ref])
        z = z_ref[...]
        zu, zvp = z[:, :gw], z[:, gw:]
        u = _gelu(zu)
        zv = _gelu(zvp)
        mu = jnp.mean(zv, axis=-1, keepdims=True)
        cen = zv - mu
        rstd = lax.rsqrt(jnp.mean(cen * cen, axis=-1, keepdims=True) + LN_EPS)
        vn = cen * rstd
        lg = g_ref[...]
        vln = (vn * lg + b_ref[...]).astype(BF16)
        bst = bs_ref[...]
        lane = lax.broadcasted_iota(jnp.int32, (chunk, LANES), 1)
        dbs = jnp.zeros((chunk, LANES), F32)
        for g in range(n_g):
            wm, keep = _causal_bf16(ws_ref[g])
            sl = slice(g * gd, (g + 1) * gd)
            vg = vln[:, sl]
            s = jnp.dot(wm, vg, preferred_element_type=F32) + bst[:, g:g + 1]
            dgt = dgt_ref[:, sl].astype(F32)
            ds = dgt * u[:, sl]
            ds16 = ds.astype(BF16)
            dz_ref[:, sl] = (dgt * s * _gelu_grad(zu[:, sl])).astype(dz_ref.dtype)
            dv_scr[:, sl] = lax.dot_general(wm, ds16, _DIMS["tn"], preferred_element_type=F32)
            dw = lax.dot_general(ds16, vg, _DIMS["nt"], preferred_element_type=F32)
            dws_ref[g] += jnp.where(keep, dw, 0.0)
            dbs = dbs + jnp.where(lane == g, jnp.sum(ds, axis=-1, keepdims=True), 0.0)
        dbs_ref[...] += dbs
        dvln = dv_scr[...]
        dlg_ref[...] += jnp.sum(dvln * vn, axis=0, keepdims=True)
        dlb_ref[...] += jnp.sum(dvln, axis=0, keepdims=True)
        dvn = dvln * lg
        dzv = rstd * (dvn - jnp.mean(dvn, axis=-1, keepdims=True) - vn * jnp.mean(dvn * vn, axis=-1, keepdims=True))
        dz_ref[:, gw:] = (dzv * _gelu_grad(zvp)).astype(dz_ref.dtype)

    return _rows(
        f"gmlp_gate_bwd_{tag}", t // chunk,
        [_row(z_pre, chunk), _row(dgated, chunk), _full(ln_g), _full(ln_b), _full(w_s), _full(b_st)],
        [_row_out((t, w2), BF16, chunk), _acc_out((n_g, chunk, chunk)), _acc_out((chunk, LANES)), _acc_out((1, gw)),
         _acc_out((1, gw))], body, scratch=[pltpu.VMEM((chunk, gw), F32)])


def _gmlp_fwd(x, gn, ln_g, ln_b, w_s, b_st, f_b, o_in, o_out, tag):
    t, d = x.shape
    gw = ln_g.shape[1]
    tm = _tm(t)
    d4 = d // N_CHIPS
    h = _rms_fwd(x, gn, f"mix_{tag}")
    tn = min(d, 1024)
    iblk, imap = _col_nn(o_in, d4, tn, d)
    z_pre = _mm(
        f"gmlp_in_{tag}", (t // tm, N_CHIPS * d // tn, N_CHIPS),
        [(h, (tm, d4), _ik, f_b, iblk, imap, "nn", 0, False)],
        [], [((t, 2 * gw), F32, (tm, tn), _ij)], lambda accs, ex: accs, [(tm, tn)])[0]
    gated = _gmlp_gate_fwd(z_pre, ln_g, ln_b, w_s, b_st, tag)
    rc = gw // N_CHIPS
    tk = min(rc, 512)
    oblk, omap = _row_nn(o_out, tk, tn, rc)
    x_new = _mm(
        f"gmlp_out_{tag}", (t // tm, d // tn, gw // tk),
        [(gated, (tm, tk), _ik, f_b, oblk, omap, "nn", 0, False)],
        [(x, (tm, tn), _ij)], [((t, d), F32, (tm, tn), _ij)], lambda accs, ex: [ex[0] + accs[0]], [(tm, tn)])[0]
    return x_new, (x, h, z_pre, gated)


def _gmlp_bwd(dx_out, saved, gn, ln_g, ln_b, w_s, b_st, f_b, o_in, o_out, tag):
    x, h, z_pre, gated = saved
    t, d = x.shape
    gw = ln_g.shape[1]
    tm = _tm(t)
    d4 = d // N_CHIPS
    rc = gw // N_CHIPS
    tnr = min(rc, 512)
    oblk, omap = _row_nt(o_out, tnr, d, rc)
    dgated = _mm(
        f"gmlp_dgated_{tag}", (t // tm, gw // tnr, 1),
        [(dx_out, (tm, d), _ik, f_b, oblk, omap, "nt", 0, False)],
        [], [((t, gw), BF16, (tm, tnr), _ij)], lambda accs, ex: accs, [(tm, tnr)], order="ji")[0]
    tk = min(t, 512)
    tn = min(d, 1024)
    rblk, rmap = _out_row(tnr, tn, rc)
    dw_out = _mm(
        f"gmlp_dwout_{tag}", (gw // tnr, d // tn, t // tk),
        [(gated, (tk, tnr), _ki, dx_out, (tk, tn), _kj, "tn", 0, False)],
        [], [((N_CHIPS, rc, d), BF16, rblk, rmap)], lambda accs, ex: accs, [(tnr, tn)])[0]
    dz, dws, dbs, dlg, dlb = _gmlp_gate_bwd(z_pre, dgated, ln_g, ln_b, w_s, b_st, tag)
    tmd = min(d, 512)
    cblk, cmap = _out_col(tmd, tn, d)
    dw_in = _mm(
        f"gmlp_dwin_{tag}", (d // tmd, N_CHIPS * d // tn, t // tk),
        [(h, (tk, tmd), _ki, dz, (tk, tn), _kj, "tn", 0, False)],
        [], [((N_CHIPS, d, d), BF16, cblk, cmap)], lambda accs, ex: accs, [(tmd, tn)])[0]
    iblk, imap = _col_nt(o_in, d4, tn, d)
    dh = _mm(
        f"gmlp_dh_{tag}", (t // tm, N_CHIPS, N_CHIPS * d // tn),
        [(dz, (tm, tn), _ik, f_b, iblk, imap, "nt", 0, False)],
        [], [((t, d), F32, (tm, d4), _ij)], lambda accs, ex: accs, [(tm, d4)])[0]
    dx, dgn = _rms_bwd(dh, x, gn, dx_out, f"mix_{tag}")
    n_g = w_s.shape[0]
    return dx, dgn, dw_in, dw_out, dws, dbs[:, :n_g].T, dlg, dlb


def _rope_tables(t, width, n_rot_heads):
    half = ROPE_DIM // 2
    inv_freq = ROPE_THETA ** (-jnp.arange(0, ROPE_DIM, 2, dtype=F32) / ROPE_DIM)
    ang = jnp.arange(t, dtype=F32)[:, None] * inv_freq[None, :]
    cos, sin = jnp.cos(ang), jnp.sin(ang)
    ones = jnp.ones((t, HEAD_DIM - ROPE_DIM), F32)
    zeros = jnp.zeros((t, HEAD_DIM - ROPE_DIM), F32)
    zh = jnp.zeros((t, half), F32)
    c = jnp.concatenate([cos, cos, ones], axis=1)
    s_next = jnp.concatenate([-sin, zh, zeros], axis=1)
    s_prev = jnp.concatenate([zh, sin, zeros], axis=1)
    rest = width - n_rot_heads * HEAD_DIM

    def widen(tab, fill):
        parts = [jnp.tile(tab, (1, n_rot_heads))]
        if rest:
            parts.append(jnp.full((t, rest), fill, F32))
        return jnp.concatenate(parts, axis=1)

    return widen(c, 1.0), widen(s_next, 0.0), widen(s_prev, 0.0)


def _rope(v, c, s_next, s_prev, sign):
    w = v.shape[1]
    half = ROPE_DIM // 2
    reps = w // c.shape[1]
    if reps > 1:
        c, s_next, s_prev = (jnp.tile(tab, (1, reps)) for tab in (c, s_next, s_prev))
    return v * c + sign * (pltpu.roll(v, w - half, 1) * s_next + pltpu.roll(v, half, 1) * s_prev)


def _scores_mask(n, blk):
    q_pos = lax.broadcasted_iota(jnp.int32, (blk, 2 * blk), 0) + blk
    k_pos = lax.broadcasted_iota(jnp.int32, (blk, 2 * blk), 1)
    diff = q_pos - k_pos
    band = (diff >= 0) & (diff < blk)
    return band & ((k_pos >= blk) | (n > 0))


def _attn_fwd(q, kv, sinks, blk):
    t, qw = q.shape
    kw = kv.shape[1] // 2
    n_kv = kw // HEAD_DIM
    q_per_kv = qw // kw
    scale = HEAD_DIM ** -0.5

    def body(sink_ref, q_ref, kvo_ref, kvp_ref, o_ref):
        n = pl.program_id(0)
        valid = _scores_mask(n, blk)
        qv = q_ref[...]
        kvo = kvo_ref[...]
        kvp = kvp_ref[...]
        for kh in range(n_kv):
            ks = slice(kh * HEAD_DIM, (kh + 1) * HEAD_DIM)
            vs = slice(kw + kh * HEAD_DIM, kw + (kh + 1) * HEAD_DIM)
            kb = jnp.concatenate([kvp[:, ks], kvo[:, ks]], axis=0)
            vb = jnp.concatenate([kvp[:, vs], kvo[:, vs]], axis=0)
            for g in range(q_per_kv):
                hd = kh * q_per_kv + g
                hs = slice(hd * HEAD_DIM, (hd + 1) * HEAD_DIM)
                sink = sink_ref[0, hd]
                s = lax.dot_general(qv[:, hs], kb, _DIMS["nt"], preferred_element_type=F32) * scale
                s = jnp.where(valid, s, -1e30)
                m = jnp.maximum(jnp.max(s, axis=-1, keepdims=True), sink)
                e = jnp.where(valid, jnp.exp(s - m), 0.0)
                denom = jnp.sum(e, axis=-1, keepdims=True) + jnp.exp(sink - m)
                p = (e / denom).astype(BF16)
                o_ref[:, hs] = jnp.dot(p, vb, preferred_element_type=F32).astype(BF16)

    return _rows(
        "swa_attn", t // blk,
        [(sinks, pl.BlockSpec(memory_space=pltpu.SMEM)), _row(q, blk), _row(kv, blk),
         (kv, (blk, kv.shape[1]), lambda i: (jnp.maximum(i - 1, 0), 0))],
        [_row_out((t, qw), BF16, blk)], body)[0]


def _attn_bwd(q, kv, do, sinks, blk):
    t, qw = q.shape
    kw = kv.shape[1] // 2
    n_kv = kw // HEAD_DIM
    q_per_kv = qw // kw
    scale = HEAD_DIM ** -0.5

    def body(sink_ref, q_ref, kvo_ref, kvp_ref, do_ref, dq_ref, dkvo_ref, dkvp_ref, dsink_ref):
        n = pl.program_id(0)
        _first(n, [dsink_ref])
        valid = _scores_mask(n, blk)
        lane = lax.broadcasted_iota(jnp.int32, (1, LANES), 1)
        qv = q_ref[...]
        kvo = kvo_ref[...]
        kvp = kvp_ref[...]
        dov = do_ref[...]
        dsink = jnp.zeros((1, LANES), F32)
        for kh in range(n_kv):
            ks = slice(kh * HEAD_DIM, (kh + 1) * HEAD_DIM)
            vs = slice(kw + kh * HEAD_DIM, kw + (kh + 1) * HEAD_DIM)
            kb = jnp.concatenate([kvp[:, ks], kvo[:, ks]], axis=0)
            vb = jnp.concatenate([kvp[:, vs], kvo[:, vs]], axis=0)
            dkb = jnp.zeros((2 * blk, HEAD_DIM), F32)
            dvb = jnp.zeros((2 * blk, HEAD_DIM), F32)
            for g in range(q_per_kv):
                hd = kh * q_per_kv + g
                hs = slice(hd * HEAD_DIM, (hd + 1) * HEAD_DIM)
                sink = sink_ref[0, hd]
                qh = qv[:, hs]
                doh = dov[:, hs]
                s = lax.dot_general(qh, kb, _DIMS["nt"], preferred_element_type=F32) * scale
                s = jnp.where(valid, s, -1e30)
                m = jnp.maximum(jnp.max(s, axis=-1, keepdims=True), sink)
                e = jnp.where(valid, jnp.exp(s - m), 0.0)
                e_sink = jnp.exp(sink - m)
                inv = 1.0 / (jnp.sum(e, axis=-1, keepdims=True) + e_sink)
                p = e * inv
                p16 = p.astype(BF16)
                dp = lax.dot_general(doh, vb, _DIMS["nt"], preferred_element_type=F32)
                dot_pd = jnp.sum(p * dp, axis=-1, keepdims=True)
                ds16 = (p * (dp - dot_pd)).astype(BF16)
                dsink = dsink + jnp.where(lane == hd, -jnp.sum(e_sink * inv * dot_pd, axis=0, keepdims=True), 0.0)
                dq_ref[:, hs] = (jnp.dot(ds16, kb, preferred_element_type=F32) * scale).astype(BF16)
                dkb = dkb + lax.dot_general(ds16, qh, _DIMS["tn"], preferred_element_type=F32) * scale
                dvb = dvb + lax.dot_general(p16, doh, _DIMS["tn"], preferred_element_type=F32)
            dkvp_ref[:, ks] = dkb[:blk]
            dkvo_ref[:, ks] = dkb[blk:]
            dkvp_ref[:, vs] = dvb[:blk]
            dkvo_ref[:, vs] = dvb[blk:]
        dsink_ref[...] += dsink

    return _rows(
        "swa_attn_bwd", t // blk,
        [(sinks, pl.BlockSpec(memory_space=pltpu.SMEM)), _row(q, blk), _row(kv, blk),
         (kv, (blk, kv.shape[1]), lambda i: (jnp.maximum(i - 1, 0), 0)), _row(do, blk)],
        [_row_out((t, qw), BF16, blk), _row_out((t, 2 * kw), F32, blk), _row_out((t, 2 * kw), F32, blk),
         _acc_out((1, LANES))], body)


def _rope_bwd(dq_r, dkv_own, dkv_prev, tabs_q, tabs_kv, blk):
    t, qw = dq_r.shape
    kvw = dkv_own.shape[1]
    nb = t // blk

    def body(dq_ref, own_ref, nxt_ref, cq, snq, spq, ck, snk, spk, dqo_ref, dkvo_ref, dbq_ref, dbkv_ref):
        i = pl.program_id(0)
        _first(i, [dbq_ref, dbkv_ref])
        dq = _rope(dq_ref[...].astype(F32), cq[...], snq[...], spq[...], -1.0)
        dkv = own_ref[...] + jnp.where(i < nb - 1, nxt_ref[...], 0.0)
        dkv = _rope(dkv, ck[...], snk[...], spk[...], -1.0)
        dqo_ref[...] = dq.astype(BF16)
        dkvo_ref[...] = dkv.astype(BF16)
        dbq_ref[...] += jnp.sum(dq, axis=0, keepdims=True)
        dbkv_ref[...] += jnp.sum(dkv, axis=0, keepdims=True)

    return _rows(
        "swa_rope_bwd", nb,
        [_row(dq_r, blk), _row(dkv_own, blk), (dkv_prev, (blk, kvw), lambda i: (jnp.minimum(i + 1, nb - 1), 0))]
        + [_row(tab, blk) for tab in tabs_q] + [_row(tab, blk) for tab in tabs_kv],
        [_row_out((t, qw), BF16, blk), _row_out((t, kvw), BF16, blk), _acc_out((1, qw)), _acc_out((1, kvw))], body)


def _swa_fwd(x, gn, bq, bkv, bo, sinks, f_b, f_d, o_q, o_o, o_kv, blk, tag):
    t, d = x.shape
    qw = bq.shape[1]
    kvw = bkv.shape[1]
    d4 = d // N_CHIPS
    tm = _tm(t)
    h = _rms_fwd(x, gn, f"mix_{tag}")
    tabs_q = _rope_tables(t, LANES, LANES // HEAD_DIM)
    tabs_kv = _rope_tables(t, kvw, kvw // 2 // HEAD_DIM)
    tn = min(qw, 1024)
    qblk, qmap = _row_nn(o_q, d4, tn, d4)

    def ep_rope(accs, ex):
        return [_rope(accs[0] + ex[0], ex[1], ex[2], ex[3], 1.0)]

    q = _mm(
        f"swa_q_{tag}", (t // tm, qw // tn, N_CHIPS),
        [(h, (tm, d4), _ik, f_b, qblk, qmap, "nn", 0, False)],
        [(bq, (1, tn), _0j)] + [(tab, (tm, LANES), _i0e) for tab in tabs_q],
        [((t, qw), BF16, (tm, tn), _ij)], ep_rope, [(tm, tn)])[0]
    assert o_kv % d4 == 0
    kv = _mm(
        f"swa_kv_{tag}", (t // tm, 1, N_CHIPS),
        [(h, (tm, d4), _ik, f_d, (None, d4, kvw), lambda i, j, k: (k, o_kv // d4, 0), "nn", 0, False)],
        [(bkv, (1, kvw), _0j)] + [(tab, (tm, kvw), _i0e) for tab in tabs_kv],
        [((t, kvw), BF16, (tm, kvw), _ij)], ep_rope, [(tm, kvw)])[0]
    o = _attn_fwd(q, kv, sinks, blk)
    tno = min(d, 1024)
    oblk, omap = _row_nn(o_o, d4, tno, d4)
    x_new = _mm(
        f"swa_out_{tag}", (t // tm, d // tno, N_CHIPS),
        [(o, (tm, d4), _ik, f_b, oblk, omap, "nn", 0, False)],
        [(x, (tm, tno), _ij), (bo, (1, tno), _0j)], [((t, d), F32, (tm, tno), _ij)],
        lambda accs, ex: [ex[0] + accs[0] + ex[1]], [(tm, tno)])[0]
    return x_new, (x, h, q, kv, o, tabs_q, tabs_kv)


def _swa_bwd(dx_out, saved, gn, sinks, f_b, f_d, o_q, o_o, o_kv, blk, tag):
    x, h, q, kv, o, tabs_q, tabs_kv = saved
    t, d = x.shape
    qw = q.shape[1]
    kvw = kv.shape[1]
    d4 = d // N_CHIPS
    qw4 = qw // N_CHIPS
    tm = _tm(t)
    oblk, omap = _row_nt(o_o, qw4, d, qw4)
    do = _mm(
        f"swa_do_{tag}", (t // tm, N_CHIPS, 1),
        [(dx_out, (tm, d), _ik, f_b, oblk, omap, "nt", 0, False)],
        [], [((t, qw), BF16, (tm, qw4), _ij)], lambda accs, ex: accs, [(tm, qw4)], order="ji")[0]
    tk = min(t, 512)
    tn = min(d, 1024)
    dwo = _mm(
        f"swa_dwo_{tag}", (N_CHIPS, d // tn, t // tk),
        [(o, (tk, qw4), _ki, dx_out, (tk, tn), _kj, "tn", 0, False)],
        [], [((N_CHIPS, qw4, d), BF16, (None, qw4, tn), lambda i, j: (i, 0, j))], lambda accs, ex: accs, [(qw4, tn)])[0]
    dbo = _colsum(dx_out, f"bo_{tag}")
    dq_r, dkv_own, dkv_prev, dsink = _attn_bwd(q, kv, do, sinks, blk)
    dq, dkv, dbq, dbkv = _rope_bwd(dq_r, dkv_own, dkv_prev, tabs_q, tabs_kv, blk)
    tnq = min(qw, 1024)
    dwq = _mm(
        f"swa_dwq_{tag}", (N_CHIPS, qw // tnq, t // tk),
        [(h, (tk, d4), _ki, dq, (tk, tnq), _kj, "tn", 0, False)],
        [], [((N_CHIPS, d4, qw), BF16, (None, d4, tnq), lambda i, j: (i, 0, j))], lambda accs, ex: accs, [(d4, tnq)])[0]
    dwkv = _mm(
        f"swa_dwkv_{tag}", (N_CHIPS, 1, t // tk),
        [(h, (tk, d4), _ki, dkv, (tk, kvw), _kj, "tn", 0, False)],
        [], [((N_CHIPS, d4, kvw), BF16, (None, d4, kvw), lambda i, j: (i, 0, j))], lambda accs, ex: accs, [(d4, kvw)])[0]
    qblk, qmap = _row_nt(o_q, d4, qw, d4)
    dh = _mm(
        f"swa_dh_{tag}", (t // tm, N_CHIPS, 1),
        [(dq, (tm, qw), _ik, f_b, qblk, qmap, "nt", 0, False),
         (dkv, (tm, kvw), _ik, f_d, (None, d4, kvw), lambda i, j, k: (j, o_kv // d4, 0), "nt", 0, False)],
        [], [((t, d), F32, (tm, d4), _ij)], lambda accs, ex: accs, [(tm, d4)], order="ji")[0]
    dx, dgn = _rms_bwd(dh, x, gn, dx_out, f"mix_{tag}")
    n_heads = qw // HEAD_DIM
    return dx, dgn, dwq, dwkv, dwo, dbq, dbkv, dbo, dsink[:, :n_heads]


_HBM = pl.BlockSpec(memory_space=pl.ANY)
_CHIP_FLIPS = ((1, 0), (0, 1), (1, 1))


def _place():
    x, y, c = lax.axis_index("x"), lax.axis_index("y"), lax.axis_index("c")
    return x, y, c


def _flip(v, bit):
    return 1 - v if bit else v


def _all_gather_small(v, tag):
    r, cdim = v.shape

    def body(v_ref, out_ref, send_sems, recv_sems, local_sem):
        x, y, c = _place()
        me = 4 * x + 2 * y + c
        mine = pltpu.make_async_copy(v_ref, out_ref.at[me], local_sem)
        mine.start()
        copies = []
        for dlt in range(1, N_DEV):
            peer = (_flip(x, dlt & 4), _flip(y, dlt & 2), _flip(c, dlt & 1))
            copies.append(pltpu.make_async_remote_copy(
                src_ref=v_ref, dst_ref=out_ref.at[me], send_sem=send_sems.at[dlt - 1], recv_sem=recv_sems.at[dlt - 1],
                device_id=peer, device_id_type=MESH_ID))
        for cp in copies:
            cp.start()
        for cp in copies:
            cp.wait()
        mine.wait()

    return pl.pallas_call(
        body, name=f"all_gather_small_{tag}", out_shape=jax.ShapeDtypeStruct((N_DEV, r, cdim), v.dtype),
        in_specs=[_HBM], out_specs=_HBM,
        scratch_shapes=[pltpu.SemaphoreType.DMA((N_DEV - 1,)), pltpu.SemaphoreType.DMA((N_DEV - 1,)), pltpu.SemaphoreType.DMA],
    )(v)


def _gather_slabs(slabs):
    n = len(slabs)

    def body(*refs):
        srcs, dsts = refs[:n], refs[n:2 * n]
        ici_send, ici_recv, d2d_send, d2d_recv, local_sems = refs[2 * n:]
        x, y, c = _place()
        chip = 2 * x + y
        local = []
        for s in range(n):
            cp = pltpu.make_async_copy(srcs[s], dsts[s].at[chip], local_sems.at[s])
            cp.start()
            local.append(cp)
        sends = []
        for s in range(n):
            half = srcs[s].shape[0] // 2
            rows = pl.ds(c * half, half)
            for j, (fx, fy) in enumerate(_CHIP_FLIPS):
                cp = pltpu.make_async_remote_copy(
                    src_ref=srcs[s].at[rows], dst_ref=dsts[s].at[chip, rows],
                    send_sem=ici_send.at[s * 3 + j], recv_sem=ici_recv.at[s * 3 + j],
                    device_id=(_flip(x, fx), _flip(y, fy), c), device_id_type=MESH_ID)
                cp.start()
                sends.append(cp)
        forwards = []
        for s in range(n):
            half = srcs[s].shape[0] // 2
            rows = pl.ds(c * half, half)
            for j, (fx, fy) in enumerate(_CHIP_FLIPS):
                landed = dsts[s].at[2 * _flip(x, fx) + _flip(y, fy), rows]
                pltpu.make_async_remote_copy(
                    src_ref=landed, dst_ref=landed, send_sem=ici_send.at[s * 3 + j], recv_sem=ici_recv.at[s * 3 + j],
                    device_id=(x, y, c), device_id_type=MESH_ID).wait_recv()
                fw = pltpu.make_async_remote_copy(
                    src_ref=landed, dst_ref=landed, send_sem=d2d_send.at[s * 3 + j], recv_sem=d2d_recv.at[s * 3 + j],
                    device_id=(x, y, 1 - c), device_id_type=MESH_ID)
                fw.start()
                forwards.append(fw)
        for s in range(n):
            half = srcs[s].shape[0] // 2
            other = pl.ds((1 - c) * half, half)
            for j, (fx, fy) in enumerate(_CHIP_FLIPS):
                arriving = dsts[s].at[2 * _flip(x, fx) + _flip(y, fy), other]
                pltpu.make_async_remote_copy(
                    src_ref=arriving, dst_ref=arriving, send_sem=d2d_send.at[s * 3 + j], recv_sem=d2d_recv.at[s * 3 + j],
                    device_id=(x, y, c), device_id_type=MESH_ID).wait_recv()
        for cp in sends + forwards:
            cp.wait_send()
        for cp in local:
            cp.wait()

    return pl.pallas_call(
        body, name="gather_slabs",
        out_shape=[jax.ShapeDtypeStruct((N_CHIPS,) + s.shape, s.dtype) for s in slabs],
        in_specs=[_HBM] * n, out_specs=[_HBM] * n,
        scratch_shapes=[pltpu.SemaphoreType.DMA((3 * n,))] * 4 + [pltpu.SemaphoreType.DMA((n,))],
    )(*slabs)


def _swap_halves(grads):
    n = len(grads)

    def body(*refs):
        srcs, dsts = refs[:n], refs[n:2 * n]
        send_sems, recv_sems = refs[2 * n:]
        x, y, c = _place()
        copies = []
        for s in range(n):
            half = srcs[s].shape[1] // 2
            cp = pltpu.make_async_remote_copy(
                src_ref=srcs[s].at[:, pl.ds((1 - c) * half, half)], dst_ref=dsts[s],
                send_sem=send_sems.at[s], recv_sem=recv_sems.at[s], device_id=(x, y, 1 - c), device_id_type=MESH_ID)
            cp.start()
            copies.append(cp)
        for cp in copies:
            cp.wait()

    return pl.pallas_call(
        body, name="swap_halves",
        out_shape=[jax.ShapeDtypeStruct((N_CHIPS, g.shape[1] // 2, g.shape[2]), g.dtype) for g in grads],
        in_specs=[_HBM] * n, out_specs=[_HBM] * n,
        scratch_shapes=[pltpu.SemaphoreType.DMA((n,)), pltpu.SemaphoreType.DMA((n,))],
    )(*grads)


def _add_halves(g, r, c_idx, tag):
    _, rows, w = g.shape
    half = rows // 2
    tr = 256 if half % 256 == 0 else half
    nb = half // tr

    def body(c_ref, g_ref, r_ref, o_ref):
        o_ref[...] = (g_ref[...].astype(F32) + r_ref[...].astype(F32)).astype(BF16)

    return pl.pallas_call(
        body, name=f"add_halves_{tag}",
        grid_spec=pltpu.PrefetchScalarGridSpec(
            num_scalar_prefetch=1, grid=(N_CHIPS, nb),
            in_specs=[pl.BlockSpec((None, tr, w), lambda k, i, c: (k, c[0] * nb + i, 0)),
                      pl.BlockSpec((None, tr, w), lambda k, i, c: (k, i, 0))],
            out_specs=pl.BlockSpec((None, tr, w), lambda k, i, c: (k, i, 0))),
        out_shape=jax.ShapeDtypeStruct((N_CHIPS, half, w), BF16), compiler_params=_params(2),
    )(c_idx, g, r)


def _send_partials(parts):
    n = len(parts)

    def body(*refs):
        srcs, dsts = refs[:n], refs[n:2 * n]
        send_sems, recv_sems = refs[2 * n:]
        x, y, c = _place()
        copies = []
        for s in range(n):
            for j, (fx, fy) in enumerate(_CHIP_FLIPS):
                px, py = _flip(x, fx), _flip(y, fy)
                cp = pltpu.make_async_remote_copy(
                    src_ref=srcs[s].at[2 * px + py], dst_ref=dsts[s].at[j],
                    send_sem=send_sems.at[s * 3 + j], recv_sem=recv_sems.at[s * 3 + j],
                    device_id=(px, py, c), device_id_type=MESH_ID)
                cp.start()
                copies.append(cp)
        for cp in copies:
            cp.wait()

    return pl.pallas_call(
        body, name="send_partials",
        out_shape=[jax.ShapeDtypeStruct((3,) + p.shape[1:], p.dtype) for p in parts],
        in_specs=[_HBM] * n, out_specs=[_HBM] * n,
        scratch_shapes=[pltpu.SemaphoreType.DMA((3 * n,)), pltpu.SemaphoreType.DMA((3 * n,))],
    )(*parts)


def _sum_partials(p, q, chip_idx, tag):
    _, half, w = p.shape
    tr = 256 if half % 256 == 0 else half
    nb = half // tr

    def body(k_ref, p_ref, q_ref, o_ref):
        acc = p_ref[...].astype(F32)
        for j in range(3):
            acc = acc + q_ref[j].astype(F32)
        o_ref[...] = acc

    return pl.pallas_call(
        body, name=f"sum_partials_{tag}",
        grid_spec=pltpu.PrefetchScalarGridSpec(
            num_scalar_prefetch=1, grid=(nb,),
            in_specs=[pl.BlockSpec((None, tr, w), lambda i, k: (k[0], i, 0)),
                      pl.BlockSpec((3, tr, w), lambda i, k: (0, i, 0))],
            out_specs=pl.BlockSpec((tr, w), lambda i, k: (i, 0))),
        out_shape=jax.ShapeDtypeStruct((half, w), F32), compiler_params=_params(1),
    )(chip_idx, p, q)


def _join_halves(tots):
    n = len(tots)

    def body(*refs):
        srcs, dsts = refs[:n], refs[n:2 * n]
        send_sems, recv_sems, local_sems = refs[2 * n:]
        x, y, c = _place()
        copies, local = [], []
        for s in range(n):
            half = srcs[s].shape[0]
            mine = dsts[s].at[pl.ds(c * half, half)]
            lc = pltpu.make_async_copy(srcs[s], mine, local_sems.at[s])
            lc.start()
            local.append(lc)
            cp = pltpu.make_async_remote_copy(
                src_ref=srcs[s], dst_ref=mine, send_sem=send_sems.at[s], recv_sem=recv_sems.at[s],
                device_id=(x, y, 1 - c), device_id_type=MESH_ID)
            cp.start()
            copies.append(cp)
        for s in range(n):
            half = srcs[s].shape[0]
            theirs = dsts[s].at[pl.ds((1 - c) * half, half)]
            pltpu.make_async_remote_copy(
                src_ref=srcs[s], dst_ref=theirs, send_sem=send_sems.at[s], recv_sem=recv_sems.at[s],
                device_id=(x, y, c), device_id_type=MESH_ID).wait_recv()
        for cp in copies:
            cp.wait_send()
        for lc in local:
            lc.wait()

    return pl.pallas_call(
        body, name="join_halves",
        out_shape=[jax.ShapeDtypeStruct((2 * tt.shape[0], tt.shape[1]), tt.dtype) for tt in tots],
        in_specs=[_HBM] * n, out_specs=[_HBM] * n,
        scratch_shapes=[pltpu.SemaphoreType.DMA((n,)), pltpu.SemaphoreType.DMA((n,)), pltpu.SemaphoreType.DMA((n,))],
    )(*tots)


def _sum_devices(gathered):
    _, r, cdim = gathered.shape

    def body(g_ref, o_ref):
        acc = g_ref[0]
        for dv in range(1, N_DEV):
            acc = acc + g_ref[dv]
        o_ref[...] = acc

    return pl.pallas_call(body, name="sum_devices", out_shape=jax.ShapeDtypeStruct((r, cdim), F32),
                          compiler_params=pltpu.CompilerParams(vmem_limit_bytes=VMEM_LIMIT_BYTES))(gathered)


def _adamw_math(w, g, m, v):
    m = ADAM_B1 * m + (1.0 - ADAM_B1) * g
    v = ADAM_B2 * v + (1.0 - ADAM_B2) * jnp.square(g)
    m_hat = m / (1.0 - ADAM_B1 ** ADAM_STEP)
    v_hat = v / (1.0 - ADAM_B2 ** ADAM_STEP)
    delta = -ADAM_LR * (m_hat / (jnp.sqrt(v_hat) + ADAM_EPS) + ADAM_WD * w)
    return delta, m, v


def _adamw_big(g_slab, r0, col, w, m, v, tag):
    shape = w.shape
    wd = shape[-1]
    w2, m2, v2 = (a.reshape(-1, wd) for a in (w, m, v))
    rows = w2.shape[0]
    tr = 256 if rows % 256 == 0 else rows
    assert r0 % tr == 0

    def body(g_ref, w_ref, m_ref, v_ref, go_ref, d_ref, mo_ref, vo_ref):
        g = g_ref[...]
        delta, mn, vn = _adamw_math(w_ref[...], g, m_ref[...], v_ref[...])
        go_ref[...] = g
        d_ref[...] = delta
        mo_ref[...] = mn
        vo_ref[...] = vn

    row = lambda i: (i, 0)
    outs = pl.pallas_call(
        body, name=f"adamw_{tag}", grid=(rows // tr,),
        in_specs=[pl.BlockSpec((tr, wd), lambda i: (r0 // tr + i, col))] + [pl.BlockSpec((tr, wd), row)] * 3,
        out_specs=[pl.BlockSpec((tr, wd), row)] * 4, out_shape=[jax.ShapeDtypeStruct((rows, wd), F32)] * 4,
        compiler_params=_params(1),
    )(g_slab, w2, m2, v2)
    return tuple(o.reshape(shape) for o in outs)


def _adamw_small(items):
    n = len(items)

    def body(*refs):
        ins, outs = refs[:4 * n], refs[4 * n:]
        for k in range(n):
            g_ref, w_ref, m_ref, v_ref = ins[4 * k: 4 * k + 4]
            delta, mn, vn = _adamw_math(w_ref[...], g_ref[...], m_ref[...], v_ref[...])
            outs[3 * k][...] = delta
            outs[3 * k + 1][...] = mn
            outs[3 * k + 2][...] = vn

    flat = [a for it in items for a in it]
    out_shape = [jax.ShapeDtypeStruct(it[1].shape, F32) for it in items for _ in range(3)]
    res = pl.pallas_call(body, name="adamw_small", out_shape=out_shape,
                         compiler_params=pltpu.CompilerParams(vmem_limit_bytes=VMEM_LIMIT_BYTES))(*flat)
    return [tuple(res[3 * k: 3 * k + 3]) for k in range(n)]


def _pack(arrays, width):
    rows, layout, r = [], [], 0
    for a in arrays:
        flat = a.reshape(-1).astype(F32)
        nr = -(-flat.shape[0] // width)
        flat = jnp.pad(flat, (0, nr * width - flat.shape[0]))
        rows.append(flat.reshape(nr, width))
        layout.append((r, nr, a.shape))
        r += nr
    pad = (-r) % 8
    if pad:
        rows.append(jnp.zeros((pad, width), F32))
    return jnp.concatenate(rows, axis=0), layout


def _unpack(packed, layout):
    out = []
    for r, nr, shape in layout:
        size = 1
        for s in shape:
            size *= s
        out.append(packed[r:r + nr].reshape(-1)[:size].reshape(shape))
    return out


_WEIGHTS = ['ffn1_norm', 'ffn1_w1', 'ffn1_w3', 'ffn1_w2', 'mix_norm', 'ffn2_norm', 'ffn2_w1', 'ffn2_w3', 'ffn2_w2',
            'ple_norm', 'ple_w_gate', 'ple_w_proj', 'gmlp_w_in', 'gmlp_ln_g', 'gmlp_ln_b', 'gmlp_w_s', 'gmlp_b_s',
            'gmlp_w_out', 'swa_wq', 'swa_bq', 'swa_wk', 'swa_bk', 'swa_wv', 'swa_bv', 'swa_sinks', 'swa_wo', 'swa_bo',
            'final_norm']
_REPLICATED = ['ffn1_norm', 'mix_norm', 'ffn2_norm', 'ple_norm', 'gmlp_ln_g', 'gmlp_ln_b', 'gmlp_w_s', 'gmlp_b_s',
               'swa_sinks', 'final_norm']
_BIASES = ['swa_bq', 'swa_bk', 'swa_bv', 'swa_bo']


def _as2d(a):
    if a.ndim == 1:
        return a.reshape(1, -1)
    return a.reshape(-1, a.shape[-1])


def _local_step(x, p, target, small, f_a, f_b, f_d, full_bias):
    t, d = x.shape
    fc = f_a.shape[2]
    d4 = d // N_CHIPS
    pdim = p.shape[2]
    n_layers = 2
    blk = small['gmlp_w_s'].shape[2]
    o_w1 = lambda f, i: ((f * 2 + 0) * 2 + i) * d
    o_w3 = lambda f, i: ((f * 2 + 1) * 2 + i) * d
    o_w2 = lambda f, i: (f * 2 + i) * fc
    o_gate = lambda i: 4 * fc + i * d4
    o_in = 4 * fc + 2 * d4
    o_out = o_in + d
    o_q = o_out + 2 * d4
    o_o = o_q + d4
    o_proj = lambda i: i * pdim
    o_kv = 2 * pdim
    bq, bkv, bo = full_bias
    norm = lambda name, i: small[name][i:i + 1]
    ln_g, ln_b = small['gmlp_ln_g'], small['gmlp_ln_b']
    w_s = small['gmlp_w_s'][0]
    b_st = jnp.pad(small['gmlp_b_s'][0].T, ((0, 0), (0, LANES - small['gmlp_b_s'].shape[1])))
    sinks = small['swa_sinks']

    saved = []
    for i in range(n_layers):
        x, s_f1 = _ffn_fwd(x, norm('ffn1_norm', i), f_a, f_b, o_w1(0, i), o_w3(0, i), o_w2(0, i), f"f1l{i}")
        if i == 0:
            x, s_mix = _gmlp_fwd(x, norm('mix_norm', i), ln_g, ln_b, w_s, b_st, f_b, o_in, o_out, f"l{i}")
        else:
            x, s_mix = _swa_fwd(x, norm('mix_norm', i), bq, bkv, bo, sinks, f_b, f_d, o_q, o_o, o_kv, blk, f"l{i}")
        x, s_f2 = _ffn_fwd(x, norm('ffn2_norm', i), f_a, f_b, o_w1(1, i), o_w3(1, i), o_w2(1, i), f"f2l{i}")
        x, s_ple = _ple_fwd(x, p[i], norm('ple_norm', i), f_b, f_d, o_gate(i), o_proj(i), f"l{i}")
        saved.append((s_f1, s_mix, s_f2, s_ple))

    dx, d_final, loss = _loss_head(x, small['final_norm'].reshape(1, d), target)

    gn = {k: [None] * n_layers for k in ('ffn1_norm', 'mix_norm', 'ffn2_norm', 'ple_norm')}
    gw = {}
    for i in reversed(range(n_layers)):
        s_f1, s_mix, s_f2, s_ple = saved[i]
        dx, gn['ple_norm'][i], gw['gate', i], gw['proj', i] = _ple_bwd(
            dx, s_ple, p[i], norm('ple_norm', i), f_b, o_gate(i), f"l{i}")
        dx, gn['ffn2_norm'][i], gw['w1', 1, i], gw['w3', 1, i], gw['w2', 1, i] = _ffn_bwd(
            dx, s_f2, norm('ffn2_norm', i), f_a, f_b, o_w1(1, i), o_w3(1, i), o_w2(1, i), f"f2l{i}")
        if i == 0:
            dx, gn['mix_norm'][i], gw['in'], gw['out'], d_ws, d_bs, d_lg, d_lb = _gmlp_bwd(
                dx, s_mix, norm('mix_norm', i), ln_g, ln_b, w_s, b_st, f_b, o_in, o_out, f"l{i}")
        else:
            dx, gn['mix_norm'][i], gw['q'], gw['kv'], gw['o'], d_bq, d_bkv, d_bo, d_sink = _swa_bwd(
                dx, s_mix, norm('mix_norm', i), sinks, f_b, f_d, o_q, o_o, o_kv, blk, f"l{i}")
        dx, gn['ffn1_norm'][i], gw['w1', 0, i], gw['w3', 0, i], gw['w2', 0, i] = _ffn_bwd(
            dx, s_f1, norm('ffn1_norm', i), f_a, f_b, o_w1(0, i), o_w3(0, i), o_w2(0, i), f"f1l{i}")

    g_a = jnp.concatenate([gw[w, f, i] for f in range(2) for w in ('w1', 'w3') for i in range(n_layers)], axis=1)
    g_b = jnp.concatenate([gw['w2', f, i] for f in range(2) for i in range(n_layers)]
                          + [gw['gate', i] for i in range(n_layers)] + [gw['in'], gw['out'], gw['q'], gw['o']], axis=1)
    g_d = jnp.concatenate([gw['proj', i] for i in range(n_layers)] + [gw['kv']], axis=1)
    kw = d_bkv.shape[1] // 2
    g_small = {
        'ffn1_norm': jnp.concatenate(gn['ffn1_norm'], axis=0), 'mix_norm': jnp.concatenate(gn['mix_norm'], axis=0),
        'ffn2_norm': jnp.concatenate(gn['ffn2_norm'], axis=0), 'ple_norm': jnp.concatenate(gn['ple_norm'], axis=0),
        'gmlp_ln_g': d_lg, 'gmlp_ln_b': d_lb, 'gmlp_w_s': d_ws[None], 'gmlp_b_s': d_bs[None], 'swa_sinks': d_sink,
        'final_norm': d_final.reshape(d), 'swa_bq': d_bq, 'swa_bk': d_bkv[:, :kw], 'swa_bv': d_bkv[:, kw:], 'swa_bo': d_bo,
    }
    return loss, dx, (g_a, g_b, g_d), g_small


def _build_slabs(a):
    d = a['x'].shape[-1]
    fc = a['ffn1_w1'].shape[-1]
    bf = lambda w: w.astype(BF16)
    s_a = jnp.concatenate([bf(a[n]).reshape(-1, fc) for n in ('ffn1_w1', 'ffn1_w3', 'ffn2_w1', 'ffn2_w3')], axis=0)
    s_b = jnp.concatenate([bf(a[n]).reshape(-1, d) for n in
                           ('ffn1_w2', 'ffn2_w2', 'ple_w_gate', 'gmlp_w_in', 'gmlp_w_out', 'swa_wq', 'swa_wo')], axis=0)
    wkv = jnp.concatenate([bf(a['swa_wk'][0]), bf(a['swa_wv'][0])], axis=1)
    s_d = jnp.concatenate([bf(a['ple_w_proj']).reshape(-1, a['ple_w_proj'].shape[-1]), wkv], axis=0)
    return s_a, s_b, s_d


def _reduce_slabs(grads, c_idx, chip_idx):
    sib = _swap_halves(list(grads))
    parts = [_add_halves(g, r, c_idx, f"s{k}") for k, (g, r) in enumerate(zip(grads, sib))]
    recv = _send_partials(parts)
    tots = [_sum_partials(pp, q, chip_idx, f"s{k}") for k, (pp, q) in enumerate(zip(parts, recv))]
    return _join_halves(tots)


def kernel(x, p, ffn1_norm, ffn1_w1, ffn1_w3, ffn1_w2, mix_norm, ffn2_norm, ffn2_w1, ffn2_w3, ffn2_w2, ple_norm, ple_w_gate, ple_w_proj, gmlp_w_in, gmlp_ln_g, gmlp_ln_b, gmlp_w_s, gmlp_b_s, gmlp_w_out, swa_wq, swa_bq, swa_wk, swa_bk, swa_wv, swa_bv, swa_sinks, swa_wo, swa_bo, final_norm, loss_target, m_ffn1_norm, m_ffn1_w1, m_ffn1_w3, m_ffn1_w2, m_mix_norm, m_ffn2_norm, m_ffn2_w1, m_ffn2_w3, m_ffn2_w2, m_ple_norm, m_ple_w_gate, m_ple_w_proj, m_gmlp_w_in, m_gmlp_ln_g, m_gmlp_ln_b, m_gmlp_w_s, m_gmlp_b_s, m_gmlp_w_out, m_swa_wq, m_swa_bq, m_swa_wk, m_swa_bk, m_swa_wv, m_swa_bv, m_swa_sinks, m_swa_wo, m_swa_bo, m_final_norm, v_ffn1_norm, v_ffn1_w1, v_ffn1_w3, v_ffn1_w2, v_mix_norm, v_ffn2_norm, v_ffn2_w1, v_ffn2_w3, v_ffn2_w2, v_ple_norm, v_ple_w_gate, v_ple_w_proj, v_gmlp_w_in, v_gmlp_ln_g, v_gmlp_ln_b, v_gmlp_w_s, v_gmlp_b_s, v_gmlp_w_out, v_swa_wq, v_swa_bq, v_swa_wk, v_swa_bk, v_swa_wv, v_swa_bv, v_swa_sinks, v_swa_wo, v_swa_bo, v_final_norm):
    a = dict(locals())
    xi, yi, ci = _place()
    chip = 2 * xi + yi
    c_idx = ci.reshape(1).astype(jnp.int32)
    chip_idx = chip.reshape(1).astype(jnp.int32)
    d = x.shape[-1]
    fc = ffn1_w1.shape[-1]
    d4 = d // N_CHIPS
    pdim = p.shape[-1]
    kw4 = swa_bk.shape[-1]

    f_a, f_b, f_d = _gather_slabs(list(_build_slabs(a)))
    bias_pack, bias_layout = _pack([a[n] for n in _BIASES], 2 * d)
    bias_all = _all_gather_small(bias_pack, "bias")

    def full_bias(idx):
        return jnp.concatenate([_unpack(bias_all[2 * k], bias_layout)[idx] for k in range(N_CHIPS)], axis=1)

    bq, bk, bv, bo = (full_bias(i) for i in range(4))
    bkv = jnp.concatenate([bk, bv], axis=1)

    small = {n: a[n] for n in _REPLICATED}
    loss, grad_x, g_slabs, g_small = _local_step(x[0], p[:, 0], loss_target[0], small, f_a, f_b, f_d, (bq, bkv, bo))

    names = _REPLICATED + _BIASES
    packed, layout = _pack([g_small[n] for n in names] + [loss[:, :1]], 2 * d)
    summed = _unpack(_sum_devices(_all_gather_small(packed, "grads")), layout)
    g_sum = dict(zip(names, summed[:-1]))
    loss_out = summed[-1].reshape(())
    for n in _BIASES:
        width = a[n].shape[-1]
        g_sum[n] = lax.dynamic_slice_in_dim(g_sum[n], chip * width, width, axis=1)
    r_a, r_b, r_d = _reduce_slabs(g_slabs, c_idx, chip_idx)

    out = {}
    big = []
    for k, n in enumerate(('ffn1_w1', 'ffn1_w3', 'ffn2_w1', 'ffn2_w3')):
        big.append((n, r_a, k * 2 * d, 0))
    for k, n in enumerate(('ffn1_w2', 'ffn2_w2')):
        big.append((n, r_b, k * 2 * fc, 0))
    off = 4 * fc
    for n, rows in (('ple_w_gate', 2 * d4), ('gmlp_w_in', d), ('gmlp_w_out', gmlp_w_out.shape[1]),
                    ('swa_wq', d4), ('swa_wo', swa_wo.shape[1])):
        big.append((n, r_b, off, 0))
        off += rows
    big.append(('ple_w_proj', r_d, 0, 0))
    big.append(('swa_wk', r_d, 2 * pdim, 0))
    big.append(('swa_wv', r_d, 2 * pdim, 1))
    for n, slab, r0, col in big:
        out[n] = _adamw_big(slab, r0, col, a[n], a['m_' + n], a['v_' + n], n)
    small_names = _REPLICATED + _BIASES
    items = [(_as2d(g_sum[n]), _as2d(a[n]), _as2d(a['m_' + n]), _as2d(a['v_' + n])) for n in small_names]
    for n, (delta, mn, vn) in zip(small_names, _adamw_small(items)):
        shape = a[n].shape
        out[n] = (g_sum[n].reshape(shape), delta.reshape(shape), mn.reshape(shape), vn.reshape(shape))

    return (loss_out, grad_x[None]) + tuple(out[n][j] for j in range(4) for n in _WEIGHTS)
```

```python
import functools

import jax
import jax.numpy as jnp
from jax import lax
from jax.experimental import pallas as pl
from jax.experimental.pallas import tpu as pltpu

F32 = jnp.float32
BF16 = jnp.bfloat16

RMS_EPS = 1e-6
LN_EPS = 1e-5
FFN_RESIDUAL_WEIGHT = 0.5
HEAD_DIM = 64
ROPE_DIM = 16
ROPE_THETA = 500000.0
ADAM_LR = 0.001
ADAM_B1 = 0.9
ADAM_B2 = 0.999
ADAM_EPS = 1e-08
ADAM_WD = 0.01
ADAM_STEP = 10
N_CHIPS = 4
N_DEV = 8
LANES = 128
VMEM_LIMIT_BYTES = 56 * 1024 * 1024
MESH_ID = pl.DeviceIdType.MESH

_DIMS = {
    "nn": (((1,), (0,)), ((), ())),
    "nt": (((1,), (1,)), ((), ())),
    "tn": (((0,), (0,)), ((), ())),
}


def _params(n_axes):
    return pltpu.CompilerParams(dimension_semantics=("arbitrary",) * n_axes, vmem_limit_bytes=VMEM_LIMIT_BYTES)


def _mm(name, grid, pairs, extras, outs, epilogue, acc_shapes, order="ij"):
    ni, nj, nk = grid
    if order == "ij":
        pgrid = (ni, nj, nk)
        ijk = lambda g0, g1, g2: (g0, g1, g2)
    else:
        pgrid = (nj, ni, nk)
        ijk = lambda g0, g1, g2: (g1, g0, g2)
    in_specs, args = [], []
    for a, ablk, amap, b, bblk, bmap, _, _, _ in pairs:
        in_specs.append(pl.BlockSpec(ablk, lambda g0, g1, g2, m=amap: m(*ijk(g0, g1, g2))))
        in_specs.append(pl.BlockSpec(bblk, lambda g0, g1, g2, m=bmap: m(*ijk(g0, g1, g2))))
        args += [a, b]
    for e, eblk, emap in extras:
        in_specs.append(pl.BlockSpec(eblk, lambda g0, g1, g2, m=emap: m(*ijk(g0, g1, g2)[:2])))
        args.append(e)
    out_specs = [pl.BlockSpec(oblk, lambda g0, g1, g2, m=omap: m(*ijk(g0, g1, g2)[:2])) for _, _, oblk, omap in outs]
    out_shape = [jax.ShapeDtypeStruct(s, d) for s, d, _, _ in outs]
    n_p, n_e, n_o = len(pairs), len(extras), len(outs)

    def body(*refs):
        p_refs = refs[: 2 * n_p]
        e_refs = refs[2 * n_p: 2 * n_p + n_e]
        o_refs = refs[2 * n_p + n_e: 2 * n_p + n_e + n_o]
        accs = refs[2 * n_p + n_e + n_o:]
        k = pl.program_id(2)

        @pl.when(k == 0)
        def _():
            for acc in accs:
                acc[...] = jnp.zeros(acc.shape, F32)

        for idx in range(n_p):
            dims, ai, k0 = pairs[idx][6], pairs[idx][7], pairs[idx][8]

            def step(idx=idx, dims=dims, ai=ai):
                a = p_refs[2 * idx][...].astype(BF16)
                b = p_refs[2 * idx + 1][...].astype(BF16)
                accs[ai][...] += lax.dot_general(a, b, _DIMS[dims], preferred_element_type=F32)

            if k0:
                pl.when(k == 0)(step)
            else:
                step()

        @pl.when(k == nk - 1)
        def _():
            vals = epilogue([acc[...] for acc in accs], [e[...] for e in e_refs])
            for o, v in zip(o_refs, vals):
                o[...] = v.astype(o.dtype)

    res = pl.pallas_call(
        body, name=name, grid=pgrid, in_specs=in_specs, out_specs=out_specs, out_shape=out_shape,
        scratch_shapes=[pltpu.VMEM(s, F32) for s in acc_shapes], compiler_params=_params(3),
    )(*args)
    return res


def _col_nn(r0, tk, tn, wc):
    assert r0 % tk == 0 and wc % tn == 0
    npc = wc // tn
    return (None, tk, tn), lambda i, j, k: (j // npc, r0 // tk + k, j % npc)


def _row_nn(r0, tk, tn, rc):
    assert r0 % tk == 0 and rc % tk == 0
    kpc = rc // tk
    return (None, tk, tn), lambda i, j, k: (k // kpc, r0 // tk + k % kpc, j)


def _col_nt(r0, tn, tk, wc):
    assert r0 % tn == 0 and wc % tk == 0
    kpc = wc // tk
    return (None, tn, tk), lambda i, j, k: (k // kpc, r0 // tn + j, k % kpc)


def _row_nt(r0, tn, tk, rc):
    assert r0 % tn == 0 and rc % tn == 0
    npc = rc // tn
    return (None, tn, tk), lambda i, j, k: (j // npc, r0 // tn + j % npc, k)


def _out_col(tm, tn, wc):
    npc = wc // tn
    return (None, tm, tn), lambda i, j: (j // npc, i, j % npc)


def _out_row(tm, tn, rc):
    mpc = rc // tm
    return (None, tm, tn), lambda i, j: (i // mpc, i % mpc, j)


def _ik(i, j, k):
    return (i, k)


def _ki(i, j, k):
    return (k, i)


def _kj(i, j, k):
    return (k, j)


def _i0(i, j, k):
    return (i, 0)


def _ij(i, j):
    return (i, j)


def _0j(i, j):
    return (0, j)


def _i0e(i, j):
    return (i, 0)


def _rows(name, n, ins, outs, body, scratch=()):
    in_specs, args = [], []
    for item in ins:
        if len(item) == 2:
            in_specs.append(item[1])
        else:
            in_specs.append(pl.BlockSpec(item[1], item[2]))
        args.append(item[0])
    out_specs = [pl.BlockSpec(blk, m) for _, _, blk, m in outs]
    out_shape = [jax.ShapeDtypeStruct(s, d) for s, d, _, _ in outs]
    return pl.pallas_call(
        body, name=name, grid=(n,), in_specs=in_specs, out_specs=out_specs, out_shape=out_shape,
        scratch_shapes=list(scratch), compiler_params=_params(1),
    )(*args)


def _row(arr, tm):
    return (arr, (tm, arr.shape[1]), lambda i: (i, 0))


def _full(arr):
    nd = arr.ndim
    return (arr, arr.shape, lambda i: (0,) * nd)


def _row_out(shape, dtype, tm):
    return (shape, dtype, (tm, shape[1]), lambda i: (i, 0))


def _acc_out(shape):
    nd = len(shape)
    return (shape, F32, shape, lambda i: (0,) * nd)


def _tm(t):
    return 512 if t >= 1024 else t // 2


def _first(i, refs):
    @pl.when(i == 0)
    def _():
        for r in refs:
            r[...] = jnp.zeros(r.shape, r.dtype)


def _rms_fwd(x, g, tag):
    t, d = x.shape
    tm = _tm(t)

    def body(x_ref, g_ref, h_ref):
        xv = x_ref[...]
        r = lax.rsqrt(jnp.mean(xv * xv, axis=-1, keepdims=True) + RMS_EPS)
        h_ref[...] = (xv * r * g_ref[...]).astype(BF16)

    return _rows(f"rms_fwd_{tag}", t // tm, [_row(x, tm), _full(g)], [_row_out((t, d), BF16, tm)], body)[0]


def _rms_bwd(dh, x, g, dx_out, tag):
    t, d = x.shape
    tm = _tm(t)

    def body(dh_ref, x_ref, g_ref, dxo_ref, dx_ref, dg_ref):
        i = pl.program_id(0)
        _first(i, [dg_ref])
        xv = x_ref[...]
        r = lax.rsqrt(jnp.mean(xv * xv, axis=-1, keepdims=True) + RMS_EPS)
        xh = xv * r
        dhv = dh_ref[...]
        dxh = dhv * g_ref[...]
        dx_ref[...] = dxo_ref[...] + r * (dxh - xh * jnp.mean(dxh * xh, axis=-1, keepdims=True))
        dg_ref[...] += jnp.sum(dhv * xh, axis=0, keepdims=True)

    return _rows(f"rms_bwd_{tag}", t // tm, [_row(dh, tm), _row(x, tm), _full(g), _row(dx_out, tm)],
                 [_row_out((t, d), F32, tm), _acc_out((1, d))], body)


def _loss_head(x, g, target):
    t, d = x.shape
    tm = _tm(t)

    def body(x_ref, g_ref, t_ref, dx_ref, dg_ref, loss_ref):
        i = pl.program_id(0)
        _first(i, [dg_ref, loss_ref])
        xv = x_ref[...]
        gv = g_ref[...]
        r = lax.rsqrt(jnp.mean(xv * xv, axis=-1, keepdims=True) + RMS_EPS)
        xh = xv * r
        err = xh * gv - t_ref[...]
        loss_ref[...] += jnp.full((1, LANES), 0.5, F32) * jnp.sum(jnp.mean(err * err, axis=-1, keepdims=True))
        dy = err * (1.0 / d)
        dxh = dy * gv
        dx_ref[...] = r * (dxh - xh * jnp.mean(dxh * xh, axis=-1, keepdims=True))
        dg_ref[...] += jnp.sum(dy * xh, axis=0, keepdims=True)

    return _rows("loss_head", t // tm, [_row(x, tm), _full(g), _row(target, tm)],
                 [_row_out((t, d), F32, tm), _acc_out((1, d)), _acc_out((1, LANES))], body)


def _colsum(arr, tag):
    t, w = arr.shape
    tm = _tm(t)

    def body(a_ref, o_ref):
        _first(pl.program_id(0), [o_ref])
        o_ref[...] += jnp.sum(a_ref[...].astype(F32), axis=0, keepdims=True)

    return _rows(f"colsum_{tag}", t // tm, [_row(arr, tm)], [_acc_out((1, w))], body)[0]


def _ffn_fwd(x, gn, f_a, f_b, o_w1, o_w3, o_w2, tag):
    t, d = x.shape
    fc = f_a.shape[2]
    f = N_CHIPS * fc
    tm = _tm(t)
    h = _rms_fwd(x, gn, f"ffn_{tag}")
    w1blk, w1map = _col_nn(o_w1, d, fc, fc)
    w3blk, w3map = _col_nn(o_w3, d, fc, fc)
    tma = min(tm, 256)

    def ep_ab(accs, ex):
        a, b = accs
        return a, b, a * jax.nn.sigmoid(a) * b

    a, b, act = _mm(
        f"ffn_ab_{tag}", (t // tma, N_CHIPS, 1),
        [(h, (tma, d), _ik, f_a, w1blk, w1map, "nn", 0, False), (h, (tma, d), _ik, f_a, w3blk, w3map, "nn", 1, False)],
        [], [((t, f), BF16, (tma, fc), _ij)] * 3, ep_ab, [(tma, fc)] * 2, order="ji")
    tn = min(d, 1024)
    w2blk, w2map = _row_nn(o_w2, fc, tn, fc)
    x_new = _mm(
        f"ffn_out_{tag}", (t // tm, d // tn, N_CHIPS),
        [(act, (tm, fc), _ik, f_b, w2blk, w2map, "nn", 0, False)],
        [(x, (tm, tn), _ij)], [((t, d), F32, (tm, tn), _ij)],
        lambda accs, ex: [ex[0] + FFN_RESIDUAL_WEIGHT * accs[0]], [(tm, tn)])[0]
    return x_new, (x, h, a, b, act)


def _ffn_bwd(dx_out, saved, gn, f_a, f_b, o_w1, o_w3, o_w2, tag):
    x, h, a, b, act = saved
    t, d = x.shape
    fc = f_a.shape[2]
    f = N_CHIPS * fc
    tm = _tm(t)
    tma = min(tm, 256)
    w2blk, w2map = _row_nt(o_w2, fc, d, fc)

    def ep_dg(accs, ex):
        dg = FFN_RESIDUAL_WEIGHT * accs[0]
        av, bv = ex[0].astype(F32), ex[1].astype(F32)
        sig = jax.nn.sigmoid(av)
        return dg * bv * sig * (1.0 + av * (1.0 - sig)), dg * av * sig

    da, db = _mm(
        f"ffn_dg_{tag}", (t // tma, N_CHIPS, 1),
        [(dx_out, (tma, d), _ik, f_b, w2blk, w2map, "nt", 0, False)],
        [(a, (tma, fc), _ij), (b, (tma, fc), _ij)], [((t, f), BF16, (tma, fc), _ij)] * 2, ep_dg, [(tma, fc)], order="ji")

    tmd = min(d, 512)
    tk = min(t, 512)
    oblk, omap = _out_col(tmd, fc, fc)
    dw1, dw3 = _mm(
        f"ffn_dw13_{tag}", (d // tmd, N_CHIPS, t // tk),
        [(h, (tk, tmd), _ki, da, (tk, fc), _kj, "tn", 0, False), (h, (tk, tmd), _ki, db, (tk, fc), _kj, "tn", 1, False)],
        [], [((N_CHIPS, d, fc), BF16, oblk, omap)] * 2, lambda accs, ex: accs, [(tmd, fc)] * 2)
    tn = min(d, 1024)
    tk2 = min(t, 256)
    dw2 = _mm(
        f"ffn_dw2_{tag}", (N_CHIPS, d // tn, t // tk2),
        [(act, (tk2, fc), _ki, dx_out, (tk2, tn), _kj, "tn", 0, False)],
        [], [((N_CHIPS, fc, d), BF16, (None, fc, tn), lambda i, j: (i, 0, j))],
        lambda accs, ex: [FFN_RESIDUAL_WEIGHT * accs[0]], [(fc, tn)])[0]
    w1blk, w1map = _col_nt(o_w1, tn, fc, fc)
    w3blk, w3map = _col_nt(o_w3, tn, fc, fc)
    dh = _mm(
        f"ffn_dh_{tag}", (t // tm, d // tn, N_CHIPS),
        [(da, (tm, fc), _ik, f_a, w1blk, w1map, "nt", 0, False), (db, (tm, fc), _ik, f_a, w3blk, w3map, "nt", 0, False)],
        [], [((t, d), F32, (tm, tn), _ij)], lambda accs, ex: accs, [(tm, tn)])[0]
    dx, dgn = _rms_bwd(dh, x, gn, dx_out, f"ffn_{tag}")
    return dx, dgn, dw1, dw3, dw2


def _ple_fwd(x, p_i, gn, f_b, f_d, o_gate, o_proj, tag):
    t, d = x.shape
    pdim = p_i.shape[1]
    d4 = d // N_CHIPS
    tm = _tm(t)
    hp = _rms_fwd(x, gn, f"ple_{tag}")
    gblk, gmap = _row_nn(o_gate, d4, d4, d4)
    assert o_proj % pdim == 0

    def ep(accs, ex):
        s = jax.nn.sigmoid(accs[0])
        return ex[0] + s * accs[1], s, accs[1]

    x_new, s, pp = _mm(
        f"ple_fwd_{tag}", (t // tm, N_CHIPS, N_CHIPS),
        [(hp, (tm, d4), _ik, f_b, gblk, gmap, "nn", 0, False),
         (p_i, (tm, pdim), _i0, f_d, (None, pdim, d4), lambda i, j, k: (j, o_proj // pdim, 0), "nn", 1, True)],
        [(x, (tm, d4), _ij)], [((t, d), F32, (tm, d4), _ij), ((t, d), BF16, (tm, d4), _ij), ((t, d), BF16, (tm, d4), _ij)],
        ep, [(tm, d4)] * 2)
    return x_new, (x, hp, s, pp)


def _ple_bwd(dx_out, saved, p_i, gn, f_b, o_gate, tag):
    x, hp, s, pp = saved
    t, d = x.shape
    pdim = p_i.shape[1]
    d4 = d // N_CHIPS
    tm = _tm(t)

    def body(dx_ref, s_ref, pp_ref, dpp_ref, dgp_ref):
        dxv = dx_ref[...]
        sv = s_ref[...].astype(F32)
        ppv = pp_ref[...].astype(F32)
        dpp_ref[...] = (dxv * sv).astype(BF16)
        dgp_ref[...] = (dxv * ppv * sv * (1.0 - sv)).astype(BF16)

    dpp, dgp = _rows(f"ple_ew_{tag}", t // tm, [_row(dx_out, tm), _row(s, tm), _row(pp, tm)],
                     [_row_out((t, d), BF16, tm)] * 2, body)
    tk = min(t, 512)
    dwp = _mm(
        f"ple_dwp_{tag}", (1, N_CHIPS, t // tk),
        [(p_i, (tk, pdim), lambda i, j, k: (k, 0), dpp, (tk, d4), _kj, "tn", 0, False)],
        [], [((N_CHIPS, pdim, d4), BF16, (None, pdim, d4), lambda i, j: (j, 0, 0))], lambda accs, ex: accs, [(pdim, d4)])[0]
    tn = min(d, 1024)
    dwg = _mm(
        f"ple_dwg_{tag}", (N_CHIPS, d // tn, t // tk),
        [(hp, (tk, d4), _ki, dgp, (tk, tn), _kj, "tn", 0, False)],
        [], [((N_CHIPS, d4, d), BF16, (None, d4, tn), lambda i, j: (i, 0, j))], lambda accs, ex: accs, [(d4, tn)])[0]
    gblk, gmap = _row_nt(o_gate, d4, d, d4)
    dhp = _mm(
        f"ple_dh_{tag}", (t // tm, N_CHIPS, 1),
        [(dgp, (tm, d), _ik, f_b, gblk, gmap, "nt", 0, False)],
        [], [((t, d), F32, (tm, d4), _ij)], lambda accs, ex: accs, [(tm, d4)], order="ji")[0]
    dx, dgn = _rms_bwd(dhp, x, gn, dx_out, f"ple_{tag}")
    return dx, dgn, dwg, dwp


_SQRT_HALF = 0.7071067811865476
_INV_SQRT_2PI = 0.3989422804014327


def _gelu(z):
    return z * (lax.erf(z * _SQRT_HALF) + 1.0) * 0.5


def _gelu_grad(z):
    return 0.5 * (1.0 + lax.erf(z * _SQRT_HALF)) + z * (_INV_SQRT_2PI * jnp.exp(-0.5 * z * z))


def _causal_bf16(w):
    c = w.shape[0]
    keep = lax.broadcasted_iota(jnp.int32, (c, c), 0) >= lax.broadcasted_iota(jnp.int32, (c, c), 1)
    return jnp.where(keep, w, 0.0).astype(BF16), keep


def _gmlp_gate_fwd(z_pre, ln_g, ln_b, w_s, b_st, tag):
    t, w2 = z_pre.shape
    gw = w2 // 2
    n_g, chunk, _ = w_s.shape
    gd = gw // n_g

    def body(z_ref, g_ref, b_ref, ws_ref, bs_ref, o_ref):
        z = z_ref[...]
        u = _gelu(z[:, :gw])
        zv = _gelu(z[:, gw:])
        mu = jnp.mean(zv, axis=-1, keepdims=True)
        cen = zv - mu
        rstd = lax.rsqrt(jnp.mean(cen * cen, axis=-1, keepdims=True) + LN_EPS)
        vln = (cen * rstd * g_ref[...] + b_ref[...]).astype(BF16)
        bst = bs_ref[...]
        for g in range(n_g):
            wm, _ = _causal_bf16(ws_ref[g])
            sl = slice(g * gd, (g + 1) * gd)
            s = jnp.dot(wm, vln[:, sl], preferred_element_type=F32) + bst[:, g:g + 1]
            o_ref[:, sl] = (u[:, sl] * s).astype(BF16)

    return _rows(f"gmlp_gate_{tag}", t // chunk, [_row(z_pre, chunk), _full(ln_g), _full(ln_b), _full(w_s), _full(b_st)],
                 [_row_out((t, gw), BF16, chunk)], body)[0]


def _gmlp_gate_bwd(z_pre, dgated, ln_g, ln_b, w_s, b_st, tag):
    t, w2 = z_pre.shape
    gw = w2 // 2
    n_g, chunk, _ = w_s.shape
    gd = gw // n_g

    def body(z_ref, dgt_ref, g_ref, b_ref, ws_ref, bs_ref, dz_ref, dws_ref, dbs_ref, dlg_ref, dlb_ref, dv_scr):
        _first(pl.program_id(0), [dws_ref, dbs_ref, dlg_ref, dlb_ref])
        z = z_ref[...]
        zu, zvp = z[:, :gw], z[:, gw:]
        u = _gelu(zu)
        zv = _gelu(zvp)
        mu = jnp.mean(zv, axis=-1, keepdims=True)
        cen = zv - mu
        rstd = lax.rsqrt(jnp.mean(cen * cen, axis=-1, keepdims=True) + LN_EPS)
        vn = cen * rstd
        lg = g_ref[...]
        vln = (vn * lg + b_ref[...]).astype(BF16)
        bst = bs_ref[...]
        lane = lax.broadcasted_iota(jnp.int32, (chunk, LANES), 1)
        dbs = jnp.zeros((chunk, LANES), F32)
        for g in range(n_g):
            wm, keep = _causal_bf16(ws_ref[g])
            sl = slice(g * gd, (g + 1) * gd)
            vg = vln[:, sl]
            s = jnp.dot(wm, vg, preferred_element_type=F32) + bst[:, g:g + 1]
            dgt = dgt_ref[:, sl].astype(F32)
            ds = dgt * u[:, sl]
            ds16 = ds.astype(BF16)
            dz_ref[:, sl] = (dgt * s * _gelu_grad(zu[:, sl])).astype(dz_ref.dtype)
            dv_scr[:, sl] = lax.dot_general(wm, ds16, _DIMS["tn"], preferred_element_type=F32)
            dw = lax.dot_general(ds16, vg, _DIMS["nt"], preferred_element_type=F32)
            dws_ref[g] += jnp.where(keep, dw, 0.0)
            dbs = dbs + jnp.where(lane == g, jnp.sum(ds, axis=-1, keepdims=True), 0.0)
        dbs_ref[...] += dbs
        dvln = dv_scr[...]
        dlg_ref[...] += jnp.sum(dvln * vn, axis=0, keepdims=True)
        dlb_ref[...] += jnp.sum(dvln, axis=0, keepdims=True)
        dvn = dvln * lg
        dzv = rstd * (dvn - jnp.mean(dvn, axis=-1, keepdims=True) - vn * jnp.mean(dvn * vn, axis=-1, keepdims=True))
        dz_ref[:, gw:] = (dzv * _gelu_grad(zvp)).astype(dz_ref.dtype)

    return _rows(
        f"gmlp_gate_bwd_{tag}", t // chunk,
        [_row(z_pre, chunk), _row(dgated, chunk), _full(ln_g), _full(ln_b), _full(w_s), _full(b_st)],
        [_row_out((t, w2), BF16, chunk), _acc_out((n_g, chunk, chunk)), _acc_out((chunk, LANES)), _acc_out((1, gw)),
         _acc_out((1, gw))], body, scratch=[pltpu.VMEM((chunk, gw), F32)])


def _gmlp_fwd(x, gn, ln_g, ln_b, w_s, b_st, f_b, o_in, o_out, tag):
    t, d = x.shape
    gw = ln_g.shape[1]
    tm = _tm(t)
    d4 = d // N_CHIPS
    h = _rms_fwd(x, gn, f"mix_{tag}")
    tn = min(d, 1024)
    iblk, imap = _col_nn(o_in, d4, tn, d)
    z_pre = _mm(
        f"gmlp_in_{tag}", (t // tm, N_CHIPS * d // tn, N_CHIPS),
        [(h, (tm, d4), _ik, f_b, iblk, imap, "nn", 0, False)],
        [], [((t, 2 * gw), F32, (tm, tn), _ij)], lambda accs, ex: accs, [(tm, tn)])[0]
    gated = _gmlp_gate_fwd(z_pre, ln_g, ln_b, w_s, b_st, tag)
    rc = gw // N_CHIPS
    tk = min(rc, 512)
    oblk, omap = _row_nn(o_out, tk, tn, rc)
    x_new = _mm(
        f"gmlp_out_{tag}", (t // tm, d // tn, gw // tk),
        [(gated, (tm, tk), _ik, f_b, oblk, omap, "nn", 0, False)],
        [(x, (tm, tn), _ij)], [((t, d), F32, (tm, tn), _ij)], lambda accs, ex: [ex[0] + accs[0]], [(tm, tn)])[0]
    return x_new, (x, h, z_pre, gated)


def _gmlp_bwd(dx_out, saved, gn, ln_g, ln_b, w_s, b_st, f_b, o_in, o_out, tag):
    x, h, z_pre, gated = saved
    t, d = x.shape
    gw = ln_g.shape[1]
    tm = _tm(t)
    d4 = d // N_CHIPS
    rc = gw // N_CHIPS
    tnr = min(rc, 512)
    oblk, omap = _row_nt(o_out, tnr, d, rc)
    dgated = _mm(
        f"gmlp_dgated_{tag}", (t // tm, gw // tnr, 1),
        [(dx_out, (tm, d), _ik, f_b, oblk, omap, "nt", 0, False)],
        [], [((t, gw), BF16, (tm, tnr), _ij)], lambda accs, ex: accs, [(tm, tnr)], order="ji")[0]
    tk = min(t, 512)
    tn = min(d, 1024)
    rblk, rmap = _out_row(tnr, tn, rc)
    dw_out = _mm(
        f"gmlp_dwout_{tag}", (gw // tnr, d // tn, t // tk),
        [(gated, (tk, tnr), _ki, dx_out, (tk, tn), _kj, "tn", 0, False)],
        [], [((N_CHIPS, rc, d), BF16, rblk, rmap)], lambda accs, ex: accs, [(tnr, tn)])[0]
    dz, dws, dbs, dlg, dlb = _gmlp_gate_bwd(z_pre, dgated, ln_g, ln_b, w_s, b_st, tag)
    tmd = min(d, 512)
    cblk, cmap = _out_col(tmd, tn, d)
    dw_in = _mm(
        f"gmlp_dwin_{tag}", (d // tmd, N_CHIPS * d // tn, t // tk),
        [(h, (tk, tmd), _ki, dz, (tk, tn), _kj, "tn", 0, False)],
        [], [((N_CHIPS, d, d), BF16, cblk, cmap)], lambda accs, ex: accs, [(tmd, tn)])[0]
    iblk, imap = _col_nt(o_in, d4, tn, d)
    dh = _mm(
        f"gmlp_dh_{tag}", (t // tm, N_CHIPS, N_CHIPS * d // tn),
        [(dz, (tm, tn), _ik, f_b, iblk, imap, "nt", 0, False)],
        [], [((t, d), F32, (tm, d4), _ij)], lambda accs, ex: accs, [(tm, d4)])[0]
    dx, dgn = _rms_bwd(dh, x, gn, dx_out, f"mix_{tag}")
    n_g = w_s.shape[0]
    return dx, dgn, dw_in, dw_out, dws, dbs[:, :n_g].T, dlg, dlb


def _rope_tables(t, width, n_rot_heads):
    half = ROPE_DIM // 2
    inv_freq = ROPE_THETA ** (-jnp.arange(0, ROPE_DIM, 2, dtype=F32) / ROPE_DIM)
    ang = jnp.arange(t, dtype=F32)[:, None] * inv_freq[None, :]
    cos, sin = jnp.cos(ang), jnp.sin(ang)
    ones = jnp.ones((t, HEAD_DIM - ROPE_DIM), F32)
    zeros = jnp.zeros((t, HEAD_DIM - ROPE_DIM), F32)
    zh = jnp.zeros((t, half), F32)
    c = jnp.concatenate([cos, cos, ones], axis=1)
    s_next = jnp.concatenate([-sin, zh, zeros], axis=1)
    s_prev = jnp.concatenate([zh, sin, zeros], axis=1)
    rest = width - n_rot_heads * HEAD_DIM

    def widen(tab, fill):
        parts = [jnp.tile(tab, (1, n_rot_heads))]
        if rest:
            parts.append(jnp.full((t, rest), fill, F32))
        return jnp.concatenate(parts, axis=1)

    return widen(c, 1.0), widen(s_next, 0.0), widen(s_prev, 0.0)


def _rope(v, c, s_next, s_prev, sign):
    w = v.shape[1]
    half = ROPE_DIM // 2
    reps = w // c.shape[1]
    if reps > 1:
        c, s_next, s_prev = (jnp.tile(tab, (1, reps)) for tab in (c, s_next, s_prev))
    return v * c + sign * (pltpu.roll(v, w - half, 1) * s_next + pltpu.roll(v, half, 1) * s_prev)


def _scores_mask(n, blk):
    q_pos = lax.broadcasted_iota(jnp.int32, (blk, 2 * blk), 0) + blk
    k_pos = lax.broadcasted_iota(jnp.int32, (blk, 2 * blk), 1)
    diff = q_pos - k_pos
    band = (diff >= 0) & (diff < blk)
    return band & ((k_pos >= blk) | (n > 0))


def _attn_fwd(q, kv, sinks, blk):
    t, qw = q.shape
    kw = kv.shape[1] // 2
    n_kv = kw // HEAD_DIM
    q_per_kv = qw // kw
    scale = HEAD_DIM ** -0.5

    def body(sink_ref, q_ref, kvo_ref, kvp_ref, o_ref):
        n = pl.program_id(0)
        valid = _scores_mask(n, blk)
        qv = q_ref[...]
        kvo = kvo_ref[...]
        kvp = kvp_ref[...]
        for kh in range(n_kv):
            ks = slice(kh * HEAD_DIM, (kh + 1) * HEAD_DIM)
            vs = slice(kw + kh * HEAD_DIM, kw + (kh + 1) * HEAD_DIM)
            kb = jnp.concatenate([kvp[:, ks], kvo[:, ks]], axis=0)
            vb = jnp.concatenate([kvp[:, vs], kvo[:, vs]], axis=0)
            for g in range(q_per_kv):
                hd = kh * q_per_kv + g
                hs = slice(hd * HEAD_DIM, (hd + 1) * HEAD_DIM)
                sink = sink_ref[0, hd]
                s = lax.dot_general(qv[:, hs], kb, _DIMS["nt"], preferred_element_type=F32) * scale
                s = jnp.where(valid, s, -1e30)
                m = jnp.maximum(jnp.max(s, axis=-1, keepdims=True), sink)
                e = jnp.where(valid, jnp.exp(s - m), 0.0)
                denom = jnp.sum(e, axis=-1, keepdims=True) + jnp.exp(sink - m)
                p = (e / denom).astype(BF16)
                o_ref[:, hs] = jnp.dot(p, vb, preferred_element_type=F32).astype(BF16)

    return _rows(
        "swa_attn", t // blk,
        [(sinks, pl.BlockSpec(memory_space=pltpu.SMEM)), _row(q, blk), _row(kv, blk),
         (kv, (blk, kv.shape[1]), lambda i: (jnp.maximum(i - 1, 0), 0))],
        [_row_out((t, qw), BF16, blk)], body)[0]


def _attn_bwd(q, kv, do, sinks, blk):
    t, qw = q.shape
    kw = kv.shape[1] // 2
    n_kv = kw // HEAD_DIM
    q_per_kv = qw // kw
    scale = HEAD_DIM ** -0.5

    def body(sink_ref, q_ref, kvo_ref, kvp_ref, do_ref, dq_ref, dkvo_ref, dkvp_ref, dsink_ref):
        n = pl.program_id(0)
        _first(n, [dsink_ref])
        valid = _scores_mask(n, blk)
        lane = lax.broadcasted_iota(jnp.int32, (1, LANES), 1)
        qv = q_ref[...]
        kvo = kvo_ref[...]
        kvp = kvp_ref[...]
        dov = do_ref[...]
        dsink = jnp.zeros((1, LANES), F32)
        for kh in range(n_kv):
            ks = slice(kh * HEAD_DIM, (kh + 1) * HEAD_DIM)
            vs = slice(kw + kh * HEAD_DIM, kw + (kh + 1) * HEAD_DIM)
            kb = jnp.concatenate([kvp[:, ks], kvo[:, ks]], axis=0)
            vb = jnp.concatenate([kvp[:, vs], kvo[:, vs]], axis=0)
            dkb = jnp.zeros((2 * blk, HEAD_DIM), F32)
            dvb = jnp.zeros((2 * blk, HEAD_DIM), F32)
            for g in range(q_per_kv):
                hd = kh * q_per_kv + g
                hs = slice(hd * HEAD_DIM, (hd + 1) * HEAD_DIM)
                sink = sink_ref[0, hd]
                qh = qv[:, hs]
                doh = dov[:, hs]
                s = lax.dot_general(qh, kb, _DIMS["nt"], preferred_element_type=F32) * scale
                s = jnp.where(valid, s, -1e30)
                m = jnp.maximum(jnp.max(s, axis=-1, keepdims=True), sink)
                e = jnp.where(valid, jnp.exp(s - m), 0.0)
                e_sink = jnp.exp(sink - m)
                inv = 1.0 / (jnp.sum(e, axis=-1, keepdims=True) + e_sink)
                p = e * inv
                p16 = p.astype(BF16)
                dp = lax.dot_general(doh, vb, _DIMS["nt"], preferred_element_type=F32)
                dot_pd = jnp.sum(p * dp, axis=-1, keepdims=True)
                ds16 = (p * (dp - dot_pd)).astype(BF16)
                dsink = dsink + jnp.where(lane == hd, -jnp.sum(e_sink * inv * dot_pd, axis=0, keepdims=True), 0.0)
                dq_ref[:, hs] = (jnp.dot(ds16, kb, preferred_element_type=F32) * scale).astype(BF16)
                dkb = dkb + lax.dot_general(ds16, qh, _DIMS["tn"], preferred_element_type=F32) * scale
                dvb = dvb + lax.dot_general(p16, doh, _DIMS["tn"], preferred_element_type=F32)
            dkvp_ref[:, ks] = dkb[:blk]
            dkvo_ref[:, ks] = dkb[blk:]
            dkvp_ref[:, vs] = dvb[:blk]
            dkvo_ref[:, vs] = dvb[blk:]
        dsink_ref[...] += dsink

    return _rows(
        "swa_attn_bwd", t // blk,
        [(sinks, pl.BlockSpec(memory_space=pltpu.SMEM)), _row(q, blk), _row(kv, blk),
         (kv, (blk, kv.shape[1]), lambda i: (jnp.maximum(i - 1, 0), 0)), _row(do, blk)],
        [_row_out((t, qw), BF16, blk), _row_out((t, 2 * kw), F32, blk), _row_out((t, 2 * kw), F32, blk),
         _acc_out((1, LANES))], body)


def _rope_bwd(dq_r, dkv_own, dkv_prev, tabs_q, tabs_kv, blk):
    t, qw = dq_r.shape
    kvw = dkv_own.shape[1]
    nb = t // blk

    def body(dq_ref, own_ref, nxt_ref, cq, snq, spq, ck, snk, spk, dqo_ref, dkvo_ref, dbq_ref, dbkv_ref):
        i = pl.program_id(0)
        _first(i, [dbq_ref, dbkv_ref])
        dq = _rope(dq_ref[...].astype(F32), cq[...], snq[...], spq[...], -1.0)
        dkv = own_ref[...] + jnp.where(i < nb - 1, nxt_ref[...], 0.0)
        dkv = _rope(dkv, ck[...], snk[...], spk[...], -1.0)
        dqo_ref[...] = dq.astype(BF16)
        dkvo_ref[...] = dkv.astype(BF16)
        dbq_ref[...] += jnp.sum(dq, axis=0, keepdims=True)
        dbkv_ref[...] += jnp.sum(dkv, axis=0, keepdims=True)

    return _rows(
        "swa_rope_bwd", nb,
        [_row(dq_r, blk), _row(dkv_own, blk), (dkv_prev, (blk, kvw), lambda i: (jnp.minimum(i + 1, nb - 1), 0))]
        + [_row(tab, blk) for tab in tabs_q] + [_row(tab, blk) for tab in tabs_kv],
        [_row_out((t, qw), BF16, blk), _row_out((t, kvw), BF16, blk), _acc_out((1, qw)), _acc_out((1, kvw))], body)


def _swa_fwd(x, gn, bq, bkv, bo, sinks, f_b, f_d, o_q, o_o, o_kv, blk, tag):
    t, d = x.shape
    qw = bq.shape[1]
    kvw = bkv.shape[1]
    d4 = d // N_CHIPS
    tm = _tm(t)
    h = _rms_fwd(x, gn, f"mix_{tag}")
    tabs_q = _rope_tables(t, LANES, LANES // HEAD_DIM)
    tabs_kv = _rope_tables(t, kvw, kvw // 2 // HEAD_DIM)
    tn = min(qw, 1024)
    qblk, qmap = _row_nn(o_q, d4, tn, d4)

    def ep_rope(accs, ex):
        return [_rope(accs[0] + ex[0], ex[1], ex[2], ex[3], 1.0)]

    q = _mm(
        f"swa_q_{tag}", (t // tm, qw // tn, N_CHIPS),
        [(h, (tm, d4), _ik, f_b, qblk, qmap, "nn", 0, False)],
        [(bq, (1, tn), _0j)] + [(tab, (tm, LANES), _i0e) for tab in tabs_q],
        [((t, qw), BF16, (tm, tn), _ij)], ep_rope, [(tm, tn)])[0]
    assert o_kv % d4 == 0
    kv = _mm(
        f"swa_kv_{tag}", (t // tm, 1, N_CHIPS),
        [(h, (tm, d4), _ik, f_d, (None, d4, kvw), lambda i, j, k: (k, o_kv // d4, 0), "nn", 0, False)],
        [(bkv, (1, kvw), _0j)] + [(tab, (tm, kvw), _i0e) for tab in tabs_kv],
        [((t, kvw), BF16, (tm, kvw), _ij)], ep_rope, [(tm, kvw)])[0]
    o = _attn_fwd(q, kv, sinks, blk)
    tno = min(d, 1024)
    oblk, omap = _row_nn(o_o, d4, tno, d4)
    x_new = _mm(
        f"swa_out_{tag}", (t // tm, d // tno, N_CHIPS),
        [(o, (tm, d4), _ik, f_b, oblk, omap, "nn", 0, False)],
        [(x, (tm, tno), _ij), (bo, (1, tno), _0j)], [((t, d), F32, (tm, tno), _ij)],
        lambda accs, ex: [ex[0] + accs[0] + ex[1]], [(tm, tno)])[0]
    return x_new, (x, h, q, kv, o, tabs_q, tabs_kv)


def _swa_bwd(dx_out, saved, gn, sinks, f_b, f_d, o_q, o_o, o_kv, blk, tag):
    x, h, q, kv, o, tabs_q, tabs_kv = saved
    t, d = x.shape
    qw = q.shape[1]
    kvw = kv.shape[1]
    d4 = d // N_CHIPS
    qw4 = qw // N_CHIPS
    tm = _tm(t)
    oblk, omap = _row_nt(o_o, qw4, d, qw4)
    do = _mm(
        f"swa_do_{tag}", (t // tm, N_CHIPS, 1),
        [(dx_out, (tm, d), _ik, f_b, oblk, omap, "nt", 0, False)],
        [], [((t, qw), BF16, (tm, qw4), _ij)], lambda accs, ex: accs, [(tm, qw4)], order="ji")[0]
    tk = min(t, 512)
    tn = min(d, 1024)
    dwo = _mm(
        f"swa_dwo_{tag}", (N_CHIPS, d // tn, t // tk),
        [(o, (tk, qw4), _ki, dx_out, (tk, tn), _kj, "tn", 0, False)],
        [], [((N_CHIPS, qw4, d), BF16, (None, qw4, tn), lambda i, j: (i, 0, j))], lambda accs, ex: accs, [(qw4, tn)])[0]
    dbo = _colsum(dx_out, f"bo_{tag}")
    dq_r, dkv_own, dkv_prev, dsink = _attn_bwd(q, kv, do, sinks, blk)
    dq, dkv, dbq, dbkv = _rope_bwd(dq_r, dkv_own, dkv_prev, tabs_q, tabs_kv, blk)
    tnq = min(qw, 1024)
    dwq = _mm(
        f"swa_dwq_{tag}", (N_CHIPS, qw // tnq, t // tk),
        [(h, (tk, d4), _ki, dq, (tk, tnq), _kj, "tn", 0, False)],
        [], [((N_CHIPS, d4, qw), BF16, (None, d4, tnq), lambda i, j: (i, 0, j))], lambda accs, ex: accs, [(d4, tnq)])[0]
    dwkv = _mm(
        f"swa_dwkv_{tag}", (N_CHIPS, 1, t // tk),
        [(h, (tk, d4), _ki, dkv, (tk, kvw), _kj, "tn", 0, False)],
        [], [((N_CHIPS, d4, kvw), BF16, (None, d4, kvw), lambda i, j: (i, 0, j))], lambda accs, ex: accs, [(d4, kvw)])[0]
    qblk, qmap = _row_nt(o_q, d4, qw, d4)
    dh = _mm(
        f"swa_dh_{tag}", (t // tm, N_CHIPS, 1),
        [(dq, (tm, qw), _ik, f_b, qblk, qmap, "nt", 0, False),
         (dkv, (tm, kvw), _ik, f_d, (None, d4, kvw), lambda i, j, k: (j, o_kv // d4, 0), "nt", 0, False)],
        [], [((t, d), F32, (tm, d4), _ij)], lambda accs, ex: accs, [(tm, d4)], order="ji")[0]
    dx, dgn = _rms_bwd(dh, x, gn, dx_out, f"mix_{tag}")
    n_heads = qw // HEAD_DIM
    return dx, dgn, dwq, dwkv, dwo, dbq, dbkv, dbo, dsink[:, :n_heads]


_HBM = pl.BlockSpec(memory_space=pl.ANY)
_CHIP_FLIPS = ((1, 0), (0, 1), (1, 1))


def _place():
    x, y, c = lax.axis_index("x"), lax.axis_index("y"), lax.axis_index("c")
    return x, y, c


def _flip(v, bit):
    return 1 - v if bit else v


def _exchange_small(v_ref, all_ref, send_sems, recv_sems):
    x, y, c = _place()
    me = 4 * x + 2 * y + c
    all_ref[me] = v_ref[...]
    copies = []
    for dlt in range(1, N_DEV):
        peer = (_flip(x, dlt & 4), _flip(y, dlt & 2), _flip(c, dlt & 1))
        copies.append(pltpu.make_async_remote_copy(
            src_ref=v_ref, dst_ref=all_ref.at[me], send_sem=send_sems.at[dlt - 1], recv_sem=recv_sems.at[dlt - 1],
            device_id=peer, device_id_type=MESH_ID))
    for cp in copies:
        cp.start()
    for cp in copies:
        cp.wait()


def _all_gather_small(v):
    r, cdim = v.shape

    def body(v_ref, out_ref, send_sems, recv_sems):
        _exchange_small(v_ref, out_ref, send_sems, recv_sems)

    return pl.pallas_call(
        body, name="all_gather_small", out_shape=jax.ShapeDtypeStruct((N_DEV, r, cdim), v.dtype),
        scratch_shapes=[pltpu.SemaphoreType.DMA((N_DEV - 1,)), pltpu.SemaphoreType.DMA((N_DEV - 1,))],
        compiler_params=pltpu.CompilerParams(vmem_limit_bytes=VMEM_LIMIT_BYTES),
    )(v)


def _all_sum_small(v):
    r, cdim = v.shape

    def body(v_ref, out_ref, all_ref, send_sems, recv_sems):
        _exchange_small(v_ref, all_ref, send_sems, recv_sems)
        acc = all_ref[0]
        for dv in range(1, N_DEV):
            acc = acc + all_ref[dv]
        out_ref[...] = acc

    return pl.pallas_call(
        body, name="all_sum_small", out_shape=jax.ShapeDtypeStruct((r, cdim), v.dtype),
        scratch_shapes=[pltpu.VMEM((N_DEV, r, cdim), v.dtype), pltpu.SemaphoreType.DMA((N_DEV - 1,)),
                        pltpu.SemaphoreType.DMA((N_DEV - 1,))],
        compiler_params=pltpu.CompilerParams(vmem_limit_bytes=VMEM_LIMIT_BYTES),
    )(v)


def _gather_slabs(slabs):
    n = len(slabs)

    def body(*refs):
        srcs, dsts = refs[:n], refs[n:2 * n]
        ici_send, ici_recv, d2d_send, d2d_recv, own_send, own_recv = refs[2 * n:]
        x, y, c = _place()
        chip = 2 * x + y
        local = []
        for s in range(n):
            cp = pltpu.make_async_remote_copy(
                src_ref=srcs[s], dst_ref=dsts[s].at[chip], send_sem=own_send.at[s], recv_sem=own_recv.at[s],
                device_id=(x, y, 1 - c), device_id_type=MESH_ID)
            cp.start()
            local.append(cp)
        sends = []
        for s in range(n):
            half = srcs[s].shape[0] // 2
            rows = pl.ds(c * half, half)
            for j, (fx, fy) in enumerate(_CHIP_FLIPS):
                cp = pltpu.make_async_remote_copy(
                    src_ref=srcs[s].at[rows], dst_ref=dsts[s].at[chip, rows],
                    send_sem=ici_send.at[s * 3 + j], recv_sem=ici_recv.at[s * 3 + j],
                    device_id=(_flip(x, fx), _flip(y, fy), c), device_id_type=MESH_ID)
                cp.start()
                sends.append(cp)
        forwards = []
        for s in range(n):
            half = srcs[s].shape[0] // 2
            rows = pl.ds(c * half, half)
            for j, (fx, fy) in enumerate(_CHIP_FLIPS):
                landed = dsts[s].at[2 * _flip(x, fx) + _flip(y, fy), rows]
                pltpu.make_async_remote_copy(
                    src_ref=landed, dst_ref=landed, send_sem=ici_send.at[s * 3 + j], recv_sem=ici_recv.at[s * 3 + j],
                    device_id=(x, y, c), device_id_type=MESH_ID).wait_recv()
                fw = pltpu.make_async_remote_copy(
                    src_ref=landed, dst_ref=landed, send_sem=d2d_send.at[s * 3 + j], recv_sem=d2d_recv.at[s * 3 + j],
                    device_id=(x, y, 1 - c), device_id_type=MESH_ID)
                fw.start()
                forwards.append(fw)
        for s in range(n):
            half = srcs[s].shape[0] // 2
            other = pl.ds((1 - c) * half, half)
            for j, (fx, fy) in enumerate(_CHIP_FLIPS):
                arriving = dsts[s].at[2 * _flip(x, fx) + _flip(y, fy), other]
                pltpu.make_async_remote_copy(
                    src_ref=arriving, dst_ref=arriving, send_sem=d2d_send.at[s * 3 + j], recv_sem=d2d_recv.at[s * 3 + j],
                    device_id=(x, y, c), device_id_type=MESH_ID).wait_recv()
        for cp in sends + forwards:
            cp.wait_send()
        for cp in local:
            cp.wait()

    return pl.pallas_call(
        body, name="gather_slabs",
        out_shape=[jax.ShapeDtypeStruct((N_CHIPS,) + s.shape, s.dtype) for s in slabs],
        in_specs=[_HBM] * n, out_specs=[_HBM] * n,
        scratch_shapes=[pltpu.SemaphoreType.DMA((3 * n,))] * 4 + [pltpu.SemaphoreType.DMA((n,))] * 2,
    )(*slabs)


def _swap_halves(grads):
    n = len(grads)

    def body(*refs):
        srcs, dsts = refs[:n], refs[n:2 * n]
        send_sems, recv_sems = refs[2 * n:]
        x, y, c = _place()
        copies = []
        for s in range(n):
            half = srcs[s].shape[1] // 2
            cp = pltpu.make_async_remote_copy(
                src_ref=srcs[s].at[:, pl.ds((1 - c) * half, half)], dst_ref=dsts[s],
                send_sem=send_sems.at[s], recv_sem=recv_sems.at[s], device_id=(x, y, 1 - c), device_id_type=MESH_ID)
            cp.start()
            copies.append(cp)
        for cp in copies:
            cp.wait()

    return pl.pallas_call(
        body, name="swap_halves",
        out_shape=[jax.ShapeDtypeStruct((N_CHIPS, g.shape[1] // 2, g.shape[2]), g.dtype) for g in grads],
        in_specs=[_HBM] * n, out_specs=[_HBM] * n,
        scratch_shapes=[pltpu.SemaphoreType.DMA((n,)), pltpu.SemaphoreType.DMA((n,))],
    )(*grads)


def _add_halves(g, r, c_idx, tag):
    _, rows, w = g.shape
    half = rows // 2
    tr = 256 if half % 256 == 0 else half
    nb = half // tr

    def body(c_ref, g_ref, r_ref, o_ref):
        o_ref[...] = (g_ref[...].astype(F32) + r_ref[...].astype(F32)).astype(BF16)

    return pl.pallas_call(
        body, name=f"add_halves_{tag}",
        grid_spec=pltpu.PrefetchScalarGridSpec(
            num_scalar_prefetch=1, grid=(N_CHIPS, nb),
            in_specs=[pl.BlockSpec((None, tr, w), lambda k, i, c: (k, c[0] * nb + i, 0)),
                      pl.BlockSpec((None, tr, w), lambda k, i, c: (k, i, 0))],
            out_specs=pl.BlockSpec((None, tr, w), lambda k, i, c: (k, i, 0))),
        out_shape=jax.ShapeDtypeStruct((N_CHIPS, half, w), BF16), compiler_params=_params(2),
    )(c_idx, g, r)


def _send_partials(parts):
    n = len(parts)

    def body(*refs):
        srcs, dsts = refs[:n], refs[n:2 * n]
        send_sems, recv_sems = refs[2 * n:]
        x, y, c = _place()
        copies = []
        for s in range(n):
            for j, (fx, fy) in enumerate(_CHIP_FLIPS):
                px, py = _flip(x, fx), _flip(y, fy)
                cp = pltpu.make_async_remote_copy(
                    src_ref=srcs[s].at[2 * px + py], dst_ref=dsts[s].at[j],
                    send_sem=send_sems.at[s * 3 + j], recv_sem=recv_sems.at[s * 3 + j],
                    device_id=(px, py, c), device_id_type=MESH_ID)
                cp.start()
                copies.append(cp)
        for cp in copies:
            cp.wait()

    return pl.pallas_call(
        body, name="send_partials",
        out_shape=[jax.ShapeDtypeStruct((3,) + p.shape[1:], p.dtype) for p in parts],
        in_specs=[_HBM] * n, out_specs=[_HBM] * n,
        scratch_shapes=[pltpu.SemaphoreType.DMA((3 * n,)), pltpu.SemaphoreType.DMA((3 * n,))],
    )(*parts)


def _sum_partials(p, q, chip_idx, c_idx, tag):
    _, half, w = p.shape
    tr = 256 if half % 256 == 0 else half
    nb = half // tr

    def body(k_ref, c_ref, p_ref, q_ref, o_ref):
        acc = p_ref[...].astype(F32)
        for j in range(3):
            acc = acc + q_ref[j].astype(F32)
        o_ref[...] = acc

    return pl.pallas_call(
        body, name=f"sum_partials_{tag}",
        grid_spec=pltpu.PrefetchScalarGridSpec(
            num_scalar_prefetch=2, grid=(nb,),
            in_specs=[pl.BlockSpec((None, tr, w), lambda i, k, c: (k[0], i, 0)),
                      pl.BlockSpec((3, tr, w), lambda i, k, c: (0, i, 0))],
            out_specs=pl.BlockSpec((tr, w), lambda i, k, c: (c[0] * nb + i, 0))),
        out_shape=jax.ShapeDtypeStruct((2 * half, w), F32), compiler_params=_params(1),
    )(chip_idx, c_idx, p, q)


def _join_halves(tots):
    n = len(tots)

    def body(*refs):
        bufs = refs[n:2 * n]
        send_sems, recv_sems = refs[2 * n:]
        x, y, c = _place()
        copies = []
        for s in range(n):
            half = bufs[s].shape[0] // 2
            mine = bufs[s].at[pl.ds(c * half, half)]
            cp = pltpu.make_async_remote_copy(
                src_ref=mine, dst_ref=mine, send_sem=send_sems.at[s], recv_sem=recv_sems.at[s],
                device_id=(x, y, 1 - c), device_id_type=MESH_ID)
            cp.start()
            copies.append(cp)
        for s in range(n):
            half = bufs[s].shape[0] // 2
            theirs = bufs[s].at[pl.ds((1 - c) * half, half)]
            pltpu.make_async_remote_copy(
                src_ref=theirs, dst_ref=theirs, send_sem=send_sems.at[s], recv_sem=recv_sems.at[s],
                device_id=(x, y, c), device_id_type=MESH_ID).wait_recv()
        for cp in copies:
            cp.wait_send()

    return pl.pallas_call(
        body, name="join_halves",
        out_shape=[jax.ShapeDtypeStruct(tt.shape, tt.dtype) for tt in tots],
        in_specs=[_HBM] * n, out_specs=[_HBM] * n, input_output_aliases={s: s for s in range(n)},
        scratch_shapes=[pltpu.SemaphoreType.DMA((n,)), pltpu.SemaphoreType.DMA((n,))],
    )(*tots)


def _adamw_math(w, g, m, v):
    m = ADAM_B1 * m + (1.0 - ADAM_B1) * g
    v = ADAM_B2 * v + (1.0 - ADAM_B2) * jnp.square(g)
    m_hat = m / (1.0 - ADAM_B1 ** ADAM_STEP)
    v_hat = v / (1.0 - ADAM_B2 ** ADAM_STEP)
    delta = -ADAM_LR * (m_hat / (jnp.sqrt(v_hat) + ADAM_EPS) + ADAM_WD * w)
    return delta, m, v


def _adamw_big(g_slab, r0, col, w, m, v, tag):
    shape = w.shape
    wd = shape[-1]
    w2, m2, v2 = (a.reshape(-1, wd) for a in (w, m, v))
    rows = w2.shape[0]
    tr = 256 if rows % 256 == 0 else rows
    assert r0 % tr == 0

    def body(g_ref, w_ref, m_ref, v_ref, go_ref, d_ref, mo_ref, vo_ref):
        g = g_ref[...]
        delta, mn, vn = _adamw_math(w_ref[...], g, m_ref[...], v_ref[...])
        go_ref[...] = g
        d_ref[...] = delta
        mo_ref[...] = mn
        vo_ref[...] = vn

    row = lambda i: (i, 0)
    outs = pl.pallas_call(
        body, name=f"adamw_{tag}", grid=(rows // tr,),
        in_specs=[pl.BlockSpec((tr, wd), lambda i: (r0 // tr + i, col))] + [pl.BlockSpec((tr, wd), row)] * 3,
        out_specs=[pl.BlockSpec((tr, wd), row)] * 4, out_shape=[jax.ShapeDtypeStruct((rows, wd), F32)] * 4,
        compiler_params=_params(1),
    )(g_slab, w2, m2, v2)
    return tuple(o.reshape(shape) for o in outs)


def _adamw_small(items):
    n = len(items)

    def body(*refs):
        ins, outs = refs[:4 * n], refs[4 * n:]
        for k in range(n):
            g_ref, w_ref, m_ref, v_ref = ins[4 * k: 4 * k + 4]
            delta, mn, vn = _adamw_math(w_ref[...], g_ref[...], m_ref[...], v_ref[...])
            outs[3 * k][...] = delta
            outs[3 * k + 1][...] = mn
            outs[3 * k + 2][...] = vn

    flat = [a for it in items for a in it]
    out_shape = [jax.ShapeDtypeStruct(it[1].shape, F32) for it in items for _ in range(3)]
    res = pl.pallas_call(body, name="adamw_small", out_shape=out_shape,
                         compiler_params=pltpu.CompilerParams(vmem_limit_bytes=VMEM_LIMIT_BYTES))(*flat)
    return [tuple(res[3 * k: 3 * k + 3]) for k in range(n)]


def _pack(arrays, width):
    rows, layout, r = [], [], 0
    for a in arrays:
        flat = a.reshape(-1).astype(F32)
        nr = -(-flat.shape[0] // (8 * width)) * 8
        flat = jnp.pad(flat, (0, nr * width - flat.shape[0]))
        rows.append(flat.reshape(nr, width))
        layout.append((r, nr, a.shape))
        r += nr
    return jnp.concatenate(rows, axis=0), layout


def _unpack(packed, layout):
    out = []
    for r, nr, shape in layout:
        size = 1
        for s in shape:
            size *= s
        out.append(packed[r:r + nr].reshape(-1)[:size].reshape(shape))
    return out


_WEIGHTS = ['ffn1_norm', 'ffn1_w1', 'ffn1_w3', 'ffn1_w2', 'mix_norm', 'ffn2_norm', 'ffn2_w1', 'ffn2_w3', 'ffn2_w2',
            'ple_norm', 'ple_w_gate', 'ple_w_proj', 'gmlp_w_in', 'gmlp_ln_g', 'gmlp_ln_b', 'gmlp_w_s', 'gmlp_b_s',
            'gmlp_w_out', 'swa_wq', 'swa_bq', 'swa_wk', 'swa_bk', 'swa_wv', 'swa_bv', 'swa_sinks', 'swa_wo', 'swa_bo',
            'final_norm']
_REPLICATED = ['ffn1_norm', 'mix_norm', 'ffn2_norm', 'ple_norm', 'gmlp_ln_g', 'gmlp_ln_b', 'gmlp_w_s', 'gmlp_b_s',
               'swa_sinks', 'final_norm']
_BIASES = ['swa_bq', 'swa_bk', 'swa_bv', 'swa_bo']


def _as2d(a):
    if a.ndim == 1:
        return a.reshape(1, -1)
    return a.reshape(-1, a.shape[-1])


def _local_step(x, p, target, small, f_a, f_b, f_d, full_bias):
    t, d = x.shape
    fc = f_a.shape[2]
    d4 = d // N_CHIPS
    pdim = p.shape[2]
    n_layers = 2
    blk = small['gmlp_w_s'].shape[2]
    o_w1 = lambda f, i: ((f * 2 + 0) * 2 + i) * d
    o_w3 = lambda f, i: ((f * 2 + 1) * 2 + i) * d
    o_w2 = lambda f, i: (f * 2 + i) * fc
    o_gate = lambda i: 4 * fc + i * d4
    o_in = 4 * fc + 2 * d4
    o_out = o_in + d
    o_q = o_out + 2 * d4
    o_o = o_q + d4
    o_proj = lambda i: i * pdim
    o_kv = 2 * pdim
    bq, bkv, bo = full_bias
    norm = lambda name, i: small[name][i:i + 1]
    ln_g, ln_b = small['gmlp_ln_g'], small['gmlp_ln_b']
    w_s = small['gmlp_w_s'][0]
    b_st = jnp.pad(small['gmlp_b_s'][0].T, ((0, 0), (0, LANES - small['gmlp_b_s'].shape[1])))
    sinks = small['swa_sinks']

    saved = []
    for i in range(n_layers):
        x, s_f1 = _ffn_fwd(x, norm('ffn1_norm', i), f_a, f_b, o_w1(0, i), o_w3(0, i), o_w2(0, i), f"f1l{i}")
        if i == 0:
            x, s_mix = _gmlp_fwd(x, norm('mix_norm', i), ln_g, ln_b, w_s, b_st, f_b, o_in, o_out, f"l{i}")
        else:
            x, s_mix = _swa_fwd(x, norm('mix_norm', i), bq, bkv, bo, sinks, f_b, f_d, o_q, o_o, o_kv, blk, f"l{i}")
        x, s_f2 = _ffn_fwd(x, norm('ffn2_norm', i), f_a, f_b, o_w1(1, i), o_w3(1, i), o_w2(1, i), f"f2l{i}")
        x, s_ple = _ple_fwd(x, p[i], norm('ple_norm', i), f_b, f_d, o_gate(i), o_proj(i), f"l{i}")
        saved.append((s_f1, s_mix, s_f2, s_ple))

    dx, d_final, loss = _loss_head(x, small['final_norm'].reshape(1, d), target)

    gn = {k: [None] * n_layers for k in ('ffn1_norm', 'mix_norm', 'ffn2_norm', 'ple_norm')}
    gw = {}
    for i in reversed(range(n_layers)):
        s_f1, s_mix, s_f2, s_ple = saved[i]
        dx, gn['ple_norm'][i], gw['gate', i], gw['proj', i] = _ple_bwd(
            dx, s_ple, p[i], norm('ple_norm', i), f_b, o_gate(i), f"l{i}")
        dx, gn['ffn2_norm'][i], gw['w1', 1, i], gw['w3', 1, i], gw['w2', 1, i] = _ffn_bwd(
            dx, s_f2, norm('ffn2_norm', i), f_a, f_b, o_w1(1, i), o_w3(1, i), o_w2(1, i), f"f2l{i}")
        if i == 0:
            dx, gn['mix_norm'][i], gw['in'], gw['out'], d_ws, d_bs, d_lg, d_lb = _gmlp_bwd(
                dx, s_mix, norm('mix_norm', i), ln_g, ln_b, w_s, b_st, f_b, o_in, o_out, f"l{i}")
        else:
            dx, gn['mix_norm'][i], gw['q'], gw['kv'], gw['o'], d_bq, d_bkv, d_bo, d_sink = _swa_bwd(
                dx, s_mix, norm('mix_norm', i), sinks, f_b, f_d, o_q, o_o, o_kv, blk, f"l{i}")
        dx, gn['ffn1_norm'][i], gw['w1', 0, i], gw['w3', 0, i], gw['w2', 0, i] = _ffn_bwd(
            dx, s_f1, norm('ffn1_norm', i), f_a, f_b, o_w1(0, i), o_w3(0, i), o_w2(0, i), f"f1l{i}")

    g_a = jnp.concatenate([gw[w, f, i] for f in range(2) for w in ('w1', 'w3') for i in range(n_layers)], axis=1)
    g_b = jnp.concatenate([gw['w2', f, i] for f in range(2) for i in range(n_layers)]
                          + [gw['gate', i] for i in range(n_layers)] + [gw['in'], gw['out'], gw['q'], gw['o']], axis=1)
    g_d = jnp.concatenate([gw['proj', i] for i in range(n_layers)] + [gw['kv']], axis=1)
    kw = d_bkv.shape[1] // 2
    g_small = {
        'ffn1_norm': jnp.concatenate(gn['ffn1_norm'], axis=0), 'mix_norm': jnp.concatenate(gn['mix_norm'], axis=0),
        'ffn2_norm': jnp.concatenate(gn['ffn2_norm'], axis=0), 'ple_norm': jnp.concatenate(gn['ple_norm'], axis=0),
        'gmlp_ln_g': d_lg, 'gmlp_ln_b': d_lb, 'gmlp_w_s': d_ws[None], 'gmlp_b_s': d_bs[None], 'swa_sinks': d_sink,
        'final_norm': d_final.reshape(d), 'swa_bq': d_bq, 'swa_bk': d_bkv[:, :kw], 'swa_bv': d_bkv[:, kw:], 'swa_bo': d_bo,
    }
    return loss, dx, (g_a, g_b, g_d), g_small


def _build_slabs(a):
    d = a['x'].shape[-1]
    fc = a['ffn1_w1'].shape[-1]
    bf = lambda w: w.astype(BF16)
    s_a = jnp.concatenate([bf(a[n]).reshape(-1, fc) for n in ('ffn1_w1', 'ffn1_w3', 'ffn2_w1', 'ffn2_w3')], axis=0)
    s_b = jnp.concatenate([bf(a[n]).reshape(-1, d) for n in
                           ('ffn1_w2', 'ffn2_w2', 'ple_w_gate', 'gmlp_w_in', 'gmlp_w_out', 'swa_wq', 'swa_wo')], axis=0)
    wkv = jnp.concatenate([bf(a['swa_wk'][0]), bf(a['swa_wv'][0])], axis=1)
    s_d = jnp.concatenate([bf(a['ple_w_proj']).reshape(-1, a['ple_w_proj'].shape[-1]), wkv], axis=0)
    return s_a, s_b, s_d


def _reduce_slabs(grads, c_idx, chip_idx):
    sib = _swap_halves(list(grads))
    parts = [_add_halves(g, r, c_idx, f"s{k}") for k, (g, r) in enumerate(zip(grads, sib))]
    recv = _send_partials(parts)
    tots = [_sum_partials(pp, q, chip_idx, c_idx, f"s{k}") for k, (pp, q) in enumerate(zip(parts, recv))]
    return _join_halves(tots)


def kernel(x, p, ffn1_norm, ffn1_w1, ffn1_w3, ffn1_w2, mix_norm, ffn2_norm, ffn2_w1, ffn2_w3, ffn2_w2, ple_norm, ple_w_gate, ple_w_proj, gmlp_w_in, gmlp_ln_g, gmlp_ln_b, gmlp_w_s, gmlp_b_s, gmlp_w_out, swa_wq, swa_bq, swa_wk, swa_bk, swa_wv, swa_bv, swa_sinks, swa_wo, swa_bo, final_norm, loss_target, m_ffn1_norm, m_ffn1_w1, m_ffn1_w3, m_ffn1_w2, m_mix_norm, m_ffn2_norm, m_ffn2_w1, m_ffn2_w3, m_ffn2_w2, m_ple_norm, m_ple_w_gate, m_ple_w_proj, m_gmlp_w_in, m_gmlp_ln_g, m_gmlp_ln_b, m_gmlp_w_s, m_gmlp_b_s, m_gmlp_w_out, m_swa_wq, m_swa_bq, m_swa_wk, m_swa_bk, m_swa_wv, m_swa_bv, m_swa_sinks, m_swa_wo, m_swa_bo, m_final_norm, v_ffn1_norm, v_ffn1_w1, v_ffn1_w3, v_ffn1_w2, v_mix_norm, v_ffn2_norm, v_ffn2_w1, v_ffn2_w3, v_ffn2_w2, v_ple_norm, v_ple_w_gate, v_ple_w_proj, v_gmlp_w_in, v_gmlp_ln_g, v_gmlp_ln_b, v_gmlp_w_s, v_gmlp_b_s, v_gmlp_w_out, v_swa_wq, v_swa_bq, v_swa_wk, v_swa_bk, v_swa_wv, v_swa_bv, v_swa_sinks, v_swa_wo, v_swa_bo, v_final_norm):
    a = dict(locals())
    xi, yi, ci = _place()
    chip = 2 * xi + yi
    c_idx = ci.reshape(1).astype(jnp.int32)
    chip_idx = chip.reshape(1).astype(jnp.int32)
    d = x.shape[-1]
    fc = ffn1_w1.shape[-1]
    d4 = d // N_CHIPS
    pdim = p.shape[-1]
    kw4 = swa_bk.shape[-1]

    f_a, f_b, f_d = _gather_slabs(list(_build_slabs(a)))
    bias_pack, bias_layout = _pack([a[n] for n in _BIASES], d4)
    bias_all = _all_gather_small(bias_pack)

    def full_bias(idx):
        return jnp.concatenate([_unpack(bias_all[2 * k], bias_layout)[idx] for k in range(N_CHIPS)], axis=1)

    bq, bk, bv, bo = (full_bias(i) for i in range(4))
    bkv = jnp.concatenate([bk, bv], axis=1)

    small = {n: a[n] for n in _REPLICATED}
    loss, grad_x, g_slabs, g_small = _local_step(x[0], p[:, 0], loss_target[0], small, f_a, f_b, f_d, (bq, bkv, bo))

    names = _REPLICATED + _BIASES
    packed, layout = _pack([g_small[n] for n in names] + [loss[:, :1]], d4)
    summed = _unpack(_all_sum_small(packed), layout)
    g_sum = dict(zip(names, summed[:-1]))
    loss_out = summed[-1].reshape(())
    for n in _BIASES:
        width = a[n].shape[-1]
        g_sum[n] = lax.dynamic_slice_in_dim(g_sum[n], chip * width, width, axis=1)
    r_a, r_b, r_d = _reduce_slabs(g_slabs, c_idx, chip_idx)

    out = {}
    big = []
    for k, n in enumerate(('ffn1_w1', 'ffn1_w3', 'ffn2_w1', 'ffn2_w3')):
        big.append((n, r_a, k * 2 * d, 0))
    for k, n in enumerate(('ffn1_w2', 'ffn2_w2')):
        big.append((n, r_b, k * 2 * fc, 0))
    off = 4 * fc
    for n, rows in (('ple_w_gate', 2 * d4), ('gmlp_w_in', d), ('gmlp_w_out', gmlp_w_out.shape[1]),
                    ('swa_wq', d4), ('swa_wo', swa_wo.shape[1])):
        big.append((n, r_b, off, 0))
        off += rows
    big.append(('ple_w_proj', r_d, 0, 0))
    big.append(('swa_wk', r_d, 2 * pdim, 0))
    big.append(('swa_wv', r_d, 2 * pdim, 1))
    for n, slab, r0, col in big:
        out[n] = _adamw_big(slab, r0, col, a[n], a['m_' + n], a['v_' + n], n)
    small_names = _REPLICATED + _BIASES
    items = [(_as2d(g_sum[n]), _as2d(a[n]), _as2d(a['m_' + n]), _as2d(a['v_' + n])) for n in small_names]
    for n, (delta, mn, vn) in zip(small_names, _adamw_small(items)):
        shape = a[n].shape
        out[n] = (g_sum[n].reshape(shape), delta.reshape(shape), mn.reshape(shape), vn.reshape(shape))

    return (loss_out, grad_x[None]) + tuple(out[n][j] for j in range(4) for n in _WEIGHTS)
```

```python
import functools

import jax
import jax.numpy as jnp
from jax import lax
from jax.experimental import pallas as pl
from jax.experimental.pallas import tpu as pltpu

F32 = jnp.float32
BF16 = jnp.bfloat16

RMS_EPS = 1e-6
LN_EPS = 1e-5
FFN_RESIDUAL_WEIGHT = 0.5
HEAD_DIM = 64
ROPE_DIM = 16
ROPE_THETA = 500000.0
ADAM_LR = 0.001
ADAM_B1 = 0.9
ADAM_B2 = 0.999
ADAM_EPS = 1e-08
ADAM_WD = 0.01
ADAM_STEP = 10
N_CHIPS = 4
N_DEV = 8
LANES = 128
VMEM_LIMIT_BYTES = 56 * 1024 * 1024
MESH_ID = pl.DeviceIdType.MESH

_DIMS = {
    "nn": (((1,), (0,)), ((), ())),
    "nt": (((1,), (1,)), ((), ())),
    "tn": (((0,), (0,)), ((), ())),
}


def _params(n_axes):
    return pltpu.CompilerParams(dimension_semantics=("arbitrary",) * n_axes, vmem_limit_bytes=VMEM_LIMIT_BYTES)


def _mm(name, grid, pairs, extras, outs, epilogue, acc_shapes, order="ij"):
    ni, nj, nk = grid
    if order == "ij":
        pgrid = (ni, nj, nk)
        ijk = lambda g0, g1, g2: (g0, g1, g2)
    else:
        pgrid = (nj, ni, nk)
        ijk = lambda g0, g1, g2: (g1, g0, g2)
    in_specs, args = [], []
    for a, ablk, amap, b, bblk, bmap, _, _, _ in pairs:
        in_specs.append(pl.BlockSpec(ablk, lambda g0, g1, g2, m=amap: m(*ijk(g0, g1, g2))))
        in_specs.append(pl.BlockSpec(bblk, lambda g0, g1, g2, m=bmap: m(*ijk(g0, g1, g2))))
        args += [a, b]
    for e, eblk, emap in extras:
        in_specs.append(pl.BlockSpec(eblk, lambda g0, g1, g2, m=emap: m(*ijk(g0, g1, g2)[:2])))
        args.append(e)
    out_specs = [pl.BlockSpec(oblk, lambda g0, g1, g2, m=omap: m(*ijk(g0, g1, g2)[:2])) for _, _, oblk, omap in outs]
    out_shape = [jax.ShapeDtypeStruct(s, d) for s, d, _, _ in outs]
    n_p, n_e, n_o = len(pairs), len(extras), len(outs)

    def body(*refs):
        p_refs = refs[: 2 * n_p]
        e_refs = refs[2 * n_p: 2 * n_p + n_e]
        o_refs = refs[2 * n_p + n_e: 2 * n_p + n_e + n_o]
        accs = refs[2 * n_p + n_e + n_o:]
        k = pl.program_id(2)

        @pl.when(k == 0)
        def _():
            for acc in accs:
                acc[...] = jnp.zeros(acc.shape, F32)

        for idx in range(n_p):
            dims, ai, k0 = pairs[idx][6], pairs[idx][7], pairs[idx][8]

            def step(idx=idx, dims=dims, ai=ai):
                a = p_refs[2 * idx][...].astype(BF16)
                b = p_refs[2 * idx + 1][...].astype(BF16)
                accs[ai][...] += lax.dot_general(a, b, _DIMS[dims], preferred_element_type=F32)

            if k0:
                pl.when(k == 0)(step)
            else:
                step()

        @pl.when(k == nk - 1)
        def _():
            vals = epilogue([acc[...] for acc in accs], [e[...] for e in e_refs])
            for o, v in zip(o_refs, vals):
                o[...] = v.astype(o.dtype)

    res = pl.pallas_call(
        body, name=name, grid=pgrid, in_specs=in_specs, out_specs=out_specs, out_shape=out_shape,
        scratch_shapes=[pltpu.VMEM(s, F32) for s in acc_shapes], compiler_params=_params(3),
    )(*args)
    return res


def _col_nn(r0, tk, tn, wc):
    assert r0 % tk == 0 and wc % tn == 0
    npc = wc // tn
    return (None, tk, tn), lambda i, j, k: (j // npc, r0 // tk + k, j % npc)


def _row_nn(r0, tk, tn, rc):
    assert r0 % tk == 0 and rc % tk == 0
    kpc = rc // tk
    return (None, tk, tn), lambda i, j, k: (k // kpc, r0 // tk + k % kpc, j)


def _col_nt(r0, tn, tk, wc):
    assert r0 % tn == 0 and wc % tk == 0
    kpc = wc // tk
    return (None, tn, tk), lambda i, j, k: (k // kpc, r0 // tn + j, k % kpc)


def _row_nt(r0, tn, tk, rc):
    assert r0 % tn == 0 and rc % tn == 0
    npc = rc // tn
    return (None, tn, tk), lambda i, j, k: (j // npc, r0 // tn + j % npc, k)


def _out_col(tm, tn, wc):
    npc = wc // tn
    return (None, tm, tn), lambda i, j: (j // npc, i, j % npc)


def _out_row(tm, tn, rc):
    mpc = rc // tm
    return (None, tm, tn), lambda i, j: (i // mpc, i % mpc, j)


def _ik(i, j, k):
    return (i, k)


def _ki(i, j, k):
    return (k, i)


def _kj(i, j, k):
    return (k, j)


def _i0(i, j, k):
    return (i, 0)


def _ij(i, j):
    return (i, j)


def _0j(i, j):
    return (0, j)


def _i0e(i, j):
    return (i, 0)


def _rows(name, n, ins, outs, body, scratch=(), deps=()):
    in_specs, args = [], []
    for item in ins:
        if len(item) == 2:
            in_specs.append(item[1])
        else:
            in_specs.append(pl.BlockSpec(item[1], item[2]))
        args.append(item[0])
    n_in = len(args)
    for dep in deps:
        in_specs.append(pl.BlockSpec(dep.shape, lambda i: (0, 0)))
        args.append(dep)
    n_dep = len(deps)

    def call_body(*refs):
        body(*refs[:n_in], *refs[n_in + n_dep:])

    out_specs = [pl.BlockSpec(blk, m) for _, _, blk, m in outs]
    out_shape = [jax.ShapeDtypeStruct(s, d) for s, d, _, _ in outs]
    return pl.pallas_call(
        call_body, name=name, grid=(n,), in_specs=in_specs, out_specs=out_specs, out_shape=out_shape,
        scratch_shapes=list(scratch), compiler_params=_params(1),
    )(*args)


def _dep_extra(dep):
    return [] if dep is None else [(dep, dep.shape, lambda i, j: (0, 0))]


def _row(arr, tm):
    return (arr, (tm, arr.shape[1]), lambda i: (i, 0))


def _full(arr):
    nd = arr.ndim
    return (arr, arr.shape, lambda i: (0,) * nd)


def _row_out(shape, dtype, tm):
    return (shape, dtype, (tm, shape[1]), lambda i: (i, 0))


def _acc_out(shape):
    nd = len(shape)
    return (shape, F32, shape, lambda i: (0,) * nd)


def _tm(t):
    return 512 if t >= 1024 else t // 2


def _row_tile(rows, cap=256):
    best = max(tr for tr in range(16, min(rows, cap) + 1, 16) if rows % tr == 0)
    return best


def _first(i, refs):
    @pl.when(i == 0)
    def _():
        for r in refs:
            r[...] = jnp.zeros(r.shape, r.dtype)


def _rms_fwd(x, g, tag):
    t, d = x.shape
    tm = _tm(t)

    def body(x_ref, g_ref, h_ref):
        xv = x_ref[...]
        r = lax.rsqrt(jnp.mean(xv * xv, axis=-1, keepdims=True) + RMS_EPS)
        h_ref[...] = (xv * r * g_ref[...]).astype(BF16)

    return _rows(f"rms_fwd_{tag}", t // tm, [_row(x, tm), _full(g)], [_row_out((t, d), BF16, tm)], body)[0]


def _rms_bwd(dh, x, g, dx_out, tag):
    t, d = x.shape
    tm = _tm(t)

    def body(dh_ref, x_ref, g_ref, dxo_ref, dx_ref, dg_ref):
        i = pl.program_id(0)
        _first(i, [dg_ref])
        xv = x_ref[...]
        r = lax.rsqrt(jnp.mean(xv * xv, axis=-1, keepdims=True) + RMS_EPS)
        xh = xv * r
        dhv = dh_ref[...]
        dxh = dhv * g_ref[...]
        dx_ref[...] = dxo_ref[...] + r * (dxh - xh * jnp.mean(dxh * xh, axis=-1, keepdims=True))
        dg_ref[...] += jnp.sum(dhv * xh, axis=0, keepdims=True)

    return _rows(f"rms_bwd_{tag}", t // tm, [_row(dh, tm), _row(x, tm), _full(g), _row(dx_out, tm)],
                 [_row_out((t, d), F32, tm), _acc_out((1, d))], body)


def _loss_head(x, g, target):
    t, d = x.shape
    tm = _tm(t)

    def body(x_ref, g_ref, t_ref, dx_ref, dg_ref, loss_ref):
        i = pl.program_id(0)
        _first(i, [dg_ref, loss_ref])
        xv = x_ref[...]
        gv = g_ref[...]
        r = lax.rsqrt(jnp.mean(xv * xv, axis=-1, keepdims=True) + RMS_EPS)
        xh = xv * r
        err = xh * gv - t_ref[...]
        loss_ref[...] += jnp.full((1, LANES), 0.5, F32) * jnp.sum(jnp.mean(err * err, axis=-1, keepdims=True))
        dy = err * (1.0 / d)
        dxh = dy * gv
        dx_ref[...] = r * (dxh - xh * jnp.mean(dxh * xh, axis=-1, keepdims=True))
        dg_ref[...] += jnp.sum(dy * xh, axis=0, keepdims=True)

    return _rows("loss_head", t // tm, [_row(x, tm), _full(g), _row(target, tm)],
                 [_row_out((t, d), F32, tm), _acc_out((1, d)), _acc_out((1, LANES))], body)


def _colsum(arr, tag):
    t, w = arr.shape
    tm = _tm(t)

    def body(a_ref, o_ref):
        _first(pl.program_id(0), [o_ref])
        o_ref[...] += jnp.sum(a_ref[...].astype(F32), axis=0, keepdims=True)

    return _rows(f"colsum_{tag}", t // tm, [_row(arr, tm)], [_acc_out((1, w))], body)[0]


def _ffn_fwd(x, gn, w1, w3, w2, tag):
    t, d = x.shape
    fc = w1.shape[2]
    f = N_CHIPS * fc
    tm = _tm(t)
    h = _rms_fwd(x, gn, f"ffn_{tag}")
    w1blk, w1map = _col_nn(0, d, fc, fc)
    w3blk, w3map = _col_nn(0, d, fc, fc)
    tma = min(tm, 256)

    def ep_ab(accs, ex):
        a, b = accs
        return a, b, a * jax.nn.sigmoid(a) * b

    a, b, act = _mm(
        f"ffn_ab_{tag}", (t // tma, N_CHIPS, 1),
        [(h, (tma, d), _ik, w1, w1blk, w1map, "nn", 0, False), (h, (tma, d), _ik, w3, w3blk, w3map, "nn", 1, False)],
        [], [((t, f), BF16, (tma, fc), _ij)] * 3, ep_ab, [(tma, fc)] * 2, order="ji")
    tn = min(d, 1024)
    w2blk, w2map = _row_nn(0, fc, tn, fc)
    x_new = _mm(
        f"ffn_out_{tag}", (t // tm, d // tn, N_CHIPS),
        [(act, (tm, fc), _ik, w2, w2blk, w2map, "nn", 0, False)],
        [(x, (tm, tn), _ij)], [((t, d), F32, (tm, tn), _ij)],
        lambda accs, ex: [ex[0] + FFN_RESIDUAL_WEIGHT * accs[0]], [(tm, tn)])[0]
    return x_new, (x, h, a, b, act)


def _ffn_bwd(dx_out, saved, gn, w1, w3, w2, tag, dep=None):
    x, h, a, b, act = saved
    t, d = x.shape
    fc = w1.shape[2]
    f = N_CHIPS * fc
    tm = _tm(t)
    tma = min(tm, 256)
    w2blk, w2map = _row_nt(0, fc, d, fc)

    def ep_dg(accs, ex):
        dg = FFN_RESIDUAL_WEIGHT * accs[0]
        av, bv = ex[0].astype(F32), ex[1].astype(F32)
        sig = jax.nn.sigmoid(av)
        return dg * bv * sig * (1.0 + av * (1.0 - sig)), dg * av * sig

    da, db = _mm(
        f"ffn_dg_{tag}", (t // tma, N_CHIPS, 1),
        [(dx_out, (tma, d), _ik, w2, w2blk, w2map, "nt", 0, False)],
        [(a, (tma, fc), _ij), (b, (tma, fc), _ij)] + _dep_extra(dep), [((t, f), BF16, (tma, fc), _ij)] * 2, ep_dg, [(tma, fc)], order="ji")

    tmd = min(d, 512)
    tk = min(t, 512)
    oblk, omap = _out_col(tmd, fc, fc)
    dw1, dw3 = _mm(
        f"ffn_dw13_{tag}", (d // tmd, N_CHIPS, t // tk),
        [(h, (tk, tmd), _ki, da, (tk, fc), _kj, "tn", 0, False), (h, (tk, tmd), _ki, db, (tk, fc), _kj, "tn", 1, False)],
        [], [((N_CHIPS, d, fc), BF16, oblk, omap)] * 2, lambda accs, ex: accs, [(tmd, fc)] * 2)
    tn = min(d, 1024)
    tk2 = min(t, 256)
    dw2 = _mm(
        f"ffn_dw2_{tag}", (N_CHIPS, d // tn, t // tk2),
        [(act, (tk2, fc), _ki, dx_out, (tk2, tn), _kj, "tn", 0, False)],
        [], [((N_CHIPS, fc, d), BF16, (None, fc, tn), lambda i, j: (i, 0, j))],
        lambda accs, ex: [FFN_RESIDUAL_WEIGHT * accs[0]], [(fc, tn)])[0]
    w1blk, w1map = _col_nt(0, tn, fc, fc)
    w3blk, w3map = _col_nt(0, tn, fc, fc)
    dh = _mm(
        f"ffn_dh_{tag}", (t // tm, d // tn, N_CHIPS),
        [(da, (tm, fc), _ik, w1, w1blk, w1map, "nt", 0, False), (db, (tm, fc), _ik, w3, w3blk, w3map, "nt", 0, False)],
        [], [((t, d), F32, (tm, tn), _ij)], lambda accs, ex: accs, [(tm, tn)])[0]
    dx, dgn = _rms_bwd(dh, x, gn, dx_out, f"ffn_{tag}")
    return dx, dgn, dw1, dw3, dw2


def _ple_fwd(x, p_i, gn, wg, wp, tag):
    t, d = x.shape
    pdim = p_i.shape[1]
    d4 = d // N_CHIPS
    tm = _tm(t)
    hp = _rms_fwd(x, gn, f"ple_{tag}")
    gblk, gmap = _row_nn(0, d4, d4, d4)

    def ep(accs, ex):
        s = jax.nn.sigmoid(accs[0])
        return ex[0] + s * accs[1], s, accs[1]

    x_new, s, pp = _mm(
        f"ple_fwd_{tag}", (t // tm, N_CHIPS, N_CHIPS),
        [(hp, (tm, d4), _ik, wg, gblk, gmap, "nn", 0, False),
         (p_i, (tm, pdim), _i0, wp, (None, pdim, d4), lambda i, j, k: (j, 0, 0), "nn", 1, True)],
        [(x, (tm, d4), _ij)], [((t, d), F32, (tm, d4), _ij), ((t, d), BF16, (tm, d4), _ij), ((t, d), BF16, (tm, d4), _ij)],
        ep, [(tm, d4)] * 2)
    return x_new, (x, hp, s, pp)


def _ple_bwd(dx_out, saved, p_i, gn, wg, tag, dep=None):
    x, hp, s, pp = saved
    t, d = x.shape
    pdim = p_i.shape[1]
    d4 = d // N_CHIPS
    tm = _tm(t)

    def body(dx_ref, s_ref, pp_ref, dpp_ref, dgp_ref):
        dxv = dx_ref[...]
        sv = s_ref[...].astype(F32)
        ppv = pp_ref[...].astype(F32)
        dpp_ref[...] = (dxv * sv).astype(BF16)
        dgp_ref[...] = (dxv * ppv * sv * (1.0 - sv)).astype(BF16)

    dpp, dgp = _rows(f"ple_ew_{tag}", t // tm, [_row(dx_out, tm), _row(s, tm), _row(pp, tm)],
                     [_row_out((t, d), BF16, tm)] * 2, body, deps=[] if dep is None else [dep])
    tk = min(t, 512)
    dwp = _mm(
        f"ple_dwp_{tag}", (1, N_CHIPS, t // tk),
        [(p_i, (tk, pdim), lambda i, j, k: (k, 0), dpp, (tk, d4), _kj, "tn", 0, False)],
        [], [((N_CHIPS, pdim, d4), BF16, (None, pdim, d4), lambda i, j: (j, 0, 0))], lambda accs, ex: accs, [(pdim, d4)])[0]
    tn = min(d, 1024)
    dwg = _mm(
        f"ple_dwg_{tag}", (N_CHIPS, d // tn, t // tk),
        [(hp, (tk, d4), _ki, dgp, (tk, tn), _kj, "tn", 0, False)],
        [], [((N_CHIPS, d4, d), BF16, (None, d4, tn), lambda i, j: (i, 0, j))], lambda accs, ex: accs, [(d4, tn)])[0]
    gblk, gmap = _row_nt(0, d4, d, d4)
    dhp = _mm(
        f"ple_dh_{tag}", (t // tm, N_CHIPS, 1),
        [(dgp, (tm, d), _ik, wg, gblk, gmap, "nt", 0, False)],
        [], [((t, d), F32, (tm, d4), _ij)], lambda accs, ex: accs, [(tm, d4)], order="ji")[0]
    dx, dgn = _rms_bwd(dhp, x, gn, dx_out, f"ple_{tag}")
    return dx, dgn, dwg, dwp


_SQRT_HALF = 0.7071067811865476
_INV_SQRT_2PI = 0.3989422804014327


def _gelu(z):
    return z * (lax.erf(z * _SQRT_HALF) + 1.0) * 0.5


def _gelu_grad(z):
    return 0.5 * (1.0 + lax.erf(z * _SQRT_HALF)) + z * (_INV_SQRT_2PI * jnp.exp(-0.5 * z * z))


def _causal_bf16(w):
    c = w.shape[0]
    keep = lax.broadcasted_iota(jnp.int32, (c, c), 0) >= lax.broadcasted_iota(jnp.int32, (c, c), 1)
    return jnp.where(keep, w, 0.0).astype(BF16), keep


def _gmlp_gate_fwd(z_pre, ln_g, ln_b, w_s, b_st, tag):
    t, w2 = z_pre.shape
    gw = w2 // 2
    n_g, chunk, _ = w_s.shape
    gd = gw // n_g

    def body(z_ref, g_ref, b_ref, ws_ref, bs_ref, o_ref):
        z = z_ref[...]
        u = _gelu(z[:, :gw])
        zv = _gelu(z[:, gw:])
        mu = jnp.mean(zv, axis=-1, keepdims=True)
        cen = zv - mu
        rstd = lax.rsqrt(jnp.mean(cen * cen, axis=-1, keepdims=True) + LN_EPS)
        vln = (cen * rstd * g_ref[...] + b_ref[...]).astype(BF16)
        bst = bs_ref[...]
        for g in range(n_g):
            wm, _ = _causal_bf16(ws_ref[g])
            sl = slice(g * gd, (g + 1) * gd)
            s = jnp.dot(wm, vln[:, sl], preferred_element_type=F32) + bst[:, g:g + 1]
            o_ref[:, sl] = (u[:, sl] * s).astype(BF16)

    return _rows(f"gmlp_gate_{tag}", t // chunk, [_row(z_pre, chunk), _full(ln_g), _full(ln_b), _full(w_s), _full(b_st)],
                 [_row_out((t, gw), BF16, chunk)], body)[0]


def _gmlp_gate_bwd(z_pre, dgated, ln_g, ln_b, w_s, b_st, tag):
    t, w2 = z_pre.shape
    gw = w2 // 2
    n_g, chunk, _ = w_s.shape
    gd = gw // n_g

    def body(z_ref, dgt_ref, g_ref, b_ref, ws_ref, bs_ref, dz_ref, dws_ref, dbs_ref, dlg_ref, dlb_ref, dv_scr):
        _first(pl.program_id(0), [dws_ref, dbs_ref, dlg_ref, dlb_ref])
        z = z_ref[...]
        zu, zvp = z[:, :gw], z[:, gw:]
        u = _gelu(zu)
        zv = _gelu(zvp)
        mu = jnp.mean(zv, axis=-1, keepdims=True)
        cen = zv - mu
        rstd = lax.rsqrt(jnp.mean(cen * cen, axis=-1, keepdims=True) + LN_EPS)
        vn = cen * rstd
        lg = g_ref[...]
        vln = (vn * lg + b_ref[...]).astype(BF16)
        bst = bs_ref[...]
        lane = lax.broadcasted_iota(jnp.int32, (chunk, LANES), 1)
        dbs = jnp.zeros((chunk, LANES), F32)
        for g in range(n_g):
            wm, keep = _causal_bf16(ws_ref[g])
            sl = slice(g * gd, (g + 1) * gd)
            vg = vln[:, sl]
            s = jnp.dot(wm, vg, preferred_element_type=F32) + bst[:, g:g + 1]
            dgt = dgt_ref[:, sl].astype(F32)
            ds = dgt * u[:, sl]
            ds16 = ds.astype(BF16)
            dz_ref[:, sl] = (dgt * s * _gelu_grad(zu[:, sl])).astype(dz_ref.dtype)
            dv_scr[:, sl] = lax.dot_general(wm, ds16, _DIMS["tn"], preferred_element_type=F32)
            dw = lax.dot_general(ds16, vg, _DIMS["nt"], preferred_element_type=F32)
            dws_ref[g] += jnp.where(keep, dw, 0.0)
            dbs = dbs + jnp.where(lane == g, jnp.sum(ds, axis=-1, keepdims=True), 0.0)
        dbs_ref[...] += dbs
        dvln = dv_scr[...]
        dlg_ref[...] += jnp.sum(dvln * vn, axis=0, keepdims=True)
        dlb_ref[...] += jnp.sum(dvln, axis=0, keepdims=True)
        dvn = dvln * lg
        dzv = rstd * (dvn - jnp.mean(dvn, axis=-1, keepdims=True) - vn * jnp.mean(dvn * vn, axis=-1, keepdims=True))
        dz_ref[:, gw:] = (dzv * _gelu_grad(zvp)).astype(dz_ref.dtype)

    return _rows(
        f"gmlp_gate_bwd_{tag}", t // chunk,
        [_row(z_pre, chunk), _row(dgated, chunk), _full(ln_g), _full(ln_b), _full(w_s), _full(b_st)],
        [_row_out((t, w2), BF16, chunk), _acc_out((n_g, chunk, chunk)), _acc_out((chunk, LANES)), _acc_out((1, gw)),
         _acc_out((1, gw))], body, scratch=[pltpu.VMEM((chunk, gw), F32)])


def _gmlp_fwd(x, gn, ln_g, ln_b, w_s, b_st, w_in, w_out, tag):
    t, d = x.shape
    gw = ln_g.shape[1]
    tm = _tm(t)
    d4 = d // N_CHIPS
    h = _rms_fwd(x, gn, f"mix_{tag}")
    tn = min(d, 1024)
    iblk, imap = _col_nn(0, d4, tn, d)
    z_pre = _mm(
        f"gmlp_in_{tag}", (t // tm, N_CHIPS * d // tn, N_CHIPS),
        [(h, (tm, d4), _ik, w_in, iblk, imap, "nn", 0, False)],
        [], [((t, 2 * gw), F32, (tm, tn), _ij)], lambda accs, ex: accs, [(tm, tn)])[0]
    gated = _gmlp_gate_fwd(z_pre, ln_g, ln_b, w_s, b_st, tag)
    rc = gw // N_CHIPS
    tk = min(rc, 512)
    oblk, omap = _row_nn(0, tk, tn, rc)
    x_new = _mm(
        f"gmlp_out_{tag}", (t // tm, d // tn, gw // tk),
        [(gated, (tm, tk), _ik, w_out, oblk, omap, "nn", 0, False)],
        [(x, (tm, tn), _ij)], [((t, d), F32, (tm, tn), _ij)], lambda accs, ex: [ex[0] + accs[0]], [(tm, tn)])[0]
    return x_new, (x, h, z_pre, gated)


def _gmlp_bwd(dx_out, saved, gn, ln_g, ln_b, w_s, b_st, w_in, w_out, tag, dep=None):
    x, h, z_pre, gated = saved
    t, d = x.shape
    gw = ln_g.shape[1]
    tm = _tm(t)
    d4 = d // N_CHIPS
    rc = gw // N_CHIPS
    tnr = min(rc, 512)
    oblk, omap = _row_nt(0, tnr, d, rc)
    dgated = _mm(
        f"gmlp_dgated_{tag}", (t // tm, gw // tnr, 1),
        [(dx_out, (tm, d), _ik, w_out, oblk, omap, "nt", 0, False)],
        _dep_extra(dep), [((t, gw), BF16, (tm, tnr), _ij)], lambda accs, ex: accs, [(tm, tnr)], order="ji")[0]
    tk = min(t, 512)
    tn = min(d, 1024)
    rblk, rmap = _out_row(tnr, tn, rc)
    dw_out = _mm(
        f"gmlp_dwout_{tag}", (gw // tnr, d // tn, t // tk),
        [(gated, (tk, tnr), _ki, dx_out, (tk, tn), _kj, "tn", 0, False)],
        [], [((N_CHIPS, rc, d), BF16, rblk, rmap)], lambda accs, ex: accs, [(tnr, tn)])[0]
    dz, dws, dbs, dlg, dlb = _gmlp_gate_bwd(z_pre, dgated, ln_g, ln_b, w_s, b_st, tag)
    tmd = min(d, 512)
    cblk, cmap = _out_col(tmd, tn, d)
    dw_in = _mm(
        f"gmlp_dwin_{tag}", (d // tmd, N_CHIPS * d // tn, t // tk),
        [(h, (tk, tmd), _ki, dz, (tk, tn), _kj, "tn", 0, False)],
        [], [((N_CHIPS, d, d), BF16, cblk, cmap)], lambda accs, ex: accs, [(tmd, tn)])[0]
    iblk, imap = _col_nt(0, d4, tn, d)
    dh = _mm(
        f"gmlp_dh_{tag}", (t // tm, N_CHIPS, N_CHIPS * d // tn),
        [(dz, (tm, tn), _ik, w_in, iblk, imap, "nt", 0, False)],
        [], [((t, d), F32, (tm, d4), _ij)], lambda accs, ex: accs, [(tm, d4)])[0]
    dx, dgn = _rms_bwd(dh, x, gn, dx_out, f"mix_{tag}")
    n_g = w_s.shape[0]
    return dx, dgn, dw_in, dw_out, dws, dbs[:, :n_g].T, dlg, dlb


def _rope_tables(t, width, n_rot_heads):
    half = ROPE_DIM // 2
    inv_freq = ROPE_THETA ** (-jnp.arange(0, ROPE_DIM, 2, dtype=F32) / ROPE_DIM)
    ang = jnp.arange(t, dtype=F32)[:, None] * inv_freq[None, :]
    cos, sin = jnp.cos(ang), jnp.sin(ang)
    ones = jnp.ones((t, HEAD_DIM - ROPE_DIM), F32)
    zeros = jnp.zeros((t, HEAD_DIM - ROPE_DIM), F32)
    zh = jnp.zeros((t, half), F32)
    c = jnp.concatenate([cos, cos, ones], axis=1)
    s_next = jnp.concatenate([-sin, zh, zeros], axis=1)
    s_prev = jnp.concatenate([zh, sin, zeros], axis=1)
    rest = width - n_rot_heads * HEAD_DIM

    def widen(tab, fill):
        parts = [jnp.tile(tab, (1, n_rot_heads))]
        if rest:
            parts.append(jnp.full((t, rest), fill, F32))
        return jnp.concatenate(parts, axis=1)

    return widen(c, 1.0), widen(s_next, 0.0), widen(s_prev, 0.0)


def _rope(v, c, s_next, s_prev, sign):
    w = v.shape[1]
    half = ROPE_DIM // 2
    reps = w // c.shape[1]
    if reps > 1:
        c, s_next, s_prev = (jnp.tile(tab, (1, reps)) for tab in (c, s_next, s_prev))
    return v * c + sign * (pltpu.roll(v, w - half, 1) * s_next + pltpu.roll(v, half, 1) * s_prev)


def _scores_mask(n, blk):
    q_pos = lax.broadcasted_iota(jnp.int32, (blk, 2 * blk), 0) + blk
    k_pos = lax.broadcasted_iota(jnp.int32, (blk, 2 * blk), 1)
    diff = q_pos - k_pos
    band = (diff >= 0) & (diff < blk)
    return band & ((k_pos >= blk) | (n > 0))


def _attn_fwd(q, kv, sinks, blk):
    t, qw = q.shape
    kw = kv.shape[1] // 2
    n_kv = kw // HEAD_DIM
    q_per_kv = qw // kw
    scale = HEAD_DIM ** -0.5

    def body(sink_ref, q_ref, kvo_ref, kvp_ref, o_ref):
        n = pl.program_id(0)
        valid = _scores_mask(n, blk)
        qv = q_ref[...]
        kvo = kvo_ref[...]
        kvp = kvp_ref[...]
        for kh in range(n_kv):
            ks = slice(kh * HEAD_DIM, (kh + 1) * HEAD_DIM)
            vs = slice(kw + kh * HEAD_DIM, kw + (kh + 1) * HEAD_DIM)
            kb = jnp.concatenate([kvp[:, ks], kvo[:, ks]], axis=0)
            vb = jnp.concatenate([kvp[:, vs], kvo[:, vs]], axis=0)
            for g in range(q_per_kv):
                hd = kh * q_per_kv + g
                hs = slice(hd * HEAD_DIM, (hd + 1) * HEAD_DIM)
                sink = sink_ref[0, hd]
                s = lax.dot_general(qv[:, hs], kb, _DIMS["nt"], preferred_element_type=F32) * scale
                s = jnp.where(valid, s, -1e30)
                m = jnp.maximum(jnp.max(s, axis=-1, keepdims=True), sink)
                e = jnp.where(valid, jnp.exp(s - m), 0.0)
                denom = jnp.sum(e, axis=-1, keepdims=True) + jnp.exp(sink - m)
                p = (e / denom).astype(BF16)
                o_ref[:, hs] = jnp.dot(p, vb, preferred_element_type=F32).astype(BF16)

    return _rows(
        "swa_attn", t // blk,
        [(sinks, pl.BlockSpec(memory_space=pltpu.SMEM)), _row(q, blk), _row(kv, blk),
         (kv, (blk, kv.shape[1]), lambda i: (jnp.maximum(i - 1, 0), 0))],
        [_row_out((t, qw), BF16, blk)], body)[0]


def _attn_bwd(q, kv, do, sinks, blk):
    t, qw = q.shape
    kw = kv.shape[1] // 2
    n_kv = kw // HEAD_DIM
    q_per_kv = qw // kw
    scale = HEAD_DIM ** -0.5

    def body(sink_ref, q_ref, kvo_ref, kvp_ref, do_ref, dq_ref, dkvo_ref, dkvp_ref, dsink_ref):
        n = pl.program_id(0)
        _first(n, [dsink_ref])
        valid = _scores_mask(n, blk)
        lane = lax.broadcasted_iota(jnp.int32, (1, LANES), 1)
        qv = q_ref[...]
        kvo = kvo_ref[...]
        kvp = kvp_ref[...]
        dov = do_ref[...]
        dsink = jnp.zeros((1, LANES), F32)
        for kh in range(n_kv):
            ks = slice(kh * HEAD_DIM, (kh + 1) * HEAD_DIM)
            vs = slice(kw + kh * HEAD_DIM, kw + (kh + 1) * HEAD_DIM)
            kb = jnp.concatenate([kvp[:, ks], kvo[:, ks]], axis=0)
            vb = jnp.concatenate([kvp[:, vs], kvo[:, vs]], axis=0)
            dkb = jnp.zeros((2 * blk, HEAD_DIM), F32)
            dvb = jnp.zeros((2 * blk, HEAD_DIM), F32)
            for g in range(q_per_kv):
                hd = kh * q_per_kv + g
                hs = slice(hd * HEAD_DIM, (hd + 1) * HEAD_DIM)
                sink = sink_ref[0, hd]
                qh = qv[:, hs]
                doh = dov[:, hs]
                s = lax.dot_general(qh, kb, _DIMS["nt"], preferred_element_type=F32) * scale
                s = jnp.where(valid, s, -1e30)
                m = jnp.maximum(jnp.max(s, axis=-1, keepdims=True), sink)
                e = jnp.where(valid, jnp.exp(s - m), 0.0)
                e_sink = jnp.exp(sink - m)
                inv = 1.0 / (jnp.sum(e, axis=-1, keepdims=True) + e_sink)
                p = e * inv
                p16 = p.astype(BF16)
                dp = lax.dot_general(doh, vb, _DIMS["nt"], preferred_element_type=F32)
                dot_pd = jnp.sum(p * dp, axis=-1, keepdims=True)
                ds16 = (p * (dp - dot_pd)).astype(BF16)
                dsink = dsink + jnp.where(lane == hd, -jnp.sum(e_sink * inv * dot_pd, axis=0, keepdims=True), 0.0)
                dq_ref[:, hs] = (jnp.dot(ds16, kb, preferred_element_type=F32) * scale).astype(BF16)
                dkb = dkb + lax.dot_general(ds16, qh, _DIMS["tn"], preferred_element_type=F32) * scale
                dvb = dvb + lax.dot_general(p16, doh, _DIMS["tn"], preferred_element_type=F32)
            dkvp_ref[:, ks] = dkb[:blk]
            dkvo_ref[:, ks] = dkb[blk:]
            dkvp_ref[:, vs] = dvb[:blk]
            dkvo_ref[:, vs] = dvb[blk:]
        dsink_ref[...] += dsink

    return _rows(
        "swa_attn_bwd", t // blk,
        [(sinks, pl.BlockSpec(memory_space=pltpu.SMEM)), _row(q, blk), _row(kv, blk),
         (kv, (blk, kv.shape[1]), lambda i: (jnp.maximum(i - 1, 0), 0)), _row(do, blk)],
        [_row_out((t, qw), BF16, blk), _row_out((t, 2 * kw), F32, blk), _row_out((t, 2 * kw), F32, blk),
         _acc_out((1, LANES))], body)


def _rope_bwd(dq_r, dkv_own, dkv_prev, tabs_q, tabs_kv, blk):
    t, qw = dq_r.shape
    kvw = dkv_own.shape[1]
    nb = t // blk

    def body(dq_ref, own_ref, nxt_ref, cq, snq, spq, ck, snk, spk, dqo_ref, dkvo_ref, dbq_ref, dbkv_ref):
        i = pl.program_id(0)
        _first(i, [dbq_ref, dbkv_ref])
        dq = _rope(dq_ref[...].astype(F32), cq[...], snq[...], spq[...], -1.0)
        dkv = own_ref[...] + jnp.where(i < nb - 1, nxt_ref[...], 0.0)
        dkv = _rope(dkv, ck[...], snk[...], spk[...], -1.0)
        dqo_ref[...] = dq.astype(BF16)
        dkvo_ref[...] = dkv.astype(BF16)
        dbq_ref[...] += jnp.sum(dq, axis=0, keepdims=True)
        dbkv_ref[...] += jnp.sum(dkv, axis=0, keepdims=True)

    return _rows(
        "swa_rope_bwd", nb,
        [_row(dq_r, blk), _row(dkv_own, blk), (dkv_prev, (blk, kvw), lambda i: (jnp.minimum(i + 1, nb - 1), 0))]
        + [_row(tab, blk) for tab in tabs_q] + [_row(tab, blk) for tab in tabs_kv],
        [_row_out((t, qw), BF16, blk), _row_out((t, kvw), BF16, blk), _acc_out((1, qw)), _acc_out((1, kvw))], body)


def _swa_fwd(x, gn, bq, bkv, bo, sinks, wq, wkv, wo, blk, tag):
    t, d = x.shape
    qw = bq.shape[1]
    kvw = bkv.shape[1]
    d4 = d // N_CHIPS
    tm = _tm(t)
    h = _rms_fwd(x, gn, f"mix_{tag}")
    tabs_q = _rope_tables(t, LANES, LANES // HEAD_DIM)
    tabs_kv = _rope_tables(t, kvw, kvw // 2 // HEAD_DIM)
    tn = min(qw, 1024)
    qblk, qmap = _row_nn(0, d4, tn, d4)

    def ep_rope(accs, ex):
        return [_rope(accs[0] + ex[0], ex[1], ex[2], ex[3], 1.0)]

    q = _mm(
        f"swa_q_{tag}", (t // tm, qw // tn, N_CHIPS),
        [(h, (tm, d4), _ik, wq, qblk, qmap, "nn", 0, False)],
        [(bq, (1, tn), _0j)] + [(tab, (tm, LANES), _i0e) for tab in tabs_q],
        [((t, qw), BF16, (tm, tn), _ij)], ep_rope, [(tm, tn)])[0]
    kv = _mm(
        f"swa_kv_{tag}", (t // tm, 1, N_CHIPS),
        [(h, (tm, d4), _ik, wkv, (None, d4, kvw), lambda i, j, k: (k, 0, 0), "nn", 0, False)],
        [(bkv, (1, kvw), _0j)] + [(tab, (tm, kvw), _i0e) for tab in tabs_kv],
        [((t, kvw), BF16, (tm, kvw), _ij)], ep_rope, [(tm, kvw)])[0]
    o = _attn_fwd(q, kv, sinks, blk)
    tno = min(d, 1024)
    oblk, omap = _row_nn(0, d4, tno, d4)
    x_new = _mm(
        f"swa_out_{tag}", (t // tm, d // tno, N_CHIPS),
        [(o, (tm, d4), _ik, wo, oblk, omap, "nn", 0, False)],
        [(x, (tm, tno), _ij), (bo, (1, tno), _0j)], [((t, d), F32, (tm, tno), _ij)],
        lambda accs, ex: [ex[0] + accs[0] + ex[1]], [(tm, tno)])[0]
    return x_new, (x, h, q, kv, o, tabs_q, tabs_kv)


def _swa_bwd(dx_out, saved, gn, sinks, wq, wkv, wo, blk, tag, dep=None):
    x, h, q, kv, o, tabs_q, tabs_kv = saved
    t, d = x.shape
    qw = q.shape[1]
    kvw = kv.shape[1]
    d4 = d // N_CHIPS
    qw4 = qw // N_CHIPS
    tm = _tm(t)
    oblk, omap = _row_nt(0, qw4, d, qw4)
    do = _mm(
        f"swa_do_{tag}", (t // tm, N_CHIPS, 1),
        [(dx_out, (tm, d), _ik, wo, oblk, omap, "nt", 0, False)],
        _dep_extra(dep), [((t, qw), BF16, (tm, qw4), _ij)], lambda accs, ex: accs, [(tm, qw4)], order="ji")[0]
    tk = min(t, 512)
    tn = min(d, 1024)
    dwo = _mm(
        f"swa_dwo_{tag}", (N_CHIPS, d // tn, t // tk),
        [(o, (tk, qw4), _ki, dx_out, (tk, tn), _kj, "tn", 0, False)],
        [], [((N_CHIPS, qw4, d), BF16, (None, qw4, tn), lambda i, j: (i, 0, j))], lambda accs, ex: accs, [(qw4, tn)])[0]
    dbo = _colsum(dx_out, f"bo_{tag}")
    dq_r, dkv_own, dkv_prev, dsink = _attn_bwd(q, kv, do, sinks, blk)
    dq, dkv, dbq, dbkv = _rope_bwd(dq_r, dkv_own, dkv_prev, tabs_q, tabs_kv, blk)
    tnq = min(qw, 1024)
    dwq = _mm(
        f"swa_dwq_{tag}", (N_CHIPS, qw // tnq, t // tk),
        [(h, (tk, d4), _ki, dq, (tk, tnq), _kj, "tn", 0, False)],
        [], [((N_CHIPS, d4, qw), BF16, (None, d4, tnq), lambda i, j: (i, 0, j))], lambda accs, ex: accs, [(d4, tnq)])[0]
    dwkv = _mm(
        f"swa_dwkv_{tag}", (N_CHIPS, 1, t // tk),
        [(h, (tk, d4), _ki, dkv, (tk, kvw), _kj, "tn", 0, False)],
        [], [((N_CHIPS, d4, kvw), BF16, (None, d4, kvw), lambda i, j: (i, 0, j))], lambda accs, ex: accs, [(d4, kvw)])[0]
    qblk, qmap = _row_nt(0, d4, qw, d4)
    dh = _mm(
        f"swa_dh_{tag}", (t // tm, N_CHIPS, 1),
        [(dq, (tm, qw), _ik, wq, qblk, qmap, "nt", 0, False),
         (dkv, (tm, kvw), _ik, wkv, (None, d4, kvw), lambda i, j, k: (j, 0, 0), "nt", 0, False)],
        [], [((t, d), F32, (tm, d4), _ij)], lambda accs, ex: accs, [(tm, d4)], order="ji")[0]
    dx, dgn = _rms_bwd(dh, x, gn, dx_out, f"mix_{tag}")
    n_heads = qw // HEAD_DIM
    return dx, dgn, dwq, dwkv, dwo, dbq, dbkv, dbo, dsink[:, :n_heads]


_HBM = pl.BlockSpec(memory_space=pl.ANY)
_CHIP_FLIPS = ((1, 0), (0, 1), (1, 1))


def _place():
    x, y, c = lax.axis_index("x"), lax.axis_index("y"), lax.axis_index("c")
    return x, y, c


def _flip(v, bit):
    return 1 - v if bit else v


def _exchange_small(v_ref, all_ref, send_sems, recv_sems):
    x, y, c = _place()
    me = 4 * x + 2 * y + c
    all_ref[me] = v_ref[...]
    copies = []
    for dlt in range(1, N_DEV):
        peer = (_flip(x, dlt & 4), _flip(y, dlt & 2), _flip(c, dlt & 1))
        copies.append(pltpu.make_async_remote_copy(
            src_ref=v_ref, dst_ref=all_ref.at[me], send_sem=send_sems.at[dlt - 1], recv_sem=recv_sems.at[dlt - 1],
            device_id=peer, device_id_type=MESH_ID))
    for cp in copies:
        cp.start()
    for cp in copies:
        cp.wait()


def _all_gather_small(v):
    r, cdim = v.shape

    def body(v_ref, out_ref, send_sems, recv_sems):
        _exchange_small(v_ref, out_ref, send_sems, recv_sems)

    return pl.pallas_call(
        body, name="all_gather_small", out_shape=jax.ShapeDtypeStruct((N_DEV, r, cdim), v.dtype),
        scratch_shapes=[pltpu.SemaphoreType.DMA((N_DEV - 1,)), pltpu.SemaphoreType.DMA((N_DEV - 1,))],
        compiler_params=pltpu.CompilerParams(vmem_limit_bytes=VMEM_LIMIT_BYTES),
    )(v)


def _all_sum_small(v):
    r, cdim = v.shape

    def body(v_ref, out_ref, all_ref, send_sems, recv_sems):
        _exchange_small(v_ref, all_ref, send_sems, recv_sems)
        acc = all_ref[0]
        for dv in range(1, N_DEV):
            acc = acc + all_ref[dv]
        out_ref[...] = acc

    return pl.pallas_call(
        body, name="all_sum_small", out_shape=jax.ShapeDtypeStruct((r, cdim), v.dtype),
        scratch_shapes=[pltpu.VMEM((N_DEV, r, cdim), v.dtype), pltpu.SemaphoreType.DMA((N_DEV - 1,)),
                        pltpu.SemaphoreType.DMA((N_DEV - 1,))],
        compiler_params=pltpu.CompilerParams(vmem_limit_bytes=VMEM_LIMIT_BYTES),
    )(v)


_HBM_SPEC = pl.BlockSpec(memory_space=pltpu.HBM)
_SEM_SPEC = pl.BlockSpec(memory_space=pltpu.SEMAPHORE)
_EFFECT = pltpu.SideEffectType.DATAFLOW_SIDE_EFFECTING
_TOKEN = jax.ShapeDtypeStruct((8, LANES), F32)


def _in_hbm(a):
    return pltpu.with_memory_space_constraint(a, pltpu.HBM)


def _hbm_like(a):
    return pltpu.HBM(a.shape, a.dtype)


def _ici_copies(shard, land, send_sems, recv_sems, base, x, y, c):
    half = shard.shape[0] // 2
    rows = pl.ds(c * half, half)
    return [pltpu.make_async_remote_copy(
        src_ref=shard.at[rows], dst_ref=land.at[2 * x + y, rows], send_sem=send_sems.at[base + j], recv_sem=recv_sems.at[base + j],
        device_id=(_flip(x, fx), _flip(y, fy), c), device_id_type=MESH_ID) for j, (fx, fy) in enumerate(_CHIP_FLIPS)]


def _d2d_copies(land, send_sems, recv_sems, base, x, y, c, c_rows):
    half = land.shape[1] // 2
    rows = pl.ds(c_rows * half, half)
    out = []
    for j, (fx, fy) in enumerate(_CHIP_FLIPS):
        piece = land.at[2 * _flip(x, fx) + _flip(y, fy), rows]
        out.append(pltpu.make_async_remote_copy(
            src_ref=piece, dst_ref=piece, send_sem=send_sems.at[base + j], recv_sem=recv_sems.at[base + j],
            device_id=(x, y, 1 - c), device_id_type=MESH_ID))
    return out


def _own_copy(shard, land, send_sems, recv_sems, s, x, y, c):
    return pltpu.make_async_remote_copy(
        src_ref=shard, dst_ref=land.at[2 * x + y], send_sem=send_sems.at[s], recv_sem=recv_sems.at[s],
        device_id=(x, y, 1 - c), device_id_type=MESH_ID)


def _gather_start(groups):
    flat = [s for grp in groups for s in grp]
    n, n_g = len(flat), len(groups)
    lands = [lax.empty((N_CHIPS,) + s.shape, s.dtype) for s in flat]

    def body(*refs):
        shards, land_refs = refs[:n], refs[n:2 * n]
        sems = refs[2 * n: 2 * n + 4 * n_g]
        token = refs[-1]
        x, y, c = _place()
        idx = 0
        for g, grp in enumerate(groups):
            ici_send, ici_recv, own_send, own_recv = sems[4 * g: 4 * g + 4]
            for s in range(len(grp)):
                _own_copy(shards[idx], land_refs[idx], own_send, own_recv, s, x, y, c).start()
                for cp in _ici_copies(shards[idx], land_refs[idx], ici_send, ici_recv, 3 * s, x, y, c):
                    cp.start()
                idx += 1
        token[...] = jnp.zeros(token.shape, F32)

    sem_shapes = []
    for grp in groups:
        k = len(grp)
        sem_shapes += [pltpu.SemaphoreType.DMA((3 * k,)), pltpu.SemaphoreType.DMA((3 * k,)),
                       pltpu.SemaphoreType.DMA((k,)), pltpu.SemaphoreType.DMA((k,))]
    res = pl.pallas_call(
        body, name="gather_start",
        out_shape=sem_shapes + [_hbm_like(s) for s in flat] + [_hbm_like(ld) for ld in lands] + [_TOKEN],
        in_specs=[_HBM_SPEC] * (2 * n),
        out_specs=[_SEM_SPEC] * (4 * n_g) + [_HBM_SPEC] * (2 * n) + [pl.BlockSpec(memory_space=pltpu.VMEM)],
        input_output_aliases={i: 4 * n_g + i for i in range(2 * n)},
        compiler_params=pltpu.CompilerParams(has_side_effects=_EFFECT),
    )(*[_in_hbm(s) for s in flat], *[_in_hbm(ld) for ld in lands])
    sems, thru, token = res[:4 * n_g], res[4 * n_g: 4 * n_g + 2 * n], res[-1]
    out, idx = [], 0
    for g, grp in enumerate(groups):
        k = len(grp)
        out.append(dict(sems=tuple(sems[4 * g: 4 * g + 4]), shards=list(thru[idx: idx + k]),
                        lands=list(thru[n + idx: n + idx + k])))
        idx += k
    return out, token


def _gather_step(name, landed, arriving, after):
    n_l = len(landed["lands"]) if landed else 0
    n_a = len(arriving["lands"]) if arriving else 0

    def body(*refs):
        pos = 0
        l_lands = refs[pos: pos + n_l]; pos += n_l
        l_sems = refs[pos: pos + (2 if landed else 0)]; pos += 2 if landed else 0
        a_shards = refs[pos: pos + n_a]; pos += n_a
        a_lands = refs[pos: pos + n_a]; pos += n_a
        a_sems = refs[pos: pos + (4 if arriving else 0)]; pos += 4 if arriving else 0
        pos += 1
        pos += n_l + n_a
        new_sems = refs[pos: pos + (2 if arriving else 0)]
        x, y, c = _place()
        if arriving:
            ici_send, ici_recv, own_send, own_recv = a_sems
            started = []
            for s in range(n_a):
                own = _own_copy(a_shards[s], a_lands[s], own_send, own_recv, s, x, y, c)
                own.wait_recv()
                for cp in _d2d_copies(a_lands[s], ici_send, ici_recv, 3 * s, x, y, c, c):
                    cp.wait_recv()
                for cp in _d2d_copies(a_lands[s], new_sems[0], new_sems[1], 3 * s, x, y, c, c):
                    cp.start()
                started.append(own)
                started += _ici_copies(a_shards[s], a_lands[s], ici_send, ici_recv, 3 * s, x, y, c)
            for cp in started:
                cp.wait_send()
        if landed:
            for s in range(n_l):
                for cp in _d2d_copies(l_lands[s], l_sems[0], l_sems[1], 3 * s, x, y, c, c):
                    cp.wait_send()
                for cp in _d2d_copies(l_lands[s], l_sems[0], l_sems[1], 3 * s, x, y, c, 1 - c):
                    cp.wait_recv()

    args, in_specs = [], []
    if landed:
        args += [_in_hbm(a) for a in landed["lands"]] + list(landed["d2d"])
        in_specs += [_HBM_SPEC] * n_l + [_SEM_SPEC] * 2
    if arriving:
        args += [_in_hbm(a) for a in arriving["shards"]] + [_in_hbm(a) for a in arriving["lands"]] + list(arriving["sems"])
        in_specs += [_HBM_SPEC] * (2 * n_a) + [_SEM_SPEC] * 4
    args.append(after)
    in_specs.append(pl.BlockSpec(memory_space=pl.ANY))
    out_shape, out_specs, aliases = [], [], {}
    if landed:
        out_shape += [_hbm_like(a) for a in landed["lands"]]
        for s in range(n_l):
            aliases[s] = s
    if arriving:
        first = (n_l + 2 if landed else 0) + n_a
        for s in range(n_a):
            aliases[first + s] = n_l + s
        out_shape += [_hbm_like(a) for a in arriving["lands"]]
    out_specs += [_HBM_SPEC] * (n_l + n_a)
    if arriving:
        out_shape += [pltpu.SemaphoreType.DMA((3 * n_a,)), pltpu.SemaphoreType.DMA((3 * n_a,))]
        out_specs += [_SEM_SPEC] * 2
    res = pl.pallas_call(
        body, name=name, out_shape=out_shape, in_specs=in_specs, out_specs=out_specs, input_output_aliases=aliases,
        compiler_params=pltpu.CompilerParams(has_side_effects=_EFFECT),
    )(*args)
    done = list(res[:n_l]) if landed else None
    nxt = None
    if arriving:
        nxt = dict(lands=list(res[n_l: n_l + n_a]), d2d=tuple(res[n_l + n_a: n_l + n_a + 2]))
    return done, nxt


def _swap_halves(grads, tag):
    n = len(grads)

    def body(*refs):
        srcs, dsts = refs[:n], refs[n:2 * n]
        send_sems, recv_sems = refs[2 * n:]
        x, y, c = _place()
        copies = []
        for s in range(n):
            half = srcs[s].shape[1] // 2
            cp = pltpu.make_async_remote_copy(
                src_ref=srcs[s].at[:, pl.ds((1 - c) * half, half)], dst_ref=dsts[s],
                send_sem=send_sems.at[s], recv_sem=recv_sems.at[s], device_id=(x, y, 1 - c), device_id_type=MESH_ID)
            cp.start()
            copies.append(cp)
        for cp in copies:
            cp.wait()

    return pl.pallas_call(
        body, name=f"swap_halves_{tag}",
        out_shape=[jax.ShapeDtypeStruct((N_CHIPS, g.shape[1] // 2, g.shape[2]), g.dtype) for g in grads],
        in_specs=[_HBM] * n, out_specs=[_HBM] * n,
        scratch_shapes=[pltpu.SemaphoreType.DMA((n,)), pltpu.SemaphoreType.DMA((n,))],
    )(*grads)


def _add_halves(g, r, c_idx, tag):
    _, rows, w = g.shape
    half = rows // 2
    tr = _row_tile(half)
    nb = half // tr

    def body(c_ref, g_ref, r_ref, o_ref):
        o_ref[...] = (g_ref[...].astype(F32) + r_ref[...].astype(F32)).astype(BF16)

    return pl.pallas_call(
        body, name=f"add_halves_{tag}",
        grid_spec=pltpu.PrefetchScalarGridSpec(
            num_scalar_prefetch=1, grid=(N_CHIPS, nb),
            in_specs=[pl.BlockSpec((None, tr, w), lambda k, i, c: (k, c[0] * nb + i, 0)),
                      pl.BlockSpec((None, tr, w), lambda k, i, c: (k, i, 0))],
            out_specs=pl.BlockSpec((None, tr, w), lambda k, i, c: (k, i, 0))),
        out_shape=jax.ShapeDtypeStruct((N_CHIPS, half, w), BF16), compiler_params=_params(2),
    )(c_idx, g, r)


def _partial_copies(part, land, send_sems, recv_sems, base, x, y, c):
    out = []
    for j, (fx, fy) in enumerate(_CHIP_FLIPS):
        px, py = _flip(x, fx), _flip(y, fy)
        out.append(pltpu.make_async_remote_copy(
            src_ref=part.at[2 * px + py], dst_ref=land.at[j], send_sem=send_sems.at[base + j], recv_sem=recv_sems.at[base + j],
            device_id=(px, py, c), device_id_type=MESH_ID))
    return out


def _send_start(tag, parts):
    n = len(parts)
    lands = [lax.empty((3,) + p.shape[1:], p.dtype) for p in parts]

    def body(*refs):
        srcs, dsts = refs[:n], refs[n:2 * n]
        send_sems, recv_sems = refs[2 * n: 2 * n + 2]
        token = refs[-1]
        x, y, c = _place()
        for s in range(n):
            for cp in _partial_copies(srcs[s], dsts[s], send_sems, recv_sems, 3 * s, x, y, c):
                cp.start()
        token[...] = jnp.zeros(token.shape, F32)

    res = pl.pallas_call(
        body, name=f"send_start_{tag}",
        out_shape=[pltpu.SemaphoreType.DMA((3 * n,)), pltpu.SemaphoreType.DMA((3 * n,))]
        + [_hbm_like(p) for p in parts] + [_hbm_like(ld) for ld in lands] + [_TOKEN],
        in_specs=[_HBM_SPEC] * (2 * n),
        out_specs=[_SEM_SPEC] * 2 + [_HBM_SPEC] * (2 * n) + [pl.BlockSpec(memory_space=pltpu.VMEM)],
        input_output_aliases={i: 2 + i for i in range(2 * n)},
        compiler_params=pltpu.CompilerParams(has_side_effects=_EFFECT),
    )(*[_in_hbm(p) for p in parts], *[_in_hbm(ld) for ld in lands])
    state = dict(sems=tuple(res[:2]), parts=list(res[2: 2 + n]), lands=list(res[2 + n: 2 + 2 * n]))
    return state, res[-1]


def _send_wait(tag, state, after):
    n = len(state["parts"])

    def body(*refs):
        srcs, dsts = refs[:n], refs[n:2 * n]
        send_sems, recv_sems = refs[2 * n: 2 * n + 2]
        x, y, c = _place()
        for s in range(n):
            for cp in _partial_copies(srcs[s], dsts[s], send_sems, recv_sems, 3 * s, x, y, c):
                cp.wait_send()
                cp.wait_recv()

    res = pl.pallas_call(
        body, name=f"send_wait_{tag}",
        out_shape=[_hbm_like(p) for p in state["parts"]] + [_hbm_like(ld) for ld in state["lands"]],
        in_specs=[_HBM_SPEC] * (2 * n) + [_SEM_SPEC] * 2 + [pl.BlockSpec(memory_space=pl.ANY)],
        out_specs=[_HBM_SPEC] * (2 * n), input_output_aliases={i: i for i in range(2 * n)},
        compiler_params=pltpu.CompilerParams(has_side_effects=_EFFECT),
    )(*state["parts"], *state["lands"], *state["sems"], after)
    return list(res[:n]), list(res[n:])


def _sum_partials(p, q, chip_idx, c_idx, tag):
    _, half, w = p.shape
    tr = _row_tile(half)
    nb = half // tr

    def body(k_ref, c_ref, p_ref, q_ref, o_ref):
        acc = p_ref[...].astype(F32)
        for j in range(3):
            acc = acc + q_ref[j].astype(F32)
        o_ref[...] = acc

    return pl.pallas_call(
        body, name=f"sum_partials_{tag}",
        grid_spec=pltpu.PrefetchScalarGridSpec(
            num_scalar_prefetch=2, grid=(nb,),
            in_specs=[pl.BlockSpec((None, tr, w), lambda i, k, c: (k[0], i, 0)),
                      pl.BlockSpec((3, tr, w), lambda i, k, c: (0, i, 0))],
            out_specs=pl.BlockSpec((tr, w), lambda i, k, c: (c[0] * nb + i, 0))),
        out_shape=jax.ShapeDtypeStruct((2 * half, w), F32), compiler_params=_params(1),
    )(chip_idx, c_idx, p, q)


def _join_halves(tots):
    n = len(tots)

    def body(*refs):
        bufs = refs[n:2 * n]
        send_sems, recv_sems = refs[2 * n:]
        x, y, c = _place()
        copies = []
        for s in range(n):
            half = bufs[s].shape[0] // 2
            mine = bufs[s].at[pl.ds(c * half, half)]
            cp = pltpu.make_async_remote_copy(
                src_ref=mine, dst_ref=mine, send_sem=send_sems.at[s], recv_sem=recv_sems.at[s],
                device_id=(x, y, 1 - c), device_id_type=MESH_ID)
            cp.start()
            copies.append(cp)
        for s in range(n):
            half = bufs[s].shape[0] // 2
            theirs = bufs[s].at[pl.ds((1 - c) * half, half)]
            pltpu.make_async_remote_copy(
                src_ref=theirs, dst_ref=theirs, send_sem=send_sems.at[s], recv_sem=recv_sems.at[s],
                device_id=(x, y, c), device_id_type=MESH_ID).wait_recv()
        for cp in copies:
            cp.wait_send()

    return pl.pallas_call(
        body, name="join_halves",
        out_shape=[jax.ShapeDtypeStruct(tt.shape, tt.dtype) for tt in tots],
        in_specs=[_HBM] * n, out_specs=[_HBM] * n, input_output_aliases={s: s for s in range(n)},
        scratch_shapes=[pltpu.SemaphoreType.DMA((n,)), pltpu.SemaphoreType.DMA((n,))],
    )(*tots)


def _adamw_math(w, g, m, v):
    m = ADAM_B1 * m + (1.0 - ADAM_B1) * g
    v = ADAM_B2 * v + (1.0 - ADAM_B2) * jnp.square(g)
    m_hat = m / (1.0 - ADAM_B1 ** ADAM_STEP)
    v_hat = v / (1.0 - ADAM_B2 ** ADAM_STEP)
    delta = -ADAM_LR * (m_hat / (jnp.sqrt(v_hat) + ADAM_EPS) + ADAM_WD * w)
    return delta, m, v


def _adamw_big(g, col, w2, m2, v2, row0, prev, tag):
    rows = g.shape[0]
    rtot, wd = w2.shape
    tr = _row_tile(rows)
    assert row0 % tr == 0
    n_prev = 0 if prev is None else 4

    def body(*refs):
        g_ref, w_ref, m_ref, v_ref = refs[:4]
        go_ref, d_ref, mo_ref, vo_ref = refs[4 + n_prev:]
        gv = g_ref[...]
        delta, mn, vn = _adamw_math(w_ref[...], gv, m_ref[...], v_ref[...])
        go_ref[...] = gv
        d_ref[...] = delta
        mo_ref[...] = mn
        vo_ref[...] = vn

    at = lambda i: (row0 // tr + i, 0)
    return pl.pallas_call(
        body, name=f"adamw_{tag}", grid=(rows // tr,),
        in_specs=[pl.BlockSpec((tr, wd), lambda i: (i, col))] + [pl.BlockSpec((tr, wd), at)] * 3
        + [pl.BlockSpec(memory_space=pl.ANY)] * n_prev,
        out_specs=[pl.BlockSpec((tr, wd), at)] * 4, out_shape=[jax.ShapeDtypeStruct((rtot, wd), F32)] * 4,
        input_output_aliases={4 + k: k for k in range(n_prev)}, compiler_params=_params(1),
    )(g, w2, m2, v2, *(prev or ()))


def _adamw_small(items):
    n = len(items)

    def body(*refs):
        ins, outs = refs[:4 * n], refs[4 * n:]
        for k in range(n):
            g_ref, w_ref, m_ref, v_ref = ins[4 * k: 4 * k + 4]
            delta, mn, vn = _adamw_math(w_ref[...], g_ref[...], m_ref[...], v_ref[...])
            outs[3 * k][...] = delta
            outs[3 * k + 1][...] = mn
            outs[3 * k + 2][...] = vn

    flat = [a for it in items for a in it]
    out_shape = [jax.ShapeDtypeStruct(it[1].shape, F32) for it in items for _ in range(3)]
    res = pl.pallas_call(body, name="adamw_small", out_shape=out_shape,
                         compiler_params=pltpu.CompilerParams(vmem_limit_bytes=VMEM_LIMIT_BYTES))(*flat)
    return [tuple(res[3 * k: 3 * k + 3]) for k in range(n)]


def _pack(arrays, width):
    rows, layout, r = [], [], 0
    for a in arrays:
        flat = a.reshape(-1).astype(F32)
        nr = -(-flat.shape[0] // (8 * width)) * 8
        flat = jnp.pad(flat, (0, nr * width - flat.shape[0]))
        rows.append(flat.reshape(nr, width))
        layout.append((r, nr, a.shape))
        r += nr
    return jnp.concatenate(rows, axis=0), layout


def _unpack(packed, layout):
    out = []
    for r, nr, shape in layout:
        size = 1
        for s in shape:
            size *= s
        out.append(packed[r:r + nr].reshape(-1)[:size].reshape(shape))
    return out


_WEIGHTS = ['ffn1_norm', 'ffn1_w1', 'ffn1_w3', 'ffn1_w2', 'mix_norm', 'ffn2_norm', 'ffn2_w1', 'ffn2_w3', 'ffn2_w2',
            'ple_norm', 'ple_w_gate', 'ple_w_proj', 'gmlp_w_in', 'gmlp_ln_g', 'gmlp_ln_b', 'gmlp_w_s', 'gmlp_b_s',
            'gmlp_w_out', 'swa_wq', 'swa_bq', 'swa_wk', 'swa_bk', 'swa_wv', 'swa_bv', 'swa_sinks', 'swa_wo', 'swa_bo',
            'final_norm']
_REPLICATED = ['ffn1_norm', 'mix_norm', 'ffn2_norm', 'ple_norm', 'gmlp_ln_g', 'gmlp_ln_b', 'gmlp_w_s', 'gmlp_b_s',
               'swa_sinks', 'final_norm']
_BIASES = ['swa_bq', 'swa_bk', 'swa_bv', 'swa_bo']


def _as2d(a):
    if a.ndim == 1:
        return a.reshape(1, -1)
    return a.reshape(-1, a.shape[-1])


_GROUPS = [("f1l0", ("w1", "w3", "w2")), ("mix0", ("in", "out")), ("f2l0", ("w1", "w3", "w2")), ("ple0", ("gate", "proj")),
           ("f1l1", ("w1", "w3", "w2")), ("mix1", ("q", "kv", "o")), ("f2l1", ("w1", "w3", "w2")), ("ple1", ("gate", "proj"))]


def _local_step(x, p, target, small, full_bias, get_group, put_group):
    t, d = x.shape
    n_layers = 2
    blk = small['gmlp_w_s'].shape[2]
    bq, bkv, bo = full_bias
    norm = lambda name, i: small[name][i:i + 1]
    ln_g, ln_b = small['gmlp_ln_g'], small['gmlp_ln_b']
    w_s = small['gmlp_w_s'][0]
    b_st = jnp.pad(small['gmlp_b_s'][0].T, ((0, 0), (0, LANES - small['gmlp_b_s'].shape[1])))
    sinks = small['swa_sinks']

    saved, wts = [], {}
    for i in range(n_layers):
        w = wts[f"f1l{i}"] = get_group(f"f1l{i}", x)
        x, s_f1 = _ffn_fwd(x, norm('ffn1_norm', i), w["w1"], w["w3"], w["w2"], f"f1l{i}")
        w = wts[f"mix{i}"] = get_group(f"mix{i}", x)
        if i == 0:
            x, s_mix = _gmlp_fwd(x, norm('mix_norm', i), ln_g, ln_b, w_s, b_st, w["in"], w["out"], f"l{i}")
        else:
            x, s_mix = _swa_fwd(x, norm('mix_norm', i), bq, bkv, bo, sinks, w["q"], w["kv"], w["o"], blk, f"l{i}")
        w = wts[f"f2l{i}"] = get_group(f"f2l{i}", x)
        x, s_f2 = _ffn_fwd(x, norm('ffn2_norm', i), w["w1"], w["w3"], w["w2"], f"f2l{i}")
        w = wts[f"ple{i}"] = get_group(f"ple{i}", x)
        x, s_ple = _ple_fwd(x, p[i], norm('ple_norm', i), w["gate"], w["proj"], f"l{i}")
        saved.append((s_f1, s_mix, s_f2, s_ple))

    dx, d_final, loss = _loss_head(x, small['final_norm'].reshape(1, d), target)

    gn = {k: [None] * n_layers for k in ('ffn1_norm', 'mix_norm', 'ffn2_norm', 'ple_norm')}
    dep = None
    for i in reversed(range(n_layers)):
        s_f1, s_mix, s_f2, s_ple = saved[i]
        w = wts[f"ple{i}"]
        dx, gn['ple_norm'][i], dwg, dwp = _ple_bwd(dx, s_ple, p[i], norm('ple_norm', i), w["gate"], f"l{i}", dep)
        dep = put_group(f"ple{i}", {"gate": dwg, "proj": dwp})
        w = wts[f"f2l{i}"]
        dx, gn['ffn2_norm'][i], dw1, dw3, dw2 = _ffn_bwd(
            dx, s_f2, norm('ffn2_norm', i), w["w1"], w["w3"], w["w2"], f"f2l{i}", dep)
        dep = put_group(f"f2l{i}", {"w1": dw1, "w3": dw3, "w2": dw2})
        w = wts[f"mix{i}"]
        if i == 0:
            dx, gn['mix_norm'][i], dw_in, dw_out, d_ws, d_bs, d_lg, d_lb = _gmlp_bwd(
                dx, s_mix, norm('mix_norm', i), ln_g, ln_b, w_s, b_st, w["in"], w["out"], f"l{i}", dep)
            dep = put_group(f"mix{i}", {"in": dw_in, "out": dw_out})
        else:
            dx, gn['mix_norm'][i], dwq, dwkv, dwo, d_bq, d_bkv, d_bo, d_sink = _swa_bwd(
                dx, s_mix, norm('mix_norm', i), sinks, w["q"], w["kv"], w["o"], blk, f"l{i}", dep)
            dep = put_group(f"mix{i}", {"q": dwq, "kv": dwkv, "o": dwo})
        w = wts[f"f1l{i}"]
        dx, gn['ffn1_norm'][i], dw1, dw3, dw2 = _ffn_bwd(
            dx, s_f1, norm('ffn1_norm', i), w["w1"], w["w3"], w["w2"], f"f1l{i}", dep)
        dep = put_group(f"f1l{i}", {"w1": dw1, "w3": dw3, "w2": dw2})

    kw = d_bkv.shape[1] // 2
    g_small = {
        'ffn1_norm': jnp.concatenate(gn['ffn1_norm'], axis=0), 'mix_norm': jnp.concatenate(gn['mix_norm'], axis=0),
        'ffn2_norm': jnp.concatenate(gn['ffn2_norm'], axis=0), 'ple_norm': jnp.concatenate(gn['ple_norm'], axis=0),
        'gmlp_ln_g': d_lg, 'gmlp_ln_b': d_lb, 'gmlp_w_s': d_ws[None], 'gmlp_b_s': d_bs[None], 'swa_sinks': d_sink,
        'final_norm': d_final.reshape(d), 'swa_bq': d_bq, 'swa_bk': d_bkv[:, :kw], 'swa_bv': d_bkv[:, kw:], 'swa_bo': d_bo,
    }
    return loss, dx, g_small


def _group_shards(a):
    bf = lambda w: w.astype(BF16)
    out = {}
    for i in range(2):
        for f in (1, 2):
            out[f"f{f}l{i}"] = {
                "w1": (bf(a[f"ffn{f}_w1"][i]), [(f"ffn{f}_w1", i, 0)]), "w3": (bf(a[f"ffn{f}_w3"][i]), [(f"ffn{f}_w3", i, 0)]),
                "w2": (bf(a[f"ffn{f}_w2"][i]), [(f"ffn{f}_w2", i, 0)])}
        out[f"ple{i}"] = {"gate": (bf(a["ple_w_gate"][i]), [("ple_w_gate", i, 0)]),
                          "proj": (bf(a["ple_w_proj"][i]), [("ple_w_proj", i, 0)])}
    out["mix0"] = {"in": (bf(a["gmlp_w_in"][0]), [("gmlp_w_in", 0, 0)]), "out": (bf(a["gmlp_w_out"][0]), [("gmlp_w_out", 0, 0)])}
    wkv = jnp.concatenate([bf(a["swa_wk"][0]), bf(a["swa_wv"][0])], axis=1)
    out["mix1"] = {"q": (bf(a["swa_wq"][0]), [("swa_wq", 0, 0)]), "kv": (wkv, [("swa_wk", 0, 0), ("swa_wv", 0, 1)]),
                   "o": (bf(a["swa_wo"][0]), [("swa_wo", 0, 0)])}
    return out


def kernel(x, p, ffn1_norm, ffn1_w1, ffn1_w3, ffn1_w2, mix_norm, ffn2_norm, ffn2_w1, ffn2_w3, ffn2_w2, ple_norm, ple_w_gate, ple_w_proj, gmlp_w_in, gmlp_ln_g, gmlp_ln_b, gmlp_w_s, gmlp_b_s, gmlp_w_out, swa_wq, swa_bq, swa_wk, swa_bk, swa_wv, swa_bv, swa_sinks, swa_wo, swa_bo, final_norm, loss_target, m_ffn1_norm, m_ffn1_w1, m_ffn1_w3, m_ffn1_w2, m_mix_norm, m_ffn2_norm, m_ffn2_w1, m_ffn2_w3, m_ffn2_w2, m_ple_norm, m_ple_w_gate, m_ple_w_proj, m_gmlp_w_in, m_gmlp_ln_g, m_gmlp_ln_b, m_gmlp_w_s, m_gmlp_b_s, m_gmlp_w_out, m_swa_wq, m_swa_bq, m_swa_wk, m_swa_bk, m_swa_wv, m_swa_bv, m_swa_sinks, m_swa_wo, m_swa_bo, m_final_norm, v_ffn1_norm, v_ffn1_w1, v_ffn1_w3, v_ffn1_w2, v_mix_norm, v_ffn2_norm, v_ffn2_w1, v_ffn2_w3, v_ffn2_w2, v_ple_norm, v_ple_w_gate, v_ple_w_proj, v_gmlp_w_in, v_gmlp_ln_g, v_gmlp_ln_b, v_gmlp_w_s, v_gmlp_b_s, v_gmlp_w_out, v_swa_wq, v_swa_bq, v_swa_wk, v_swa_bk, v_swa_wv, v_swa_bv, v_swa_sinks, v_swa_wo, v_swa_bo, v_final_norm):
    a = dict(locals())
    xi, yi, ci = _place()
    chip = 2 * xi + yi
    c_idx = ci.reshape(1).astype(jnp.int32)
    chip_idx = chip.reshape(1).astype(jnp.int32)
    d = x.shape[-1]
    d4 = d // N_CHIPS

    shards = _group_shards(a)
    started, token = _gather_start([[shards[g][part][0] for part in parts] for g, parts in _GROUPS])
    bias_pack, bias_layout = _pack([a[n] for n in _BIASES], d4)
    bias_all = _all_gather_small(bias_pack)

    def full_bias(idx):
        return jnp.concatenate([_unpack(bias_all[2 * k], bias_layout)[idx] for k in range(N_CHIPS)], axis=1)

    bq, bk, bv, bo = (full_bias(i) for i in range(4))
    bkv = jnp.concatenate([bk, bv], axis=1)

    order = [g for g, _ in _GROUPS]
    state = {"handed": _gather_step("gather_step_first", None, started[0], token)[1], "next": 1}

    def get_group(name, after):
        k = state["next"]
        assert order[k - 1] == name
        arriving = started[k] if k < len(order) else None
        done, state["handed"] = _gather_step(f"gather_step_{name}", state["handed"], arriving, after)
        state["next"] = k + 1
        return dict(zip(dict(_GROUPS)[name], done))

    pending = []

    def put_group(name, grads):
        parts = dict(_GROUPS)[name]
        glist = [grads[part] for part in parts]
        sib = _swap_halves(glist, name)
        partial = [_add_halves(g, r, c_idx, f"{name}_{part}") for part, g, r in zip(parts, glist, sib)]
        st, tok = _send_start(name, partial)
        pending.append((name, st))
        return tok

    small = {n: a[n] for n in _REPLICATED}
    loss, grad_x, g_small = _local_step(x[0], p[:, 0], loss_target[0], small, (bq, bkv, bo), get_group, put_group)

    names = _REPLICATED + _BIASES
    packed, layout = _pack([g_small[n] for n in names] + [loss[:, :1]], d4)
    summed = _unpack(_all_sum_small(packed), layout)
    g_sum = dict(zip(names, summed[:-1]))
    loss_out = summed[-1].reshape(())
    for n in _BIASES:
        width = a[n].shape[-1]
        g_sum[n] = lax.dynamic_slice_in_dim(g_sum[n], chip * width, width, axis=1)
    tots, where = [], []
    for name, st in pending:
        mine, recv = _send_wait(name, st, grad_x)
        for part, pm, q in zip(dict(_GROUPS)[name], mine, recv):
            tots.append(_sum_partials(pm, q, chip_idx, c_idx, f"{name}_{part}"))
            where.append(shards[name][part][1])
    tots = _join_halves(tots)

    out, chain = {}, {}
    todo = sorted(((n, layer, col, g) for g, dests in zip(tots, where) for n, layer, col in dests), key=lambda e: (e[0], e[1]))
    for n, layer, col, g in todo:
        wd = a[n].shape[-1]
        rows = a[n].shape[-2]
        w2, m2, v2 = (a[pre + n].reshape(-1, wd) for pre in ('', 'm_', 'v_'))
        chain[n] = _adamw_big(g, col, w2, m2, v2, layer * rows, chain.get(n), f"{n}_{layer}")
    for n, res in chain.items():
        out[n] = tuple(o.reshape(a[n].shape) for o in res)
    small_names = _REPLICATED + _BIASES
    items = [(_as2d(g_sum[n]), _as2d(a[n]), _as2d(a['m_' + n]), _as2d(a['v_' + n])) for n in small_names]
    for n, (delta, mn, vn) in zip(small_names, _adamw_small(items)):
        shape = a[n].shape
        out[n] = (g_sum[n].reshape(shape), delta.reshape(shape), mn.reshape(shape), vn.reshape(shape))

    return (loss_out, grad_x[None]) + tuple(out[n][j] for j in range(4) for n in _WEIGHTS)
```

```python
import functools

import jax
import jax.numpy as jnp
from jax import lax
from jax.experimental import pallas as pl
from jax.experimental.pallas import tpu as pltpu

F32 = jnp.float32
BF16 = jnp.bfloat16

RMS_EPS = 1e-6
LN_EPS = 1e-5
FFN_RESIDUAL_WEIGHT = 0.5
HEAD_DIM = 64
ROPE_DIM = 16
ROPE_THETA = 500000.0
ADAM_LR = 0.001
ADAM_B1 = 0.9
ADAM_B2 = 0.999
ADAM_EPS = 1e-08
ADAM_WD = 0.01
ADAM_STEP = 10
N_CHIPS = 4
N_DEV = 8
LANES = 128
VMEM_LIMIT_BYTES = 56 * 1024 * 1024
MESH_ID = pl.DeviceIdType.MESH

_DIMS = {
    "nn": (((1,), (0,)), ((), ())),
    "nt": (((1,), (1,)), ((), ())),
    "tn": (((0,), (0,)), ((), ())),
}


def _params(n_axes):
    return pltpu.CompilerParams(dimension_semantics=("arbitrary",) * n_axes, vmem_limit_bytes=VMEM_LIMIT_BYTES)


def _mm(name, grid, pairs, extras, outs, epilogue, acc_shapes, order="ij"):
    ni, nj, nk = grid
    if order == "ij":
        pgrid = (ni, nj, nk)
        ijk = lambda g0, g1, g2: (g0, g1, g2)
    else:
        pgrid = (nj, ni, nk)
        ijk = lambda g0, g1, g2: (g1, g0, g2)
    in_specs, args = [], []
    for a, ablk, amap, b, bblk, bmap, _, _, _ in pairs:
        in_specs.append(pl.BlockSpec(ablk, lambda g0, g1, g2, m=amap: m(*ijk(g0, g1, g2))))
        in_specs.append(pl.BlockSpec(bblk, lambda g0, g1, g2, m=bmap: m(*ijk(g0, g1, g2))))
        args += [a, b]
    for e, eblk, emap in extras:
        in_specs.append(pl.BlockSpec(eblk, lambda g0, g1, g2, m=emap: m(*ijk(g0, g1, g2)[:2])))
        args.append(e)
    out_specs = [pl.BlockSpec(oblk, lambda g0, g1, g2, m=omap: m(*ijk(g0, g1, g2)[:2])) for _, _, oblk, omap in outs]
    out_shape = [jax.ShapeDtypeStruct(s, d) for s, d, _, _ in outs]
    n_p, n_e, n_o = len(pairs), len(extras), len(outs)

    def body(*refs):
        p_refs = refs[: 2 * n_p]
        e_refs = refs[2 * n_p: 2 * n_p + n_e]
        o_refs = refs[2 * n_p + n_e: 2 * n_p + n_e + n_o]
        accs = refs[2 * n_p + n_e + n_o:]
        k = pl.program_id(2)

        @pl.when(k == 0)
        def _():
            for acc in accs:
                acc[...] = jnp.zeros(acc.shape, F32)

        for idx in range(n_p):
            dims, ai, k0 = pairs[idx][6], pairs[idx][7], pairs[idx][8]

            def step(idx=idx, dims=dims, ai=ai):
                a = p_refs[2 * idx][...].astype(BF16)
                b = p_refs[2 * idx + 1][...].astype(BF16)
                if b.ndim == 3:
                    b = b.reshape(-1, b.shape[-1])
                accs[ai][...] += lax.dot_general(a, b, _DIMS[dims], preferred_element_type=F32)

            if k0:
                pl.when(k == 0)(step)
            else:
                step()

        @pl.when(k == nk - 1)
        def _():
            vals = epilogue([acc[...] for acc in accs], [e[...] for e in e_refs])
            for o, v in zip(o_refs, vals):
                o[...] = v.astype(o.dtype)

    res = pl.pallas_call(
        body, name=name, grid=pgrid, in_specs=in_specs, out_specs=out_specs, out_shape=out_shape,
        scratch_shapes=[pltpu.VMEM(s, F32) for s in acc_shapes], compiler_params=_params(3),
    )(*args)
    return res


def _col_nn(r0, tk, tn, wc):
    assert r0 % tk == 0 and wc % tn == 0
    npc = wc // tn
    return (None, tk, tn), lambda i, j, k: (j // npc, r0 // tk + k, j % npc)


def _row_nn(r0, tk, tn, rc):
    assert r0 % tk == 0 and rc % tk == 0
    kpc = rc // tk
    return (None, tk, tn), lambda i, j, k: (k // kpc, r0 // tk + k % kpc, j)


def _col_nt(r0, tn, tk, wc):
    assert r0 % tn == 0 and wc % tk == 0
    kpc = wc // tk
    return (None, tn, tk), lambda i, j, k: (k // kpc, r0 // tn + j, k % kpc)


def _row_nt(r0, tn, tk, rc):
    assert r0 % tn == 0 and rc % tn == 0
    npc = rc // tn
    return (None, tn, tk), lambda i, j, k: (j // npc, r0 // tn + j % npc, k)


def _out_col(tm, tn, wc):
    npc = wc // tn
    return (None, tm, tn), lambda i, j: (j // npc, i, j % npc)


def _out_row(tm, tn, rc):
    mpc = rc // tm
    return (None, tm, tn), lambda i, j: (i // mpc, i % mpc, j)


def _ik(i, j, k):
    return (i, k)


def _ki(i, j, k):
    return (k, i)


def _kj(i, j, k):
    return (k, j)


def _i0(i, j, k):
    return (i, 0)


def _ij(i, j):
    return (i, j)


def _0j(i, j):
    return (0, j)


def _i0e(i, j):
    return (i, 0)


def _rows(name, n, ins, outs, body, scratch=(), deps=()):
    in_specs, args = [], []
    for item in ins:
        if len(item) == 2:
            in_specs.append(item[1])
        else:
            in_specs.append(pl.BlockSpec(item[1], item[2]))
        args.append(item[0])
    n_in = len(args)
    for dep in deps:
        in_specs.append(pl.BlockSpec(dep.shape, lambda i: (0, 0)))
        args.append(dep)
    n_dep = len(deps)

    def call_body(*refs):
        body(*refs[:n_in], *refs[n_in + n_dep:])

    out_specs = [pl.BlockSpec(blk, m) for _, _, blk, m in outs]
    out_shape = [jax.ShapeDtypeStruct(s, d) for s, d, _, _ in outs]
    return pl.pallas_call(
        call_body, name=name, grid=(n,), in_specs=in_specs, out_specs=out_specs, out_shape=out_shape,
        scratch_shapes=list(scratch), compiler_params=_params(1),
    )(*args)


def _dep_extra(dep):
    return [] if dep is None else [(dep, dep.shape, lambda i, j: (0, 0))]


def _row(arr, tm):
    return (arr, (tm, arr.shape[1]), lambda i: (i, 0))


def _full(arr):
    nd = arr.ndim
    return (arr, arr.shape, lambda i: (0,) * nd)


def _row_out(shape, dtype, tm):
    return (shape, dtype, (tm, shape[1]), lambda i: (i, 0))


def _acc_out(shape):
    nd = len(shape)
    return (shape, F32, shape, lambda i: (0,) * nd)


def _tm(t):
    return 512 if t >= 1024 else t // 2


def _row_tile(rows, cap=256):
    best = max(tr for tr in range(16, min(rows, cap) + 1, 16) if rows % tr == 0)
    return best


def _first(i, refs):
    @pl.when(i == 0)
    def _():
        for r in refs:
            r[...] = jnp.zeros(r.shape, r.dtype)


def _rms_fwd(x, g, tag):
    t, d = x.shape
    tm = _tm(t)

    def body(x_ref, g_ref, h_ref):
        xv = x_ref[...]
        r = lax.rsqrt(jnp.mean(xv * xv, axis=-1, keepdims=True) + RMS_EPS)
        h_ref[...] = (xv * r * g_ref[...]).astype(BF16)

    return _rows(f"rms_fwd_{tag}", t // tm, [_row(x, tm), _full(g)], [_row_out((t, d), BF16, tm)], body)[0]


def _rms_bwd(dh, x, g, dx_out, tag):
    t, d = x.shape
    tm = _tm(t)

    def body(dh_ref, x_ref, g_ref, dxo_ref, dx_ref, dg_ref):
        i = pl.program_id(0)
        _first(i, [dg_ref])
        xv = x_ref[...]
        r = lax.rsqrt(jnp.mean(xv * xv, axis=-1, keepdims=True) + RMS_EPS)
        xh = xv * r
        dhv = dh_ref[...]
        dxh = dhv * g_ref[...]
        dx_ref[...] = dxo_ref[...] + r * (dxh - xh * jnp.mean(dxh * xh, axis=-1, keepdims=True))
        dg_ref[...] += jnp.sum(dhv * xh, axis=0, keepdims=True)

    return _rows(f"rms_bwd_{tag}", t // tm, [_row(dh, tm), _row(x, tm), _full(g), _row(dx_out, tm)],
                 [_row_out((t, d), F32, tm), _acc_out((1, d))], body)


def _loss_head(x, g, target):
    t, d = x.shape
    tm = _tm(t)

    def body(x_ref, g_ref, t_ref, dx_ref, dg_ref, loss_ref):
        i = pl.program_id(0)
        _first(i, [dg_ref, loss_ref])
        xv = x_ref[...]
        gv = g_ref[...]
        r = lax.rsqrt(jnp.mean(xv * xv, axis=-1, keepdims=True) + RMS_EPS)
        xh = xv * r
        err = xh * gv - t_ref[...]
        loss_ref[...] += jnp.full((1, LANES), 0.5, F32) * jnp.sum(jnp.mean(err * err, axis=-1, keepdims=True))
        dy = err * (1.0 / d)
        dxh = dy * gv
        dx_ref[...] = r * (dxh - xh * jnp.mean(dxh * xh, axis=-1, keepdims=True))
        dg_ref[...] += jnp.sum(dy * xh, axis=0, keepdims=True)

    return _rows("loss_head", t // tm, [_row(x, tm), _full(g), _row(target, tm)],
                 [_row_out((t, d), F32, tm), _acc_out((1, d)), _acc_out((1, LANES))], body)


def _colsum(arr, tag):
    t, w = arr.shape
    tm = _tm(t)

    def body(a_ref, o_ref):
        _first(pl.program_id(0), [o_ref])
        o_ref[...] += jnp.sum(a_ref[...].astype(F32), axis=0, keepdims=True)

    return _rows(f"colsum_{tag}", t // tm, [_row(arr, tm)], [_acc_out((1, w))], body)[0]


def _ffn_fwd(x, gn, w1, w3, get_w2, tag):
    t, d = x.shape
    fc = w1.shape[2]
    f = N_CHIPS * fc
    tm = _tm(t)
    h = _rms_fwd(x, gn, f"ffn_{tag}")
    w1blk, w1map = _col_nn(0, d, fc, fc)
    w3blk, w3map = _col_nn(0, d, fc, fc)
    tma = min(tm, 256)

    def ep_ab(accs, ex):
        a, b = accs
        return a, b, a * jax.nn.sigmoid(a) * b

    a, b, act = _mm(
        f"ffn_ab_{tag}", (t // tma, N_CHIPS, 1),
        [(h, (tma, d), _ik, w1, w1blk, w1map, "nn", 0, False), (h, (tma, d), _ik, w3, w3blk, w3map, "nn", 1, False)],
        [], [((t, f), BF16, (tma, fc), _ij)] * 3, ep_ab, [(tma, fc)] * 2, order="ji")
    w2 = get_w2(act)
    tn = min(d, 1024)
    tmo = min(t, 2 * tm)
    w2blk, w2map = _row_nn(0, fc, tn, fc)
    x_new = _mm(
        f"ffn_out_{tag}", (t // tmo, d // tn, N_CHIPS),
        [(act, (tmo, fc), _ik, w2, w2blk, w2map, "nn", 0, False)],
        [(x, (tmo, tn), _ij)], [((t, d), F32, (tmo, tn), _ij)],
        lambda accs, ex: [ex[0] + FFN_RESIDUAL_WEIGHT * accs[0]], [(tmo, tn)])[0]
    return x_new, (x, h, a, b, act), w2


def _ffn_bwd(dx_out, saved, gn, w1, w3, w2, tag, dep, put):
    x, h, a, b, act = saved
    t, d = x.shape
    fc = w1.shape[2]
    f = N_CHIPS * fc
    tm = _tm(t)
    tma = min(tm, 256)
    w2blk, w2map = _row_nt(0, fc, d, fc)

    def ep_dg(accs, ex):
        dg = FFN_RESIDUAL_WEIGHT * accs[0]
        av, bv = ex[0].astype(F32), ex[1].astype(F32)
        sig = jax.nn.sigmoid(av)
        return dg * bv * sig * (1.0 + av * (1.0 - sig)), dg * av * sig

    da, db = _mm(
        f"ffn_dg_{tag}", (t // tma, N_CHIPS, 1),
        [(dx_out, (tma, d), _ik, w2, w2blk, w2map, "nt", 0, False)],
        [(a, (tma, fc), _ij), (b, (tma, fc), _ij)] + _dep_extra(dep), [((t, f), BF16, (tma, fc), _ij)] * 2, ep_dg, [(tma, fc)], order="ji")

    tmd = min(d, 512)
    tk = min(t, 2048)
    oblk, omap = _out_col(tmd, fc, fc)
    dw1, dw3 = _mm(
        f"ffn_dw13_{tag}", (d // tmd, N_CHIPS, t // tk),
        [(h, (tk, tmd), _ki, da, (tk, fc), _kj, "tn", 0, False), (h, (tk, tmd), _ki, db, (tk, fc), _kj, "tn", 1, False)],
        [], [((N_CHIPS, d, fc), BF16, oblk, omap)] * 2, lambda accs, ex: accs, [(tmd, fc)] * 2)
    tn = min(d, 1024)
    tk2 = min(t, 2048)
    dw2 = _mm(
        f"ffn_dw2_{tag}", (N_CHIPS, d // tn, t // tk2),
        [(act, (tk2, fc), _ki, dx_out, (tk2, tn), _kj, "tn", 0, False)],
        [], [((N_CHIPS, fc, d), BF16, (None, fc, tn), lambda i, j: (i, 0, j))],
        lambda accs, ex: [FFN_RESIDUAL_WEIGHT * accs[0]], [(fc, tn)])[0]
    token = put({"w1": dw1, "w3": dw3, "w2": dw2})
    w1blk, w1map = _col_nt(0, tn, fc, fc)
    w3blk, w3map = _col_nt(0, tn, fc, fc)
    dh = _mm(
        f"ffn_dh_{tag}", (t // tm, d // tn, N_CHIPS),
        [(da, (tm, fc), _ik, w1, w1blk, w1map, "nt", 0, False), (db, (tm, fc), _ik, w3, w3blk, w3map, "nt", 0, False)],
        _dep_extra(token), [((t, d), F32, (tm, tn), _ij)], lambda accs, ex: accs, [(tm, tn)])[0]
    dx, dgn = _rms_bwd(dh, x, gn, dx_out, f"ffn_{tag}")
    return dx, dgn, token


def _ple_fwd(x, p_i, gn, wg, wp, tag):
    t, d = x.shape
    pdim = p_i.shape[1]
    d4 = d // N_CHIPS
    tm = _tm(t)
    hp = _rms_fwd(x, gn, f"ple_{tag}")
    tm = min(t, 2 * tm)

    def ep(accs, ex):
        s = jax.nn.sigmoid(accs[0])
        return ex[0] + s * accs[1], s, accs[1]

    x_new, s, pp = _mm(
        f"ple_fwd_{tag}", (t // tm, N_CHIPS, 1),
        [(hp, (tm, d), _i0, wg, (N_CHIPS, d4, d4), lambda i, j, k: (0, 0, j), "nn", 0, False),
         (p_i, (tm, pdim), _i0, wp, (None, pdim, d4), lambda i, j, k: (j, 0, 0), "nn", 1, False)],
        [(x, (tm, d4), _ij)], [((t, d), F32, (tm, d4), _ij), ((t, d), BF16, (tm, d4), _ij), ((t, d), BF16, (tm, d4), _ij)],
        ep, [(tm, d4)] * 2)
    return x_new, (x, hp, s, pp)


def _ple_bwd(dx_out, saved, p_i, gn, wg, tag, dep=None):
    x, hp, s, pp = saved
    t, d = x.shape
    pdim = p_i.shape[1]
    d4 = d // N_CHIPS
    tm = _tm(t)

    def body(dx_ref, s_ref, pp_ref, dpp_ref, dgp_ref):
        dxv = dx_ref[...]
        sv = s_ref[...].astype(F32)
        ppv = pp_ref[...].astype(F32)
        dpp_ref[...] = (dxv * sv).astype(BF16)
        dgp_ref[...] = (dxv * ppv * sv * (1.0 - sv)).astype(BF16)

    dpp, dgp = _rows(f"ple_ew_{tag}", t // tm, [_row(dx_out, tm), _row(s, tm), _row(pp, tm)],
                     [_row_out((t, d), BF16, tm)] * 2, body, deps=[] if dep is None else [dep])
    tk = min(t, 2048)
    dwp = _mm(
        f"ple_dwp_{tag}", (1, N_CHIPS, t // tk),
        [(p_i, (tk, pdim), lambda i, j, k: (k, 0), dpp, (tk, d4), _kj, "tn", 0, False)],
        [], [((N_CHIPS, pdim, d4), BF16, (None, pdim, d4), lambda i, j: (j, 0, 0))], lambda accs, ex: accs, [(pdim, d4)])[0]
    tn = min(d, 1024)
    dwg = _mm(
        f"ple_dwg_{tag}", (N_CHIPS, d // tn, t // tk),
        [(hp, (tk, d4), _ki, dgp, (tk, tn), _kj, "tn", 0, False)],
        [], [((N_CHIPS, d4, d), BF16, (None, d4, tn), lambda i, j: (i, 0, j))], lambda accs, ex: accs, [(d4, tn)])[0]
    dhp = _mm(
        f"ple_dh_{tag}", (t // tm, 1, 1),
        [(dgp, (tm, d), _ik, wg, (N_CHIPS, d4, d), lambda i, j, k: (0, 0, 0), "nt", 0, False)],
        [], [((t, d), F32, (tm, d), _ij)], lambda accs, ex: accs, [(tm, d)])[0]
    dx, dgn = _rms_bwd(dhp, x, gn, dx_out, f"ple_{tag}")
    return dx, dgn, dwg, dwp


_SQRT_HALF = 0.7071067811865476
_INV_SQRT_2PI = 0.3989422804014327


def _gelu(z):
    return z * (lax.erf(z * _SQRT_HALF) + 1.0) * 0.5


def _gelu_grad(z):
    return 0.5 * (1.0 + lax.erf(z * _SQRT_HALF)) + z * (_INV_SQRT_2PI * jnp.exp(-0.5 * z * z))


def _causal_bf16(w):
    c = w.shape[0]
    keep = lax.broadcasted_iota(jnp.int32, (c, c), 0) >= lax.broadcasted_iota(jnp.int32, (c, c), 1)
    return jnp.where(keep, w, 0.0).astype(BF16), keep


def _gmlp_gate_fwd(z_pre, ln_g, ln_b, w_s, b_st, tag):
    t, w2 = z_pre.shape
    gw = w2 // 2
    n_g, chunk, _ = w_s.shape
    gd = gw // n_g

    def body(z_ref, g_ref, b_ref, ws_ref, bs_ref, o_ref):
        z = z_ref[...]
        u = _gelu(z[:, :gw])
        zv = _gelu(z[:, gw:])
        mu = jnp.mean(zv, axis=-1, keepdims=True)
        cen = zv - mu
        rstd = lax.rsqrt(jnp.mean(cen * cen, axis=-1, keepdims=True) + LN_EPS)
        vln = (cen * rstd * g_ref[...] + b_ref[...]).astype(BF16)
        bst = bs_ref[...]
        for g in range(n_g):
            wm, _ = _causal_bf16(ws_ref[g])
            sl = slice(g * gd, (g + 1) * gd)
            s = jnp.dot(wm, vln[:, sl], preferred_element_type=F32) + bst[:, g:g + 1]
            o_ref[:, sl] = (u[:, sl] * s).astype(BF16)

    return _rows(f"gmlp_gate_{tag}", t // chunk, [_row(z_pre, chunk), _full(ln_g), _full(ln_b), _full(w_s), _full(b_st)],
                 [_row_out((t, gw), BF16, chunk)], body)[0]


def _gmlp_gate_bwd(z_pre, dgated, ln_g, ln_b, w_s, b_st, tag):
    t, w2 = z_pre.shape
    gw = w2 // 2
    n_g, chunk, _ = w_s.shape
    gd = gw // n_g

    def body(z_ref, dgt_ref, g_ref, b_ref, ws_ref, bs_ref, dz_ref, dws_ref, dbs_ref, dlg_ref, dlb_ref, dv_scr):
        _first(pl.program_id(0), [dws_ref, dbs_ref, dlg_ref, dlb_ref])
        z = z_ref[...]
        zu, zvp = z[:, :gw], z[:, gw:]
        u = _gelu(zu)
        zv = _gelu(zvp)
        mu = jnp.mean(zv, axis=-1, keepdims=True)
        cen = zv - mu
        rstd = lax.rsqrt(jnp.mean(cen * cen, axis=-1, keepdims=True) + LN_EPS)
        vn = cen * rstd
        lg = g_ref[...]
        vln = (vn * lg + b_ref[...]).astype(BF16)
        bst = bs_ref[...]
        lane = lax.broadcasted_iota(jnp.int32, (chunk, LANES), 1)
        dbs = jnp.zeros((chunk, LANES), F32)
        for g in range(n_g):
            wm, keep = _causal_bf16(ws_ref[g])
            sl = slice(g * gd, (g + 1) * gd)
            vg = vln[:, sl]
            s = jnp.dot(wm, vg, preferred_element_type=F32) + bst[:, g:g + 1]
            dgt = dgt_ref[:, sl].astype(F32)
            ds = dgt * u[:, sl]
            ds16 = ds.astype(BF16)
            dz_ref[:, sl] = (dgt * s * _gelu_grad(zu[:, sl])).astype(dz_ref.dtype)
            dv_scr[:, sl] = lax.dot_general(wm, ds16, _DIMS["tn"], preferred_element_type=F32)
            dw = lax.dot_general(ds16, vg, _DIMS["nt"], preferred_element_type=F32)
            dws_ref[g] += jnp.where(keep, dw, 0.0)
            dbs = dbs + jnp.where(lane == g, jnp.sum(ds, axis=-1, keepdims=True), 0.0)
        dbs_ref[...] += dbs
        dvln = dv_scr[...]
        dlg_ref[...] += jnp.sum(dvln * vn, axis=0, keepdims=True)
        dlb_ref[...] += jnp.sum(dvln, axis=0, keepdims=True)
        dvn = dvln * lg
        dzv = rstd * (dvn - jnp.mean(dvn, axis=-1, keepdims=True) - vn * jnp.mean(dvn * vn, axis=-1, keepdims=True))
        dz_ref[:, gw:] = (dzv * _gelu_grad(zvp)).astype(dz_ref.dtype)

    return _rows(
        f"gmlp_gate_bwd_{tag}", t // chunk,
        [_row(z_pre, chunk), _row(dgated, chunk), _full(ln_g), _full(ln_b), _full(w_s), _full(b_st)],
        [_row_out((t, w2), BF16, chunk), _acc_out((n_g, chunk, chunk)), _acc_out((chunk, LANES)), _acc_out((1, gw)),
         _acc_out((1, gw))], body, scratch=[pltpu.VMEM((chunk, gw), F32)])


def _gmlp_fwd(x, gn, ln_g, ln_b, w_s, b_st, w_in, w_out, tag):
    t, d = x.shape
    gw = ln_g.shape[1]
    tm = _tm(t)
    d4 = d // N_CHIPS
    h = _rms_fwd(x, gn, f"mix_{tag}")
    tn = min(d, 1024)
    iblk, imap = _col_nn(0, d, tn, d)
    z_pre = _mm(
        f"gmlp_in_{tag}", (t // tm, N_CHIPS * d // tn, 1),
        [(h, (tm, d), _ik, w_in, iblk, imap, "nn", 0, False)],
        [], [((t, 2 * gw), F32, (tm, tn), _ij)], lambda accs, ex: accs, [(tm, tn)], order="ji")[0]
    gated = _gmlp_gate_fwd(z_pre, ln_g, ln_b, w_s, b_st, tag)
    rc = gw // N_CHIPS
    tk = min(rc, 1024)
    tmo = min(t, 2 * tm)
    oblk, omap = _row_nn(0, tk, tn, rc)
    x_new = _mm(
        f"gmlp_out_{tag}", (t // tmo, d // tn, gw // tk),
        [(gated, (tmo, tk), _ik, w_out, oblk, omap, "nn", 0, False)],
        [(x, (tmo, tn), _ij)], [((t, d), F32, (tmo, tn), _ij)], lambda accs, ex: [ex[0] + accs[0]], [(tmo, tn)])[0]
    return x_new, (x, h, z_pre, gated)


def _gmlp_bwd(dx_out, saved, gn, ln_g, ln_b, w_s, b_st, w_in, w_out, tag, dep=None):
    x, h, z_pre, gated = saved
    t, d = x.shape
    gw = ln_g.shape[1]
    tm = _tm(t)
    d4 = d // N_CHIPS
    rc = gw // N_CHIPS
    tnr = min(rc, 1024)
    oblk, omap = _row_nt(0, tnr, d, rc)
    dgated = _mm(
        f"gmlp_dgated_{tag}", (t // tm, gw // tnr, 1),
        [(dx_out, (tm, d), _ik, w_out, oblk, omap, "nt", 0, False)],
        _dep_extra(dep), [((t, gw), BF16, (tm, tnr), _ij)], lambda accs, ex: accs, [(tm, tnr)], order="ji")[0]
    tk = min(t, 2048)
    tn = min(d, 1024)
    rblk, rmap = _out_row(tnr, tn, rc)
    dw_out = _mm(
        f"gmlp_dwout_{tag}", (gw // tnr, d // tn, t // tk),
        [(gated, (tk, tnr), _ki, dx_out, (tk, tn), _kj, "tn", 0, False)],
        [], [((N_CHIPS, rc, d), BF16, rblk, rmap)], lambda accs, ex: accs, [(tnr, tn)])[0]
    dz, dws, dbs, dlg, dlb = _gmlp_gate_bwd(z_pre, dgated, ln_g, ln_b, w_s, b_st, tag)
    tmd = min(d, 512)
    cblk, cmap = _out_col(tmd, tn, d)
    dw_in = _mm(
        f"gmlp_dwin_{tag}", (d // tmd, N_CHIPS * d // tn, t // tk),
        [(h, (tk, tmd), _ki, dz, (tk, tn), _kj, "tn", 0, False)],
        [], [((N_CHIPS, d, d), BF16, cblk, cmap)], lambda accs, ex: accs, [(tmd, tn)])[0]
    iblk, imap = _col_nt(0, tn, d, d)
    dh = _mm(
        f"gmlp_dh_{tag}", (t // tm, d // tn, N_CHIPS),
        [(dz, (tm, d), _ik, w_in, iblk, imap, "nt", 0, False)],
        [], [((t, d), F32, (tm, tn), _ij)], lambda accs, ex: accs, [(tm, tn)])[0]
    dx, dgn = _rms_bwd(dh, x, gn, dx_out, f"mix_{tag}")
    n_g = w_s.shape[0]
    return dx, dgn, dw_in, dw_out, dws, dbs[:, :n_g].T, dlg, dlb


def _rope_tables(t, width, n_rot_heads):
    half = ROPE_DIM // 2
    inv_freq = ROPE_THETA ** (-jnp.arange(0, ROPE_DIM, 2, dtype=F32) / ROPE_DIM)
    ang = jnp.arange(t, dtype=F32)[:, None] * inv_freq[None, :]
    cos, sin = jnp.cos(ang), jnp.sin(ang)
    ones = jnp.ones((t, HEAD_DIM - ROPE_DIM), F32)
    zeros = jnp.zeros((t, HEAD_DIM - ROPE_DIM), F32)
    zh = jnp.zeros((t, half), F32)
    c = jnp.concatenate([cos, cos, ones], axis=1)
    s_next = jnp.concatenate([-sin, zh, zeros], axis=1)
    s_prev = jnp.concatenate([zh, sin, zeros], axis=1)
    rest = width - n_rot_heads * HEAD_DIM

    def widen(tab, fill):
        parts = [jnp.tile(tab, (1, n_rot_heads))]
        if rest:
            parts.append(jnp.full((t, rest), fill, F32))
        return jnp.concatenate(parts, axis=1)

    return widen(c, 1.0), widen(s_next, 0.0), widen(s_prev, 0.0)


def _rope(v, c, s_next, s_prev, sign):
    w = v.shape[1]
    half = ROPE_DIM // 2
    reps = w // c.shape[1]
    if reps > 1:
        c, s_next, s_prev = (jnp.tile(tab, (1, reps)) for tab in (c, s_next, s_prev))
    return v * c + sign * (pltpu.roll(v, w - half, 1) * s_next + pltpu.roll(v, half, 1) * s_prev)


def _scores_mask(n, blk):
    q_pos = lax.broadcasted_iota(jnp.int32, (blk, 2 * blk), 0) + blk
    k_pos = lax.broadcasted_iota(jnp.int32, (blk, 2 * blk), 1)
    diff = q_pos - k_pos
    band = (diff >= 0) & (diff < blk)
    return band & ((k_pos >= blk) | (n > 0))


def _attn_fwd(q, kv, sinks, blk):
    t, qw = q.shape
    kw = kv.shape[1] // 2
    n_kv = kw // HEAD_DIM
    q_per_kv = qw // kw
    scale = HEAD_DIM ** -0.5

    def body(sink_ref, q_ref, kvo_ref, kvp_ref, o_ref):
        n = pl.program_id(0)
        valid = _scores_mask(n, blk)
        qv = q_ref[...]
        kvo = kvo_ref[...]
        kvp = kvp_ref[...]
        for kh in range(n_kv):
            ks = slice(kh * HEAD_DIM, (kh + 1) * HEAD_DIM)
            vs = slice(kw + kh * HEAD_DIM, kw + (kh + 1) * HEAD_DIM)
            kb = jnp.concatenate([kvp[:, ks], kvo[:, ks]], axis=0)
            vb = jnp.concatenate([kvp[:, vs], kvo[:, vs]], axis=0)
            for g in range(q_per_kv):
                hd = kh * q_per_kv + g
                hs = slice(hd * HEAD_DIM, (hd + 1) * HEAD_DIM)
                sink = sink_ref[0, hd]
                s = lax.dot_general(qv[:, hs], kb, _DIMS["nt"], preferred_element_type=F32) * scale
                s = jnp.where(valid, s, -1e30)
                m = jnp.maximum(jnp.max(s, axis=-1, keepdims=True), sink)
                e = jnp.where(valid, jnp.exp(s - m), 0.0)
                denom = jnp.sum(e, axis=-1, keepdims=True) + jnp.exp(sink - m)
                p = (e / denom).astype(BF16)
                o_ref[:, hs] = jnp.dot(p, vb, preferred_element_type=F32).astype(BF16)

    return _rows(
        "swa_attn", t // blk,
        [(sinks, pl.BlockSpec(memory_space=pltpu.SMEM)), _row(q, blk), _row(kv, blk),
         (kv, (blk, kv.shape[1]), lambda i: (jnp.maximum(i - 1, 0), 0))],
        [_row_out((t, qw), BF16, blk)], body)[0]


def _attn_bwd(q, kv, do, sinks, blk):
    t, qw = q.shape
    kw = kv.shape[1] // 2
    n_kv = kw // HEAD_DIM
    q_per_kv = qw // kw
    scale = HEAD_DIM ** -0.5

    def body(sink_ref, q_ref, kvo_ref, kvp_ref, do_ref, dq_ref, dkvo_ref, dkvp_ref, dsink_ref):
        n = pl.program_id(0)
        _first(n, [dsink_ref])
        valid = _scores_mask(n, blk)
        lane = lax.broadcasted_iota(jnp.int32, (1, LANES), 1)
        qv = q_ref[...]
        kvo = kvo_ref[...]
        kvp = kvp_ref[...]
        dov = do_ref[...]
        dsink = jnp.zeros((1, LANES), F32)
        for kh in range(n_kv):
            ks = slice(kh * HEAD_DIM, (kh + 1) * HEAD_DIM)
            vs = slice(kw + kh * HEAD_DIM, kw + (kh + 1) * HEAD_DIM)
            kb = jnp.concatenate([kvp[:, ks], kvo[:, ks]], axis=0)
            vb = jnp.concatenate([kvp[:, vs], kvo[:, vs]], axis=0)
            dkb = jnp.zeros((2 * blk, HEAD_DIM), F32)
            dvb = jnp.zeros((2 * blk, HEAD_DIM), F32)
            for g in range(q_per_kv):
                hd = kh * q_per_kv + g
                hs = slice(hd * HEAD_DIM, (hd + 1) * HEAD_DIM)
                sink = sink_ref[0, hd]
                qh = qv[:, hs]
                doh = dov[:, hs]
                s = lax.dot_general(qh, kb, _DIMS["nt"], preferred_element_type=F32) * scale
                s = jnp.where(valid, s, -1e30)
                m = jnp.maximum(jnp.max(s, axis=-1, keepdims=True), sink)
                e = jnp.where(valid, jnp.exp(s - m), 0.0)
                e_sink = jnp.exp(sink - m)
                inv = 1.0 / (jnp.sum(e, axis=-1, keepdims=True) + e_sink)
                p = e * inv
                p16 = p.astype(BF16)
                dp = lax.dot_general(doh, vb, _DIMS["nt"], preferred_element_type=F32)
                dot_pd = jnp.sum(p * dp, axis=-1, keepdims=True)
                ds16 = (p * (dp - dot_pd)).astype(BF16)
                dsink = dsink + jnp.where(lane == hd, -jnp.sum(e_sink * inv * dot_pd, axis=0, keepdims=True), 0.0)
                dq_ref[:, hs] = (jnp.dot(ds16, kb, preferred_element_type=F32) * scale).astype(BF16)
                dkb = dkb + lax.dot_general(ds16, qh, _DIMS["tn"], preferred_element_type=F32) * scale
                dvb = dvb + lax.dot_general(p16, doh, _DIMS["tn"], preferred_element_type=F32)
            dkvp_ref[:, ks] = dkb[:blk]
            dkvo_ref[:, ks] = dkb[blk:]
            dkvp_ref[:, vs] = dvb[:blk]
            dkvo_ref[:, vs] = dvb[blk:]
        dsink_ref[...] += dsink

    return _rows(
        "swa_attn_bwd", t // blk,
        [(sinks, pl.BlockSpec(memory_space=pltpu.SMEM)), _row(q, blk), _row(kv, blk),
         (kv, (blk, kv.shape[1]), lambda i: (jnp.maximum(i - 1, 0), 0)), _row(do, blk)],
        [_row_out((t, qw), BF16, blk), _row_out((t, 2 * kw), F32, blk), _row_out((t, 2 * kw), F32, blk),
         _acc_out((1, LANES))], body)


def _rope_bwd(dq_r, dkv_own, dkv_prev, tabs_q, tabs_kv, blk):
    t, qw = dq_r.shape
    kvw = dkv_own.shape[1]
    nb = t // blk

    def body(dq_ref, own_ref, nxt_ref, cq, snq, spq, ck, snk, spk, dqo_ref, dkvo_ref, dbq_ref, dbkv_ref):
        i = pl.program_id(0)
        _first(i, [dbq_ref, dbkv_ref])
        dq = _rope(dq_ref[...].astype(F32), cq[...], snq[...], spq[...], -1.0)
        dkv = own_ref[...] + jnp.where(i < nb - 1, nxt_ref[...], 0.0)
        dkv = _rope(dkv, ck[...], snk[...], spk[...], -1.0)
        dqo_ref[...] = dq.astype(BF16)
        dkvo_ref[...] = dkv.astype(BF16)
        dbq_ref[...] += jnp.sum(dq, axis=0, keepdims=True)
        dbkv_ref[...] += jnp.sum(dkv, axis=0, keepdims=True)

    return _rows(
        "swa_rope_bwd", nb,
        [_row(dq_r, blk), _row(dkv_own, blk), (dkv_prev, (blk, kvw), lambda i: (jnp.minimum(i + 1, nb - 1), 0))]
        + [_row(tab, blk) for tab in tabs_q] + [_row(tab, blk) for tab in tabs_kv],
        [_row_out((t, qw), BF16, blk), _row_out((t, kvw), BF16, blk), _acc_out((1, qw)), _acc_out((1, kvw))], body)


def _swa_fwd(x, gn, bq, bkv, bo, sinks, wq, wkv, wo, blk, tag):
    t, d = x.shape
    qw = bq.shape[1]
    kvw = bkv.shape[1]
    d4 = d // N_CHIPS
    tm = _tm(t)
    h = _rms_fwd(x, gn, f"mix_{tag}")
    tabs_q = _rope_tables(t, LANES, LANES // HEAD_DIM)
    tabs_kv = _rope_tables(t, kvw, kvw // 2 // HEAD_DIM)
    tn = min(qw, 1024)
    whole = lambda i, j, k: (0, 0, j)

    def ep_rope(accs, ex):
        return [_rope(accs[0] + ex[0], ex[1], ex[2], ex[3], 1.0)]

    q = _mm(
        f"swa_q_{tag}", (t // tm, qw // tn, 1),
        [(h, (tm, d), _ik, wq, (N_CHIPS, d4, tn), whole, "nn", 0, False)],
        [(bq, (1, tn), _0j)] + [(tab, (tm, LANES), _i0e) for tab in tabs_q],
        [((t, qw), BF16, (tm, tn), _ij)], ep_rope, [(tm, tn)], order="ji")[0]
    kv = _mm(
        f"swa_kv_{tag}", (t // tm, 1, 1),
        [(h, (tm, d), _ik, wkv, (N_CHIPS, d4, kvw), whole, "nn", 0, False)],
        [(bkv, (1, kvw), _0j)] + [(tab, (tm, kvw), _i0e) for tab in tabs_kv],
        [((t, kvw), BF16, (tm, kvw), _ij)], ep_rope, [(tm, kvw)])[0]
    o = _attn_fwd(q, kv, sinks, blk)
    tno = min(d, 1024)
    x_new = _mm(
        f"swa_out_{tag}", (t // tm, d // tno, 1),
        [(o, (tm, qw), _ik, wo, (N_CHIPS, qw // N_CHIPS, tno), whole, "nn", 0, False)],
        [(x, (tm, tno), _ij), (bo, (1, tno), _0j)], [((t, d), F32, (tm, tno), _ij)],
        lambda accs, ex: [ex[0] + accs[0] + ex[1]], [(tm, tno)], order="ji")[0]
    return x_new, (x, h, q, kv, o, tabs_q, tabs_kv)


def _swa_bwd(dx_out, saved, gn, sinks, wq, wkv, wo, blk, tag, dep=None):
    x, h, q, kv, o, tabs_q, tabs_kv = saved
    t, d = x.shape
    qw = q.shape[1]
    kvw = kv.shape[1]
    d4 = d // N_CHIPS
    qw4 = qw // N_CHIPS
    tm = _tm(t)
    whole = lambda i, j, k: (0, 0, 0)
    do = _mm(
        f"swa_do_{tag}", (t // tm, 1, 1),
        [(dx_out, (tm, d), _ik, wo, (N_CHIPS, qw4, d), whole, "nt", 0, False)],
        _dep_extra(dep), [((t, qw), BF16, (tm, qw), _ij)], lambda accs, ex: accs, [(tm, qw)])[0]
    tk = min(t, 2048)
    tn = min(d, 1024)
    dwo = _mm(
        f"swa_dwo_{tag}", (N_CHIPS, d // tn, t // tk),
        [(o, (tk, qw4), _ki, dx_out, (tk, tn), _kj, "tn", 0, False)],
        [], [((N_CHIPS, qw4, d), BF16, (None, qw4, tn), lambda i, j: (i, 0, j))], lambda accs, ex: accs, [(qw4, tn)])[0]
    dbo = _colsum(dx_out, f"bo_{tag}")
    dq_r, dkv_own, dkv_prev, dsink = _attn_bwd(q, kv, do, sinks, blk)
    dq, dkv, dbq, dbkv = _rope_bwd(dq_r, dkv_own, dkv_prev, tabs_q, tabs_kv, blk)
    tnq = min(qw, 1024)
    dwq = _mm(
        f"swa_dwq_{tag}", (N_CHIPS, qw // tnq, t // tk),
        [(h, (tk, d4), _ki, dq, (tk, tnq), _kj, "tn", 0, False)],
        [], [((N_CHIPS, d4, qw), BF16, (None, d4, tnq), lambda i, j: (i, 0, j))], lambda accs, ex: accs, [(d4, tnq)])[0]
    dwkv = _mm(
        f"swa_dwkv_{tag}", (N_CHIPS, 1, t // tk),
        [(h, (tk, d4), _ki, dkv, (tk, kvw), _kj, "tn", 0, False)],
        [], [((N_CHIPS, d4, kvw), BF16, (None, d4, kvw), lambda i, j: (i, 0, j))], lambda accs, ex: accs, [(d4, kvw)])[0]
    dh = _mm(
        f"swa_dh_{tag}", (t // tm, 1, 1),
        [(dq, (tm, qw), _ik, wq, (N_CHIPS, d4, qw), whole, "nt", 0, False),
         (dkv, (tm, kvw), _ik, wkv, (N_CHIPS, d4, kvw), whole, "nt", 0, False)],
        [], [((t, d), F32, (tm, d), _ij)], lambda accs, ex: accs, [(tm, d)])[0]
    dx, dgn = _rms_bwd(dh, x, gn, dx_out, f"mix_{tag}")
    n_heads = qw // HEAD_DIM
    return dx, dgn, dwq, dwkv, dwo, dbq, dbkv, dbo, dsink[:, :n_heads]


_HBM = pl.BlockSpec(memory_space=pl.ANY)
_CHIP_FLIPS = ((1, 0), (0, 1), (1, 1))


def _place():
    x, y, c = lax.axis_index("x"), lax.axis_index("y"), lax.axis_index("c")
    return x, y, c


def _flip(v, bit):
    return 1 - v if bit else v


def _exchange_small(v_ref, all_ref, send_sems, recv_sems):
    x, y, c = _place()
    me = 4 * x + 2 * y + c
    all_ref[me] = v_ref[...]
    copies = []
    for dlt in range(1, N_DEV):
        peer = (_flip(x, dlt & 4), _flip(y, dlt & 2), _flip(c, dlt & 1))
        copies.append(pltpu.make_async_remote_copy(
            src_ref=v_ref, dst_ref=all_ref.at[me], send_sem=send_sems.at[dlt - 1], recv_sem=recv_sems.at[dlt - 1],
            device_id=peer, device_id_type=MESH_ID))
    for cp in copies:
        cp.start()
    for cp in copies:
        cp.wait()


def _all_gather_small(v):
    r, cdim = v.shape

    def body(v_ref, out_ref, send_sems, recv_sems):
        _exchange_small(v_ref, out_ref, send_sems, recv_sems)

    return pl.pallas_call(
        body, name="all_gather_small", out_shape=jax.ShapeDtypeStruct((N_DEV, r, cdim), v.dtype),
        scratch_shapes=[pltpu.SemaphoreType.DMA((N_DEV - 1,)), pltpu.SemaphoreType.DMA((N_DEV - 1,))],
        compiler_params=pltpu.CompilerParams(vmem_limit_bytes=VMEM_LIMIT_BYTES),
    )(v)


def _all_sum_small(v):
    r, cdim = v.shape

    def body(v_ref, out_ref, all_ref, send_sems, recv_sems):
        _exchange_small(v_ref, all_ref, send_sems, recv_sems)
        acc = all_ref[0]
        for dv in range(1, N_DEV):
            acc = acc + all_ref[dv]
        out_ref[...] = acc

    return pl.pallas_call(
        body, name="all_sum_small", out_shape=jax.ShapeDtypeStruct((r, cdim), v.dtype),
        scratch_shapes=[pltpu.VMEM((N_DEV, r, cdim), v.dtype), pltpu.SemaphoreType.DMA((N_DEV - 1,)),
                        pltpu.SemaphoreType.DMA((N_DEV - 1,))],
        compiler_params=pltpu.CompilerParams(vmem_limit_bytes=VMEM_LIMIT_BYTES),
    )(v)


_HBM_SPEC = pl.BlockSpec(memory_space=pltpu.HBM)
_SEM_SPEC = pl.BlockSpec(memory_space=pltpu.SEMAPHORE)
_EFFECT = pltpu.SideEffectType.DATAFLOW_SIDE_EFFECTING
_TOKEN = jax.ShapeDtypeStruct((8, LANES), F32)


def _in_hbm(a):
    return pltpu.with_memory_space_constraint(a, pltpu.HBM)


def _hbm_like(a):
    return pltpu.HBM(a.shape, a.dtype)


def _ici_copies(shard, land, send_sems, recv_sems, base, x, y, c):
    half = shard.shape[0] // 2
    rows = pl.ds(c * half, half)
    return [pltpu.make_async_remote_copy(
        src_ref=shard.at[rows], dst_ref=land.at[2 * x + y, rows], send_sem=send_sems.at[base + j], recv_sem=recv_sems.at[base + j],
        device_id=(_flip(x, fx), _flip(y, fy), c), device_id_type=MESH_ID) for j, (fx, fy) in enumerate(_CHIP_FLIPS)]


def _d2d_copies(land, send_sems, recv_sems, base, x, y, c, c_rows):
    half = land.shape[1] // 2
    rows = pl.ds(c_rows * half, half)
    out = []
    for j, (fx, fy) in enumerate(_CHIP_FLIPS):
        piece = land.at[2 * _flip(x, fx) + _flip(y, fy), rows]
        out.append(pltpu.make_async_remote_copy(
            src_ref=piece, dst_ref=piece, send_sem=send_sems.at[base + j], recv_sem=recv_sems.at[base + j],
            device_id=(x, y, 1 - c), device_id_type=MESH_ID))
    return out


def _own_copy(shard, land, send_sems, recv_sems, s, x, y, c):
    return pltpu.make_async_remote_copy(
        src_ref=shard, dst_ref=land.at[2 * x + y], send_sem=send_sems.at[s], recv_sem=recv_sems.at[s],
        device_id=(x, y, 1 - c), device_id_type=MESH_ID)


def _gather_start(groups):
    flat = [s for grp in groups for s in grp]
    n, n_g = len(flat), len(groups)
    lands = [lax.empty((N_CHIPS,) + s.shape, s.dtype) for s in flat]

    def body(*refs):
        shards, land_refs = refs[:n], refs[n:2 * n]
        sems = refs[2 * n: 2 * n + 4 * n_g]
        token = refs[-1]
        x, y, c = _place()
        idx = 0
        for g, grp in enumerate(groups):
            ici_send, ici_recv, own_send, own_recv = sems[4 * g: 4 * g + 4]
            for s in range(len(grp)):
                _own_copy(shards[idx], land_refs[idx], own_send, own_recv, s, x, y, c).start()
                for cp in _ici_copies(shards[idx], land_refs[idx], ici_send, ici_recv, 3 * s, x, y, c):
                    cp.start()
                idx += 1
        token[...] = jnp.zeros(token.shape, F32)

    sem_shapes = []
    for grp in groups:
        k = len(grp)
        sem_shapes += [pltpu.SemaphoreType.DMA((3 * k,)), pltpu.SemaphoreType.DMA((3 * k,)),
                       pltpu.SemaphoreType.DMA((k,)), pltpu.SemaphoreType.DMA((k,))]
    res = pl.pallas_call(
        body, name="gather_start",
        out_shape=sem_shapes + [_hbm_like(s) for s in flat] + [_hbm_like(ld) for ld in lands] + [_TOKEN],
        in_specs=[_HBM_SPEC] * (2 * n),
        out_specs=[_SEM_SPEC] * (4 * n_g) + [_HBM_SPEC] * (2 * n) + [pl.BlockSpec(memory_space=pltpu.VMEM)],
        input_output_aliases={i: 4 * n_g + i for i in range(2 * n)},
        compiler_params=pltpu.CompilerParams(has_side_effects=_EFFECT),
    )(*[_in_hbm(s) for s in flat], *[_in_hbm(ld) for ld in lands])
    sems, thru, token = res[:4 * n_g], res[4 * n_g: 4 * n_g + 2 * n], res[-1]
    out, idx = [], 0
    for g, grp in enumerate(groups):
        k = len(grp)
        out.append(dict(sems=tuple(sems[4 * g: 4 * g + 4]), shards=list(thru[idx: idx + k]),
                        lands=list(thru[n + idx: n + idx + k])))
        idx += k
    return out, token


def _gather_step(name, landed, arriving, after):
    n_l = len(landed["lands"]) if landed else 0
    n_a = len(arriving["lands"]) if arriving else 0

    def body(*refs):
        pos = 0
        l_lands = refs[pos: pos + n_l]; pos += n_l
        l_sems = refs[pos: pos + (2 if landed else 0)]; pos += 2 if landed else 0
        a_shards = refs[pos: pos + n_a]; pos += n_a
        a_lands = refs[pos: pos + n_a]; pos += n_a
        a_sems = refs[pos: pos + (4 if arriving else 0)]; pos += 4 if arriving else 0
        pos += 1
        pos += n_l + n_a
        new_sems = refs[pos: pos + (2 if arriving else 0)]
        x, y, c = _place()
        if arriving:
            ici_send, ici_recv, own_send, own_recv = a_sems
            started = []
            for s in range(n_a):
                own = _own_copy(a_shards[s], a_lands[s], own_send, own_recv, s, x, y, c)
                own.wait_recv()
                for cp in _d2d_copies(a_lands[s], ici_send, ici_recv, 3 * s, x, y, c, c):
                    cp.wait_recv()
                for cp in _d2d_copies(a_lands[s], new_sems[0], new_sems[1], 3 * s, x, y, c, c):
                    cp.start()
                started.append(own)
                started += _ici_copies(a_shards[s], a_lands[s], ici_send, ici_recv, 3 * s, x, y, c)
            for cp in started:
                cp.wait_send()
        if landed:
            for s in range(n_l):
                for cp in _d2d_copies(l_lands[s], l_sems[0], l_sems[1], 3 * s, x, y, c, c):
                    cp.wait_send()
                for cp in _d2d_copies(l_lands[s], l_sems[0], l_sems[1], 3 * s, x, y, c, 1 - c):
                    cp.wait_recv()

    args, in_specs = [], []
    if landed:
        args += [_in_hbm(a) for a in landed["lands"]] + list(landed["d2d"])
        in_specs += [_HBM_SPEC] * n_l + [_SEM_SPEC] * 2
    if arriving:
        args += [_in_hbm(a) for a in arriving["shards"]] + [_in_hbm(a) for a in arriving["lands"]] + list(arriving["sems"])
        in_specs += [_HBM_SPEC] * (2 * n_a) + [_SEM_SPEC] * 4
    args.append(after)
    in_specs.append(pl.BlockSpec(memory_space=pl.ANY))
    out_shape, out_specs, aliases = [], [], {}
    if landed:
        out_shape += [_hbm_like(a) for a in landed["lands"]]
        for s in range(n_l):
            aliases[s] = s
    if arriving:
        first = (n_l + 2 if landed else 0) + n_a
        for s in range(n_a):
            aliases[first + s] = n_l + s
        out_shape += [_hbm_like(a) for a in arriving["lands"]]
    out_specs += [_HBM_SPEC] * (n_l + n_a)
    if arriving:
        out_shape += [pltpu.SemaphoreType.DMA((3 * n_a,)), pltpu.SemaphoreType.DMA((3 * n_a,))]
        out_specs += [_SEM_SPEC] * 2
    res = pl.pallas_call(
        body, name=name, out_shape=out_shape, in_specs=in_specs, out_specs=out_specs, input_output_aliases=aliases,
        compiler_params=pltpu.CompilerParams(has_side_effects=_EFFECT),
    )(*args)
    done = list(res[:n_l]) if landed else None
    nxt = None
    if arriving:
        nxt = dict(lands=list(res[n_l: n_l + n_a]), d2d=tuple(res[n_l + n_a: n_l + n_a + 2]))
    return done, nxt


def _swap_halves(grads, tag):
    n = len(grads)

    def body(*refs):
        srcs, dsts = refs[:n], refs[n:2 * n]
        send_sems, recv_sems = refs[2 * n:]
        x, y, c = _place()
        copies = []
        for s in range(n):
            half = srcs[s].shape[1] // 2
            cp = pltpu.make_async_remote_copy(
                src_ref=srcs[s].at[:, pl.ds((1 - c) * half, half)], dst_ref=dsts[s],
                send_sem=send_sems.at[s], recv_sem=recv_sems.at[s], device_id=(x, y, 1 - c), device_id_type=MESH_ID)
            cp.start()
            copies.append(cp)
        for cp in copies:
            cp.wait()

    return pl.pallas_call(
        body, name=f"swap_halves_{tag}",
        out_shape=[jax.ShapeDtypeStruct((N_CHIPS, g.shape[1] // 2, g.shape[2]), g.dtype) for g in grads],
        in_specs=[_HBM] * n, out_specs=[_HBM] * n,
        scratch_shapes=[pltpu.SemaphoreType.DMA((n,)), pltpu.SemaphoreType.DMA((n,))],
    )(*grads)


def _add_halves(g, r, c_idx, tag):
    _, rows, w = g.shape
    half = rows // 2
    tr = _row_tile(half)
    nb = half // tr

    def body(c_ref, g_ref, r_ref, o_ref):
        o_ref[...] = (g_ref[...].astype(F32) + r_ref[...].astype(F32)).astype(BF16)

    return pl.pallas_call(
        body, name=f"add_halves_{tag}",
        grid_spec=pltpu.PrefetchScalarGridSpec(
            num_scalar_prefetch=1, grid=(N_CHIPS, nb),
            in_specs=[pl.BlockSpec((None, tr, w), lambda k, i, c: (k, c[0] * nb + i, 0)),
                      pl.BlockSpec((None, tr, w), lambda k, i, c: (k, i, 0))],
            out_specs=pl.BlockSpec((None, tr, w), lambda k, i, c: (k, i, 0))),
        out_shape=jax.ShapeDtypeStruct((N_CHIPS, half, w), BF16), compiler_params=_params(2),
    )(c_idx, g, r)


def _partial_copies(part, land, send_sems, recv_sems, base, x, y, c):
    out = []
    for j, (fx, fy) in enumerate(_CHIP_FLIPS):
        px, py = _flip(x, fx), _flip(y, fy)
        out.append(pltpu.make_async_remote_copy(
            src_ref=part.at[2 * px + py], dst_ref=land.at[j], send_sem=send_sems.at[base + j], recv_sem=recv_sems.at[base + j],
            device_id=(px, py, c), device_id_type=MESH_ID))
    return out


def _send_start(tag, parts):
    n = len(parts)
    lands = [lax.empty((3,) + p.shape[1:], p.dtype) for p in parts]

    def body(*refs):
        srcs, dsts = refs[:n], refs[n:2 * n]
        send_sems, recv_sems = refs[2 * n: 2 * n + 2]
        token = refs[-1]
        x, y, c = _place()
        for s in range(n):
            for cp in _partial_copies(srcs[s], dsts[s], send_sems, recv_sems, 3 * s, x, y, c):
                cp.start()
        token[...] = jnp.zeros(token.shape, F32)

    res = pl.pallas_call(
        body, name=f"send_start_{tag}",
        out_shape=[pltpu.SemaphoreType.DMA((3 * n,)), pltpu.SemaphoreType.DMA((3 * n,))]
        + [_hbm_like(p) for p in parts] + [_hbm_like(ld) for ld in lands] + [_TOKEN],
        in_specs=[_HBM_SPEC] * (2 * n),
        out_specs=[_SEM_SPEC] * 2 + [_HBM_SPEC] * (2 * n) + [pl.BlockSpec(memory_space=pltpu.VMEM)],
        input_output_aliases={i: 2 + i for i in range(2 * n)},
        compiler_params=pltpu.CompilerParams(has_side_effects=_EFFECT),
    )(*[_in_hbm(p) for p in parts], *[_in_hbm(ld) for ld in lands])
    state = dict(sems=tuple(res[:2]), parts=list(res[2: 2 + n]), lands=list(res[2 + n: 2 + 2 * n]))
    return state, res[-1]


def _send_wait(tag, state, after):
    n = len(state["parts"])

    def body(*refs):
        srcs, dsts = refs[:n], refs[n:2 * n]
        send_sems, recv_sems = refs[2 * n: 2 * n + 2]
        x, y, c = _place()
        for s in range(n):
            for cp in _partial_copies(srcs[s], dsts[s], send_sems, recv_sems, 3 * s, x, y, c):
                cp.wait_send()
                cp.wait_recv()

    res = pl.pallas_call(
        body, name=f"send_wait_{tag}",
        out_shape=[_hbm_like(p) for p in state["parts"]] + [_hbm_like(ld) for ld in state["lands"]],
        in_specs=[_HBM_SPEC] * (2 * n) + [_SEM_SPEC] * 2 + [pl.BlockSpec(memory_space=pl.ANY)],
        out_specs=[_HBM_SPEC] * (2 * n), input_output_aliases={i: i for i in range(2 * n)},
        compiler_params=pltpu.CompilerParams(has_side_effects=_EFFECT),
    )(*state["parts"], *state["lands"], *state["sems"], after)
    return list(res[:n]), list(res[n:])


def _sum_partials(p, q, chip_idx, c_idx, tag):
    _, half, w = p.shape
    tr = _row_tile(half)
    nb = half // tr

    def body(k_ref, c_ref, p_ref, q_ref, o_ref):
        acc = p_ref[...].astype(F32)
        for j in range(3):
            acc = acc + q_ref[j].astype(F32)
        o_ref[...] = acc

    return pl.pallas_call(
        body, name=f"sum_partials_{tag}",
        grid_spec=pltpu.PrefetchScalarGridSpec(
            num_scalar_prefetch=2, grid=(nb,),
            in_specs=[pl.BlockSpec((None, tr, w), lambda i, k, c: (k[0], i, 0)),
                      pl.BlockSpec((3, tr, w), lambda i, k, c: (0, i, 0))],
            out_specs=pl.BlockSpec((tr, w), lambda i, k, c: (c[0] * nb + i, 0))),
        out_shape=jax.ShapeDtypeStruct((2 * half, w), F32), compiler_params=_params(1),
    )(chip_idx, c_idx, p, q)


def _join_halves(tots, tag):
    n = len(tots)

    def body(*refs):
        bufs = refs[n:2 * n]
        send_sems, recv_sems = refs[2 * n:]
        x, y, c = _place()
        copies = []
        for s in range(n):
            half = bufs[s].shape[0] // 2
            mine = bufs[s].at[pl.ds(c * half, half)]
            cp = pltpu.make_async_remote_copy(
                src_ref=mine, dst_ref=mine, send_sem=send_sems.at[s], recv_sem=recv_sems.at[s],
                device_id=(x, y, 1 - c), device_id_type=MESH_ID)
            cp.start()
            copies.append(cp)
        for s in range(n):
            half = bufs[s].shape[0] // 2
            theirs = bufs[s].at[pl.ds((1 - c) * half, half)]
            pltpu.make_async_remote_copy(
                src_ref=theirs, dst_ref=theirs, send_sem=send_sems.at[s], recv_sem=recv_sems.at[s],
                device_id=(x, y, c), device_id_type=MESH_ID).wait_recv()
        for cp in copies:
            cp.wait_send()

    return pl.pallas_call(
        body, name=f"join_halves_{tag}",
        out_shape=[jax.ShapeDtypeStruct(tt.shape, tt.dtype) for tt in tots],
        in_specs=[_HBM] * n, out_specs=[_HBM] * n, input_output_aliases={s: s for s in range(n)},
        scratch_shapes=[pltpu.SemaphoreType.DMA((n,)), pltpu.SemaphoreType.DMA((n,))],
    )(*tots)


def _adamw_math(w, g, m, v):
    m = ADAM_B1 * m + (1.0 - ADAM_B1) * g
    v = ADAM_B2 * v + (1.0 - ADAM_B2) * jnp.square(g)
    m_hat = m / (1.0 - ADAM_B1 ** ADAM_STEP)
    v_hat = v / (1.0 - ADAM_B2 ** ADAM_STEP)
    delta = -ADAM_LR * (m_hat / (jnp.sqrt(v_hat) + ADAM_EPS) + ADAM_WD * w)
    return delta, m, v


def _adamw_big(g, col, w2, m2, v2, row0, prev, tag):
    rows = g.shape[0]
    rtot, wd = w2.shape
    tr = _row_tile(rows)
    assert row0 % tr == 0
    n_prev = 0 if prev is None else 4

    def body(*refs):
        g_ref, w_ref, m_ref, v_ref = refs[:4]
        go_ref, d_ref, mo_ref, vo_ref = refs[4 + n_prev:]
        gv = g_ref[...]
        delta, mn, vn = _adamw_math(w_ref[...], gv, m_ref[...], v_ref[...])
        go_ref[...] = gv
        d_ref[...] = delta
        mo_ref[...] = mn
        vo_ref[...] = vn

    at = lambda i: (row0 // tr + i, 0)
    return pl.pallas_call(
        body, name=f"adamw_{tag}", grid=(rows // tr,),
        in_specs=[pl.BlockSpec((tr, wd), lambda i: (i, col))] + [pl.BlockSpec((tr, wd), at)] * 3
        + [pl.BlockSpec(memory_space=pl.ANY)] * n_prev,
        out_specs=[pl.BlockSpec((tr, wd), at)] * 4, out_shape=[jax.ShapeDtypeStruct((rtot, wd), F32)] * 4,
        input_output_aliases={4 + k: k for k in range(n_prev)}, compiler_params=_params(1),
    )(g, w2, m2, v2, *(prev or ()))


def _adamw_small(items):
    n = len(items)

    def body(*refs):
        ins, outs = refs[:4 * n], refs[4 * n:]
        for k in range(n):
            g_ref, w_ref, m_ref, v_ref = ins[4 * k: 4 * k + 4]
            delta, mn, vn = _adamw_math(w_ref[...], g_ref[...], m_ref[...], v_ref[...])
            outs[3 * k][...] = delta
            outs[3 * k + 1][...] = mn
            outs[3 * k + 2][...] = vn

    flat = [a for it in items for a in it]
    out_shape = [jax.ShapeDtypeStruct(it[1].shape, F32) for it in items for _ in range(3)]
    res = pl.pallas_call(body, name="adamw_small", out_shape=out_shape,
                         compiler_params=pltpu.CompilerParams(vmem_limit_bytes=VMEM_LIMIT_BYTES))(*flat)
    return [tuple(res[3 * k: 3 * k + 3]) for k in range(n)]


def _pack(arrays, width):
    rows, layout, r = [], [], 0
    for a in arrays:
        flat = a.reshape(-1).astype(F32)
        nr = -(-flat.shape[0] // (8 * width)) * 8
        flat = jnp.pad(flat, (0, nr * width - flat.shape[0]))
        rows.append(flat.reshape(nr, width))
        layout.append((r, nr, a.shape))
        r += nr
    return jnp.concatenate(rows, axis=0), layout


def _unpack(packed, layout):
    out = []
    for r, nr, shape in layout:
        size = 1
        for s in shape:
            size *= s
        out.append(packed[r:r + nr].reshape(-1)[:size].reshape(shape))
    return out


_WEIGHTS = ['ffn1_norm', 'ffn1_w1', 'ffn1_w3', 'ffn1_w2', 'mix_norm', 'ffn2_norm', 'ffn2_w1', 'ffn2_w3', 'ffn2_w2',
            'ple_norm', 'ple_w_gate', 'ple_w_proj', 'gmlp_w_in', 'gmlp_ln_g', 'gmlp_ln_b', 'gmlp_w_s', 'gmlp_b_s',
            'gmlp_w_out', 'swa_wq', 'swa_bq', 'swa_wk', 'swa_bk', 'swa_wv', 'swa_bv', 'swa_sinks', 'swa_wo', 'swa_bo',
            'final_norm']
_REPLICATED = ['ffn1_norm', 'mix_norm', 'ffn2_norm', 'ple_norm', 'gmlp_ln_g', 'gmlp_ln_b', 'gmlp_w_s', 'gmlp_b_s',
               'swa_sinks', 'final_norm']
_BIASES = ['swa_bq', 'swa_bk', 'swa_bv', 'swa_bo']


def _as2d(a):
    if a.ndim == 1:
        return a.reshape(1, -1)
    return a.reshape(-1, a.shape[-1])


_GROUPS = [("f1l0", ("w1", "w3", "w2")), ("mix0", ("in", "out")), ("f2l0", ("w1", "w3", "w2")), ("ple0", ("gate", "proj")),
           ("f1l1", ("w1", "w3", "w2")), ("mix1", ("q", "kv", "o")), ("f2l1", ("w1", "w3", "w2")), ("ple1", ("gate", "proj"))]
_GATHER = []
for _g, _parts in _GROUPS:
    if _g.startswith("f"):
        _GATHER += [(_g + "a", _g, ("w1", "w3")), (_g + "b", _g, ("w2",))]
    else:
        _GATHER.append((_g, _g, _parts))


def _local_step(x, p, target, small, full_bias, get_group, put_group):
    t, d = x.shape
    n_layers = 2
    blk = small['gmlp_w_s'].shape[2]
    bq, bkv, bo = full_bias
    norm = lambda name, i: small[name][i:i + 1]
    ln_g, ln_b = small['gmlp_ln_g'], small['gmlp_ln_b']
    w_s = small['gmlp_w_s'][0]
    b_st = jnp.pad(small['gmlp_b_s'][0].T, ((0, 0), (0, LANES - small['gmlp_b_s'].shape[1])))
    sinks = small['swa_sinks']

    saved, wts = [], {}

    def ffn_fwd(x, name, norm_w):
        w = get_group(name + "a", x)
        x, s_ffn, w["w2"] = _ffn_fwd(x, norm_w, w["w1"], w["w3"], lambda after: get_group(name + "b", after)["w2"], name)
        wts[name] = w
        return x, s_ffn

    for i in range(n_layers):
        x, s_f1 = ffn_fwd(x, f"f1l{i}", norm('ffn1_norm', i))
        w = wts[f"mix{i}"] = get_group(f"mix{i}", x)
        if i == 0:
            x, s_mix = _gmlp_fwd(x, norm('mix_norm', i), ln_g, ln_b, w_s, b_st, w["in"], w["out"], f"l{i}")
        else:
            x, s_mix = _swa_fwd(x, norm('mix_norm', i), bq, bkv, bo, sinks, w["q"], w["kv"], w["o"], blk, f"l{i}")
        x, s_f2 = ffn_fwd(x, f"f2l{i}", norm('ffn2_norm', i))
        w = wts[f"ple{i}"] = get_group(f"ple{i}", x)
        x, s_ple = _ple_fwd(x, p[i], norm('ple_norm', i), w["gate"], w["proj"], f"l{i}")
        saved.append((s_f1, s_mix, s_f2, s_ple))

    dx, d_final, loss = _loss_head(x, small['final_norm'].reshape(1, d), target)

    gn = {k: [None] * n_layers for k in ('ffn1_norm', 'mix_norm', 'ffn2_norm', 'ple_norm')}
    dep = None
    for i in reversed(range(n_layers)):
        s_f1, s_mix, s_f2, s_ple = saved[i]
        w = wts[f"ple{i}"]
        dx, gn['ple_norm'][i], dwg, dwp = _ple_bwd(dx, s_ple, p[i], norm('ple_norm', i), w["gate"], f"l{i}", dep)
        dep = put_group(f"ple{i}", {"gate": dwg, "proj": dwp})
        w = wts[f"f2l{i}"]
        dx, gn['ffn2_norm'][i], dep = _ffn_bwd(
            dx, s_f2, norm('ffn2_norm', i), w["w1"], w["w3"], w["w2"], f"f2l{i}", dep,
            functools.partial(put_group, f"f2l{i}"))
        w = wts[f"mix{i}"]
        if i == 0:
            dx, gn['mix_norm'][i], dw_in, dw_out, d_ws, d_bs, d_lg, d_lb = _gmlp_bwd(
                dx, s_mix, norm('mix_norm', i), ln_g, ln_b, w_s, b_st, w["in"], w["out"], f"l{i}", dep)
            dep = put_group(f"mix{i}", {"in": dw_in, "out": dw_out})
        else:
            dx, gn['mix_norm'][i], dwq, dwkv, dwo, d_bq, d_bkv, d_bo, d_sink = _swa_bwd(
                dx, s_mix, norm('mix_norm', i), sinks, w["q"], w["kv"], w["o"], blk, f"l{i}", dep)
            dep = put_group(f"mix{i}", {"q": dwq, "kv": dwkv, "o": dwo})
        w = wts[f"f1l{i}"]
        dx, gn['ffn1_norm'][i], dep = _ffn_bwd(
            dx, s_f1, norm('ffn1_norm', i), w["w1"], w["w3"], w["w2"], f"f1l{i}", dep,
            functools.partial(put_group, f"f1l{i}"))

    kw = d_bkv.shape[1] // 2
    g_small = {
        'ffn1_norm': jnp.concatenate(gn['ffn1_norm'], axis=0), 'mix_norm': jnp.concatenate(gn['mix_norm'], axis=0),
        'ffn2_norm': jnp.concatenate(gn['ffn2_norm'], axis=0), 'ple_norm': jnp.concatenate(gn['ple_norm'], axis=0),
        'gmlp_ln_g': d_lg, 'gmlp_ln_b': d_lb, 'gmlp_w_s': d_ws[None], 'gmlp_b_s': d_bs[None], 'swa_sinks': d_sink,
        'final_norm': d_final.reshape(d), 'swa_bq': d_bq, 'swa_bk': d_bkv[:, :kw], 'swa_bv': d_bkv[:, kw:], 'swa_bo': d_bo,
    }
    return loss, dx, g_small


def _group_shards(a):
    bf = lambda w: w.astype(BF16)
    out = {}
    for i in range(2):
        for f in (1, 2):
            out[f"f{f}l{i}"] = {
                "w1": (bf(a[f"ffn{f}_w1"][i]), [(f"ffn{f}_w1", i, 0)]), "w3": (bf(a[f"ffn{f}_w3"][i]), [(f"ffn{f}_w3", i, 0)]),
                "w2": (bf(a[f"ffn{f}_w2"][i]), [(f"ffn{f}_w2", i, 0)])}
        out[f"ple{i}"] = {"gate": (bf(a["ple_w_gate"][i]), [("ple_w_gate", i, 0)]),
                          "proj": (bf(a["ple_w_proj"][i]), [("ple_w_proj", i, 0)])}
    out["mix0"] = {"in": (bf(a["gmlp_w_in"][0]), [("gmlp_w_in", 0, 0)]), "out": (bf(a["gmlp_w_out"][0]), [("gmlp_w_out", 0, 0)])}
    wkv = jnp.concatenate([bf(a["swa_wk"][0]), bf(a["swa_wv"][0])], axis=1)
    out["mix1"] = {"q": (bf(a["swa_wq"][0]), [("swa_wq", 0, 0)]), "kv": (wkv, [("swa_wk", 0, 0), ("swa_wv", 0, 1)]),
                   "o": (bf(a["swa_wo"][0]), [("swa_wo", 0, 0)])}
    return out


def kernel(x, p, ffn1_norm, ffn1_w1, ffn1_w3, ffn1_w2, mix_norm, ffn2_norm, ffn2_w1, ffn2_w3, ffn2_w2, ple_norm, ple_w_gate, ple_w_proj, gmlp_w_in, gmlp_ln_g, gmlp_ln_b, gmlp_w_s, gmlp_b_s, gmlp_w_out, swa_wq, swa_bq, swa_wk, swa_bk, swa_wv, swa_bv, swa_sinks, swa_wo, swa_bo, final_norm, loss_target, m_ffn1_norm, m_ffn1_w1, m_ffn1_w3, m_ffn1_w2, m_mix_norm, m_ffn2_norm, m_ffn2_w1, m_ffn2_w3, m_ffn2_w2, m_ple_norm, m_ple_w_gate, m_ple_w_proj, m_gmlp_w_in, m_gmlp_ln_g, m_gmlp_ln_b, m_gmlp_w_s, m_gmlp_b_s, m_gmlp_w_out, m_swa_wq, m_swa_bq, m_swa_wk, m_swa_bk, m_swa_wv, m_swa_bv, m_swa_sinks, m_swa_wo, m_swa_bo, m_final_norm, v_ffn1_norm, v_ffn1_w1, v_ffn1_w3, v_ffn1_w2, v_mix_norm, v_ffn2_norm, v_ffn2_w1, v_ffn2_w3, v_ffn2_w2, v_ple_norm, v_ple_w_gate, v_ple_w_proj, v_gmlp_w_in, v_gmlp_ln_g, v_gmlp_ln_b, v_gmlp_w_s, v_gmlp_b_s, v_gmlp_w_out, v_swa_wq, v_swa_bq, v_swa_wk, v_swa_bk, v_swa_wv, v_swa_bv, v_swa_sinks, v_swa_wo, v_swa_bo, v_final_norm):
    a = dict(locals())
    xi, yi, ci = _place()
    chip = 2 * xi + yi
    c_idx = ci.reshape(1).astype(jnp.int32)
    chip_idx = chip.reshape(1).astype(jnp.int32)
    d = x.shape[-1]
    d4 = d // N_CHIPS

    shards = _group_shards(a)
    started, token = _gather_start([[shards[g][part][0] for part in parts] for _, g, parts in _GATHER])
    bias_pack, bias_layout = _pack([a[n] for n in _BIASES], d4)
    bias_all = _all_gather_small(bias_pack)

    def full_bias(idx):
        return jnp.concatenate([_unpack(bias_all[2 * k], bias_layout)[idx] for k in range(N_CHIPS)], axis=1)

    bq, bk, bv, bo = (full_bias(i) for i in range(4))
    bkv = jnp.concatenate([bk, bv], axis=1)

    state = {"handed": _gather_step("gather_step_first", None, started[0], token)[1], "next": 1}

    def get_group(name, after):
        k = state["next"]
        assert _GATHER[k - 1][0] == name
        arriving = started[k] if k < len(_GATHER) else None
        done, state["handed"] = _gather_step(f"gather_step_{name}", state["handed"], arriving, after)
        state["next"] = k + 1
        return dict(zip(_GATHER[k - 1][2], done))

    pending = []

    def put_group(name, grads):
        parts = dict(_GROUPS)[name]
        glist = [grads[part] for part in parts]
        sib = _swap_halves(glist, name)
        partial = [_add_halves(g, r, c_idx, f"{name}_{part}") for part, g, r in zip(parts, glist, sib)]
        st, tok = _send_start(name, partial)
        pending.append((name, st))
        return tok

    small = {n: a[n] for n in _REPLICATED}
    loss, grad_x, g_small = _local_step(x[0], p[:, 0], loss_target[0], small, (bq, bkv, bo), get_group, put_group)

    names = _REPLICATED + _BIASES
    packed, layout = _pack([g_small[n] for n in names] + [loss[:, :1]], d4)
    summed = _unpack(_all_sum_small(packed), layout)
    g_sum = dict(zip(names, summed[:-1]))
    loss_out = summed[-1].reshape(())
    for n in _BIASES:
        width = a[n].shape[-1]
        g_sum[n] = lax.dynamic_slice_in_dim(g_sum[n], chip * width, width, axis=1)
    out, chain, last_done = {}, {}, grad_x
    for idx, (name, st) in enumerate(pending):
        mine, recv = _send_wait(name, st, last_done if idx == len(pending) - 1 else grad_x)
        parts = dict(_GROUPS)[name]
        tots = _join_halves([_sum_partials(pm, q, chip_idx, c_idx, f"{name}_{part}")
                             for part, pm, q in zip(parts, mine, recv)], name)
        for part, g in zip(parts, tots):
            for n, layer, col in shards[name][part][1]:
                wd = a[n].shape[-1]
                rows = a[n].shape[-2]
                w2, m2, v2 = (a[pre + n].reshape(-1, wd) for pre in ('', 'm_', 'v_'))
                chain[n] = _adamw_big(g, col, w2, m2, v2, layer * rows, chain.get(n), f"{n}_{layer}")
                last_done = chain[n][1]
    for n, res in chain.items():
        out[n] = tuple(o.reshape(a[n].shape) for o in res)
    small_names = _REPLICATED + _BIASES
    items = [(_as2d(g_sum[n]), _as2d(a[n]), _as2d(a['m_' + n]), _as2d(a['v_' + n])) for n in small_names]
    for n, (delta, mn, vn) in zip(small_names, _adamw_small(items)):
        shape = a[n].shape
        out[n] = (g_sum[n].reshape(shape), delta.reshape(shape), mn.reshape(shape), vn.reshape(shape))

    return (loss_out, grad_x[None]) + tuple(out[n][j] for j in range(4) for n in _WEIGHTS)
```

```python
import functools

import jax
import jax.numpy as jnp
from jax import lax
from jax.experimental import pallas as pl
from jax.experimental.pallas import tpu as pltpu

F32 = jnp.float32
BF16 = jnp.bfloat16

RMS_EPS = 1e-6
LN_EPS = 1e-5
FFN_RESIDUAL_WEIGHT = 0.5
HEAD_DIM = 64
ROPE_DIM = 16
ROPE_THETA = 500000.0
ADAM_LR = 0.001
ADAM_B1 = 0.9
ADAM_B2 = 0.999
ADAM_EPS = 1e-08
ADAM_WD = 0.01
ADAM_STEP = 10
N_CHIPS = 4
N_DEV = 8
LANES = 128
VMEM_LIMIT_BYTES = 56 * 1024 * 1024
MESH_ID = pl.DeviceIdType.MESH

_DIMS = {
    "nn": (((1,), (0,)), ((), ())),
    "nt": (((1,), (1,)), ((), ())),
    "tn": (((0,), (0,)), ((), ())),
}


def _params(n_axes):
    return pltpu.CompilerParams(dimension_semantics=("arbitrary",) * n_axes, vmem_limit_bytes=VMEM_LIMIT_BYTES)


def _mm(name, grid, pairs, extras, outs, epilogue, acc_shapes, order="ij"):
    ni, nj, nk = grid
    if order == "ij":
        pgrid = (ni, nj, nk)
        ijk = lambda g0, g1, g2: (g0, g1, g2)
    else:
        pgrid = (nj, ni, nk)
        ijk = lambda g0, g1, g2: (g1, g0, g2)
    in_specs, args = [], []
    for a, ablk, amap, b, bblk, bmap, _, _, _ in pairs:
        in_specs.append(pl.BlockSpec(ablk, lambda g0, g1, g2, m=amap: m(*ijk(g0, g1, g2))))
        in_specs.append(pl.BlockSpec(bblk, lambda g0, g1, g2, m=bmap: m(*ijk(g0, g1, g2))))
        args += [a, b]
    for e, eblk, emap in extras:
        in_specs.append(pl.BlockSpec(eblk, lambda g0, g1, g2, m=emap: m(*ijk(g0, g1, g2)[:2])))
        args.append(e)
    out_specs = [pl.BlockSpec(oblk, lambda g0, g1, g2, m=omap: m(*ijk(g0, g1, g2)[:2])) for _, _, oblk, omap in outs]
    out_shape = [jax.ShapeDtypeStruct(s, d) for s, d, _, _ in outs]
    n_p, n_e, n_o = len(pairs), len(extras), len(outs)

    def body(*refs):
        p_refs = refs[: 2 * n_p]
        e_refs = refs[2 * n_p: 2 * n_p + n_e]
        o_refs = refs[2 * n_p + n_e: 2 * n_p + n_e + n_o]
        accs = refs[2 * n_p + n_e + n_o:]
        k = pl.program_id(2)

        def product(idx):
            a = p_refs[2 * idx][...].astype(BF16)
            b = p_refs[2 * idx + 1][...].astype(BF16)
            if b.ndim == 3:
                b = b.reshape(-1, b.shape[-1])
            return lax.dot_general(a, b, _DIMS[pairs[idx][6]], preferred_element_type=F32)

        def finish(sums):
            vals = epilogue(sums, [e[...] for e in e_refs])
            for o, v in zip(o_refs, vals):
                o[...] = v.astype(o.dtype)

        if nk == 1:
            sums = [None] * len(acc_shapes)
            for idx in range(n_p):
                ai = pairs[idx][7]
                sums[ai] = product(idx) if sums[ai] is None else sums[ai] + product(idx)
            finish(sums)
            return

        written = set()
        for idx in range(n_p):
            ai, k0 = pairs[idx][7], pairs[idx][8]
            if k0 or ai not in written:
                @pl.when(k == 0)
                def _(idx=idx, ai=ai, first=ai not in written):
                    if first:
                        accs[ai][...] = product(idx)
                    else:
                        accs[ai][...] += product(idx)
            if not k0:
                if ai in written:
                    accs[ai][...] += product(idx)
                else:
                    @pl.when(k > 0)
                    def _(idx=idx, ai=ai):
                        accs[ai][...] += product(idx)
            written.add(ai)

        @pl.when(k == nk - 1)
        def _():
            finish([acc[...] for acc in accs])

    res = pl.pallas_call(
        body, name=name, grid=pgrid, in_specs=in_specs, out_specs=out_specs, out_shape=out_shape,
        scratch_shapes=[] if nk == 1 else [pltpu.VMEM(s, F32) for s in acc_shapes], compiler_params=_params(3),
    )(*args)
    return res


def _col_nn(r0, tk, tn, wc):
    assert r0 % tk == 0 and wc % tn == 0
    npc = wc // tn
    return (None, tk, tn), lambda i, j, k: (j // npc, r0 // tk + k, j % npc)


def _row_nn(r0, tk, tn, rc):
    assert r0 % tk == 0 and rc % tk == 0
    kpc = rc // tk
    return (None, tk, tn), lambda i, j, k: (k // kpc, r0 // tk + k % kpc, j)


def _col_nt(r0, tn, tk, wc):
    assert r0 % tn == 0 and wc % tk == 0
    kpc = wc // tk
    return (None, tn, tk), lambda i, j, k: (k // kpc, r0 // tn + j, k % kpc)


def _row_nt(r0, tn, tk, rc):
    assert r0 % tn == 0 and rc % tn == 0
    npc = rc // tn
    return (None, tn, tk), lambda i, j, k: (j // npc, r0 // tn + j % npc, k)


def _out_col(tm, tn, wc):
    npc = wc // tn
    return (None, tm, tn), lambda i, j: (j // npc, i, j % npc)


def _out_row(tm, tn, rc):
    mpc = rc // tm
    return (None, tm, tn), lambda i, j: (i // mpc, i % mpc, j)


def _ik(i, j, k):
    return (i, k)


def _ki(i, j, k):
    return (k, i)


def _kj(i, j, k):
    return (k, j)


def _i0(i, j, k):
    return (i, 0)


def _ij(i, j):
    return (i, j)


def _0j(i, j):
    return (0, j)


def _i0e(i, j):
    return (i, 0)


def _rows(name, n, ins, outs, body, scratch=(), deps=()):
    in_specs, args = [], []
    for item in ins:
        if len(item) == 2:
            in_specs.append(item[1])
        else:
            in_specs.append(pl.BlockSpec(item[1], item[2]))
        args.append(item[0])
    n_in = len(args)
    for dep in deps:
        in_specs.append(pl.BlockSpec(dep.shape, lambda i: (0, 0)))
        args.append(dep)
    n_dep = len(deps)

    def call_body(*refs):
        body(*refs[:n_in], *refs[n_in + n_dep:])

    out_specs = [pl.BlockSpec(blk, m) for _, _, blk, m in outs]
    out_shape = [jax.ShapeDtypeStruct(s, d) for s, d, _, _ in outs]
    return pl.pallas_call(
        call_body, name=name, grid=(n,), in_specs=in_specs, out_specs=out_specs, out_shape=out_shape,
        scratch_shapes=list(scratch), compiler_params=_params(1),
    )(*args)


def _dep_extra(dep):
    return [] if dep is None else [(dep, dep.shape, lambda i, j: (0, 0))]


def _row(arr, tm):
    return (arr, (tm, arr.shape[1]), lambda i: (i, 0))


def _full(arr):
    nd = arr.ndim
    return (arr, arr.shape, lambda i: (0,) * nd)


def _row_out(shape, dtype, tm):
    return (shape, dtype, (tm, shape[1]), lambda i: (i, 0))


def _acc_out(shape):
    nd = len(shape)
    return (shape, F32, shape, lambda i: (0,) * nd)


def _tm(t):
    return 512 if t >= 1024 else t // 2


def _row_tile(rows, cap=256):
    best = max(tr for tr in range(16, min(rows, cap) + 1, 16) if rows % tr == 0)
    return best


def _first(i, refs):
    @pl.when(i == 0)
    def _():
        for r in refs:
            r[...] = jnp.zeros(r.shape, r.dtype)


def _rms_fwd(x, g, tag):
    t, d = x.shape
    tm = _tm(t)

    def body(x_ref, g_ref, h_ref):
        xv = x_ref[...]
        r = lax.rsqrt(jnp.mean(xv * xv, axis=-1, keepdims=True) + RMS_EPS)
        h_ref[...] = (xv * r * g_ref[...]).astype(BF16)

    return _rows(f"rms_fwd_{tag}", t // tm, [_row(x, tm), _full(g)], [_row_out((t, d), BF16, tm)], body)[0]


def _rms_bwd(dh, x, g, dx_out, tag):
    t, d = x.shape
    tm = _tm(t)

    def body(dh_ref, x_ref, g_ref, dxo_ref, dx_ref, dg_ref):
        i = pl.program_id(0)
        _first(i, [dg_ref])
        xv = x_ref[...]
        r = lax.rsqrt(jnp.mean(xv * xv, axis=-1, keepdims=True) + RMS_EPS)
        xh = xv * r
        dhv = dh_ref[...]
        dxh = dhv * g_ref[...]
        dx_ref[...] = dxo_ref[...] + r * (dxh - xh * jnp.mean(dxh * xh, axis=-1, keepdims=True))
        dg_ref[...] += jnp.sum(dhv * xh, axis=0, keepdims=True)

    return _rows(f"rms_bwd_{tag}", t // tm, [_row(dh, tm), _row(x, tm), _full(g), _row(dx_out, tm)],
                 [_row_out((t, d), F32, tm), _acc_out((1, d))], body)


def _loss_head(x, g, target):
    t, d = x.shape
    tm = _tm(t)

    def body(x_ref, g_ref, t_ref, dx_ref, dg_ref, loss_ref):
        i = pl.program_id(0)
        _first(i, [dg_ref, loss_ref])
        xv = x_ref[...]
        gv = g_ref[...]
        r = lax.rsqrt(jnp.mean(xv * xv, axis=-1, keepdims=True) + RMS_EPS)
        xh = xv * r
        err = xh * gv - t_ref[...]
        loss_ref[...] += jnp.full((1, LANES), 0.5, F32) * jnp.sum(jnp.mean(err * err, axis=-1, keepdims=True))
        dy = err * (1.0 / d)
        dxh = dy * gv
        dx_ref[...] = r * (dxh - xh * jnp.mean(dxh * xh, axis=-1, keepdims=True))
        dg_ref[...] += jnp.sum(dy * xh, axis=0, keepdims=True)

    return _rows("loss_head", t // tm, [_row(x, tm), _full(g), _row(target, tm)],
                 [_row_out((t, d), F32, tm), _acc_out((1, d)), _acc_out((1, LANES))], body)


def _colsum(arr, tag):
    t, w = arr.shape
    tm = _tm(t)

    def body(a_ref, o_ref):
        _first(pl.program_id(0), [o_ref])
        o_ref[...] += jnp.sum(a_ref[...].astype(F32), axis=0, keepdims=True)

    return _rows(f"colsum_{tag}", t // tm, [_row(arr, tm)], [_acc_out((1, w))], body)[0]


def _ffn_fwd(x, gn, w1, w3, get_w2, tag):
    t, d = x.shape
    fc = w1.shape[2]
    f = N_CHIPS * fc
    tm = _tm(t)
    h = _rms_fwd(x, gn, f"ffn_{tag}")
    w1blk, w1map = _col_nn(0, d, fc, fc)
    w3blk, w3map = _col_nn(0, d, fc, fc)
    tma = min(tm, 256)

    def ep_ab(accs, ex):
        a, b = accs
        return a, b, a * jax.nn.sigmoid(a) * b

    a, b, act = _mm(
        f"ffn_ab_{tag}", (t // tma, N_CHIPS, 1),
        [(h, (tma, d), _ik, w1, w1blk, w1map, "nn", 0, False), (h, (tma, d), _ik, w3, w3blk, w3map, "nn", 1, False)],
        [], [((t, f), BF16, (tma, fc), _ij)] * 3, ep_ab, [(tma, fc)] * 2, order="ji")
    w2 = get_w2(act)
    tn = min(d, 1024)
    tmo = min(t, 2 * tm)
    w2blk, w2map = _row_nn(0, fc, tn, fc)
    x_new = _mm(
        f"ffn_out_{tag}", (t // tmo, d // tn, N_CHIPS),
        [(act, (tmo, fc), _ik, w2, w2blk, w2map, "nn", 0, False)],
        [(x, (tmo, tn), _ij)], [((t, d), F32, (tmo, tn), _ij)],
        lambda accs, ex: [ex[0] + FFN_RESIDUAL_WEIGHT * accs[0]], [(tmo, tn)])[0]
    return x_new, (x, h, a, b, act), w2


def _ffn_bwd(dx_out, saved, gn, w1, w3, w2, tag, dep, put):
    x, h, a, b, act = saved
    t, d = x.shape
    fc = w1.shape[2]
    f = N_CHIPS * fc
    tm = _tm(t)
    tma = min(tm, 256)
    w2blk, w2map = _row_nt(0, fc, d, fc)

    def ep_dg(accs, ex):
        dg = FFN_RESIDUAL_WEIGHT * accs[0]
        av, bv = ex[0].astype(F32), ex[1].astype(F32)
        sig = jax.nn.sigmoid(av)
        return dg * bv * sig * (1.0 + av * (1.0 - sig)), dg * av * sig

    da, db = _mm(
        f"ffn_dg_{tag}", (t // tma, N_CHIPS, 1),
        [(dx_out, (tma, d), _ik, w2, w2blk, w2map, "nt", 0, False)],
        [(a, (tma, fc), _ij), (b, (tma, fc), _ij)] + _dep_extra(dep), [((t, f), BF16, (tma, fc), _ij)] * 2, ep_dg, [(tma, fc)], order="ji")

    tmd = min(d, 512)
    tk = min(t, 2048)
    oblk, omap = _out_col(tmd, fc, fc)
    dw1, dw3 = _mm(
        f"ffn_dw13_{tag}", (d // tmd, N_CHIPS, t // tk),
        [(h, (tk, tmd), _ki, da, (tk, fc), _kj, "tn", 0, False), (h, (tk, tmd), _ki, db, (tk, fc), _kj, "tn", 1, False)],
        [], [((N_CHIPS, d, fc), BF16, oblk, omap)] * 2, lambda accs, ex: accs, [(tmd, fc)] * 2)
    tn = min(d, 1024)
    tk2 = min(t, 2048)
    dw2 = _mm(
        f"ffn_dw2_{tag}", (N_CHIPS, d // tn, t // tk2),
        [(act, (tk2, fc), _ki, dx_out, (tk2, tn), _kj, "tn", 0, False)],
        [], [((N_CHIPS, fc, d), BF16, (None, fc, tn), lambda i, j: (i, 0, j))],
        lambda accs, ex: [FFN_RESIDUAL_WEIGHT * accs[0]], [(fc, tn)])[0]
    token = put({"w1": dw1, "w3": dw3, "w2": dw2})
    w1blk, w1map = _col_nt(0, tn, fc, fc)
    w3blk, w3map = _col_nt(0, tn, fc, fc)
    dh = _mm(
        f"ffn_dh_{tag}", (t // tm, d // tn, N_CHIPS),
        [(da, (tm, fc), _ik, w1, w1blk, w1map, "nt", 0, False), (db, (tm, fc), _ik, w3, w3blk, w3map, "nt", 0, False)],
        _dep_extra(token), [((t, d), F32, (tm, tn), _ij)], lambda accs, ex: accs, [(tm, tn)])[0]
    dx, dgn = _rms_bwd(dh, x, gn, dx_out, f"ffn_{tag}")
    return dx, dgn, token


def _ple_fwd(x, p_i, gn, wg, wp, tag):
    t, d = x.shape
    pdim = p_i.shape[1]
    d4 = d // N_CHIPS
    tm = _tm(t)
    hp = _rms_fwd(x, gn, f"ple_{tag}")
    tm = min(t, 2 * tm)

    def ep(accs, ex):
        s = jax.nn.sigmoid(accs[0])
        return ex[0] + s * accs[1], s, accs[1]

    x_new, s, pp = _mm(
        f"ple_fwd_{tag}", (t // tm, N_CHIPS, 1),
        [(hp, (tm, d), _i0, wg, (N_CHIPS, d4, d4), lambda i, j, k: (0, 0, j), "nn", 0, False),
         (p_i, (tm, pdim), _i0, wp, (None, pdim, d4), lambda i, j, k: (j, 0, 0), "nn", 1, False)],
        [(x, (tm, d4), _ij)], [((t, d), F32, (tm, d4), _ij), ((t, d), BF16, (tm, d4), _ij), ((t, d), BF16, (tm, d4), _ij)],
        ep, [(tm, d4)] * 2)
    return x_new, (x, hp, s, pp)


def _ple_bwd(dx_out, saved, p_i, gn, wg, tag, dep=None):
    x, hp, s, pp = saved
    t, d = x.shape
    pdim = p_i.shape[1]
    d4 = d // N_CHIPS
    tm = _tm(t)

    def body(dx_ref, s_ref, pp_ref, dpp_ref, dgp_ref):
        dxv = dx_ref[...]
        sv = s_ref[...].astype(F32)
        ppv = pp_ref[...].astype(F32)
        dpp_ref[...] = (dxv * sv).astype(BF16)
        dgp_ref[...] = (dxv * ppv * sv * (1.0 - sv)).astype(BF16)

    dpp, dgp = _rows(f"ple_ew_{tag}", t // tm, [_row(dx_out, tm), _row(s, tm), _row(pp, tm)],
                     [_row_out((t, d), BF16, tm)] * 2, body, deps=[] if dep is None else [dep])
    tk = min(t, 2048)
    dwp = _mm(
        f"ple_dwp_{tag}", (1, N_CHIPS, t // tk),
        [(p_i, (tk, pdim), lambda i, j, k: (k, 0), dpp, (tk, d4), _kj, "tn", 0, False)],
        [], [((N_CHIPS, pdim, d4), BF16, (None, pdim, d4), lambda i, j: (j, 0, 0))], lambda accs, ex: accs, [(pdim, d4)])[0]
    tn = min(d, 1024)
    dwg = _mm(
        f"ple_dwg_{tag}", (N_CHIPS, d // tn, t // tk),
        [(hp, (tk, d4), _ki, dgp, (tk, tn), _kj, "tn", 0, False)],
        [], [((N_CHIPS, d4, d), BF16, (None, d4, tn), lambda i, j: (i, 0, j))], lambda accs, ex: accs, [(d4, tn)])[0]
    dhp = _mm(
        f"ple_dh_{tag}", (t // tm, 1, 1),
        [(dgp, (tm, d), _ik, wg, (N_CHIPS, d4, d), lambda i, j, k: (0, 0, 0), "nt", 0, False)],
        [], [((t, d), F32, (tm, d), _ij)], lambda accs, ex: accs, [(tm, d)])[0]
    dx, dgn = _rms_bwd(dhp, x, gn, dx_out, f"ple_{tag}")
    return dx, dgn, dwg, dwp


_SQRT_HALF = 0.7071067811865476
_INV_SQRT_2PI = 0.3989422804014327


def _gelu(z):
    return z * (lax.erf(z * _SQRT_HALF) + 1.0) * 0.5


def _gelu_grad(z):
    return 0.5 * (1.0 + lax.erf(z * _SQRT_HALF)) + z * (_INV_SQRT_2PI * jnp.exp(-0.5 * z * z))


def _causal_bf16(w):
    c = w.shape[0]
    keep = lax.broadcasted_iota(jnp.int32, (c, c), 0) >= lax.broadcasted_iota(jnp.int32, (c, c), 1)
    return jnp.where(keep, w, 0.0).astype(BF16), keep


def _gmlp_gate_fwd(z_pre, ln_g, ln_b, w_s, b_st, tag):
    t, w2 = z_pre.shape
    gw = w2 // 2
    n_g, chunk, _ = w_s.shape
    gd = gw // n_g

    def body(z_ref, g_ref, b_ref, ws_ref, bs_ref, o_ref):
        z = z_ref[...]
        u = _gelu(z[:, :gw])
        zv = _gelu(z[:, gw:])
        mu = jnp.mean(zv, axis=-1, keepdims=True)
        cen = zv - mu
        rstd = lax.rsqrt(jnp.mean(cen * cen, axis=-1, keepdims=True) + LN_EPS)
        vln = (cen * rstd * g_ref[...] + b_ref[...]).astype(BF16)
        bst = bs_ref[...]
        for g in range(n_g):
            wm, _ = _causal_bf16(ws_ref[g])
            sl = slice(g * gd, (g + 1) * gd)
            s = jnp.dot(wm, vln[:, sl], preferred_element_type=F32) + bst[:, g:g + 1]
            o_ref[:, sl] = (u[:, sl] * s).astype(BF16)

    return _rows(f"gmlp_gate_{tag}", t // chunk, [_row(z_pre, chunk), _full(ln_g), _full(ln_b), _full(w_s), _full(b_st)],
                 [_row_out((t, gw), BF16, chunk)], body)[0]


def _gmlp_gate_bwd(z_pre, dgated, ln_g, ln_b, w_s, b_st, tag):
    t, w2 = z_pre.shape
    gw = w2 // 2
    n_g, chunk, _ = w_s.shape
    gd = gw // n_g

    def body(z_ref, dgt_ref, g_ref, b_ref, ws_ref, bs_ref, dz_ref, dws_ref, dbs_ref, dlg_ref, dlb_ref, dv_scr):
        _first(pl.program_id(0), [dws_ref, dbs_ref, dlg_ref, dlb_ref])
        z = z_ref[...]
        zu, zvp = z[:, :gw], z[:, gw:]
        u = _gelu(zu)
        zv = _gelu(zvp)
        mu = jnp.mean(zv, axis=-1, keepdims=True)
        cen = zv - mu
        rstd = lax.rsqrt(jnp.mean(cen * cen, axis=-1, keepdims=True) + LN_EPS)
        vn = cen * rstd
        lg = g_ref[...]
        vln = (vn * lg + b_ref[...]).astype(BF16)
        bst = bs_ref[...]
        lane = lax.broadcasted_iota(jnp.int32, (chunk, LANES), 1)
        dbs = jnp.zeros((chunk, LANES), F32)
        for g in range(n_g):
            wm, keep = _causal_bf16(ws_ref[g])
            sl = slice(g * gd, (g + 1) * gd)
            vg = vln[:, sl]
            s = jnp.dot(wm, vg, preferred_element_type=F32) + bst[:, g:g + 1]
            dgt = dgt_ref[:, sl].astype(F32)
            ds = dgt * u[:, sl]
            ds16 = ds.astype(BF16)
            dz_ref[:, sl] = (dgt * s * _gelu_grad(zu[:, sl])).astype(dz_ref.dtype)
            dv_scr[:, sl] = lax.dot_general(wm, ds16, _DIMS["tn"], preferred_element_type=F32)
            dw = lax.dot_general(ds16, vg, _DIMS["nt"], preferred_element_type=F32)
            dws_ref[g] += jnp.where(keep, dw, 0.0)
            dbs = dbs + jnp.where(lane == g, jnp.sum(ds, axis=-1, keepdims=True), 0.0)
        dbs_ref[...] += dbs
        dvln = dv_scr[...]
        dlg_ref[...] += jnp.sum(dvln * vn, axis=0, keepdims=True)
        dlb_ref[...] += jnp.sum(dvln, axis=0, keepdims=True)
        dvn = dvln * lg
        dzv = rstd * (dvn - jnp.mean(dvn, axis=-1, keepdims=True) - vn * jnp.mean(dvn * vn, axis=-1, keepdims=True))
        dz_ref[:, gw:] = (dzv * _gelu_grad(zvp)).astype(dz_ref.dtype)

    return _rows(
        f"gmlp_gate_bwd_{tag}", t // chunk,
        [_row(z_pre, chunk), _row(dgated, chunk), _full(ln_g), _full(ln_b), _full(w_s), _full(b_st)],
        [_row_out((t, w2), BF16, chunk), _acc_out((n_g, chunk, chunk)), _acc_out((chunk, LANES)), _acc_out((1, gw)),
         _acc_out((1, gw))], body, scratch=[pltpu.VMEM((chunk, gw), F32)])


def _gmlp_fwd(x, gn, ln_g, ln_b, w_s, b_st, w_in, w_out, tag):
    t, d = x.shape
    gw = ln_g.shape[1]
    tm = _tm(t)
    d4 = d // N_CHIPS
    h = _rms_fwd(x, gn, f"mix_{tag}")
    tn = min(d, 1024)
    iblk, imap = _col_nn(0, d, tn, d)
    z_pre = _mm(
        f"gmlp_in_{tag}", (t // tm, N_CHIPS * d // tn, 1),
        [(h, (tm, d), _ik, w_in, iblk, imap, "nn", 0, False)],
        [], [((t, 2 * gw), F32, (tm, tn), _ij)], lambda accs, ex: accs, [(tm, tn)], order="ji")[0]
    gated = _gmlp_gate_fwd(z_pre, ln_g, ln_b, w_s, b_st, tag)
    rc = gw // N_CHIPS
    tk = min(rc, 1024)
    tmo = min(t, 2 * tm)
    oblk, omap = _row_nn(0, tk, tn, rc)
    x_new = _mm(
        f"gmlp_out_{tag}", (t // tmo, d // tn, gw // tk),
        [(gated, (tmo, tk), _ik, w_out, oblk, omap, "nn", 0, False)],
        [(x, (tmo, tn), _ij)], [((t, d), F32, (tmo, tn), _ij)], lambda accs, ex: [ex[0] + accs[0]], [(tmo, tn)])[0]
    return x_new, (x, h, z_pre, gated)


def _gmlp_bwd(dx_out, saved, gn, ln_g, ln_b, w_s, b_st, w_in, w_out, tag, dep=None):
    x, h, z_pre, gated = saved
    t, d = x.shape
    gw = ln_g.shape[1]
    tm = _tm(t)
    d4 = d // N_CHIPS
    rc = gw // N_CHIPS
    tnr = min(rc, 1024)
    oblk, omap = _row_nt(0, tnr, d, rc)
    dgated = _mm(
        f"gmlp_dgated_{tag}", (t // tm, gw // tnr, 1),
        [(dx_out, (tm, d), _ik, w_out, oblk, omap, "nt", 0, False)],
        _dep_extra(dep), [((t, gw), BF16, (tm, tnr), _ij)], lambda accs, ex: accs, [(tm, tnr)], order="ji")[0]
    tk = min(t, 2048)
    tn = min(d, 1024)
    rblk, rmap = _out_row(tnr, tn, rc)
    dw_out = _mm(
        f"gmlp_dwout_{tag}", (gw // tnr, d // tn, t // tk),
        [(gated, (tk, tnr), _ki, dx_out, (tk, tn), _kj, "tn", 0, False)],
        [], [((N_CHIPS, rc, d), BF16, rblk, rmap)], lambda accs, ex: accs, [(tnr, tn)])[0]
    dz, dws, dbs, dlg, dlb = _gmlp_gate_bwd(z_pre, dgated, ln_g, ln_b, w_s, b_st, tag)
    tmd = min(d, 512)
    cblk, cmap = _out_col(tmd, tn, d)
    dw_in = _mm(
        f"gmlp_dwin_{tag}", (d // tmd, N_CHIPS * d // tn, t // tk),
        [(h, (tk, tmd), _ki, dz, (tk, tn), _kj, "tn", 0, False)],
        [], [((N_CHIPS, d, d), BF16, cblk, cmap)], lambda accs, ex: accs, [(tmd, tn)])[0]
    iblk, imap = _col_nt(0, tn, d, d)
    dh = _mm(
        f"gmlp_dh_{tag}", (t // tm, d // tn, N_CHIPS),
        [(dz, (tm, d), _ik, w_in, iblk, imap, "nt", 0, False)],
        [], [((t, d), F32, (tm, tn), _ij)], lambda accs, ex: accs, [(tm, tn)])[0]
    dx, dgn = _rms_bwd(dh, x, gn, dx_out, f"mix_{tag}")
    n_g = w_s.shape[0]
    return dx, dgn, dw_in, dw_out, dws, dbs[:, :n_g].T, dlg, dlb


def _rope_tables(t, width, n_rot_heads):
    half = ROPE_DIM // 2
    inv_freq = ROPE_THETA ** (-jnp.arange(0, ROPE_DIM, 2, dtype=F32) / ROPE_DIM)
    ang = jnp.arange(t, dtype=F32)[:, None] * inv_freq[None, :]
    cos, sin = jnp.cos(ang), jnp.sin(ang)
    ones = jnp.ones((t, HEAD_DIM - ROPE_DIM), F32)
    zeros = jnp.zeros((t, HEAD_DIM - ROPE_DIM), F32)
    zh = jnp.zeros((t, half), F32)
    c = jnp.concatenate([cos, cos, ones], axis=1)
    s_next = jnp.concatenate([-sin, zh, zeros], axis=1)
    s_prev = jnp.concatenate([zh, sin, zeros], axis=1)
    rest = width - n_rot_heads * HEAD_DIM

    def widen(tab, fill):
        parts = [jnp.tile(tab, (1, n_rot_heads))]
        if rest:
            parts.append(jnp.full((t, rest), fill, F32))
        return jnp.concatenate(parts, axis=1)

    return widen(c, 1.0), widen(s_next, 0.0), widen(s_prev, 0.0)


def _rope(v, c, s_next, s_prev, sign):
    w = v.shape[1]
    half = ROPE_DIM // 2
    reps = w // c.shape[1]
    if reps > 1:
        c, s_next, s_prev = (jnp.tile(tab, (1, reps)) for tab in (c, s_next, s_prev))
    return v * c + sign * (pltpu.roll(v, w - half, 1) * s_next + pltpu.roll(v, half, 1) * s_prev)


def _scores_mask(n, blk, heads):
    row = lax.broadcasted_iota(jnp.int32, (heads * blk, 2 * blk), 0)
    q_pos = lax.rem(row, blk) + blk
    k_pos = lax.broadcasted_iota(jnp.int32, (heads * blk, 2 * blk), 1)
    diff = q_pos - k_pos
    band = (diff >= 0) & (diff < blk)
    return band & ((k_pos >= blk) | (n > 0))


def _heads_to_rows(v, first, heads):
    return jnp.concatenate([v[:, (first + g) * HEAD_DIM:(first + g + 1) * HEAD_DIM] for g in range(heads)], axis=0)


def _sink_column(sink_ref, first, heads, blk):
    return jnp.concatenate([jnp.full((blk, 1), sink_ref[0, first + g], F32) for g in range(heads)], axis=0)


def _attn_fwd(q, kv, sinks, blk):
    t, qw = q.shape
    kw = kv.shape[1] // 2
    n_kv = kw // HEAD_DIM
    q_per_kv = qw // kw
    scale = HEAD_DIM ** -0.5

    def body(sink_ref, q_ref, kvo_ref, kvp_ref, o_ref):
        n = pl.program_id(0)
        valid = _scores_mask(n, blk, 1)
        qv = q_ref[...]
        kvo = kvo_ref[...]
        kvp = kvp_ref[...]
        for kh in range(n_kv):
            ks = slice(kh * HEAD_DIM, (kh + 1) * HEAD_DIM)
            vs = slice(kw + kh * HEAD_DIM, kw + (kh + 1) * HEAD_DIM)
            kb = jnp.concatenate([kvp[:, ks], kvo[:, ks]], axis=0)
            vb = jnp.concatenate([kvp[:, vs], kvo[:, vs]], axis=0)
            for g in range(q_per_kv):
                hd = kh * q_per_kv + g
                hs = slice(hd * HEAD_DIM, (hd + 1) * HEAD_DIM)
                sink = sink_ref[0, hd]
                s = lax.dot_general(qv[:, hs], kb, _DIMS["nt"], preferred_element_type=F32) * scale
                s = jnp.where(valid, s, -1e30)
                m = jnp.maximum(jnp.max(s, axis=-1, keepdims=True), sink)
                e = jnp.where(valid, jnp.exp(s - m), 0.0)
                denom = jnp.sum(e, axis=-1, keepdims=True) + jnp.exp(sink - m)
                p = (e / denom).astype(BF16)
                o_ref[:, hs] = jnp.dot(p, vb, preferred_element_type=F32).astype(BF16)

    return _rows(
        "swa_attn", t // blk,
        [(sinks, pl.BlockSpec(memory_space=pltpu.SMEM)), _row(q, blk), _row(kv, blk),
         (kv, (blk, kv.shape[1]), lambda i: (jnp.maximum(i - 1, 0), 0))],
        [_row_out((t, qw), BF16, blk)], body)[0]


def _attn_bwd(q, kv, do, sinks, blk):
    t, qw = q.shape
    kw = kv.shape[1] // 2
    n_kv = kw // HEAD_DIM
    q_per_kv = qw // kw
    scale = HEAD_DIM ** -0.5

    def body(sink_ref, q_ref, kvo_ref, kvp_ref, do_ref, dq_ref, dkvo_ref, dkvp_ref, dsink_ref):
        n = pl.program_id(0)
        _first(n, [dsink_ref])
        valid = _scores_mask(n, blk, q_per_kv)
        lane = lax.broadcasted_iota(jnp.int32, (1, LANES), 1)
        qv = q_ref[...]
        kvo = kvo_ref[...]
        kvp = kvp_ref[...]
        dov = do_ref[...]
        dsink = jnp.zeros((1, LANES), F32)
        for kh in range(n_kv):
            ks = slice(kh * HEAD_DIM, (kh + 1) * HEAD_DIM)
            vs = slice(kw + kh * HEAD_DIM, kw + (kh + 1) * HEAD_DIM)
            kb = jnp.concatenate([kvp[:, ks], kvo[:, ks]], axis=0)
            vb = jnp.concatenate([kvp[:, vs], kvo[:, vs]], axis=0)
            qg = _heads_to_rows(qv, kh * q_per_kv, q_per_kv)
            dog = _heads_to_rows(dov, kh * q_per_kv, q_per_kv)
            sink = _sink_column(sink_ref, kh * q_per_kv, q_per_kv, blk)
            s = lax.dot_general(qg, kb, _DIMS["nt"], preferred_element_type=F32) * scale
            s = jnp.where(valid, s, -1e30)
            m = jnp.maximum(jnp.max(s, axis=-1, keepdims=True), sink)
            e = jnp.where(valid, jnp.exp(s - m), 0.0)
            e_sink = jnp.exp(sink - m)
            inv = 1.0 / (jnp.sum(e, axis=-1, keepdims=True) + e_sink)
            p = e * inv
            p16 = p.astype(BF16)
            dp = lax.dot_general(dog, vb, _DIMS["nt"], preferred_element_type=F32)
            dot_pd = jnp.sum(p * dp, axis=-1, keepdims=True)
            ds16 = (p * (dp - dot_pd)).astype(BF16)
            d_sink_rows = e_sink * inv * dot_pd
            dqg = (jnp.dot(ds16, kb, preferred_element_type=F32) * scale).astype(BF16)
            for g in range(q_per_kv):
                hd = kh * q_per_kv + g
                rows = slice(g * blk, (g + 1) * blk)
                dsink = dsink + jnp.where(lane == hd, -jnp.sum(d_sink_rows[rows], axis=0, keepdims=True), 0.0)
                dq_ref[:, hd * HEAD_DIM:(hd + 1) * HEAD_DIM] = dqg[rows]
            dkb = lax.dot_general(ds16, qg, _DIMS["tn"], preferred_element_type=F32) * scale
            dvb = lax.dot_general(p16, dog, _DIMS["tn"], preferred_element_type=F32)
            dkvp_ref[:, ks] = dkb[:blk]
            dkvo_ref[:, ks] = dkb[blk:]
            dkvp_ref[:, vs] = dvb[:blk]
            dkvo_ref[:, vs] = dvb[blk:]
        dsink_ref[...] += dsink

    return _rows(
        "swa_attn_bwd", t // blk,
        [(sinks, pl.BlockSpec(memory_space=pltpu.SMEM)), _row(q, blk), _row(kv, blk),
         (kv, (blk, kv.shape[1]), lambda i: (jnp.maximum(i - 1, 0), 0)), _row(do, blk)],
        [_row_out((t, qw), BF16, blk), _row_out((t, 2 * kw), F32, blk), _row_out((t, 2 * kw), F32, blk),
         _acc_out((1, LANES))], body)


def _rope_bwd(dq_r, dkv_own, dkv_prev, tabs_q, tabs_kv, blk):
    t, qw = dq_r.shape
    kvw = dkv_own.shape[1]
    nb = t // blk

    def body(dq_ref, own_ref, nxt_ref, cq, snq, spq, ck, snk, spk, dqo_ref, dkvo_ref, dbq_ref, dbkv_ref):
        i = pl.program_id(0)
        _first(i, [dbq_ref, dbkv_ref])
        dq = _rope(dq_ref[...].astype(F32), cq[...], snq[...], spq[...], -1.0)
        dkv = own_ref[...] + jnp.where(i < nb - 1, nxt_ref[...], 0.0)
        dkv = _rope(dkv, ck[...], snk[...], spk[...], -1.0)
        dqo_ref[...] = dq.astype(BF16)
        dkvo_ref[...] = dkv.astype(BF16)
        dbq_ref[...] += jnp.sum(dq, axis=0, keepdims=True)
        dbkv_ref[...] += jnp.sum(dkv, axis=0, keepdims=True)

    return _rows(
        "swa_rope_bwd", nb,
        [_row(dq_r, blk), _row(dkv_own, blk), (dkv_prev, (blk, kvw), lambda i: (jnp.minimum(i + 1, nb - 1), 0))]
        + [_row(tab, blk) for tab in tabs_q] + [_row(tab, blk) for tab in tabs_kv],
        [_row_out((t, qw), BF16, blk), _row_out((t, kvw), BF16, blk), _acc_out((1, qw)), _acc_out((1, kvw))], body)


def _swa_fwd(x, gn, bq, bkv, bo, sinks, wq, wkv, wo, blk, tag):
    t, d = x.shape
    qw = bq.shape[1]
    kvw = bkv.shape[1]
    d4 = d // N_CHIPS
    tm = _tm(t)
    h = _rms_fwd(x, gn, f"mix_{tag}")
    tabs_q = _rope_tables(t, LANES, LANES // HEAD_DIM)
    tabs_kv = _rope_tables(t, kvw, kvw // 2 // HEAD_DIM)
    tn = min(qw, 1024)
    whole = lambda i, j, k: (0, 0, j)

    def ep_rope(accs, ex):
        return [_rope(accs[0] + ex[0], ex[1], ex[2], ex[3], 1.0)]

    q = _mm(
        f"swa_q_{tag}", (t // tm, qw // tn, 1),
        [(h, (tm, d), _ik, wq, (N_CHIPS, d4, tn), whole, "nn", 0, False)],
        [(bq, (1, tn), _0j)] + [(tab, (tm, LANES), _i0e) for tab in tabs_q],
        [((t, qw), BF16, (tm, tn), _ij)], ep_rope, [(tm, tn)], order="ji")[0]
    kv = _mm(
        f"swa_kv_{tag}", (t // tm, 1, 1),
        [(h, (tm, d), _ik, wkv, (N_CHIPS, d4, kvw), whole, "nn", 0, False)],
        [(bkv, (1, kvw), _0j)] + [(tab, (tm, kvw), _i0e) for tab in tabs_kv],
        [((t, kvw), BF16, (tm, kvw), _ij)], ep_rope, [(tm, kvw)])[0]
    o = _attn_fwd(q, kv, sinks, blk)
    tno = min(d, 1024)
    x_new = _mm(
        f"swa_out_{tag}", (t // tm, d // tno, 1),
        [(o, (tm, qw), _ik, wo, (N_CHIPS, qw // N_CHIPS, tno), whole, "nn", 0, False)],
        [(x, (tm, tno), _ij), (bo, (1, tno), _0j)], [((t, d), F32, (tm, tno), _ij)],
        lambda accs, ex: [ex[0] + accs[0] + ex[1]], [(tm, tno)], order="ji")[0]
    return x_new, (x, h, q, kv, o, tabs_q, tabs_kv)


def _swa_bwd(dx_out, saved, gn, sinks, wq, wkv, wo, blk, tag, dep=None):
    x, h, q, kv, o, tabs_q, tabs_kv = saved
    t, d = x.shape
    qw = q.shape[1]
    kvw = kv.shape[1]
    d4 = d // N_CHIPS
    qw4 = qw // N_CHIPS
    tm = _tm(t)
    whole = lambda i, j, k: (0, 0, 0)
    do = _mm(
        f"swa_do_{tag}", (t // tm, 1, 1),
        [(dx_out, (tm, d), _ik, wo, (N_CHIPS, qw4, d), whole, "nt", 0, False)],
        _dep_extra(dep), [((t, qw), BF16, (tm, qw), _ij)], lambda accs, ex: accs, [(tm, qw)])[0]
    tk = min(t, 2048)
    tn = min(d, 1024)
    dwo = _mm(
        f"swa_dwo_{tag}", (N_CHIPS, d // tn, t // tk),
        [(o, (tk, qw4), _ki, dx_out, (tk, tn), _kj, "tn", 0, False)],
        [], [((N_CHIPS, qw4, d), BF16, (None, qw4, tn), lambda i, j: (i, 0, j))], lambda accs, ex: accs, [(qw4, tn)])[0]
    dbo = _colsum(dx_out, f"bo_{tag}")
    dq_r, dkv_own, dkv_prev, dsink = _attn_bwd(q, kv, do, sinks, blk)
    dq, dkv, dbq, dbkv = _rope_bwd(dq_r, dkv_own, dkv_prev, tabs_q, tabs_kv, blk)
    tnq = min(qw, 1024)
    dwq = _mm(
        f"swa_dwq_{tag}", (N_CHIPS, qw // tnq, t // tk),
        [(h, (tk, d4), _ki, dq, (tk, tnq), _kj, "tn", 0, False)],
        [], [((N_CHIPS, d4, qw), BF16, (None, d4, tnq), lambda i, j: (i, 0, j))], lambda accs, ex: accs, [(d4, tnq)])[0]
    dwkv = _mm(
        f"swa_dwkv_{tag}", (N_CHIPS, 1, t // tk),
        [(h, (tk, d4), _ki, dkv, (tk, kvw), _kj, "tn", 0, False)],
        [], [((N_CHIPS, d4, kvw), BF16, (None, d4, kvw), lambda i, j: (i, 0, j))], lambda accs, ex: accs, [(d4, kvw)])[0]
    dh = _mm(
        f"swa_dh_{tag}", (t // tm, 1, 1),
        [(dq, (tm, qw), _ik, wq, (N_CHIPS, d4, qw), whole, "nt", 0, False),
         (dkv, (tm, kvw), _ik, wkv, (N_CHIPS, d4, kvw), whole, "nt", 0, False)],
        [], [((t, d), F32, (tm, d), _ij)], lambda accs, ex: accs, [(tm, d)])[0]
    dx, dgn = _rms_bwd(dh, x, gn, dx_out, f"mix_{tag}")
    n_heads = qw // HEAD_DIM
    return dx, dgn, dwq, dwkv, dwo, dbq, dbkv, dbo, dsink[:, :n_heads]


_HBM = pl.BlockSpec(memory_space=pl.ANY)
_CHIP_FLIPS = ((1, 0), (0, 1), (1, 1))


def _place():
    x, y, c = lax.axis_index("x"), lax.axis_index("y"), lax.axis_index("c")
    return x, y, c


def _flip(v, bit):
    return 1 - v if bit else v


def _exchange_small(v_ref, all_ref, send_sems, recv_sems):
    x, y, c = _place()
    me = 4 * x + 2 * y + c
    all_ref[me] = v_ref[...]
    copies = []
    for dlt in range(1, N_DEV):
        peer = (_flip(x, dlt & 4), _flip(y, dlt & 2), _flip(c, dlt & 1))
        copies.append(pltpu.make_async_remote_copy(
            src_ref=v_ref, dst_ref=all_ref.at[me], send_sem=send_sems.at[dlt - 1], recv_sem=recv_sems.at[dlt - 1],
            device_id=peer, device_id_type=MESH_ID))
    for cp in copies:
        cp.start()
    for cp in copies:
        cp.wait()


def _all_gather_small(v):
    r, cdim = v.shape

    def body(v_ref, out_ref, send_sems, recv_sems):
        _exchange_small(v_ref, out_ref, send_sems, recv_sems)

    return pl.pallas_call(
        body, name="all_gather_small", out_shape=jax.ShapeDtypeStruct((N_DEV, r, cdim), v.dtype),
        scratch_shapes=[pltpu.SemaphoreType.DMA((N_DEV - 1,)), pltpu.SemaphoreType.DMA((N_DEV - 1,))],
        compiler_params=pltpu.CompilerParams(vmem_limit_bytes=VMEM_LIMIT_BYTES),
    )(v)


def _all_sum_small(v):
    r, cdim = v.shape

    def body(v_ref, out_ref, all_ref, send_sems, recv_sems):
        _exchange_small(v_ref, all_ref, send_sems, recv_sems)
        acc = all_ref[0]
        for dv in range(1, N_DEV):
            acc = acc + all_ref[dv]
        out_ref[...] = acc

    return pl.pallas_call(
        body, name="all_sum_small", out_shape=jax.ShapeDtypeStruct((r, cdim), v.dtype),
        scratch_shapes=[pltpu.VMEM((N_DEV, r, cdim), v.dtype), pltpu.SemaphoreType.DMA((N_DEV - 1,)),
                        pltpu.SemaphoreType.DMA((N_DEV - 1,))],
        compiler_params=pltpu.CompilerParams(vmem_limit_bytes=VMEM_LIMIT_BYTES),
    )(v)


_HBM_SPEC = pl.BlockSpec(memory_space=pltpu.HBM)
_SEM_SPEC = pl.BlockSpec(memory_space=pltpu.SEMAPHORE)
_EFFECT = pltpu.SideEffectType.DATAFLOW_SIDE_EFFECTING
_TOKEN = jax.ShapeDtypeStruct((8, LANES), F32)


def _in_hbm(a):
    return pltpu.with_memory_space_constraint(a, pltpu.HBM)


def _hbm_like(a):
    return pltpu.HBM(a.shape, a.dtype)


def _ici_copies(shard, land, send_sems, recv_sems, base, x, y, c):
    half = shard.shape[0] // 2
    rows = pl.ds(c * half, half)
    return [pltpu.make_async_remote_copy(
        src_ref=shard.at[rows], dst_ref=land.at[2 * x + y, rows], send_sem=send_sems.at[base + j], recv_sem=recv_sems.at[base + j],
        device_id=(_flip(x, fx), _flip(y, fy), c), device_id_type=MESH_ID) for j, (fx, fy) in enumerate(_CHIP_FLIPS)]


def _d2d_copies(land, send_sems, recv_sems, base, x, y, c, c_rows):
    half = land.shape[1] // 2
    rows = pl.ds(c_rows * half, half)
    out = []
    for j, (fx, fy) in enumerate(_CHIP_FLIPS):
        piece = land.at[2 * _flip(x, fx) + _flip(y, fy), rows]
        out.append(pltpu.make_async_remote_copy(
            src_ref=piece, dst_ref=piece, send_sem=send_sems.at[base + j], recv_sem=recv_sems.at[base + j],
            device_id=(x, y, 1 - c), device_id_type=MESH_ID))
    return out


def _own_copy(shard, land, send_sems, recv_sems, s, x, y, c):
    return pltpu.make_async_remote_copy(
        src_ref=shard, dst_ref=land.at[2 * x + y], send_sem=send_sems.at[s], recv_sem=recv_sems.at[s],
        device_id=(x, y, 1 - c), device_id_type=MESH_ID)


def _gather_start(groups):
    flat = [s for grp in groups for s in grp]
    n, n_g = len(flat), len(groups)
    lands = [lax.empty((N_CHIPS,) + s.shape, s.dtype) for s in flat]

    def body(*refs):
        shards, land_refs = refs[:n], refs[n:2 * n]
        sems = refs[2 * n: 2 * n + 4 * n_g]
        token = refs[-1]
        x, y, c = _place()
        idx = 0
        for g, grp in enumerate(groups):
            ici_send, ici_recv, own_send, own_recv = sems[4 * g: 4 * g + 4]
            for s in range(len(grp)):
                _own_copy(shards[idx], land_refs[idx], own_send, own_recv, s, x, y, c).start()
                for cp in _ici_copies(shards[idx], land_refs[idx], ici_send, ici_recv, 3 * s, x, y, c):
                    cp.start()
                idx += 1
        token[...] = jnp.zeros(token.shape, F32)

    sem_shapes = []
    for grp in groups:
        k = len(grp)
        sem_shapes += [pltpu.SemaphoreType.DMA((3 * k,)), pltpu.SemaphoreType.DMA((3 * k,)),
                       pltpu.SemaphoreType.DMA((k,)), pltpu.SemaphoreType.DMA((k,))]
    res = pl.pallas_call(
        body, name="gather_start",
        out_shape=sem_shapes + [_hbm_like(s) for s in flat] + [_hbm_like(ld) for ld in lands] + [_TOKEN],
        in_specs=[_HBM_SPEC] * (2 * n),
        out_specs=[_SEM_SPEC] * (4 * n_g) + [_HBM_SPEC] * (2 * n) + [pl.BlockSpec(memory_space=pltpu.VMEM)],
        input_output_aliases={i: 4 * n_g + i for i in range(2 * n)},
        compiler_params=pltpu.CompilerParams(has_side_effects=_EFFECT),
    )(*[_in_hbm(s) for s in flat], *[_in_hbm(ld) for ld in lands])
    sems, thru, token = res[:4 * n_g], res[4 * n_g: 4 * n_g + 2 * n], res[-1]
    out, idx = [], 0
    for g, grp in enumerate(groups):
        k = len(grp)
        out.append(dict(sems=tuple(sems[4 * g: 4 * g + 4]), shards=list(thru[idx: idx + k]),
                        lands=list(thru[n + idx: n + idx + k])))
        idx += k
    return out, token


def _gather_step(name, landed, arriving, after):
    n_l = len(landed["lands"]) if landed else 0
    n_a = len(arriving["lands"]) if arriving else 0

    def body(*refs):
        pos = 0
        l_lands = refs[pos: pos + n_l]; pos += n_l
        l_sems = refs[pos: pos + (2 if landed else 0)]; pos += 2 if landed else 0
        a_shards = refs[pos: pos + n_a]; pos += n_a
        a_lands = refs[pos: pos + n_a]; pos += n_a
        a_sems = refs[pos: pos + (4 if arriving else 0)]; pos += 4 if arriving else 0
        pos += 1
        pos += n_l + n_a
        new_sems = refs[pos: pos + (2 if arriving else 0)]
        x, y, c = _place()
        if arriving:
            ici_send, ici_recv, own_send, own_recv = a_sems
            started = []
            for s in range(n_a):
                own = _own_copy(a_shards[s], a_lands[s], own_send, own_recv, s, x, y, c)
                own.wait_recv()
                for cp in _d2d_copies(a_lands[s], ici_send, ici_recv, 3 * s, x, y, c, c):
                    cp.wait_recv()
                for cp in _d2d_copies(a_lands[s], new_sems[0], new_sems[1], 3 * s, x, y, c, c):
                    cp.start()
                started.append(own)
                started += _ici_copies(a_shards[s], a_lands[s], ici_send, ici_recv, 3 * s, x, y, c)
            for cp in started:
                cp.wait_send()
        if landed:
            for s in range(n_l):
                for cp in _d2d_copies(l_lands[s], l_sems[0], l_sems[1], 3 * s, x, y, c, c):
                    cp.wait_send()
                for cp in _d2d_copies(l_lands[s], l_sems[0], l_sems[1], 3 * s, x, y, c, 1 - c):
                    cp.wait_recv()

    args, in_specs = [], []
    if landed:
        args += [_in_hbm(a) for a in landed["lands"]] + list(landed["d2d"])
        in_specs += [_HBM_SPEC] * n_l + [_SEM_SPEC] * 2
    if arriving:
        args += [_in_hbm(a) for a in arriving["shards"]] + [_in_hbm(a) for a in arriving["lands"]] + list(arriving["sems"])
        in_specs += [_HBM_SPEC] * (2 * n_a) + [_SEM_SPEC] * 4
    args.append(after)
    in_specs.append(pl.BlockSpec(memory_space=pl.ANY))
    out_shape, out_specs, aliases = [], [], {}
    if landed:
        out_shape += [_hbm_like(a) for a in landed["lands"]]
        for s in range(n_l):
            aliases[s] = s
    if arriving:
        first = (n_l + 2 if landed else 0) + n_a
        for s in range(n_a):
            aliases[first + s] = n_l + s
        out_shape += [_hbm_like(a) for a in arriving["lands"]]
    out_specs += [_HBM_SPEC] * (n_l + n_a)
    if arriving:
        out_shape += [pltpu.SemaphoreType.DMA((3 * n_a,)), pltpu.SemaphoreType.DMA((3 * n_a,))]
        out_specs += [_SEM_SPEC] * 2
    res = pl.pallas_call(
        body, name=name, out_shape=out_shape, in_specs=in_specs, out_specs=out_specs, input_output_aliases=aliases,
        compiler_params=pltpu.CompilerParams(has_side_effects=_EFFECT),
    )(*args)
    done = list(res[:n_l]) if landed else None
    nxt = None
    if arriving:
        nxt = dict(lands=list(res[n_l: n_l + n_a]), d2d=tuple(res[n_l + n_a: n_l + n_a + 2]))
    return done, nxt


def _swap_copies(grad, land, send_sems, recv_sems, base, x, y, c):
    half = grad.shape[1] // 2
    return [pltpu.make_async_remote_copy(
        src_ref=grad.at[:, pl.ds((1 - c) * half, half)], dst_ref=land, send_sem=send_sems.at[base], recv_sem=recv_sems.at[base],
        device_id=(x, y, 1 - c), device_id_type=MESH_ID)]


def _add_halves(g, r, c_idx, tag):
    _, rows, w = g.shape
    half = rows // 2
    tr = _row_tile(half)
    nb = half // tr

    def body(c_ref, g_ref, r_ref, o_ref):
        o_ref[...] = (g_ref[...].astype(F32) + r_ref[...].astype(F32)).astype(BF16)

    return pl.pallas_call(
        body, name=f"add_halves_{tag}",
        grid_spec=pltpu.PrefetchScalarGridSpec(
            num_scalar_prefetch=1, grid=(N_CHIPS, nb),
            in_specs=[pl.BlockSpec((None, tr, w), lambda k, i, c: (k, c[0] * nb + i, 0)),
                      pl.BlockSpec((None, tr, w), lambda k, i, c: (k, i, 0))],
            out_specs=pl.BlockSpec((None, tr, w), lambda k, i, c: (k, i, 0))),
        out_shape=jax.ShapeDtypeStruct((N_CHIPS, half, w), BF16), compiler_params=_params(2),
    )(c_idx, g, r)


def _partial_copies(part, land, send_sems, recv_sems, base, x, y, c):
    out = []
    for j, (fx, fy) in enumerate(_CHIP_FLIPS):
        px, py = _flip(x, fx), _flip(y, fy)
        out.append(pltpu.make_async_remote_copy(
            src_ref=part.at[2 * px + py], dst_ref=land.at[j], send_sem=send_sems.at[base + j], recv_sem=recv_sems.at[base + j],
            device_id=(px, py, c), device_id_type=MESH_ID))
    return out


def _split_start(name, srcs, land_shapes, copies_of, per):
    n = len(srcs)
    lands = [lax.empty(shape, s.dtype) for shape, s in zip(land_shapes, srcs)]

    def body(*refs):
        src_refs, land_refs = refs[:n], refs[n:2 * n]
        send_sems, recv_sems = refs[2 * n: 2 * n + 2]
        token = refs[-1]
        x, y, c = _place()
        for s in range(n):
            for cp in copies_of(src_refs[s], land_refs[s], send_sems, recv_sems, per * s, x, y, c):
                cp.start()
        token[...] = jnp.zeros(token.shape, F32)

    res = pl.pallas_call(
        body, name=name,
        out_shape=[pltpu.SemaphoreType.DMA((per * n,)), pltpu.SemaphoreType.DMA((per * n,))]
        + [_hbm_like(s) for s in srcs] + [_hbm_like(ld) for ld in lands] + [_TOKEN],
        in_specs=[_HBM_SPEC] * (2 * n),
        out_specs=[_SEM_SPEC] * 2 + [_HBM_SPEC] * (2 * n) + [pl.BlockSpec(memory_space=pltpu.VMEM)],
        input_output_aliases={i: 2 + i for i in range(2 * n)},
        compiler_params=pltpu.CompilerParams(has_side_effects=_EFFECT),
    )(*[_in_hbm(s) for s in srcs], *[_in_hbm(ld) for ld in lands])
    state = dict(sems=tuple(res[:2]), srcs=list(res[2: 2 + n]), lands=list(res[2 + n: 2 + 2 * n]))
    return state, res[-1]


def _split_wait(name, state, after, copies_of, per):
    n = len(state["srcs"])

    def body(*refs):
        src_refs, land_refs = refs[:n], refs[n:2 * n]
        send_sems, recv_sems = refs[2 * n: 2 * n + 2]
        x, y, c = _place()
        for s in range(n):
            for cp in copies_of(src_refs[s], land_refs[s], send_sems, recv_sems, per * s, x, y, c):
                cp.wait_send()
                cp.wait_recv()

    res = pl.pallas_call(
        body, name=name,
        out_shape=[_hbm_like(s) for s in state["srcs"]] + [_hbm_like(ld) for ld in state["lands"]],
        in_specs=[_HBM_SPEC] * (2 * n) + [_SEM_SPEC] * 2 + [pl.BlockSpec(memory_space=pl.ANY)],
        out_specs=[_HBM_SPEC] * (2 * n), input_output_aliases={i: i for i in range(2 * n)},
        compiler_params=pltpu.CompilerParams(has_side_effects=_EFFECT),
    )(*state["srcs"], *state["lands"], *state["sems"], after)
    return list(res[:n]), list(res[n:])


def _sum_partials(p, q, chip_idx, c_idx, tag):
    _, half, w = p.shape
    tr = _row_tile(half)
    nb = half // tr

    def body(k_ref, c_ref, p_ref, q_ref, o_ref):
        acc = p_ref[...].astype(F32)
        for j in range(3):
            acc = acc + q_ref[j].astype(F32)
        o_ref[...] = acc

    return pl.pallas_call(
        body, name=f"sum_partials_{tag}",
        grid_spec=pltpu.PrefetchScalarGridSpec(
            num_scalar_prefetch=2, grid=(nb,),
            in_specs=[pl.BlockSpec((None, tr, w), lambda i, k, c: (k[0], i, 0)),
                      pl.BlockSpec((3, tr, w), lambda i, k, c: (0, i, 0))],
            out_specs=pl.BlockSpec((tr, w), lambda i, k, c: (c[0] * nb + i, 0))),
        out_shape=jax.ShapeDtypeStruct((2 * half, w), F32), compiler_params=_params(1),
    )(chip_idx, c_idx, p, q)


def _join_halves(tots, tag):
    n = len(tots)

    def body(*refs):
        bufs = refs[n:2 * n]
        send_sems, recv_sems = refs[2 * n:]
        x, y, c = _place()
        copies = []
        for s in range(n):
            half = bufs[s].shape[0] // 2
            mine = bufs[s].at[pl.ds(c * half, half)]
            cp = pltpu.make_async_remote_copy(
                src_ref=mine, dst_ref=mine, send_sem=send_sems.at[s], recv_sem=recv_sems.at[s],
                device_id=(x, y, 1 - c), device_id_type=MESH_ID)
            cp.start()
            copies.append(cp)
        for s in range(n):
            half = bufs[s].shape[0] // 2
            theirs = bufs[s].at[pl.ds((1 - c) * half, half)]
            pltpu.make_async_remote_copy(
                src_ref=theirs, dst_ref=theirs, send_sem=send_sems.at[s], recv_sem=recv_sems.at[s],
                device_id=(x, y, c), device_id_type=MESH_ID).wait_recv()
        for cp in copies:
            cp.wait_send()

    return pl.pallas_call(
        body, name=f"join_halves_{tag}",
        out_shape=[jax.ShapeDtypeStruct(tt.shape, tt.dtype) for tt in tots],
        in_specs=[_HBM] * n, out_specs=[_HBM] * n, input_output_aliases={s: s for s in range(n)},
        scratch_shapes=[pltpu.SemaphoreType.DMA((n,)), pltpu.SemaphoreType.DMA((n,))],
    )(*tots)


def _adamw_math(w, g, m, v):
    m = ADAM_B1 * m + (1.0 - ADAM_B1) * g
    v = ADAM_B2 * v + (1.0 - ADAM_B2) * jnp.square(g)
    m_hat = m / (1.0 - ADAM_B1 ** ADAM_STEP)
    v_hat = v / (1.0 - ADAM_B2 ** ADAM_STEP)
    delta = -ADAM_LR * (m_hat / (jnp.sqrt(v_hat) + ADAM_EPS) + ADAM_WD * w)
    return delta, m, v


def _adamw_big(g, col, w2, m2, v2, row0, prev, tag):
    rows = g.shape[0]
    rtot, wd = w2.shape
    tr = _row_tile(rows)
    assert row0 % tr == 0
    n_prev = 0 if prev is None else 4

    def body(*refs):
        g_ref, w_ref, m_ref, v_ref = refs[:4]
        go_ref, d_ref, mo_ref, vo_ref = refs[4 + n_prev:]
        gv = g_ref[...]
        delta, mn, vn = _adamw_math(w_ref[...], gv, m_ref[...], v_ref[...])
        go_ref[...] = gv
        d_ref[...] = delta
        mo_ref[...] = mn
        vo_ref[...] = vn

    at = lambda i: (row0 // tr + i, 0)
    return pl.pallas_call(
        body, name=f"adamw_{tag}", grid=(rows // tr,),
        in_specs=[pl.BlockSpec((tr, wd), lambda i: (i, col))] + [pl.BlockSpec((tr, wd), at)] * 3
        + [pl.BlockSpec(memory_space=pl.ANY)] * n_prev,
        out_specs=[pl.BlockSpec((tr, wd), at)] * 4, out_shape=[jax.ShapeDtypeStruct((rtot, wd), F32)] * 4,
        input_output_aliases={4 + k: k for k in range(n_prev)}, compiler_params=_params(1),
    )(g, w2, m2, v2, *(prev or ()))


def _adamw_small(items):
    n = len(items)

    def body(*refs):
        ins, outs = refs[:4 * n], refs[4 * n:]
        for k in range(n):
            g_ref, w_ref, m_ref, v_ref = ins[4 * k: 4 * k + 4]
            delta, mn, vn = _adamw_math(w_ref[...], g_ref[...], m_ref[...], v_ref[...])
            outs[3 * k][...] = delta
            outs[3 * k + 1][...] = mn
            outs[3 * k + 2][...] = vn

    flat = [a for it in items for a in it]
    out_shape = [jax.ShapeDtypeStruct(it[1].shape, F32) for it in items for _ in range(3)]
    res = pl.pallas_call(body, name="adamw_small", out_shape=out_shape,
                         compiler_params=pltpu.CompilerParams(vmem_limit_bytes=VMEM_LIMIT_BYTES))(*flat)
    return [tuple(res[3 * k: 3 * k + 3]) for k in range(n)]


def _pack(arrays, width):
    rows, layout, r = [], [], 0
    for a in arrays:
        flat = a.reshape(-1).astype(F32)
        nr = -(-flat.shape[0] // (8 * width)) * 8
        flat = jnp.pad(flat, (0, nr * width - flat.shape[0]))
        rows.append(flat.reshape(nr, width))
        layout.append((r, nr, a.shape))
        r += nr
    return jnp.concatenate(rows, axis=0), layout


def _unpack(packed, layout):
    out = []
    for r, nr, shape in layout:
        size = 1
        for s in shape:
            size *= s
        out.append(packed[r:r + nr].reshape(-1)[:size].reshape(shape))
    return out


_WEIGHTS = ['ffn1_norm', 'ffn1_w1', 'ffn1_w3', 'ffn1_w2', 'mix_norm', 'ffn2_norm', 'ffn2_w1', 'ffn2_w3', 'ffn2_w2',
            'ple_norm', 'ple_w_gate', 'ple_w_proj', 'gmlp_w_in', 'gmlp_ln_g', 'gmlp_ln_b', 'gmlp_w_s', 'gmlp_b_s',
            'gmlp_w_out', 'swa_wq', 'swa_bq', 'swa_wk', 'swa_bk', 'swa_wv', 'swa_bv', 'swa_sinks', 'swa_wo', 'swa_bo',
            'final_norm']
_REPLICATED = ['ffn1_norm', 'mix_norm', 'ffn2_norm', 'ple_norm', 'gmlp_ln_g', 'gmlp_ln_b', 'gmlp_w_s', 'gmlp_b_s',
               'swa_sinks', 'final_norm']
_BIASES = ['swa_bq', 'swa_bk', 'swa_bv', 'swa_bo']


def _as2d(a):
    if a.ndim == 1:
        return a.reshape(1, -1)
    return a.reshape(-1, a.shape[-1])


_GROUPS = [("f1l0", ("w1", "w3", "w2")), ("mix0", ("in", "out")), ("f2l0", ("w1", "w3", "w2")), ("ple0", ("gate", "proj")),
           ("f1l1", ("w1", "w3", "w2")), ("mix1", ("q", "kv", "o")), ("f2l1", ("w1", "w3", "w2")), ("ple1", ("gate", "proj"))]
_GATHER = []
for _g, _parts in _GROUPS:
    if _g.startswith("f"):
        _GATHER += [(_g + "a", _g, ("w1", "w3")), (_g + "b", _g, ("w2",))]
    else:
        _GATHER.append((_g, _g, _parts))


def _local_step(x, p, target, small, full_bias, get_group, put_group):
    t, d = x.shape
    n_layers = 2
    blk = small['gmlp_w_s'].shape[2]
    bq, bkv, bo = full_bias
    norm = lambda name, i: small[name][i:i + 1]
    ln_g, ln_b = small['gmlp_ln_g'], small['gmlp_ln_b']
    w_s = small['gmlp_w_s'][0]
    b_st = jnp.pad(small['gmlp_b_s'][0].T, ((0, 0), (0, LANES - small['gmlp_b_s'].shape[1])))
    sinks = small['swa_sinks']

    saved, wts = [], {}

    def ffn_fwd(x, name, norm_w):
        w = get_group(name + "a", x)
        x, s_ffn, w["w2"] = _ffn_fwd(x, norm_w, w["w1"], w["w3"], lambda after: get_group(name + "b", after)["w2"], name)
        wts[name] = w
        return x, s_ffn

    for i in range(n_layers):
        x, s_f1 = ffn_fwd(x, f"f1l{i}", norm('ffn1_norm', i))
        w = wts[f"mix{i}"] = get_group(f"mix{i}", x)
        if i == 0:
            x, s_mix = _gmlp_fwd(x, norm('mix_norm', i), ln_g, ln_b, w_s, b_st, w["in"], w["out"], f"l{i}")
        else:
            x, s_mix = _swa_fwd(x, norm('mix_norm', i), bq, bkv, bo, sinks, w["q"], w["kv"], w["o"], blk, f"l{i}")
        x, s_f2 = ffn_fwd(x, f"f2l{i}", norm('ffn2_norm', i))
        w = wts[f"ple{i}"] = get_group(f"ple{i}", x)
        x, s_ple = _ple_fwd(x, p[i], norm('ple_norm', i), w["gate"], w["proj"], f"l{i}")
        saved.append((s_f1, s_mix, s_f2, s_ple))

    dx, d_final, loss = _loss_head(x, small['final_norm'].reshape(1, d), target)

    gn = {k: [None] * n_layers for k in ('ffn1_norm', 'mix_norm', 'ffn2_norm', 'ple_norm')}
    dep = None
    for i in reversed(range(n_layers)):
        s_f1, s_mix, s_f2, s_ple = saved[i]
        w = wts[f"ple{i}"]
        dx, gn['ple_norm'][i], dwg, dwp = _ple_bwd(dx, s_ple, p[i], norm('ple_norm', i), w["gate"], f"l{i}", dep)
        dep = put_group(f"ple{i}", {"gate": dwg, "proj": dwp})
        w = wts[f"f2l{i}"]
        dx, gn['ffn2_norm'][i], dep = _ffn_bwd(
            dx, s_f2, norm('ffn2_norm', i), w["w1"], w["w3"], w["w2"], f"f2l{i}", dep,
            functools.partial(put_group, f"f2l{i}"))
        w = wts[f"mix{i}"]
        if i == 0:
            dx, gn['mix_norm'][i], dw_in, dw_out, d_ws, d_bs, d_lg, d_lb = _gmlp_bwd(
                dx, s_mix, norm('mix_norm', i), ln_g, ln_b, w_s, b_st, w["in"], w["out"], f"l{i}", dep)
            dep = put_group(f"mix{i}", {"in": dw_in, "out": dw_out})
        else:
            dx, gn['mix_norm'][i], dwq, dwkv, dwo, d_bq, d_bkv, d_bo, d_sink = _swa_bwd(
                dx, s_mix, norm('mix_norm', i), sinks, w["q"], w["kv"], w["o"], blk, f"l{i}", dep)
            dep = put_group(f"mix{i}", {"q": dwq, "kv": dwkv, "o": dwo})
        w = wts[f"f1l{i}"]
        dx, gn['ffn1_norm'][i], dep = _ffn_bwd(
            dx, s_f1, norm('ffn1_norm', i), w["w1"], w["w3"], w["w2"], f"f1l{i}", dep,
            functools.partial(put_group, f"f1l{i}"))

    kw = d_bkv.shape[1] // 2
    g_small = {
        'ffn1_norm': jnp.concatenate(gn['ffn1_norm'], axis=0), 'mix_norm': jnp.concatenate(gn['mix_norm'], axis=0),
        'ffn2_norm': jnp.concatenate(gn['ffn2_norm'], axis=0), 'ple_norm': jnp.concatenate(gn['ple_norm'], axis=0),
        'gmlp_ln_g': d_lg, 'gmlp_ln_b': d_lb, 'gmlp_w_s': d_ws[None], 'gmlp_b_s': d_bs[None], 'swa_sinks': d_sink,
        'final_norm': d_final.reshape(d), 'swa_bq': d_bq, 'swa_bk': d_bkv[:, :kw], 'swa_bv': d_bkv[:, kw:], 'swa_bo': d_bo,
    }
    return loss, dx, g_small


def _group_shards(a):
    bf = lambda w: w.astype(BF16)
    out = {}
    for i in range(2):
        for f in (1, 2):
            out[f"f{f}l{i}"] = {
                "w1": (bf(a[f"ffn{f}_w1"][i]), [(f"ffn{f}_w1", i, 0)]), "w3": (bf(a[f"ffn{f}_w3"][i]), [(f"ffn{f}_w3", i, 0)]),
                "w2": (bf(a[f"ffn{f}_w2"][i]), [(f"ffn{f}_w2", i, 0)])}
        out[f"ple{i}"] = {"gate": (bf(a["ple_w_gate"][i]), [("ple_w_gate", i, 0)]),
                          "proj": (bf(a["ple_w_proj"][i]), [("ple_w_proj", i, 0)])}
    out["mix0"] = {"in": (bf(a["gmlp_w_in"][0]), [("gmlp_w_in", 0, 0)]), "out": (bf(a["gmlp_w_out"][0]), [("gmlp_w_out", 0, 0)])}
    wkv = jnp.concatenate([bf(a["swa_wk"][0]), bf(a["swa_wv"][0])], axis=1)
    out["mix1"] = {"q": (bf(a["swa_wq"][0]), [("swa_wq", 0, 0)]), "kv": (wkv, [("swa_wk", 0, 0), ("swa_wv", 0, 1)]),
                   "o": (bf(a["swa_wo"][0]), [("swa_wo", 0, 0)])}
    return out


def kernel(x, p, ffn1_norm, ffn1_w1, ffn1_w3, ffn1_w2, mix_norm, ffn2_norm, ffn2_w1, ffn2_w3, ffn2_w2, ple_norm, ple_w_gate, ple_w_proj, gmlp_w_in, gmlp_ln_g, gmlp_ln_b, gmlp_w_s, gmlp_b_s, gmlp_w_out, swa_wq, swa_bq, swa_wk, swa_bk, swa_wv, swa_bv, swa_sinks, swa_wo, swa_bo, final_norm, loss_target, m_ffn1_norm, m_ffn1_w1, m_ffn1_w3, m_ffn1_w2, m_mix_norm, m_ffn2_norm, m_ffn2_w1, m_ffn2_w3, m_ffn2_w2, m_ple_norm, m_ple_w_gate, m_ple_w_proj, m_gmlp_w_in, m_gmlp_ln_g, m_gmlp_ln_b, m_gmlp_w_s, m_gmlp_b_s, m_gmlp_w_out, m_swa_wq, m_swa_bq, m_swa_wk, m_swa_bk, m_swa_wv, m_swa_bv, m_swa_sinks, m_swa_wo, m_swa_bo, m_final_norm, v_ffn1_norm, v_ffn1_w1, v_ffn1_w3, v_ffn1_w2, v_mix_norm, v_ffn2_norm, v_ffn2_w1, v_ffn2_w3, v_ffn2_w2, v_ple_norm, v_ple_w_gate, v_ple_w_proj, v_gmlp_w_in, v_gmlp_ln_g, v_gmlp_ln_b, v_gmlp_w_s, v_gmlp_b_s, v_gmlp_w_out, v_swa_wq, v_swa_bq, v_swa_wk, v_swa_bk, v_swa_wv, v_swa_bv, v_swa_sinks, v_swa_wo, v_swa_bo, v_final_norm):
    a = dict(locals())
    xi, yi, ci = _place()
    chip = 2 * xi + yi
    c_idx = ci.reshape(1).astype(jnp.int32)
    chip_idx = chip.reshape(1).astype(jnp.int32)
    d = x.shape[-1]
    d4 = d // N_CHIPS

    shards = _group_shards(a)
    started, token = _gather_start([[shards[g][part][0] for part in parts] for _, g, parts in _GATHER])
    bias_pack, bias_layout = _pack([a[n] for n in _BIASES], d4)
    bias_all = _all_gather_small(bias_pack)

    def full_bias(idx):
        return jnp.concatenate([_unpack(bias_all[2 * k], bias_layout)[idx] for k in range(N_CHIPS)], axis=1)

    bq, bk, bv, bo = (full_bias(i) for i in range(4))
    bkv = jnp.concatenate([bk, bv], axis=1)

    state = {"handed": _gather_step("gather_step_first", None, started[0], token)[1], "next": 1}

    def get_group(name, after):
        k = state["next"]
        assert _GATHER[k - 1][0] == name
        arriving = started[k] if k < len(_GATHER) else None
        done, state["handed"] = _gather_step(f"gather_step_{name}", state["handed"], arriving, after)
        state["next"] = k + 1
        return dict(zip(_GATHER[k - 1][2], done))

    pending, swapping = [], []

    def send_swapped(after):
        name, st = swapping.pop()
        parts = dict(_GROUPS)[name]
        glist, sib = _split_wait(f"swap_wait_{name}", st, after, _swap_copies, 1)
        partial = [_add_halves(g, r, c_idx, f"{name}_{part}") for part, g, r in zip(parts, glist, sib)]
        st, tok = _split_start(f"send_start_{name}", partial, [(3,) + pt.shape[1:] for pt in partial], _partial_copies, 3)
        pending.append((name, st))
        return tok

    def put_group(name, grads):
        glist = [grads[part] for part in dict(_GROUPS)[name]]
        st, tok = _split_start(f"swap_start_{name}", glist, [(N_CHIPS, g.shape[1] // 2, g.shape[2]) for g in glist],
                               _swap_copies, 1)
        if swapping:
            tok = send_swapped(tok)
        swapping.append((name, st))
        return tok

    small = {n: a[n] for n in _REPLICATED}
    loss, grad_x, g_small = _local_step(x[0], p[:, 0], loss_target[0], small, (bq, bkv, bo), get_group, put_group)
    send_swapped(grad_x)

    names = _REPLICATED + _BIASES
    packed, layout = _pack([g_small[n] for n in names] + [loss[:, :1]], d4)
    summed = _unpack(_all_sum_small(packed), layout)
    g_sum = dict(zip(names, summed[:-1]))
    loss_out = summed[-1].reshape(())
    for n in _BIASES:
        width = a[n].shape[-1]
        g_sum[n] = lax.dynamic_slice_in_dim(g_sum[n], chip * width, width, axis=1)
    out, chain, last_done = {}, {}, grad_x
    for idx, (name, st) in enumerate(pending):
        mine, recv = _split_wait(f"send_wait_{name}", st, last_done if idx == len(pending) - 1 else grad_x, _partial_copies, 3)
        parts = dict(_GROUPS)[name]
        tots = _join_halves([_sum_partials(pm, q, chip_idx, c_idx, f"{name}_{part}")
                             for part, pm, q in zip(parts, mine, recv)], name)
        for part, g in zip(parts, tots):
            for n, layer, col in shards[name][part][1]:
                wd = a[n].shape[-1]
                rows = a[n].shape[-2]
                w2, m2, v2 = (a[pre + n].reshape(-1, wd) for pre in ('', 'm_', 'v_'))
                chain[n] = _adamw_big(g, col, w2, m2, v2, layer * rows, chain.get(n), f"{n}_{layer}")
                last_done = chain[n][1]
    for n, res in chain.items():
        out[n] = tuple(o.reshape(a[n].shape) for o in res)
    small_names = _REPLICATED + _BIASES
    items = [(_as2d(g_sum[n]), _as2d(a[n]), _as2d(a['m_' + n]), _as2d(a['v_' + n])) for n in small_names]
    for n, (delta, mn, vn) in zip(small_names, _adamw_small(items)):
        shape = a[n].shape
        out[n] = (g_sum[n].reshape(shape), delta.reshape(shape), mn.reshape(shape), vn.reshape(shape))

    return (loss_out, grad_x[None]) + tuple(out[n][j] for j in range(4) for n in _WEIGHTS)
```

```python
import functools

import jax
import jax.numpy as jnp
from jax import lax
from jax.experimental import pallas as pl
from jax.experimental.pallas import tpu as pltpu

F32 = jnp.float32
BF16 = jnp.bfloat16

RMS_EPS = 1e-6
LN_EPS = 1e-5
FFN_RESIDUAL_WEIGHT = 0.5
HEAD_DIM = 64
ROPE_DIM = 16
ROPE_THETA = 500000.0
ADAM_LR = 0.001
ADAM_B1 = 0.9
ADAM_B2 = 0.999
ADAM_EPS = 1e-08
ADAM_WD = 0.01
ADAM_STEP = 10
N_CHIPS = 4
N_DEV = 8
LANES = 128
VMEM_LIMIT_BYTES = 56 * 1024 * 1024
MESH_ID = pl.DeviceIdType.MESH

_DIMS = {
    "nn": (((1,), (0,)), ((), ())),
    "nt": (((1,), (1,)), ((), ())),
    "tn": (((0,), (0,)), ((), ())),
}


def _params(n_axes):
    return pltpu.CompilerParams(dimension_semantics=("arbitrary",) * n_axes, vmem_limit_bytes=VMEM_LIMIT_BYTES)


def _mm(name, grid, pairs, extras, outs, epilogue, acc_shapes, order="ij", split=()):
    ni, nj, nk = grid
    if order == "ij":
        pgrid = (ni, nj, nk)
        ijk = lambda g0, g1, g2: (g0, g1, g2)
    else:
        pgrid = (nj, ni, nk)
        ijk = lambda g0, g1, g2: (g1, g0, g2)
    in_specs, args = [], []
    for a, ablk, amap, b, bblk, bmap, _, _, _ in pairs:
        in_specs.append(pl.BlockSpec(ablk, lambda g0, g1, g2, m=amap: m(*ijk(g0, g1, g2))))
        in_specs.append(pl.BlockSpec(bblk, lambda g0, g1, g2, m=bmap: m(*ijk(g0, g1, g2))))
        args += [a, b]
    for e, eblk, emap in extras:
        in_specs.append(pl.BlockSpec(eblk, lambda g0, g1, g2, m=emap: m(*ijk(g0, g1, g2)[:2])))
        args.append(e)
    out_specs = [pl.BlockSpec(oblk, lambda g0, g1, g2, m=omap: m(*ijk(g0, g1, g2)[:2])) for _, _, oblk, omap in outs]
    out_shape = [jax.ShapeDtypeStruct(s, d) for s, d, _, _ in outs]
    n_p, n_e, n_o = len(pairs), len(extras), len(outs)

    def body(*refs):
        p_refs = refs[: 2 * n_p]
        e_refs = refs[2 * n_p: 2 * n_p + n_e]
        o_refs = refs[2 * n_p + n_e: 2 * n_p + n_e + n_o]
        accs = refs[2 * n_p + n_e + n_o:]
        k = pl.program_id(2)

        def product(idx):
            a = p_refs[2 * idx][...].astype(BF16)
            b = p_refs[2 * idx + 1][...].astype(BF16)
            if b.ndim == 3:
                b = b.reshape(-1, b.shape[-1])
            return lax.dot_general(a, b, _DIMS[pairs[idx][6]], preferred_element_type=F32)

        def finish(sums):
            vals = epilogue(sums, [e[...] for e in e_refs])
            for o, v in zip(o_refs, vals):
                o[...] = v.astype(o.dtype)

        if nk == 1 and split:
            width = acc_shapes[0][1]
            bounds = (0,) + tuple(split) + (width,)
            lhs = [p_refs[2 * idx][...].astype(BF16) for idx in range(n_p)]
            for lo, hi in zip(bounds[:-1], bounds[1:]):
                sums = [None] * len(acc_shapes)
                for idx in range(n_p):
                    dims, ai = pairs[idx][6], pairs[idx][7]
                    b_ref = p_refs[2 * idx + 1]
                    b = (b_ref[lo:hi, :] if dims == "nt" else b_ref[:, lo:hi]).astype(BF16)
                    prod = lax.dot_general(lhs[idx], b, _DIMS[dims], preferred_element_type=F32)
                    sums[ai] = prod if sums[ai] is None else sums[ai] + prod
                vals = epilogue(sums, [e[:, lo:hi] if e.shape[-1] == width else e[...] for e in e_refs])
                for o, v in zip(o_refs, vals):
                    o[:, lo:hi] = v.astype(o.dtype)
            return

        if nk == 1:
            sums = [None] * len(acc_shapes)
            for idx in range(n_p):
                ai = pairs[idx][7]
                sums[ai] = product(idx) if sums[ai] is None else sums[ai] + product(idx)
            finish(sums)
            return

        written = set()
        for idx in range(n_p):
            ai, k0 = pairs[idx][7], pairs[idx][8]
            if k0 or ai not in written:
                @pl.when(k == 0)
                def _(idx=idx, ai=ai, first=ai not in written):
                    if first:
                        accs[ai][...] = product(idx)
                    else:
                        accs[ai][...] += product(idx)
            if not k0:
                if ai in written:
                    accs[ai][...] += product(idx)
                else:
                    @pl.when(k > 0)
                    def _(idx=idx, ai=ai):
                        accs[ai][...] += product(idx)
            written.add(ai)

        @pl.when(k == nk - 1)
        def _():
            finish([acc[...] for acc in accs])

    res = pl.pallas_call(
        body, name=name, grid=pgrid, in_specs=in_specs, out_specs=out_specs, out_shape=out_shape,
        scratch_shapes=[] if nk == 1 else [pltpu.VMEM(s, F32) for s in acc_shapes], compiler_params=_params(3),
    )(*args)
    return res


def _col_nn(r0, tk, tn, wc):
    assert r0 % tk == 0 and wc % tn == 0
    npc = wc // tn
    return (None, tk, tn), lambda i, j, k: (j // npc, r0 // tk + k, j % npc)


def _row_nn(r0, tk, tn, rc):
    assert r0 % tk == 0 and rc % tk == 0
    kpc = rc // tk
    return (None, tk, tn), lambda i, j, k: (k // kpc, r0 // tk + k % kpc, j)


def _col_nt(r0, tn, tk, wc):
    assert r0 % tn == 0 and wc % tk == 0
    kpc = wc // tk
    return (None, tn, tk), lambda i, j, k: (k // kpc, r0 // tn + j, k % kpc)


def _row_nt(r0, tn, tk, rc):
    assert r0 % tn == 0 and rc % tn == 0
    npc = rc // tn
    return (None, tn, tk), lambda i, j, k: (j // npc, r0 // tn + j % npc, k)


def _out_col(tm, tn, wc):
    npc = wc // tn
    return (None, tm, tn), lambda i, j: (j // npc, i, j % npc)


def _out_row(tm, tn, rc):
    mpc = rc // tm
    return (None, tm, tn), lambda i, j: (i // mpc, i % mpc, j)


def _ik(i, j, k):
    return (i, k)


def _ki(i, j, k):
    return (k, i)


def _kj(i, j, k):
    return (k, j)


def _i0(i, j, k):
    return (i, 0)


def _ij(i, j):
    return (i, j)


def _0j(i, j):
    return (0, j)


def _i0e(i, j):
    return (i, 0)


def _rows(name, n, ins, outs, body, scratch=(), deps=()):
    in_specs, args = [], []
    for item in ins:
        if len(item) == 2:
            in_specs.append(item[1])
        else:
            in_specs.append(pl.BlockSpec(item[1], item[2]))
        args.append(item[0])
    n_in = len(args)
    for dep in deps:
        in_specs.append(pl.BlockSpec(dep.shape, lambda i: (0, 0)))
        args.append(dep)
    n_dep = len(deps)

    def call_body(*refs):
        body(*refs[:n_in], *refs[n_in + n_dep:])

    out_specs = [pl.BlockSpec(blk, m) for _, _, blk, m in outs]
    out_shape = [jax.ShapeDtypeStruct(s, d) for s, d, _, _ in outs]
    return pl.pallas_call(
        call_body, name=name, grid=(n,), in_specs=in_specs, out_specs=out_specs, out_shape=out_shape,
        scratch_shapes=list(scratch), compiler_params=_params(1),
    )(*args)


def _dep_extra(dep):
    return [] if dep is None else [(dep, dep.shape, lambda i, j: (0, 0))]


def _row(arr, tm):
    return (arr, (tm, arr.shape[1]), lambda i: (i, 0))


def _full(arr):
    nd = arr.ndim
    return (arr, arr.shape, lambda i: (0,) * nd)


def _row_out(shape, dtype, tm):
    return (shape, dtype, (tm, shape[1]), lambda i: (i, 0))


def _acc_out(shape):
    nd = len(shape)
    return (shape, F32, shape, lambda i: (0,) * nd)


def _tm(t):
    return 512 if t >= 1024 else t // 2


MXU_COLUMNS = 256


def _mxu_split(width):
    cut = -(-(width // 2) // MXU_COLUMNS) * MXU_COLUMNS
    return (cut,) if 0 < cut < width else ()


def _row_tile(rows, cap=256):
    best = max(tr for tr in range(16, min(rows, cap) + 1, 16) if rows % tr == 0)
    return best


def _first(i, refs):
    @pl.when(i == 0)
    def _():
        for r in refs:
            r[...] = jnp.zeros(r.shape, r.dtype)


def _rms_fwd(x, g, tag):
    t, d = x.shape
    tm = _tm(t)

    def body(x_ref, g_ref, h_ref):
        xv = x_ref[...]
        r = lax.rsqrt(jnp.mean(xv * xv, axis=-1, keepdims=True) + RMS_EPS)
        h_ref[...] = (xv * r * g_ref[...]).astype(BF16)

    return _rows(f"rms_fwd_{tag}", t // tm, [_row(x, tm), _full(g)], [_row_out((t, d), BF16, tm)], body)[0]


def _rms_bwd(dh, x, g, dx_out, tag):
    t, d = x.shape
    tm = _tm(t)

    def body(dh_ref, x_ref, g_ref, dxo_ref, dx_ref, dg_ref):
        i = pl.program_id(0)
        _first(i, [dg_ref])
        xv = x_ref[...]
        r = lax.rsqrt(jnp.mean(xv * xv, axis=-1, keepdims=True) + RMS_EPS)
        xh = xv * r
        dhv = dh_ref[...]
        dxh = dhv * g_ref[...]
        dx_ref[...] = dxo_ref[...] + r * (dxh - xh * jnp.mean(dxh * xh, axis=-1, keepdims=True))
        dg_ref[...] += jnp.sum(dhv * xh, axis=0, keepdims=True)

    return _rows(f"rms_bwd_{tag}", t // tm, [_row(dh, tm), _row(x, tm), _full(g), _row(dx_out, tm)],
                 [_row_out((t, d), F32, tm), _acc_out((1, d))], body)


def _loss_head(x, g, target):
    t, d = x.shape
    tm = _tm(t)

    def body(x_ref, g_ref, t_ref, dx_ref, dg_ref, loss_ref):
        i = pl.program_id(0)
        _first(i, [dg_ref, loss_ref])
        xv = x_ref[...]
        gv = g_ref[...]
        r = lax.rsqrt(jnp.mean(xv * xv, axis=-1, keepdims=True) + RMS_EPS)
        xh = xv * r
        err = xh * gv - t_ref[...]
        loss_ref[...] += jnp.full((1, LANES), 0.5, F32) * jnp.sum(jnp.mean(err * err, axis=-1, keepdims=True))
        dy = err * (1.0 / d)
        dxh = dy * gv
        dx_ref[...] = r * (dxh - xh * jnp.mean(dxh * xh, axis=-1, keepdims=True))
        dg_ref[...] += jnp.sum(dy * xh, axis=0, keepdims=True)

    return _rows("loss_head", t // tm, [_row(x, tm), _full(g), _row(target, tm)],
                 [_row_out((t, d), F32, tm), _acc_out((1, d)), _acc_out((1, LANES))], body)


def _colsum(arr, tag):
    t, w = arr.shape
    tm = _tm(t)

    def body(a_ref, o_ref):
        _first(pl.program_id(0), [o_ref])
        o_ref[...] += jnp.sum(a_ref[...].astype(F32), axis=0, keepdims=True)

    return _rows(f"colsum_{tag}", t // tm, [_row(arr, tm)], [_acc_out((1, w))], body)[0]


def _ffn_fwd(x, gn, w1, w3, get_w2, tag):
    t, d = x.shape
    fc = w1.shape[2]
    f = N_CHIPS * fc
    tm = _tm(t)
    h = _rms_fwd(x, gn, f"ffn_{tag}")
    w1blk, w1map = _col_nn(0, d, fc, fc)
    w3blk, w3map = _col_nn(0, d, fc, fc)
    tma = min(tm, 256)

    def ep_ab(accs, ex):
        a, b = accs
        return a, b, a * jax.nn.sigmoid(a) * b

    a, b, act = _mm(
        f"ffn_ab_{tag}", (t // tma, N_CHIPS, 1),
        [(h, (tma, d), _ik, w1, w1blk, w1map, "nn", 0, False), (h, (tma, d), _ik, w3, w3blk, w3map, "nn", 1, False)],
        [], [((t, f), BF16, (tma, fc), _ij)] * 3, ep_ab, [(tma, fc)] * 2, order="ji")
    w2 = get_w2(act)
    tn = min(d, 1024)
    tmo = min(t, 2 * tm)
    w2blk, w2map = _row_nn(0, fc, tn, fc)
    x_new = _mm(
        f"ffn_out_{tag}", (t // tmo, d // tn, N_CHIPS),
        [(act, (tmo, fc), _ik, w2, w2blk, w2map, "nn", 0, False)],
        [(x, (tmo, tn), _ij)], [((t, d), F32, (tmo, tn), _ij)],
        lambda accs, ex: [ex[0] + FFN_RESIDUAL_WEIGHT * accs[0]], [(tmo, tn)])[0]
    return x_new, (x, h, a, b, act), w2


def _ffn_bwd(dx_out, saved, gn, w1, w3, w2, tag, dep, put):
    x, h, a, b, act = saved
    t, d = x.shape
    fc = w1.shape[2]
    f = N_CHIPS * fc
    tm = _tm(t)
    tma = min(tm, 256)
    w2blk, w2map = _row_nt(0, fc, d, fc)

    def ep_dg(accs, ex):
        dg = FFN_RESIDUAL_WEIGHT * accs[0]
        av, bv = ex[0].astype(F32), ex[1].astype(F32)
        sig = jax.nn.sigmoid(av)
        return dg * bv * sig * (1.0 + av * (1.0 - sig)), dg * av * sig

    da, db = _mm(
        f"ffn_dg_{tag}", (t // tma, N_CHIPS, 1),
        [(dx_out, (tma, d), _ik, w2, w2blk, w2map, "nt", 0, False)],
        [(a, (tma, fc), _ij), (b, (tma, fc), _ij)] + _dep_extra(dep), [((t, f), BF16, (tma, fc), _ij)] * 2, ep_dg, [(tma, fc)], order="ji",
        split=_mxu_split(fc))

    tmd = min(d, 512)
    tk = min(t, 2048)
    oblk, omap = _out_col(tmd, fc, fc)
    dw1, dw3 = _mm(
        f"ffn_dw13_{tag}", (d // tmd, N_CHIPS, t // tk),
        [(h, (tk, tmd), _ki, da, (tk, fc), _kj, "tn", 0, False), (h, (tk, tmd), _ki, db, (tk, fc), _kj, "tn", 1, False)],
        [], [((N_CHIPS, d, fc), BF16, oblk, omap)] * 2, lambda accs, ex: accs, [(tmd, fc)] * 2)
    tn = min(d, 1024)
    tk2 = min(t, 2048)
    dw2 = _mm(
        f"ffn_dw2_{tag}", (N_CHIPS, d // tn, t // tk2),
        [(act, (tk2, fc), _ki, dx_out, (tk2, tn), _kj, "tn", 0, False)],
        [], [((N_CHIPS, fc, d), BF16, (None, fc, tn), lambda i, j: (i, 0, j))],
        lambda accs, ex: [FFN_RESIDUAL_WEIGHT * accs[0]], [(fc, tn)])[0]
    token = put({"w1": dw1, "w3": dw3, "w2": dw2})
    w1blk, w1map = _col_nt(0, tn, fc, fc)
    w3blk, w3map = _col_nt(0, tn, fc, fc)
    dh = _mm(
        f"ffn_dh_{tag}", (t // tm, d // tn, N_CHIPS),
        [(da, (tm, fc), _ik, w1, w1blk, w1map, "nt", 0, False), (db, (tm, fc), _ik, w3, w3blk, w3map, "nt", 0, False)],
        _dep_extra(token), [((t, d), F32, (tm, tn), _ij)], lambda accs, ex: accs, [(tm, tn)])[0]
    dx, dgn = _rms_bwd(dh, x, gn, dx_out, f"ffn_{tag}")
    return dx, dgn, token


def _ple_fwd(x, p_i, gn, wg, wp, tag):
    t, d = x.shape
    pdim = p_i.shape[1]
    d4 = d // N_CHIPS
    tm = _tm(t)
    hp = _rms_fwd(x, gn, f"ple_{tag}")
    tm = min(t, 2 * tm)

    def ep(accs, ex):
        s = jax.nn.sigmoid(accs[0])
        return ex[0] + s * accs[1], s, accs[1]

    x_new, s, pp = _mm(
        f"ple_fwd_{tag}", (t // tm, N_CHIPS, 1),
        [(hp, (tm, d), _i0, wg, (N_CHIPS, d4, d4), lambda i, j, k: (0, 0, j), "nn", 0, False),
         (p_i, (tm, pdim), _i0, wp, (None, pdim, d4), lambda i, j, k: (j, 0, 0), "nn", 1, False)],
        [(x, (tm, d4), _ij)], [((t, d), F32, (tm, d4), _ij), ((t, d), BF16, (tm, d4), _ij), ((t, d), BF16, (tm, d4), _ij)],
        ep, [(tm, d4)] * 2)
    return x_new, (x, hp, s, pp)


def _ple_bwd(dx_out, saved, p_i, gn, wg, tag, dep=None):
    x, hp, s, pp = saved
    t, d = x.shape
    pdim = p_i.shape[1]
    d4 = d // N_CHIPS
    tm = _tm(t)

    def body(dx_ref, s_ref, pp_ref, dpp_ref, dgp_ref):
        dxv = dx_ref[...]
        sv = s_ref[...].astype(F32)
        ppv = pp_ref[...].astype(F32)
        dpp_ref[...] = (dxv * sv).astype(BF16)
        dgp_ref[...] = (dxv * ppv * sv * (1.0 - sv)).astype(BF16)

    dpp, dgp = _rows(f"ple_ew_{tag}", t // tm, [_row(dx_out, tm), _row(s, tm), _row(pp, tm)],
                     [_row_out((t, d), BF16, tm)] * 2, body, deps=[] if dep is None else [dep])
    tk = min(t, 2048)
    dwp = _mm(
        f"ple_dwp_{tag}", (1, N_CHIPS, t // tk),
        [(p_i, (tk, pdim), lambda i, j, k: (k, 0), dpp, (tk, d4), _kj, "tn", 0, False)],
        [], [((N_CHIPS, pdim, d4), BF16, (None, pdim, d4), lambda i, j: (j, 0, 0))], lambda accs, ex: accs, [(pdim, d4)])[0]
    tn = min(d, 1024)
    dwg = _mm(
        f"ple_dwg_{tag}", (N_CHIPS, d // tn, t // tk),
        [(hp, (tk, d4), _ki, dgp, (tk, tn), _kj, "tn", 0, False)],
        [], [((N_CHIPS, d4, d), BF16, (None, d4, tn), lambda i, j: (i, 0, j))], lambda accs, ex: accs, [(d4, tn)])[0]
    dhp = _mm(
        f"ple_dh_{tag}", (t // tm, 1, 1),
        [(dgp, (tm, d), _ik, wg, (N_CHIPS, d4, d), lambda i, j, k: (0, 0, 0), "nt", 0, False)],
        [], [((t, d), F32, (tm, d), _ij)], lambda accs, ex: accs, [(tm, d)])[0]
    dx, dgn = _rms_bwd(dhp, x, gn, dx_out, f"ple_{tag}")
    return dx, dgn, dwg, dwp


_SQRT_HALF = 0.7071067811865476
_INV_SQRT_2PI = 0.3989422804014327


def _gelu(z):
    return z * (lax.erf(z * _SQRT_HALF) + 1.0) * 0.5


def _gelu_grad(z):
    return 0.5 * (1.0 + lax.erf(z * _SQRT_HALF)) + z * (_INV_SQRT_2PI * jnp.exp(-0.5 * z * z))


def _causal_bf16(w):
    c = w.shape[0]
    keep = lax.broadcasted_iota(jnp.int32, (c, c), 0) >= lax.broadcasted_iota(jnp.int32, (c, c), 1)
    return jnp.where(keep, w, 0.0).astype(BF16), keep


def _gmlp_gate_fwd(z_pre, ln_g, ln_b, w_s, b_st, tag):
    t, w2 = z_pre.shape
    gw = w2 // 2
    n_g, chunk, _ = w_s.shape
    gd = gw // n_g

    def body(z_ref, g_ref, b_ref, ws_ref, bs_ref, o_ref):
        z = z_ref[...]
        u = _gelu(z[:, :gw])
        zv = _gelu(z[:, gw:])
        mu = jnp.mean(zv, axis=-1, keepdims=True)
        cen = zv - mu
        rstd = lax.rsqrt(jnp.mean(cen * cen, axis=-1, keepdims=True) + LN_EPS)
        vln = (cen * rstd * g_ref[...] + b_ref[...]).astype(BF16)
        bst = bs_ref[...]
        for g in range(n_g):
            wm, _ = _causal_bf16(ws_ref[g])
            sl = slice(g * gd, (g + 1) * gd)
            s = jnp.dot(wm, vln[:, sl], preferred_element_type=F32) + bst[:, g:g + 1]
            o_ref[:, sl] = (u[:, sl] * s).astype(BF16)

    return _rows(f"gmlp_gate_{tag}", t // chunk, [_row(z_pre, chunk), _full(ln_g), _full(ln_b), _full(w_s), _full(b_st)],
                 [_row_out((t, gw), BF16, chunk)], body)[0]


def _gmlp_gate_bwd(z_pre, dgated, ln_g, ln_b, w_s, b_st, tag):
    t, w2 = z_pre.shape
    gw = w2 // 2
    n_g, chunk, _ = w_s.shape
    gd = gw // n_g

    def body(z_ref, dgt_ref, g_ref, b_ref, ws_ref, bs_ref, dz_ref, dws_ref, dbs_ref, dlg_ref, dlb_ref, dv_scr):
        _first(pl.program_id(0), [dws_ref, dbs_ref, dlg_ref, dlb_ref])
        z = z_ref[...]
        zu, zvp = z[:, :gw], z[:, gw:]
        u = _gelu(zu)
        zv = _gelu(zvp)
        mu = jnp.mean(zv, axis=-1, keepdims=True)
        cen = zv - mu
        rstd = lax.rsqrt(jnp.mean(cen * cen, axis=-1, keepdims=True) + LN_EPS)
        vn = cen * rstd
        lg = g_ref[...]
        vln = (vn * lg + b_ref[...]).astype(BF16)
        bst = bs_ref[...]
        lane = lax.broadcasted_iota(jnp.int32, (chunk, LANES), 1)
        dbs = jnp.zeros((chunk, LANES), F32)
        for g in range(n_g):
            wm, keep = _causal_bf16(ws_ref[g])
            sl = slice(g * gd, (g + 1) * gd)
            vg = vln[:, sl]
            s = jnp.dot(wm, vg, preferred_element_type=F32) + bst[:, g:g + 1]
            dgt = dgt_ref[:, sl].astype(F32)
            ds = dgt * u[:, sl]
            ds16 = ds.astype(BF16)
            dz_ref[:, sl] = (dgt * s * _gelu_grad(zu[:, sl])).astype(dz_ref.dtype)
            dv_scr[:, sl] = lax.dot_general(wm, ds16, _DIMS["tn"], preferred_element_type=F32)
            dw = lax.dot_general(ds16, vg, _DIMS["nt"], preferred_element_type=F32)
            dws_ref[g] += jnp.where(keep, dw, 0.0)
            dbs = dbs + jnp.where(lane == g, jnp.sum(ds, axis=-1, keepdims=True), 0.0)
        dbs_ref[...] += dbs
        dvln = dv_scr[...]
        dlg_ref[...] += jnp.sum(dvln * vn, axis=0, keepdims=True)
        dlb_ref[...] += jnp.sum(dvln, axis=0, keepdims=True)
        dvn = dvln * lg
        dzv = rstd * (dvn - jnp.mean(dvn, axis=-1, keepdims=True) - vn * jnp.mean(dvn * vn, axis=-1, keepdims=True))
        dz_ref[:, gw:] = (dzv * _gelu_grad(zvp)).astype(dz_ref.dtype)

    return _rows(
        f"gmlp_gate_bwd_{tag}", t // chunk,
        [_row(z_pre, chunk), _row(dgated, chunk), _full(ln_g), _full(ln_b), _full(w_s), _full(b_st)],
        [_row_out((t, w2), BF16, chunk), _acc_out((n_g, chunk, chunk)), _acc_out((chunk, LANES)), _acc_out((1, gw)),
         _acc_out((1, gw))], body, scratch=[pltpu.VMEM((chunk, gw), F32)])


def _gmlp_fwd(x, gn, ln_g, ln_b, w_s, b_st, w_in, w_out, tag):
    t, d = x.shape
    gw = ln_g.shape[1]
    tm = _tm(t)
    d4 = d // N_CHIPS
    h = _rms_fwd(x, gn, f"mix_{tag}")
    tn = min(d, 1024)
    iblk, imap = _col_nn(0, d, tn, d)
    z_pre = _mm(
        f"gmlp_in_{tag}", (t // tm, N_CHIPS * d // tn, 1),
        [(h, (tm, d), _ik, w_in, iblk, imap, "nn", 0, False)],
        [], [((t, 2 * gw), F32, (tm, tn), _ij)], lambda accs, ex: accs, [(tm, tn)], order="ji")[0]
    gated = _gmlp_gate_fwd(z_pre, ln_g, ln_b, w_s, b_st, tag)
    rc = gw // N_CHIPS
    tk = min(rc, 1024)
    tmo = min(t, 2 * tm)
    oblk, omap = _row_nn(0, tk, tn, rc)
    x_new = _mm(
        f"gmlp_out_{tag}", (t // tmo, d // tn, gw // tk),
        [(gated, (tmo, tk), _ik, w_out, oblk, omap, "nn", 0, False)],
        [(x, (tmo, tn), _ij)], [((t, d), F32, (tmo, tn), _ij)], lambda accs, ex: [ex[0] + accs[0]], [(tmo, tn)])[0]
    return x_new, (x, h, z_pre, gated)


def _gmlp_bwd(dx_out, saved, gn, ln_g, ln_b, w_s, b_st, w_in, w_out, tag, dep=None):
    x, h, z_pre, gated = saved
    t, d = x.shape
    gw = ln_g.shape[1]
    tm = _tm(t)
    d4 = d // N_CHIPS
    rc = gw // N_CHIPS
    tnr = min(rc, 1024)
    oblk, omap = _row_nt(0, tnr, d, rc)
    dgated = _mm(
        f"gmlp_dgated_{tag}", (t // tm, gw // tnr, 1),
        [(dx_out, (tm, d), _ik, w_out, oblk, omap, "nt", 0, False)],
        _dep_extra(dep), [((t, gw), BF16, (tm, tnr), _ij)], lambda accs, ex: accs, [(tm, tnr)], order="ji")[0]
    tk = min(t, 2048)
    tn = min(d, 1024)
    rblk, rmap = _out_row(tnr, tn, rc)
    dw_out = _mm(
        f"gmlp_dwout_{tag}", (gw // tnr, d // tn, t // tk),
        [(gated, (tk, tnr), _ki, dx_out, (tk, tn), _kj, "tn", 0, False)],
        [], [((N_CHIPS, rc, d), BF16, rblk, rmap)], lambda accs, ex: accs, [(tnr, tn)])[0]
    dz, dws, dbs, dlg, dlb = _gmlp_gate_bwd(z_pre, dgated, ln_g, ln_b, w_s, b_st, tag)
    tmd = min(d, 512)
    cblk, cmap = _out_col(tmd, tn, d)
    dw_in = _mm(
        f"gmlp_dwin_{tag}", (d // tmd, N_CHIPS * d // tn, t // tk),
        [(h, (tk, tmd), _ki, dz, (tk, tn), _kj, "tn", 0, False)],
        [], [((N_CHIPS, d, d), BF16, cblk, cmap)], lambda accs, ex: accs, [(tmd, tn)])[0]
    iblk, imap = _col_nt(0, tn, d, d)
    dh = _mm(
        f"gmlp_dh_{tag}", (t // tm, d // tn, N_CHIPS),
        [(dz, (tm, d), _ik, w_in, iblk, imap, "nt", 0, False)],
        [], [((t, d), F32, (tm, tn), _ij)], lambda accs, ex: accs, [(tm, tn)])[0]
    dx, dgn = _rms_bwd(dh, x, gn, dx_out, f"mix_{tag}")
    n_g = w_s.shape[0]
    return dx, dgn, dw_in, dw_out, dws, dbs[:, :n_g].T, dlg, dlb


def _rope_tables(t, width, n_rot_heads):
    half = ROPE_DIM // 2
    inv_freq = ROPE_THETA ** (-jnp.arange(0, ROPE_DIM, 2, dtype=F32) / ROPE_DIM)
    ang = jnp.arange(t, dtype=F32)[:, None] * inv_freq[None, :]
    cos, sin = jnp.cos(ang), jnp.sin(ang)
    ones = jnp.ones((t, HEAD_DIM - ROPE_DIM), F32)
    zeros = jnp.zeros((t, HEAD_DIM - ROPE_DIM), F32)
    zh = jnp.zeros((t, half), F32)
    c = jnp.concatenate([cos, cos, ones], axis=1)
    s_next = jnp.concatenate([-sin, zh, zeros], axis=1)
    s_prev = jnp.concatenate([zh, sin, zeros], axis=1)
    rest = width - n_rot_heads * HEAD_DIM

    def widen(tab, fill):
        parts = [jnp.tile(tab, (1, n_rot_heads))]
        if rest:
            parts.append(jnp.full((t, rest), fill, F32))
        return jnp.concatenate(parts, axis=1)

    return widen(c, 1.0), widen(s_next, 0.0), widen(s_prev, 0.0)


def _rope(v, c, s_next, s_prev, sign):
    w = v.shape[1]
    half = ROPE_DIM // 2
    reps = w // c.shape[1]
    if reps > 1:
        c, s_next, s_prev = (jnp.tile(tab, (1, reps)) for tab in (c, s_next, s_prev))
    return v * c + sign * (pltpu.roll(v, w - half, 1) * s_next + pltpu.roll(v, half, 1) * s_prev)


def _scores_mask(n, blk, heads):
    row = lax.broadcasted_iota(jnp.int32, (heads * blk, 2 * blk), 0)
    q_pos = lax.rem(row, blk) + blk
    k_pos = lax.broadcasted_iota(jnp.int32, (heads * blk, 2 * blk), 1)
    diff = q_pos - k_pos
    band = (diff >= 0) & (diff < blk)
    return band & ((k_pos >= blk) | (n > 0))


def _heads_to_rows(v, first, heads):
    return jnp.concatenate([v[:, (first + g) * HEAD_DIM:(first + g + 1) * HEAD_DIM] for g in range(heads)], axis=0)


def _sink_column(sink_ref, first, heads, blk):
    return jnp.concatenate([jnp.full((blk, 1), sink_ref[0, first + g], F32) for g in range(heads)], axis=0)


def _attn_fwd(q, kv, sinks, blk):
    t, qw = q.shape
    kw = kv.shape[1] // 2
    n_kv = kw // HEAD_DIM
    q_per_kv = qw // kw
    scale = HEAD_DIM ** -0.5

    def body(sink_ref, q_ref, kvo_ref, kvp_ref, o_ref):
        n = pl.program_id(0)
        valid = _scores_mask(n, blk, 1)
        qv = q_ref[...]
        kvo = kvo_ref[...]
        kvp = kvp_ref[...]
        for kh in range(n_kv):
            ks = slice(kh * HEAD_DIM, (kh + 1) * HEAD_DIM)
            vs = slice(kw + kh * HEAD_DIM, kw + (kh + 1) * HEAD_DIM)
            kb = jnp.concatenate([kvp[:, ks], kvo[:, ks]], axis=0)
            vb = jnp.concatenate([kvp[:, vs], kvo[:, vs]], axis=0)
            for g in range(q_per_kv):
                hd = kh * q_per_kv + g
                hs = slice(hd * HEAD_DIM, (hd + 1) * HEAD_DIM)
                sink = sink_ref[0, hd]
                s = lax.dot_general(qv[:, hs], kb, _DIMS["nt"], preferred_element_type=F32) * scale
                s = jnp.where(valid, s, -1e30)
                m = jnp.maximum(jnp.max(s, axis=-1, keepdims=True), sink)
                e = jnp.where(valid, jnp.exp(s - m), 0.0)
                denom = jnp.sum(e, axis=-1, keepdims=True) + jnp.exp(sink - m)
                p = (e / denom).astype(BF16)
                o_ref[:, hs] = jnp.dot(p, vb, preferred_element_type=F32).astype(BF16)

    return _rows(
        "swa_attn", t // blk,
        [(sinks, pl.BlockSpec(memory_space=pltpu.SMEM)), _row(q, blk), _row(kv, blk),
         (kv, (blk, kv.shape[1]), lambda i: (jnp.maximum(i - 1, 0), 0))],
        [_row_out((t, qw), BF16, blk)], body)[0]


def _attn_bwd(q, kv, do, sinks, blk):
    t, qw = q.shape
    kw = kv.shape[1] // 2
    n_kv = kw // HEAD_DIM
    q_per_kv = qw // kw
    scale = HEAD_DIM ** -0.5

    def body(sink_ref, q_ref, kvo_ref, kvp_ref, do_ref, dq_ref, dkvo_ref, dkvp_ref, dsink_ref):
        n = pl.program_id(0)
        _first(n, [dsink_ref])
        valid = _scores_mask(n, blk, q_per_kv)
        lane = lax.broadcasted_iota(jnp.int32, (1, LANES), 1)
        qv = q_ref[...]
        kvo = kvo_ref[...]
        kvp = kvp_ref[...]
        dov = do_ref[...]
        dsink = jnp.zeros((1, LANES), F32)
        for kh in range(n_kv):
            ks = slice(kh * HEAD_DIM, (kh + 1) * HEAD_DIM)
            vs = slice(kw + kh * HEAD_DIM, kw + (kh + 1) * HEAD_DIM)
            kb = jnp.concatenate([kvp[:, ks], kvo[:, ks]], axis=0)
            vb = jnp.concatenate([kvp[:, vs], kvo[:, vs]], axis=0)
            qg = _heads_to_rows(qv, kh * q_per_kv, q_per_kv)
            dog = _heads_to_rows(dov, kh * q_per_kv, q_per_kv)
            sink = _sink_column(sink_ref, kh * q_per_kv, q_per_kv, blk)
            s = lax.dot_general(qg, kb, _DIMS["nt"], preferred_element_type=F32) * scale
            s = jnp.where(valid, s, -1e30)
            m = jnp.maximum(jnp.max(s, axis=-1, keepdims=True), sink)
            e = jnp.where(valid, jnp.exp(s - m), 0.0)
            e_sink = jnp.exp(sink - m)
            inv = 1.0 / (jnp.sum(e, axis=-1, keepdims=True) + e_sink)
            p = e * inv
            p16 = p.astype(BF16)
            dp = lax.dot_general(dog, vb, _DIMS["nt"], preferred_element_type=F32)
            dot_pd = jnp.sum(p * dp, axis=-1, keepdims=True)
            ds16 = (p * (dp - dot_pd)).astype(BF16)
            d_sink_rows = e_sink * inv * dot_pd
            dqg = (jnp.dot(ds16, kb, preferred_element_type=F32) * scale).astype(BF16)
            for g in range(q_per_kv):
                hd = kh * q_per_kv + g
                rows = slice(g * blk, (g + 1) * blk)
                dsink = dsink + jnp.where(lane == hd, -jnp.sum(d_sink_rows[rows], axis=0, keepdims=True), 0.0)
                dq_ref[:, hd * HEAD_DIM:(hd + 1) * HEAD_DIM] = dqg[rows]
            dkb = lax.dot_general(ds16, qg, _DIMS["tn"], preferred_element_type=F32) * scale
            dvb = lax.dot_general(p16, dog, _DIMS["tn"], preferred_element_type=F32)
            dkvp_ref[:, ks] = dkb[:blk]
            dkvo_ref[:, ks] = dkb[blk:]
            dkvp_ref[:, vs] = dvb[:blk]
            dkvo_ref[:, vs] = dvb[blk:]
        dsink_ref[...] += dsink

    return _rows(
        "swa_attn_bwd", t // blk,
        [(sinks, pl.BlockSpec(memory_space=pltpu.SMEM)), _row(q, blk), _row(kv, blk),
         (kv, (blk, kv.shape[1]), lambda i: (jnp.maximum(i - 1, 0), 0)), _row(do, blk)],
        [_row_out((t, qw), BF16, blk), _row_out((t, 2 * kw), F32, blk), _row_out((t, 2 * kw), F32, blk),
         _acc_out((1, LANES))], body)


def _rope_bwd(dq_r, dkv_own, dkv_prev, tabs_q, tabs_kv, blk):
    t, qw = dq_r.shape
    kvw = dkv_own.shape[1]
    nb = t // blk

    def body(dq_ref, own_ref, nxt_ref, cq, snq, spq, ck, snk, spk, dqo_ref, dkvo_ref, dbq_ref, dbkv_ref):
        i = pl.program_id(0)
        _first(i, [dbq_ref, dbkv_ref])
        dq = _rope(dq_ref[...].astype(F32), cq[...], snq[...], spq[...], -1.0)
        dkv = own_ref[...] + jnp.where(i < nb - 1, nxt_ref[...], 0.0)
        dkv = _rope(dkv, ck[...], snk[...], spk[...], -1.0)
        dqo_ref[...] = dq.astype(BF16)
        dkvo_ref[...] = dkv.astype(BF16)
        dbq_ref[...] += jnp.sum(dq, axis=0, keepdims=True)
        dbkv_ref[...] += jnp.sum(dkv, axis=0, keepdims=True)

    return _rows(
        "swa_rope_bwd", nb,
        [_row(dq_r, blk), _row(dkv_own, blk), (dkv_prev, (blk, kvw), lambda i: (jnp.minimum(i + 1, nb - 1), 0))]
        + [_row(tab, blk) for tab in tabs_q] + [_row(tab, blk) for tab in tabs_kv],
        [_row_out((t, qw), BF16, blk), _row_out((t, kvw), BF16, blk), _acc_out((1, qw)), _acc_out((1, kvw))], body)


def _swa_fwd(x, gn, bq, bkv, bo, sinks, wq, wkv, wo, blk, tag):
    t, d = x.shape
    qw = bq.shape[1]
    kvw = bkv.shape[1]
    d4 = d // N_CHIPS
    tm = _tm(t)
    h = _rms_fwd(x, gn, f"mix_{tag}")
    tabs_q = _rope_tables(t, LANES, LANES // HEAD_DIM)
    tabs_kv = _rope_tables(t, kvw, kvw // 2 // HEAD_DIM)
    tn = min(qw, 1024)
    whole = lambda i, j, k: (0, 0, j)

    def ep_rope(accs, ex):
        return [_rope(accs[0] + ex[0], ex[1], ex[2], ex[3], 1.0)]

    q = _mm(
        f"swa_q_{tag}", (t // tm, qw // tn, 1),
        [(h, (tm, d), _ik, wq, (N_CHIPS, d4, tn), whole, "nn", 0, False)],
        [(bq, (1, tn), _0j)] + [(tab, (tm, LANES), _i0e) for tab in tabs_q],
        [((t, qw), BF16, (tm, tn), _ij)], ep_rope, [(tm, tn)], order="ji")[0]
    kv = _mm(
        f"swa_kv_{tag}", (t // tm, 1, 1),
        [(h, (tm, d), _ik, wkv, (N_CHIPS, d4, kvw), whole, "nn", 0, False)],
        [(bkv, (1, kvw), _0j)] + [(tab, (tm, kvw), _i0e) for tab in tabs_kv],
        [((t, kvw), BF16, (tm, kvw), _ij)], ep_rope, [(tm, kvw)])[0]
    o = _attn_fwd(q, kv, sinks, blk)
    tno = min(d, 1024)
    x_new = _mm(
        f"swa_out_{tag}", (t // tm, d // tno, 1),
        [(o, (tm, qw), _ik, wo, (N_CHIPS, qw // N_CHIPS, tno), whole, "nn", 0, False)],
        [(x, (tm, tno), _ij), (bo, (1, tno), _0j)], [((t, d), F32, (tm, tno), _ij)],
        lambda accs, ex: [ex[0] + accs[0] + ex[1]], [(tm, tno)], order="ji")[0]
    return x_new, (x, h, q, kv, o, tabs_q, tabs_kv)


def _swa_bwd(dx_out, saved, gn, sinks, wq, wkv, wo, blk, tag, dep=None):
    x, h, q, kv, o, tabs_q, tabs_kv = saved
    t, d = x.shape
    qw = q.shape[1]
    kvw = kv.shape[1]
    d4 = d // N_CHIPS
    qw4 = qw // N_CHIPS
    tm = _tm(t)
    whole = lambda i, j, k: (0, 0, 0)
    do = _mm(
        f"swa_do_{tag}", (t // tm, 1, 1),
        [(dx_out, (tm, d), _ik, wo, (N_CHIPS, qw4, d), whole, "nt", 0, False)],
        _dep_extra(dep), [((t, qw), BF16, (tm, qw), _ij)], lambda accs, ex: accs, [(tm, qw)])[0]
    tk = min(t, 2048)
    tn = min(d, 1024)
    dwo = _mm(
        f"swa_dwo_{tag}", (N_CHIPS, d // tn, t // tk),
        [(o, (tk, qw4), _ki, dx_out, (tk, tn), _kj, "tn", 0, False)],
        [], [((N_CHIPS, qw4, d), BF16, (None, qw4, tn), lambda i, j: (i, 0, j))], lambda accs, ex: accs, [(qw4, tn)])[0]
    dbo = _colsum(dx_out, f"bo_{tag}")
    dq_r, dkv_own, dkv_prev, dsink = _attn_bwd(q, kv, do, sinks, blk)
    dq, dkv, dbq, dbkv = _rope_bwd(dq_r, dkv_own, dkv_prev, tabs_q, tabs_kv, blk)
    tnq = min(qw, 1024)
    dwq = _mm(
        f"swa_dwq_{tag}", (N_CHIPS, qw // tnq, t // tk),
        [(h, (tk, d4), _ki, dq, (tk, tnq), _kj, "tn", 0, False)],
        [], [((N_CHIPS, d4, qw), BF16, (None, d4, tnq), lambda i, j: (i, 0, j))], lambda accs, ex: accs, [(d4, tnq)])[0]
    dwkv = _mm(
        f"swa_dwkv_{tag}", (N_CHIPS, 1, t // tk),
        [(h, (tk, d4), _ki, dkv, (tk, kvw), _kj, "tn", 0, False)],
        [], [((N_CHIPS, d4, kvw), BF16, (None, d4, kvw), lambda i, j: (i, 0, j))], lambda accs, ex: accs, [(d4, kvw)])[0]
    dh = _mm(
        f"swa_dh_{tag}", (t // tm, 1, 1),
        [(dq, (tm, qw), _ik, wq, (N_CHIPS, d4, qw), whole, "nt", 0, False),
         (dkv, (tm, kvw), _ik, wkv, (N_CHIPS, d4, kvw), whole, "nt", 0, False)],
        [], [((t, d), F32, (tm, d), _ij)], lambda accs, ex: accs, [(tm, d)])[0]
    dx, dgn = _rms_bwd(dh, x, gn, dx_out, f"mix_{tag}")
    n_heads = qw // HEAD_DIM
    return dx, dgn, dwq, dwkv, dwo, dbq, dbkv, dbo, dsink[:, :n_heads]


_HBM = pl.BlockSpec(memory_space=pl.ANY)
_CHIP_FLIPS = ((1, 0), (0, 1), (1, 1))


def _place():
    x, y, c = lax.axis_index("x"), lax.axis_index("y"), lax.axis_index("c")
    return x, y, c


def _flip(v, bit):
    return 1 - v if bit else v


def _exchange_small(v_ref, all_ref, send_sems, recv_sems):
    x, y, c = _place()
    me = 4 * x + 2 * y + c
    all_ref[me] = v_ref[...]
    copies = []
    for dlt in range(1, N_DEV):
        peer = (_flip(x, dlt & 4), _flip(y, dlt & 2), _flip(c, dlt & 1))
        copies.append(pltpu.make_async_remote_copy(
            src_ref=v_ref, dst_ref=all_ref.at[me], send_sem=send_sems.at[dlt - 1], recv_sem=recv_sems.at[dlt - 1],
            device_id=peer, device_id_type=MESH_ID))
    for cp in copies:
        cp.start()
    for cp in copies:
        cp.wait()


def _all_gather_small(v):
    r, cdim = v.shape

    def body(v_ref, out_ref, send_sems, recv_sems):
        _exchange_small(v_ref, out_ref, send_sems, recv_sems)

    return pl.pallas_call(
        body, name="all_gather_small", out_shape=jax.ShapeDtypeStruct((N_DEV, r, cdim), v.dtype),
        scratch_shapes=[pltpu.SemaphoreType.DMA((N_DEV - 1,)), pltpu.SemaphoreType.DMA((N_DEV - 1,))],
        compiler_params=pltpu.CompilerParams(vmem_limit_bytes=VMEM_LIMIT_BYTES),
    )(v)


def _all_sum_small(v):
    r, cdim = v.shape

    def body(v_ref, out_ref, all_ref, send_sems, recv_sems):
        _exchange_small(v_ref, all_ref, send_sems, recv_sems)
        acc = all_ref[0]
        for dv in range(1, N_DEV):
            acc = acc + all_ref[dv]
        out_ref[...] = acc

    return pl.pallas_call(
        body, name="all_sum_small", out_shape=jax.ShapeDtypeStruct((r, cdim), v.dtype),
        scratch_shapes=[pltpu.VMEM((N_DEV, r, cdim), v.dtype), pltpu.SemaphoreType.DMA((N_DEV - 1,)),
                        pltpu.SemaphoreType.DMA((N_DEV - 1,))],
        compiler_params=pltpu.CompilerParams(vmem_limit_bytes=VMEM_LIMIT_BYTES),
    )(v)


_HBM_SPEC = pl.BlockSpec(memory_space=pltpu.HBM)
_SEM_SPEC = pl.BlockSpec(memory_space=pltpu.SEMAPHORE)
_EFFECT = pltpu.SideEffectType.DATAFLOW_SIDE_EFFECTING
_TOKEN = jax.ShapeDtypeStruct((8, LANES), F32)


def _in_hbm(a):
    return pltpu.with_memory_space_constraint(a, pltpu.HBM)


def _hbm_like(a):
    return pltpu.HBM(a.shape, a.dtype)


def _ici_copies(shard, land, send_sems, recv_sems, base, x, y, c):
    half = shard.shape[0] // 2
    rows = pl.ds(c * half, half)
    return [pltpu.make_async_remote_copy(
        src_ref=shard.at[rows], dst_ref=land.at[2 * x + y, rows], send_sem=send_sems.at[base + j], recv_sem=recv_sems.at[base + j],
        device_id=(_flip(x, fx), _flip(y, fy), c), device_id_type=MESH_ID) for j, (fx, fy) in enumerate(_CHIP_FLIPS)]


def _d2d_copies(land, send_sems, recv_sems, base, x, y, c, c_rows):
    half = land.shape[1] // 2
    rows = pl.ds(c_rows * half, half)
    out = []
    for j, (fx, fy) in enumerate(_CHIP_FLIPS):
        piece = land.at[2 * _flip(x, fx) + _flip(y, fy), rows]
        out.append(pltpu.make_async_remote_copy(
            src_ref=piece, dst_ref=piece, send_sem=send_sems.at[base + j], recv_sem=recv_sems.at[base + j],
            device_id=(x, y, 1 - c), device_id_type=MESH_ID))
    return out


def _own_copy(shard, land, send_sems, recv_sems, s, x, y, c):
    return pltpu.make_async_remote_copy(
        src_ref=shard, dst_ref=land.at[2 * x + y], send_sem=send_sems.at[s], recv_sem=recv_sems.at[s],
        device_id=(x, y, 1 - c), device_id_type=MESH_ID)


def _gather_start(groups):
    flat = [s for grp in groups for s in grp]
    n, n_g = len(flat), len(groups)
    lands = [lax.empty((N_CHIPS,) + s.shape, s.dtype) for s in flat]

    def body(*refs):
        shards, land_refs = refs[:n], refs[n:2 * n]
        sems = refs[2 * n: 2 * n + 4 * n_g]
        token = refs[-1]
        x, y, c = _place()
        idx = 0
        for g, grp in enumerate(groups):
            ici_send, ici_recv, own_send, own_recv = sems[4 * g: 4 * g + 4]
            for s in range(len(grp)):
                _own_copy(shards[idx], land_refs[idx], own_send, own_recv, s, x, y, c).start()
                for cp in _ici_copies(shards[idx], land_refs[idx], ici_send, ici_recv, 3 * s, x, y, c):
                    cp.start()
                idx += 1
        token[...] = jnp.zeros(token.shape, F32)

    sem_shapes = []
    for grp in groups:
        k = len(grp)
        sem_shapes += [pltpu.SemaphoreType.DMA((3 * k,)), pltpu.SemaphoreType.DMA((3 * k,)),
                       pltpu.SemaphoreType.DMA((k,)), pltpu.SemaphoreType.DMA((k,))]
    res = pl.pallas_call(
        body, name="gather_start",
        out_shape=sem_shapes + [_hbm_like(s) for s in flat] + [_hbm_like(ld) for ld in lands] + [_TOKEN],
        in_specs=[_HBM_SPEC] * (2 * n),
        out_specs=[_SEM_SPEC] * (4 * n_g) + [_HBM_SPEC] * (2 * n) + [pl.BlockSpec(memory_space=pltpu.VMEM)],
        input_output_aliases={i: 4 * n_g + i for i in range(2 * n)},
        compiler_params=pltpu.CompilerParams(has_side_effects=_EFFECT),
    )(*[_in_hbm(s) for s in flat], *[_in_hbm(ld) for ld in lands])
    sems, thru, token = res[:4 * n_g], res[4 * n_g: 4 * n_g + 2 * n], res[-1]
    out, idx = [], 0
    for g, grp in enumerate(groups):
        k = len(grp)
        out.append(dict(sems=tuple(sems[4 * g: 4 * g + 4]), shards=list(thru[idx: idx + k]),
                        lands=list(thru[n + idx: n + idx + k])))
        idx += k
    return out, token


def _gather_step(name, landed, arriving, after):
    n_l = len(landed["lands"]) if landed else 0
    n_a = len(arriving["lands"]) if arriving else 0

    def body(*refs):
        pos = 0
        l_lands = refs[pos: pos + n_l]; pos += n_l
        l_sems = refs[pos: pos + (2 if landed else 0)]; pos += 2 if landed else 0
        a_shards = refs[pos: pos + n_a]; pos += n_a
        a_lands = refs[pos: pos + n_a]; pos += n_a
        a_sems = refs[pos: pos + (4 if arriving else 0)]; pos += 4 if arriving else 0
        pos += 1
        pos += n_l + n_a
        new_sems = refs[pos: pos + (2 if arriving else 0)]
        x, y, c = _place()
        if arriving:
            ici_send, ici_recv, own_send, own_recv = a_sems
            started = []
            for s in range(n_a):
                own = _own_copy(a_shards[s], a_lands[s], own_send, own_recv, s, x, y, c)
                own.wait_recv()
                for cp in _d2d_copies(a_lands[s], ici_send, ici_recv, 3 * s, x, y, c, c):
                    cp.wait_recv()
                for cp in _d2d_copies(a_lands[s], new_sems[0], new_sems[1], 3 * s, x, y, c, c):
                    cp.start()
                started.append(own)
                started += _ici_copies(a_shards[s], a_lands[s], ici_send, ici_recv, 3 * s, x, y, c)
            for cp in started:
                cp.wait_send()
        if landed:
            for s in range(n_l):
                for cp in _d2d_copies(l_lands[s], l_sems[0], l_sems[1], 3 * s, x, y, c, c):
                    cp.wait_send()
                for cp in _d2d_copies(l_lands[s], l_sems[0], l_sems[1], 3 * s, x, y, c, 1 - c):
                    cp.wait_recv()

    args, in_specs = [], []
    if landed:
        args += [_in_hbm(a) for a in landed["lands"]] + list(landed["d2d"])
        in_specs += [_HBM_SPEC] * n_l + [_SEM_SPEC] * 2
    if arriving:
        args += [_in_hbm(a) for a in arriving["shards"]] + [_in_hbm(a) for a in arriving["lands"]] + list(arriving["sems"])
        in_specs += [_HBM_SPEC] * (2 * n_a) + [_SEM_SPEC] * 4
    args.append(after)
    in_specs.append(pl.BlockSpec(memory_space=pl.ANY))
    out_shape, out_specs, aliases = [], [], {}
    if landed:
        out_shape += [_hbm_like(a) for a in landed["lands"]]
        for s in range(n_l):
            aliases[s] = s
    if arriving:
        first = (n_l + 2 if landed else 0) + n_a
        for s in range(n_a):
            aliases[first + s] = n_l + s
        out_shape += [_hbm_like(a) for a in arriving["lands"]]
    out_specs += [_HBM_SPEC] * (n_l + n_a)
    if arriving:
        out_shape += [pltpu.SemaphoreType.DMA((3 * n_a,)), pltpu.SemaphoreType.DMA((3 * n_a,))]
        out_specs += [_SEM_SPEC] * 2
    res = pl.pallas_call(
        body, name=name, out_shape=out_shape, in_specs=in_specs, out_specs=out_specs, input_output_aliases=aliases,
        compiler_params=pltpu.CompilerParams(has_side_effects=_EFFECT),
    )(*args)
    done = list(res[:n_l]) if landed else None
    nxt = None
    if arriving:
        nxt = dict(lands=list(res[n_l: n_l + n_a]), d2d=tuple(res[n_l + n_a: n_l + n_a + 2]))
    return done, nxt


def _swap_copies(grad, land, send_sems, recv_sems, base, x, y, c):
    half = grad.shape[1] // 2
    return [pltpu.make_async_remote_copy(
        src_ref=grad.at[:, pl.ds((1 - c) * half, half)], dst_ref=land, send_sem=send_sems.at[base], recv_sem=recv_sems.at[base],
        device_id=(x, y, 1 - c), device_id_type=MESH_ID)]


def _add_halves(g, r, c_idx, tag):
    _, rows, w = g.shape
    half = rows // 2
    tr = _row_tile(half, 1024)
    nb = half // tr

    def body(c_ref, g_ref, r_ref, o_ref):
        o_ref[...] = (g_ref[...].astype(F32) + r_ref[...].astype(F32)).astype(BF16)

    return pl.pallas_call(
        body, name=f"add_halves_{tag}",
        grid_spec=pltpu.PrefetchScalarGridSpec(
            num_scalar_prefetch=1, grid=(N_CHIPS, nb),
            in_specs=[pl.BlockSpec((None, tr, w), lambda k, i, c: (k, c[0] * nb + i, 0)),
                      pl.BlockSpec((None, tr, w), lambda k, i, c: (k, i, 0))],
            out_specs=pl.BlockSpec((None, tr, w), lambda k, i, c: (k, i, 0))),
        out_shape=jax.ShapeDtypeStruct((N_CHIPS, half, w), BF16), compiler_params=_params(2),
    )(c_idx, g, r)


def _partial_copies(part, land, send_sems, recv_sems, base, x, y, c):
    out = []
    for j, (fx, fy) in enumerate(_CHIP_FLIPS):
        px, py = _flip(x, fx), _flip(y, fy)
        out.append(pltpu.make_async_remote_copy(
            src_ref=part.at[2 * px + py], dst_ref=land.at[j], send_sem=send_sems.at[base + j], recv_sem=recv_sems.at[base + j],
            device_id=(px, py, c), device_id_type=MESH_ID))
    return out


def _split_start(name, srcs, land_shapes, copies_of, per):
    n = len(srcs)
    lands = [lax.empty(shape, s.dtype) for shape, s in zip(land_shapes, srcs)]

    def body(*refs):
        src_refs, land_refs = refs[:n], refs[n:2 * n]
        send_sems, recv_sems = refs[2 * n: 2 * n + 2]
        token = refs[-1]
        x, y, c = _place()
        for s in range(n):
            for cp in copies_of(src_refs[s], land_refs[s], send_sems, recv_sems, per * s, x, y, c):
                cp.start()
        token[...] = jnp.zeros(token.shape, F32)

    res = pl.pallas_call(
        body, name=name,
        out_shape=[pltpu.SemaphoreType.DMA((per * n,)), pltpu.SemaphoreType.DMA((per * n,))]
        + [_hbm_like(s) for s in srcs] + [_hbm_like(ld) for ld in lands] + [_TOKEN],
        in_specs=[_HBM_SPEC] * (2 * n),
        out_specs=[_SEM_SPEC] * 2 + [_HBM_SPEC] * (2 * n) + [pl.BlockSpec(memory_space=pltpu.VMEM)],
        input_output_aliases={i: 2 + i for i in range(2 * n)},
        compiler_params=pltpu.CompilerParams(has_side_effects=_EFFECT),
    )(*[_in_hbm(s) for s in srcs], *[_in_hbm(ld) for ld in lands])
    state = dict(sems=tuple(res[:2]), srcs=list(res[2: 2 + n]), lands=list(res[2 + n: 2 + 2 * n]))
    return state, res[-1]


def _split_wait(name, state, after, copies_of, per):
    n = len(state["srcs"])

    def body(*refs):
        src_refs, land_refs = refs[:n], refs[n:2 * n]
        send_sems, recv_sems = refs[2 * n: 2 * n + 2]
        x, y, c = _place()
        for s in range(n):
            for cp in copies_of(src_refs[s], land_refs[s], send_sems, recv_sems, per * s, x, y, c):
                cp.wait_send()
                cp.wait_recv()

    res = pl.pallas_call(
        body, name=name,
        out_shape=[_hbm_like(s) for s in state["srcs"]] + [_hbm_like(ld) for ld in state["lands"]],
        in_specs=[_HBM_SPEC] * (2 * n) + [_SEM_SPEC] * 2 + [pl.BlockSpec(memory_space=pl.ANY)],
        out_specs=[_HBM_SPEC] * (2 * n), input_output_aliases={i: i for i in range(2 * n)},
        compiler_params=pltpu.CompilerParams(has_side_effects=_EFFECT),
    )(*state["srcs"], *state["lands"], *state["sems"], after)
    return list(res[:n]), list(res[n:])


def _sum_partials(p, q, chip_idx, c_idx, tag):
    _, half, w = p.shape
    tr = _row_tile(half, 512)
    nb = half // tr

    def body(k_ref, c_ref, p_ref, q_ref, o_ref):
        acc = p_ref[...].astype(F32)
        for j in range(3):
            acc = acc + q_ref[j].astype(F32)
        o_ref[...] = acc

    return pl.pallas_call(
        body, name=f"sum_partials_{tag}",
        grid_spec=pltpu.PrefetchScalarGridSpec(
            num_scalar_prefetch=2, grid=(nb,),
            in_specs=[pl.BlockSpec((None, tr, w), lambda i, k, c: (k[0], i, 0)),
                      pl.BlockSpec((3, tr, w), lambda i, k, c: (0, i, 0))],
            out_specs=pl.BlockSpec((tr, w), lambda i, k, c: (c[0] * nb + i, 0))),
        out_shape=jax.ShapeDtypeStruct((2 * half, w), F32), compiler_params=_params(1),
    )(chip_idx, c_idx, p, q)


def _join_halves(tots, tag):
    n = len(tots)

    def body(*refs):
        bufs = refs[n:2 * n]
        send_sems, recv_sems = refs[2 * n:]
        x, y, c = _place()
        copies = []
        for s in range(n):
            half = bufs[s].shape[0] // 2
            mine = bufs[s].at[pl.ds(c * half, half)]
            cp = pltpu.make_async_remote_copy(
                src_ref=mine, dst_ref=mine, send_sem=send_sems.at[s], recv_sem=recv_sems.at[s],
                device_id=(x, y, 1 - c), device_id_type=MESH_ID)
            cp.start()
            copies.append(cp)
        for s in range(n):
            half = bufs[s].shape[0] // 2
            theirs = bufs[s].at[pl.ds((1 - c) * half, half)]
            pltpu.make_async_remote_copy(
                src_ref=theirs, dst_ref=theirs, send_sem=send_sems.at[s], recv_sem=recv_sems.at[s],
                device_id=(x, y, c), device_id_type=MESH_ID).wait_recv()
        for cp in copies:
            cp.wait_send()

    return pl.pallas_call(
        body, name=f"join_halves_{tag}",
        out_shape=[jax.ShapeDtypeStruct(tt.shape, tt.dtype) for tt in tots],
        in_specs=[_HBM] * n, out_specs=[_HBM] * n, input_output_aliases={s: s for s in range(n)},
        scratch_shapes=[pltpu.SemaphoreType.DMA((n,)), pltpu.SemaphoreType.DMA((n,))],
    )(*tots)


def _adamw_math(w, g, m, v):
    m = ADAM_B1 * m + (1.0 - ADAM_B1) * g
    v = ADAM_B2 * v + (1.0 - ADAM_B2) * jnp.square(g)
    m_hat = m / (1.0 - ADAM_B1 ** ADAM_STEP)
    v_hat = v / (1.0 - ADAM_B2 ** ADAM_STEP)
    delta = -ADAM_LR * (m_hat / (jnp.sqrt(v_hat) + ADAM_EPS) + ADAM_WD * w)
    return delta, m, v


def _adamw_big(g, col, w2, m2, v2, row0, prev, tag):
    rows = g.shape[0]
    rtot, wd = w2.shape
    tr = _row_tile(rows)
    assert row0 % tr == 0
    n_prev = 0 if prev is None else 4

    def body(*refs):
        g_ref, w_ref, m_ref, v_ref = refs[:4]
        go_ref, d_ref, mo_ref, vo_ref = refs[4 + n_prev:]
        gv = g_ref[...]
        delta, mn, vn = _adamw_math(w_ref[...], gv, m_ref[...], v_ref[...])
        go_ref[...] = gv
        d_ref[...] = delta
        mo_ref[...] = mn
        vo_ref[...] = vn

    at = lambda i: (row0 // tr + i, 0)
    return pl.pallas_call(
        body, name=f"adamw_{tag}", grid=(rows // tr,),
        in_specs=[pl.BlockSpec((tr, wd), lambda i: (i, col))] + [pl.BlockSpec((tr, wd), at)] * 3
        + [pl.BlockSpec(memory_space=pl.ANY)] * n_prev,
        out_specs=[pl.BlockSpec((tr, wd), at)] * 4, out_shape=[jax.ShapeDtypeStruct((rtot, wd), F32)] * 4,
        input_output_aliases={4 + k: k for k in range(n_prev)}, compiler_params=_params(1),
    )(g, w2, m2, v2, *(prev or ()))


def _adamw_small(items):
    n = len(items)

    def body(*refs):
        ins, outs = refs[:4 * n], refs[4 * n:]
        for k in range(n):
            g_ref, w_ref, m_ref, v_ref = ins[4 * k: 4 * k + 4]
            delta, mn, vn = _adamw_math(w_ref[...], g_ref[...], m_ref[...], v_ref[...])
            outs[3 * k][...] = delta
            outs[3 * k + 1][...] = mn
            outs[3 * k + 2][...] = vn

    flat = [a for it in items for a in it]
    out_shape = [jax.ShapeDtypeStruct(it[1].shape, F32) for it in items for _ in range(3)]
    res = pl.pallas_call(body, name="adamw_small", out_shape=out_shape,
                         compiler_params=pltpu.CompilerParams(vmem_limit_bytes=VMEM_LIMIT_BYTES))(*flat)
    return [tuple(res[3 * k: 3 * k + 3]) for k in range(n)]


def _pack(arrays, width):
    rows, layout, r = [], [], 0
    for a in arrays:
        flat = a.reshape(-1).astype(F32)
        nr = -(-flat.shape[0] // (8 * width)) * 8
        flat = jnp.pad(flat, (0, nr * width - flat.shape[0]))
        rows.append(flat.reshape(nr, width))
        layout.append((r, nr, a.shape))
        r += nr
    return jnp.concatenate(rows, axis=0), layout


def _unpack(packed, layout):
    out = []
    for r, nr, shape in layout:
        size = 1
        for s in shape:
            size *= s
        out.append(packed[r:r + nr].reshape(-1)[:size].reshape(shape))
    return out


_WEIGHTS = ['ffn1_norm', 'ffn1_w1', 'ffn1_w3', 'ffn1_w2', 'mix_norm', 'ffn2_norm', 'ffn2_w1', 'ffn2_w3', 'ffn2_w2',
            'ple_norm', 'ple_w_gate', 'ple_w_proj', 'gmlp_w_in', 'gmlp_ln_g', 'gmlp_ln_b', 'gmlp_w_s', 'gmlp_b_s',
            'gmlp_w_out', 'swa_wq', 'swa_bq', 'swa_wk', 'swa_bk', 'swa_wv', 'swa_bv', 'swa_sinks', 'swa_wo', 'swa_bo',
            'final_norm']
_REPLICATED = ['ffn1_norm', 'mix_norm', 'ffn2_norm', 'ple_norm', 'gmlp_ln_g', 'gmlp_ln_b', 'gmlp_w_s', 'gmlp_b_s',
               'swa_sinks', 'final_norm']
_BIASES = ['swa_bq', 'swa_bk', 'swa_bv', 'swa_bo']


def _as2d(a):
    if a.ndim == 1:
        return a.reshape(1, -1)
    return a.reshape(-1, a.shape[-1])


_GROUPS = [("f1l0", ("w1", "w3", "w2")), ("mix0", ("in", "out")), ("f2l0", ("w1", "w3", "w2")), ("ple0", ("gate", "proj")),
           ("f1l1", ("w1", "w3", "w2")), ("mix1", ("q", "kv", "o")), ("f2l1", ("w1", "w3", "w2")), ("ple1", ("gate", "proj"))]
_GATHER = []
for _g, _parts in _GROUPS:
    if _g.startswith("f"):
        _GATHER += [(_g + "a", _g, ("w1", "w3")), (_g + "b", _g, ("w2",))]
    else:
        _GATHER.append((_g, _g, _parts))


def _local_step(x, p, target, small, full_bias, get_group, put_group):
    t, d = x.shape
    n_layers = 2
    blk = small['gmlp_w_s'].shape[2]
    bq, bkv, bo = full_bias
    norm = lambda name, i: small[name][i:i + 1]
    ln_g, ln_b = small['gmlp_ln_g'], small['gmlp_ln_b']
    w_s = small['gmlp_w_s'][0]
    b_st = jnp.pad(small['gmlp_b_s'][0].T, ((0, 0), (0, LANES - small['gmlp_b_s'].shape[1])))
    sinks = small['swa_sinks']

    saved, wts = [], {}

    def ffn_fwd(x, name, norm_w):
        w = get_group(name + "a", x)
        x, s_ffn, w["w2"] = _ffn_fwd(x, norm_w, w["w1"], w["w3"], lambda after: get_group(name + "b", after)["w2"], name)
        wts[name] = w
        return x, s_ffn

    for i in range(n_layers):
        x, s_f1 = ffn_fwd(x, f"f1l{i}", norm('ffn1_norm', i))
        w = wts[f"mix{i}"] = get_group(f"mix{i}", x)
        if i == 0:
            x, s_mix = _gmlp_fwd(x, norm('mix_norm', i), ln_g, ln_b, w_s, b_st, w["in"], w["out"], f"l{i}")
        else:
            x, s_mix = _swa_fwd(x, norm('mix_norm', i), bq, bkv, bo, sinks, w["q"], w["kv"], w["o"], blk, f"l{i}")
        x, s_f2 = ffn_fwd(x, f"f2l{i}", norm('ffn2_norm', i))
        w = wts[f"ple{i}"] = get_group(f"ple{i}", x)
        x, s_ple = _ple_fwd(x, p[i], norm('ple_norm', i), w["gate"], w["proj"], f"l{i}")
        saved.append((s_f1, s_mix, s_f2, s_ple))

    dx, d_final, loss = _loss_head(x, small['final_norm'].reshape(1, d), target)

    gn = {k: [None] * n_layers for k in ('ffn1_norm', 'mix_norm', 'ffn2_norm', 'ple_norm')}
    dep = None
    for i in reversed(range(n_layers)):
        s_f1, s_mix, s_f2, s_ple = saved[i]
        w = wts[f"ple{i}"]
        dx, gn['ple_norm'][i], dwg, dwp = _ple_bwd(dx, s_ple, p[i], norm('ple_norm', i), w["gate"], f"l{i}", dep)
        dep = put_group(f"ple{i}", {"gate": dwg, "proj": dwp})
        w = wts[f"f2l{i}"]
        dx, gn['ffn2_norm'][i], dep = _ffn_bwd(
            dx, s_f2, norm('ffn2_norm', i), w["w1"], w["w3"], w["w2"], f"f2l{i}", dep,
            functools.partial(put_group, f"f2l{i}"))
        w = wts[f"mix{i}"]
        if i == 0:
            dx, gn['mix_norm'][i], dw_in, dw_out, d_ws, d_bs, d_lg, d_lb = _gmlp_bwd(
                dx, s_mix, norm('mix_norm', i), ln_g, ln_b, w_s, b_st, w["in"], w["out"], f"l{i}", dep)
            dep = put_group(f"mix{i}", {"in": dw_in, "out": dw_out})
        else:
            dx, gn['mix_norm'][i], dwq, dwkv, dwo, d_bq, d_bkv, d_bo, d_sink = _swa_bwd(
                dx, s_mix, norm('mix_norm', i), sinks, w["q"], w["kv"], w["o"], blk, f"l{i}", dep)
            dep = put_group(f"mix{i}", {"q": dwq, "kv": dwkv, "o": dwo})
        w = wts[f"f1l{i}"]
        dx, gn['ffn1_norm'][i], dep = _ffn_bwd(
            dx, s_f1, norm('ffn1_norm', i), w["w1"], w["w3"], w["w2"], f"f1l{i}", dep,
            functools.partial(put_group, f"f1l{i}"))

    kw = d_bkv.shape[1] // 2
    g_small = {
        'ffn1_norm': jnp.concatenate(gn['ffn1_norm'], axis=0), 'mix_norm': jnp.concatenate(gn['mix_norm'], axis=0),
        'ffn2_norm': jnp.concatenate(gn['ffn2_norm'], axis=0), 'ple_norm': jnp.concatenate(gn['ple_norm'], axis=0),
        'gmlp_ln_g': d_lg, 'gmlp_ln_b': d_lb, 'gmlp_w_s': d_ws[None], 'gmlp_b_s': d_bs[None], 'swa_sinks': d_sink,
        'final_norm': d_final.reshape(d), 'swa_bq': d_bq, 'swa_bk': d_bkv[:, :kw], 'swa_bv': d_bkv[:, kw:], 'swa_bo': d_bo,
    }
    return loss, dx, g_small


def _group_shards(a):
    bf = lambda w: w.astype(BF16)
    out = {}
    for i in range(2):
        for f in (1, 2):
            out[f"f{f}l{i}"] = {
                "w1": (bf(a[f"ffn{f}_w1"][i]), [(f"ffn{f}_w1", i, 0)]), "w3": (bf(a[f"ffn{f}_w3"][i]), [(f"ffn{f}_w3", i, 0)]),
                "w2": (bf(a[f"ffn{f}_w2"][i]), [(f"ffn{f}_w2", i, 0)])}
        out[f"ple{i}"] = {"gate": (bf(a["ple_w_gate"][i]), [("ple_w_gate", i, 0)]),
                          "proj": (bf(a["ple_w_proj"][i]), [("ple_w_proj", i, 0)])}
    out["mix0"] = {"in": (bf(a["gmlp_w_in"][0]), [("gmlp_w_in", 0, 0)]), "out": (bf(a["gmlp_w_out"][0]), [("gmlp_w_out", 0, 0)])}
    wkv = jnp.concatenate([bf(a["swa_wk"][0]), bf(a["swa_wv"][0])], axis=1)
    out["mix1"] = {"q": (bf(a["swa_wq"][0]), [("swa_wq", 0, 0)]), "kv": (wkv, [("swa_wk", 0, 0), ("swa_wv", 0, 1)]),
                   "o": (bf(a["swa_wo"][0]), [("swa_wo", 0, 0)])}
    return out


def kernel(x, p, ffn1_norm, ffn1_w1, ffn1_w3, ffn1_w2, mix_norm, ffn2_norm, ffn2_w1, ffn2_w3, ffn2_w2, ple_norm, ple_w_gate, ple_w_proj, gmlp_w_in, gmlp_ln_g, gmlp_ln_b, gmlp_w_s, gmlp_b_s, gmlp_w_out, swa_wq, swa_bq, swa_wk, swa_bk, swa_wv, swa_bv, swa_sinks, swa_wo, swa_bo, final_norm, loss_target, m_ffn1_norm, m_ffn1_w1, m_ffn1_w3, m_ffn1_w2, m_mix_norm, m_ffn2_norm, m_ffn2_w1, m_ffn2_w3, m_ffn2_w2, m_ple_norm, m_ple_w_gate, m_ple_w_proj, m_gmlp_w_in, m_gmlp_ln_g, m_gmlp_ln_b, m_gmlp_w_s, m_gmlp_b_s, m_gmlp_w_out, m_swa_wq, m_swa_bq, m_swa_wk, m_swa_bk, m_swa_wv, m_swa_bv, m_swa_sinks, m_swa_wo, m_swa_bo, m_final_norm, v_ffn1_norm, v_ffn1_w1, v_ffn1_w3, v_ffn1_w2, v_mix_norm, v_ffn2_norm, v_ffn2_w1, v_ffn2_w3, v_ffn2_w2, v_ple_norm, v_ple_w_gate, v_ple_w_proj, v_gmlp_w_in, v_gmlp_ln_g, v_gmlp_ln_b, v_gmlp_w_s, v_gmlp_b_s, v_gmlp_w_out, v_swa_wq, v_swa_bq, v_swa_wk, v_swa_bk, v_swa_wv, v_swa_bv, v_swa_sinks, v_swa_wo, v_swa_bo, v_final_norm):
    a = dict(locals())
    xi, yi, ci = _place()
    chip = 2 * xi + yi
    c_idx = ci.reshape(1).astype(jnp.int32)
    chip_idx = chip.reshape(1).astype(jnp.int32)
    d = x.shape[-1]
    d4 = d // N_CHIPS

    shards = _group_shards(a)
    started, token = _gather_start([[shards[g][part][0] for part in parts] for _, g, parts in _GATHER])
    bias_pack, bias_layout = _pack([a[n] for n in _BIASES], d4)
    bias_all = _all_gather_small(bias_pack)

    def full_bias(idx):
        return jnp.concatenate([_unpack(bias_all[2 * k], bias_layout)[idx] for k in range(N_CHIPS)], axis=1)

    bq, bk, bv, bo = (full_bias(i) for i in range(4))
    bkv = jnp.concatenate([bk, bv], axis=1)

    state = {"handed": _gather_step("gather_step_first", None, started[0], token)[1], "next": 1}

    def get_group(name, after):
        k = state["next"]
        assert _GATHER[k - 1][0] == name
        arriving = started[k] if k < len(_GATHER) else None
        done, state["handed"] = _gather_step(f"gather_step_{name}", state["handed"], arriving, after)
        state["next"] = k + 1
        return dict(zip(_GATHER[k - 1][2], done))

    pending, swapping = [], []

    def send_swapped(after):
        name, st = swapping.pop()
        parts = dict(_GROUPS)[name]
        glist, sib = _split_wait(f"swap_wait_{name}", st, after, _swap_copies, 1)
        partial = [_add_halves(g, r, c_idx, f"{name}_{part}") for part, g, r in zip(parts, glist, sib)]
        st, tok = _split_start(f"send_start_{name}", partial, [(3,) + pt.shape[1:] for pt in partial], _partial_copies, 3)
        pending.append((name, st))
        return tok

    def put_group(name, grads):
        glist = [grads[part] for part in dict(_GROUPS)[name]]
        st, tok = _split_start(f"swap_start_{name}", glist, [(N_CHIPS, g.shape[1] // 2, g.shape[2]) for g in glist],
                               _swap_copies, 1)
        if swapping:
            tok = send_swapped(tok)
        swapping.append((name, st))
        return tok

    small = {n: a[n] for n in _REPLICATED}
    loss, grad_x, g_small = _local_step(x[0], p[:, 0], loss_target[0], small, (bq, bkv, bo), get_group, put_group)
    last_sent = send_swapped(grad_x)

    names = _REPLICATED + _BIASES
    packed, layout = _pack([g_small[n] for n in names] + [loss[:, :1]], d4)
    summed = _unpack(_all_sum_small(packed), layout)
    g_sum = dict(zip(names, summed[:-1]))
    loss_out = summed[-1].reshape(())
    for n in _BIASES:
        width = a[n].shape[-1]
        g_sum[n] = lax.dynamic_slice_in_dim(g_sum[n], chip * width, width, axis=1)
    out, chain, last_done = {}, {}, grad_x
    for idx, (name, st) in enumerate(pending):
        mine, recv = _split_wait(f"send_wait_{name}", st, last_done if idx == len(pending) - 1 else last_sent,
                                 _partial_copies, 3)
        parts = dict(_GROUPS)[name]
        tots = _join_halves([_sum_partials(pm, q, chip_idx, c_idx, f"{name}_{part}")
                             for part, pm, q in zip(parts, mine, recv)], name)
        for part, g in zip(parts, tots):
            for n, layer, col in shards[name][part][1]:
                wd = a[n].shape[-1]
                rows = a[n].shape[-2]
                w2, m2, v2 = (a[pre + n].reshape(-1, wd) for pre in ('', 'm_', 'v_'))
                chain[n] = _adamw_big(g, col, w2, m2, v2, layer * rows, chain.get(n), f"{n}_{layer}")
                last_done = chain[n][1]
    for n, res in chain.items():
        out[n] = tuple(o.reshape(a[n].shape) for o in res)
    small_names = _REPLICATED + _BIASES
    items = [(_as2d(g_sum[n]), _as2d(a[n]), _as2d(a['m_' + n]), _as2d(a['v_' + n])) for n in small_names]
    for n, (delta, mn, vn) in zip(small_names, _adamw_small(items)):
        shape = a[n].shape
        out[n] = (g_sum[n].reshape(shape), delta.reshape(shape), mn.reshape(shape), vn.reshape(shape))

    return (loss_out, grad_x[None]) + tuple(out[n][j] for j in range(4) for n in _WEIGHTS)
```

```python
import functools

import jax
import jax.numpy as jnp
from jax import lax
from jax.experimental import pallas as pl
from jax.experimental.pallas import tpu as pltpu

F32 = jnp.float32
BF16 = jnp.bfloat16

RMS_EPS = 1e-6
LN_EPS = 1e-5
FFN_RESIDUAL_WEIGHT = 0.5
HEAD_DIM = 64
ROPE_DIM = 16
ROPE_THETA = 500000.0
ADAM_LR = 0.001
ADAM_B1 = 0.9
ADAM_B2 = 0.999
ADAM_EPS = 1e-08
ADAM_WD = 0.01
ADAM_STEP = 10
N_CHIPS = 4
N_DEV = 8
LANES = 128
VMEM_LIMIT_BYTES = 56 * 1024 * 1024
MESH_ID = pl.DeviceIdType.MESH

_DIMS = {
    "nn": (((1,), (0,)), ((), ())),
    "nt": (((1,), (1,)), ((), ())),
    "tn": (((0,), (0,)), ((), ())),
}


def _params(n_axes):
    return pltpu.CompilerParams(dimension_semantics=("arbitrary",) * n_axes, vmem_limit_bytes=VMEM_LIMIT_BYTES)


def _mm(name, grid, pairs, extras, outs, epilogue, acc_shapes, order="ij"):
    ni, nj, nk = grid
    if order == "ij":
        pgrid = (ni, nj, nk)
        ijk = lambda g0, g1, g2: (g0, g1, g2)
    else:
        pgrid = (nj, ni, nk)
        ijk = lambda g0, g1, g2: (g1, g0, g2)
    in_specs, args = [], []
    for a, ablk, amap, b, bblk, bmap, _, _, _ in pairs:
        in_specs.append(pl.BlockSpec(ablk, lambda g0, g1, g2, m=amap: m(*ijk(g0, g1, g2))))
        in_specs.append(pl.BlockSpec(bblk, lambda g0, g1, g2, m=bmap: m(*ijk(g0, g1, g2))))
        args += [a, b]
    for e, eblk, emap in extras:
        in_specs.append(pl.BlockSpec(eblk, lambda g0, g1, g2, m=emap: m(*ijk(g0, g1, g2)[:2])))
        args.append(e)
    out_specs = [pl.BlockSpec(oblk, lambda g0, g1, g2, m=omap: m(*ijk(g0, g1, g2)[:2])) for _, _, oblk, omap in outs]
    out_shape = [jax.ShapeDtypeStruct(s, d) for s, d, _, _ in outs]
    n_p, n_e, n_o = len(pairs), len(extras), len(outs)

    def body(*refs):
        p_refs = refs[: 2 * n_p]
        e_refs = refs[2 * n_p: 2 * n_p + n_e]
        o_refs = refs[2 * n_p + n_e: 2 * n_p + n_e + n_o]
        accs = refs[2 * n_p + n_e + n_o:]
        k = pl.program_id(2)

        def product(idx):
            a = p_refs[2 * idx][...].astype(BF16)
            b = p_refs[2 * idx + 1][...].astype(BF16)
            if b.ndim == 3:
                b = b.reshape(-1, b.shape[-1])
            return lax.dot_general(a, b, _DIMS[pairs[idx][6]], preferred_element_type=F32)

        def finish(sums):
            vals = epilogue(sums, [e[...] for e in e_refs])
            for o, v in zip(o_refs, vals):
                o[...] = v.astype(o.dtype)

        if nk == 1:
            sums = [None] * len(acc_shapes)
            for idx in range(n_p):
                ai = pairs[idx][7]
                sums[ai] = product(idx) if sums[ai] is None else sums[ai] + product(idx)
            finish(sums)
            return

        written = set()
        for idx in range(n_p):
            ai, k0 = pairs[idx][7], pairs[idx][8]
            if k0 or ai not in written:
                @pl.when(k == 0)
                def _(idx=idx, ai=ai, first=ai not in written):
                    if first:
                        accs[ai][...] = product(idx)
                    else:
                        accs[ai][...] += product(idx)
            if not k0:
                if ai in written:
                    accs[ai][...] += product(idx)
                else:
                    @pl.when(k > 0)
                    def _(idx=idx, ai=ai):
                        accs[ai][...] += product(idx)
            written.add(ai)

        @pl.when(k == nk - 1)
        def _():
            finish([acc[...] for acc in accs])

    res = pl.pallas_call(
        body, name=name, grid=pgrid, in_specs=in_specs, out_specs=out_specs, out_shape=out_shape,
        scratch_shapes=[] if nk == 1 else [pltpu.VMEM(s, F32) for s in acc_shapes], compiler_params=_params(3),
    )(*args)
    return res


def _col_nn(r0, tk, tn, wc):
    assert r0 % tk == 0 and wc % tn == 0
    npc = wc // tn
    return (None, tk, tn), lambda i, j, k: (j // npc, r0 // tk + k, j % npc)


def _row_nn(r0, tk, tn, rc):
    assert r0 % tk == 0 and rc % tk == 0
    kpc = rc // tk
    return (None, tk, tn), lambda i, j, k: (k // kpc, r0 // tk + k % kpc, j)


def _col_nt(r0, tn, tk, wc):
    assert r0 % tn == 0 and wc % tk == 0
    kpc = wc // tk
    return (None, tn, tk), lambda i, j, k: (k // kpc, r0 // tn + j, k % kpc)


def _row_nt(r0, tn, tk, rc):
    assert r0 % tn == 0 and rc % tn == 0
    npc = rc // tn
    return (None, tn, tk), lambda i, j, k: (j // npc, r0 // tn + j % npc, k)


def _out_col(tm, tn, wc):
    npc = wc // tn
    return (None, tm, tn), lambda i, j: (j // npc, i, j % npc)


def _out_row(tm, tn, rc):
    mpc = rc // tm
    return (None, tm, tn), lambda i, j: (i // mpc, i % mpc, j)


def _ik(i, j, k):
    return (i, k)


def _ki(i, j, k):
    return (k, i)


def _kj(i, j, k):
    return (k, j)


def _i0(i, j, k):
    return (i, 0)


def _ij(i, j):
    return (i, j)


def _0j(i, j):
    return (0, j)


def _i0e(i, j):
    return (i, 0)


def _rows(name, n, ins, outs, body, scratch=(), deps=()):
    in_specs, args = [], []
    for item in ins:
        if len(item) == 2:
            in_specs.append(item[1])
        else:
            in_specs.append(pl.BlockSpec(item[1], item[2]))
        args.append(item[0])
    n_in = len(args)
    for dep in deps:
        in_specs.append(pl.BlockSpec(dep.shape, lambda i: (0, 0)))
        args.append(dep)
    n_dep = len(deps)

    def call_body(*refs):
        body(*refs[:n_in], *refs[n_in + n_dep:])

    out_specs = [pl.BlockSpec(blk, m) for _, _, blk, m in outs]
    out_shape = [jax.ShapeDtypeStruct(s, d) for s, d, _, _ in outs]
    return pl.pallas_call(
        call_body, name=name, grid=(n,), in_specs=in_specs, out_specs=out_specs, out_shape=out_shape,
        scratch_shapes=list(scratch), compiler_params=_params(1),
    )(*args)


def _dep_extra(dep):
    return [] if dep is None else [(dep, dep.shape, lambda i, j: (0, 0))]


def _row(arr, tm):
    return (arr, (tm, arr.shape[1]), lambda i: (i, 0))


def _full(arr):
    nd = arr.ndim
    return (arr, arr.shape, lambda i: (0,) * nd)


def _row_out(shape, dtype, tm):
    return (shape, dtype, (tm, shape[1]), lambda i: (i, 0))


def _acc_out(shape):
    nd = len(shape)
    return (shape, F32, shape, lambda i: (0,) * nd)


def _tm(t):
    return 512 if t >= 1024 else t // 2


def _row_tile(rows, cap=256):
    best = max(tr for tr in range(16, min(rows, cap) + 1, 16) if rows % tr == 0)
    return best


def _first(i, refs):
    @pl.when(i == 0)
    def _():
        for r in refs:
            r[...] = jnp.zeros(r.shape, r.dtype)


def _rms_fwd(x, g, tag):
    t, d = x.shape
    tm = _tm(t)

    def body(x_ref, g_ref, h_ref):
        xv = x_ref[...]
        r = lax.rsqrt(jnp.mean(xv * xv, axis=-1, keepdims=True) + RMS_EPS)
        h_ref[...] = (xv * r * g_ref[...]).astype(BF16)

    return _rows(f"rms_fwd_{tag}", t // tm, [_row(x, tm), _full(g)], [_row_out((t, d), BF16, tm)], body)[0]


def _rms_bwd(dh, x, g, dx_out, tag):
    t, d = x.shape
    tm = _tm(t)

    def body(dh_ref, x_ref, g_ref, dxo_ref, dx_ref, dg_ref):
        i = pl.program_id(0)
        _first(i, [dg_ref])
        xv = x_ref[...]
        r = lax.rsqrt(jnp.mean(xv * xv, axis=-1, keepdims=True) + RMS_EPS)
        xh = xv * r
        dhv = dh_ref[...]
        dxh = dhv * g_ref[...]
        dx_ref[...] = dxo_ref[...] + r * (dxh - xh * jnp.mean(dxh * xh, axis=-1, keepdims=True))
        dg_ref[...] += jnp.sum(dhv * xh, axis=0, keepdims=True)

    return _rows(f"rms_bwd_{tag}", t // tm, [_row(dh, tm), _row(x, tm), _full(g), _row(dx_out, tm)],
                 [_row_out((t, d), F32, tm), _acc_out((1, d))], body)


def _loss_head(x, g, target):
    t, d = x.shape
    tm = _tm(t)

    def body(x_ref, g_ref, t_ref, dx_ref, dg_ref, loss_ref):
        i = pl.program_id(0)
        _first(i, [dg_ref, loss_ref])
        xv = x_ref[...]
        gv = g_ref[...]
        r = lax.rsqrt(jnp.mean(xv * xv, axis=-1, keepdims=True) + RMS_EPS)
        xh = xv * r
        err = xh * gv - t_ref[...]
        loss_ref[...] += jnp.full((1, LANES), 0.5, F32) * jnp.sum(jnp.mean(err * err, axis=-1, keepdims=True))
        dy = err * (1.0 / d)
        dxh = dy * gv
        dx_ref[...] = r * (dxh - xh * jnp.mean(dxh * xh, axis=-1, keepdims=True))
        dg_ref[...] += jnp.sum(dy * xh, axis=0, keepdims=True)

    return _rows("loss_head", t // tm, [_row(x, tm), _full(g), _row(target, tm)],
                 [_row_out((t, d), F32, tm), _acc_out((1, d)), _acc_out((1, LANES))], body)


def _colsum(arr, tag):
    t, w = arr.shape
    tm = _tm(t)

    def body(a_ref, o_ref):
        _first(pl.program_id(0), [o_ref])
        o_ref[...] += jnp.sum(a_ref[...].astype(F32), axis=0, keepdims=True)

    return _rows(f"colsum_{tag}", t // tm, [_row(arr, tm)], [_acc_out((1, w))], body)[0]


def _ffn_fwd(x, gn, w1, w3, get_w2, tag):
    t, d = x.shape
    fc = w1.shape[2]
    f = N_CHIPS * fc
    tm = _tm(t)
    h = _rms_fwd(x, gn, f"ffn_{tag}")
    w1blk, w1map = _col_nn(0, d, fc, fc)
    w3blk, w3map = _col_nn(0, d, fc, fc)
    tma = min(tm, 256)

    def ep_ab(accs, ex):
        a, b = accs
        sig = jax.nn.sigmoid(a)
        silu = a * sig
        return b * sig * (1.0 + a * (1.0 - sig)), silu, silu * b

    a, b, act = _mm(
        f"ffn_ab_{tag}", (t // tma, N_CHIPS, 1),
        [(h, (tma, d), _ik, w1, w1blk, w1map, "nn", 0, False), (h, (tma, d), _ik, w3, w3blk, w3map, "nn", 1, False)],
        [], [((t, f), BF16, (tma, fc), _ij)] * 3, ep_ab, [(tma, fc)] * 2, order="ji")
    w2 = get_w2(act)
    tn = min(d, 1024)
    tmo = min(t, 2 * tm)
    w2blk, w2map = _row_nn(0, fc, tn, fc)
    x_new = _mm(
        f"ffn_out_{tag}", (t // tmo, d // tn, N_CHIPS),
        [(act, (tmo, fc), _ik, w2, w2blk, w2map, "nn", 0, False)],
        [(x, (tmo, tn), _ij)], [((t, d), F32, (tmo, tn), _ij)],
        lambda accs, ex: [ex[0] + FFN_RESIDUAL_WEIGHT * accs[0]], [(tmo, tn)])[0]
    return x_new, (x, h, a, b, act), w2


def _ffn_bwd(dx_out, saved, gn, w1, w3, w2, tag, dep, put):
    x, h, a, b, act = saved
    t, d = x.shape
    fc = w1.shape[2]
    f = N_CHIPS * fc
    tm = _tm(t)
    tma = min(tm, 256)
    w2blk, w2map = _row_nt(0, fc, d, fc)

    def ep_dg(accs, ex):
        dg = FFN_RESIDUAL_WEIGHT * accs[0]
        return dg * ex[0].astype(F32), dg * ex[1].astype(F32)

    da, db = _mm(
        f"ffn_dg_{tag}", (t // tma, N_CHIPS, 1),
        [(dx_out, (tma, d), _ik, w2, w2blk, w2map, "nt", 0, False)],
        [(a, (tma, fc), _ij), (b, (tma, fc), _ij)] + _dep_extra(dep), [((t, f), BF16, (tma, fc), _ij)] * 2, ep_dg, [(tma, fc)], order="ji")

    tmd = min(d, 512)
    tk = min(t, 2048)
    oblk, omap = _out_col(tmd, fc, fc)
    dw1, dw3 = _mm(
        f"ffn_dw13_{tag}", (d // tmd, N_CHIPS, t // tk),
        [(h, (tk, tmd), _ki, da, (tk, fc), _kj, "tn", 0, False), (h, (tk, tmd), _ki, db, (tk, fc), _kj, "tn", 1, False)],
        [], [((N_CHIPS, d, fc), BF16, oblk, omap)] * 2, lambda accs, ex: accs, [(tmd, fc)] * 2)
    tn = min(d, 1024)
    tk2 = min(t, 2048)
    dw2 = _mm(
        f"ffn_dw2_{tag}", (N_CHIPS, d // tn, t // tk2),
        [(act, (tk2, fc), _ki, dx_out, (tk2, tn), _kj, "tn", 0, False)],
        [], [((N_CHIPS, fc, d), BF16, (None, fc, tn), lambda i, j: (i, 0, j))],
        lambda accs, ex: [FFN_RESIDUAL_WEIGHT * accs[0]], [(fc, tn)])[0]
    token = put({"w1": dw1, "w3": dw3, "w2": dw2})
    w1blk, w1map = _col_nt(0, tn, fc, fc)
    w3blk, w3map = _col_nt(0, tn, fc, fc)
    tmh = min(t, 2 * tm)
    dh = _mm(
        f"ffn_dh_{tag}", (t // tmh, d // tn, N_CHIPS),
        [(da, (tmh, fc), _ik, w1, w1blk, w1map, "nt", 0, False), (db, (tmh, fc), _ik, w3, w3blk, w3map, "nt", 0, False)],
        _dep_extra(token), [((t, d), F32, (tmh, tn), _ij)], lambda accs, ex: accs, [(tmh, tn)])[0]
    dx, dgn = _rms_bwd(dh, x, gn, dx_out, f"ffn_{tag}")
    return dx, dgn, token


def _ple_fwd(x, p_i, gn, wg, wp, tag):
    t, d = x.shape
    pdim = p_i.shape[1]
    d4 = d // N_CHIPS
    tm = _tm(t)
    hp = _rms_fwd(x, gn, f"ple_{tag}")
    tm = min(t, 2 * tm)

    def ep(accs, ex):
        s = jax.nn.sigmoid(accs[0])
        return ex[0] + s * accs[1], s, accs[1]

    x_new, s, pp = _mm(
        f"ple_fwd_{tag}", (t // tm, N_CHIPS, 1),
        [(hp, (tm, d), _i0, wg, (N_CHIPS, d4, d4), lambda i, j, k: (0, 0, j), "nn", 0, False),
         (p_i, (tm, pdim), _i0, wp, (None, pdim, d4), lambda i, j, k: (j, 0, 0), "nn", 1, False)],
        [(x, (tm, d4), _ij)], [((t, d), F32, (tm, d4), _ij), ((t, d), BF16, (tm, d4), _ij), ((t, d), BF16, (tm, d4), _ij)],
        ep, [(tm, d4)] * 2)
    return x_new, (x, hp, s, pp)


def _ple_bwd(dx_out, saved, p_i, gn, wg, tag, dep=None):
    x, hp, s, pp = saved
    t, d = x.shape
    pdim = p_i.shape[1]
    d4 = d // N_CHIPS
    tm = _tm(t)

    def body(dx_ref, s_ref, pp_ref, dpp_ref, dgp_ref):
        dxv = dx_ref[...]
        sv = s_ref[...].astype(F32)
        ppv = pp_ref[...].astype(F32)
        dpp_ref[...] = (dxv * sv).astype(BF16)
        dgp_ref[...] = (dxv * ppv * sv * (1.0 - sv)).astype(BF16)

    dpp, dgp = _rows(f"ple_ew_{tag}", t // tm, [_row(dx_out, tm), _row(s, tm), _row(pp, tm)],
                     [_row_out((t, d), BF16, tm)] * 2, body, deps=[] if dep is None else [dep])
    tk = min(t, 2048)
    dwp = _mm(
        f"ple_dwp_{tag}", (1, N_CHIPS, t // tk),
        [(p_i, (tk, pdim), lambda i, j, k: (k, 0), dpp, (tk, d4), _kj, "tn", 0, False)],
        [], [((N_CHIPS, pdim, d4), BF16, (None, pdim, d4), lambda i, j: (j, 0, 0))], lambda accs, ex: accs, [(pdim, d4)])[0]
    tn = min(d, 1024)
    dwg = _mm(
        f"ple_dwg_{tag}", (N_CHIPS, d // tn, t // tk),
        [(hp, (tk, d4), _ki, dgp, (tk, tn), _kj, "tn", 0, False)],
        [], [((N_CHIPS, d4, d), BF16, (None, d4, tn), lambda i, j: (i, 0, j))], lambda accs, ex: accs, [(d4, tn)])[0]
    dhp = _mm(
        f"ple_dh_{tag}", (t // tm, 1, 1),
        [(dgp, (tm, d), _ik, wg, (N_CHIPS, d4, d), lambda i, j, k: (0, 0, 0), "nt", 0, False)],
        [], [((t, d), F32, (tm, d), _ij)], lambda accs, ex: accs, [(tm, d)])[0]
    dx, dgn = _rms_bwd(dhp, x, gn, dx_out, f"ple_{tag}")
    return dx, dgn, dwg, dwp


_SQRT_HALF = 0.7071067811865476
_INV_SQRT_2PI = 0.3989422804014327


def _gelu(z):
    return z * (lax.erf(z * _SQRT_HALF) + 1.0) * 0.5


def _gelu_grad(z):
    return 0.5 * (1.0 + lax.erf(z * _SQRT_HALF)) + z * (_INV_SQRT_2PI * jnp.exp(-0.5 * z * z))


def _causal_bf16(w):
    c = w.shape[0]
    keep = lax.broadcasted_iota(jnp.int32, (c, c), 0) >= lax.broadcasted_iota(jnp.int32, (c, c), 1)
    return jnp.where(keep, w, 0.0).astype(BF16), keep


def _gmlp_gate_fwd(z_pre, ln_g, ln_b, w_s, b_st, tag):
    t, w2 = z_pre.shape
    gw = w2 // 2
    n_g, chunk, _ = w_s.shape
    gd = gw // n_g

    def body(z_ref, g_ref, b_ref, ws_ref, bs_ref, o_ref):
        z = z_ref[...]
        u = _gelu(z[:, :gw])
        zv = _gelu(z[:, gw:])
        mu = jnp.mean(zv, axis=-1, keepdims=True)
        cen = zv - mu
        rstd = lax.rsqrt(jnp.mean(cen * cen, axis=-1, keepdims=True) + LN_EPS)
        vln = (cen * rstd * g_ref[...] + b_ref[...]).astype(BF16)
        bst = bs_ref[...]
        for g in range(n_g):
            wm, _ = _causal_bf16(ws_ref[g])
            sl = slice(g * gd, (g + 1) * gd)
            s = jnp.dot(wm, vln[:, sl], preferred_element_type=F32) + bst[:, g:g + 1]
            o_ref[:, sl] = (u[:, sl] * s).astype(BF16)

    return _rows(f"gmlp_gate_{tag}", t // chunk, [_row(z_pre, chunk), _full(ln_g), _full(ln_b), _full(w_s), _full(b_st)],
                 [_row_out((t, gw), BF16, chunk)], body)[0]


def _gmlp_gate_bwd(z_pre, dgated, ln_g, ln_b, w_s, b_st, tag):
    t, w2 = z_pre.shape
    gw = w2 // 2
    n_g, chunk, _ = w_s.shape
    gd = gw // n_g

    def body(z_ref, dgt_ref, g_ref, b_ref, ws_ref, bs_ref, dz_ref, dws_ref, dbs_ref, dlg_ref, dlb_ref, dv_scr):
        _first(pl.program_id(0), [dws_ref, dbs_ref, dlg_ref, dlb_ref])
        z = z_ref[...]
        zu, zvp = z[:, :gw], z[:, gw:]
        u = _gelu(zu)
        zv = _gelu(zvp)
        mu = jnp.mean(zv, axis=-1, keepdims=True)
        cen = zv - mu
        rstd = lax.rsqrt(jnp.mean(cen * cen, axis=-1, keepdims=True) + LN_EPS)
        vn = cen * rstd
        lg = g_ref[...]
        vln = (vn * lg + b_ref[...]).astype(BF16)
        bst = bs_ref[...]
        lane = lax.broadcasted_iota(jnp.int32, (chunk, LANES), 1)
        dbs = jnp.zeros((chunk, LANES), F32)
        for g in range(n_g):
            wm, keep = _causal_bf16(ws_ref[g])
            sl = slice(g * gd, (g + 1) * gd)
            vg = vln[:, sl]
            s = jnp.dot(wm, vg, preferred_element_type=F32) + bst[:, g:g + 1]
            dgt = dgt_ref[:, sl].astype(F32)
            ds = dgt * u[:, sl]
            ds16 = ds.astype(BF16)
            dz_ref[:, sl] = (dgt * s * _gelu_grad(zu[:, sl])).astype(dz_ref.dtype)
            dv_scr[:, sl] = lax.dot_general(wm, ds16, _DIMS["tn"], preferred_element_type=F32)
            dw = lax.dot_general(ds16, vg, _DIMS["nt"], preferred_element_type=F32)
            dws_ref[g] += jnp.where(keep, dw, 0.0)
            dbs = dbs + jnp.where(lane == g, jnp.sum(ds, axis=-1, keepdims=True), 0.0)
        dbs_ref[...] += dbs
        dvln = dv_scr[...]
        dlg_ref[...] += jnp.sum(dvln * vn, axis=0, keepdims=True)
        dlb_ref[...] += jnp.sum(dvln, axis=0, keepdims=True)
        dvn = dvln * lg
        dzv = rstd * (dvn - jnp.mean(dvn, axis=-1, keepdims=True) - vn * jnp.mean(dvn * vn, axis=-1, keepdims=True))
        dz_ref[:, gw:] = (dzv * _gelu_grad(zvp)).astype(dz_ref.dtype)

    return _rows(
        f"gmlp_gate_bwd_{tag}", t // chunk,
        [_row(z_pre, chunk), _row(dgated, chunk), _full(ln_g), _full(ln_b), _full(w_s), _full(b_st)],
        [_row_out((t, w2), BF16, chunk), _acc_out((n_g, chunk, chunk)), _acc_out((chunk, LANES)), _acc_out((1, gw)),
         _acc_out((1, gw))], body, scratch=[pltpu.VMEM((chunk, gw), F32)])


def _gmlp_fwd(x, gn, ln_g, ln_b, w_s, b_st, w_in, w_out, tag):
    t, d = x.shape
    gw = ln_g.shape[1]
    tm = _tm(t)
    d4 = d // N_CHIPS
    h = _rms_fwd(x, gn, f"mix_{tag}")
    tn = min(d, 1024)
    iblk, imap = _col_nn(0, d, tn, d)
    z_pre = _mm(
        f"gmlp_in_{tag}", (t // tm, N_CHIPS * d // tn, 1),
        [(h, (tm, d), _ik, w_in, iblk, imap, "nn", 0, False)],
        [], [((t, 2 * gw), F32, (tm, tn), _ij)], lambda accs, ex: accs, [(tm, tn)], order="ji")[0]
    gated = _gmlp_gate_fwd(z_pre, ln_g, ln_b, w_s, b_st, tag)
    rc = gw // N_CHIPS
    tk = min(rc, 1024)
    tmo = min(t, 2 * tm)
    oblk, omap = _row_nn(0, tk, tn, rc)
    x_new = _mm(
        f"gmlp_out_{tag}", (t // tmo, d // tn, gw // tk),
        [(gated, (tmo, tk), _ik, w_out, oblk, omap, "nn", 0, False)],
        [(x, (tmo, tn), _ij)], [((t, d), F32, (tmo, tn), _ij)], lambda accs, ex: [ex[0] + accs[0]], [(tmo, tn)])[0]
    return x_new, (x, h, z_pre, gated)


def _gmlp_bwd(dx_out, saved, gn, ln_g, ln_b, w_s, b_st, w_in, w_out, tag, dep=None):
    x, h, z_pre, gated = saved
    t, d = x.shape
    gw = ln_g.shape[1]
    tm = _tm(t)
    d4 = d // N_CHIPS
    rc = gw // N_CHIPS
    tnr = min(rc, 1024)
    oblk, omap = _row_nt(0, tnr, d, rc)
    dgated = _mm(
        f"gmlp_dgated_{tag}", (t // tm, gw // tnr, 1),
        [(dx_out, (tm, d), _ik, w_out, oblk, omap, "nt", 0, False)],
        _dep_extra(dep), [((t, gw), BF16, (tm, tnr), _ij)], lambda accs, ex: accs, [(tm, tnr)], order="ji")[0]
    tk = min(t, 2048)
    tn = min(d, 1024)
    rblk, rmap = _out_row(tnr, tn, rc)
    dw_out = _mm(
        f"gmlp_dwout_{tag}", (gw // tnr, d // tn, t // tk),
        [(gated, (tk, tnr), _ki, dx_out, (tk, tn), _kj, "tn", 0, False)],
        [], [((N_CHIPS, rc, d), BF16, rblk, rmap)], lambda accs, ex: accs, [(tnr, tn)])[0]
    dz, dws, dbs, dlg, dlb = _gmlp_gate_bwd(z_pre, dgated, ln_g, ln_b, w_s, b_st, tag)
    tmd = min(d, 512)
    cblk, cmap = _out_col(tmd, tn, d)
    dw_in = _mm(
        f"gmlp_dwin_{tag}", (d // tmd, N_CHIPS * d // tn, t // tk),
        [(h, (tk, tmd), _ki, dz, (tk, tn), _kj, "tn", 0, False)],
        [], [((N_CHIPS, d, d), BF16, cblk, cmap)], lambda accs, ex: accs, [(tmd, tn)])[0]
    iblk, imap = _col_nt(0, tn, d, d)
    dh = _mm(
        f"gmlp_dh_{tag}", (t // tm, d // tn, N_CHIPS),
        [(dz, (tm, d), _ik, w_in, iblk, imap, "nt", 0, False)],
        [], [((t, d), F32, (tm, tn), _ij)], lambda accs, ex: accs, [(tm, tn)])[0]
    dx, dgn = _rms_bwd(dh, x, gn, dx_out, f"mix_{tag}")
    n_g = w_s.shape[0]
    return dx, dgn, dw_in, dw_out, dws, dbs[:, :n_g].T, dlg, dlb


def _rope_tables(t, width, n_rot_heads):
    half = ROPE_DIM // 2
    inv_freq = ROPE_THETA ** (-jnp.arange(0, ROPE_DIM, 2, dtype=F32) / ROPE_DIM)
    ang = jnp.arange(t, dtype=F32)[:, None] * inv_freq[None, :]
    cos, sin = jnp.cos(ang), jnp.sin(ang)
    ones = jnp.ones((t, HEAD_DIM - ROPE_DIM), F32)
    zeros = jnp.zeros((t, HEAD_DIM - ROPE_DIM), F32)
    zh = jnp.zeros((t, half), F32)
    c = jnp.concatenate([cos, cos, ones], axis=1)
    s_next = jnp.concatenate([-sin, zh, zeros], axis=1)
    s_prev = jnp.concatenate([zh, sin, zeros], axis=1)
    rest = width - n_rot_heads * HEAD_DIM

    def widen(tab, fill):
        parts = [jnp.tile(tab, (1, n_rot_heads))]
        if rest:
            parts.append(jnp.full((t, rest), fill, F32))
        return jnp.concatenate(parts, axis=1)

    return widen(c, 1.0), widen(s_next, 0.0), widen(s_prev, 0.0)


def _rope(v, c, s_next, s_prev, sign):
    w = v.shape[1]
    half = ROPE_DIM // 2
    reps = w // c.shape[1]
    if reps > 1:
        c, s_next, s_prev = (jnp.tile(tab, (1, reps)) for tab in (c, s_next, s_prev))
    return v * c + sign * (pltpu.roll(v, w - half, 1) * s_next + pltpu.roll(v, half, 1) * s_prev)


def _scores_mask(n, blk, heads):
    row = lax.broadcasted_iota(jnp.int32, (heads * blk, 2 * blk), 0)
    q_pos = lax.rem(row, blk) + blk
    k_pos = lax.broadcasted_iota(jnp.int32, (heads * blk, 2 * blk), 1)
    diff = q_pos - k_pos
    band = (diff >= 0) & (diff < blk)
    return band & ((k_pos >= blk) | (n > 0))


def _heads_to_rows(v, first, heads):
    return jnp.concatenate([v[:, (first + g) * HEAD_DIM:(first + g + 1) * HEAD_DIM] for g in range(heads)], axis=0)


def _sink_column(sink_ref, first, heads, blk):
    return jnp.concatenate([jnp.full((blk, 1), sink_ref[0, first + g], F32) for g in range(heads)], axis=0)


def _attn_fwd(q, kv, sinks, blk):
    t, qw = q.shape
    kw = kv.shape[1] // 2
    n_kv = kw // HEAD_DIM
    q_per_kv = qw // kw
    scale = HEAD_DIM ** -0.5

    def body(sink_ref, q_ref, kvo_ref, kvp_ref, o_ref):
        n = pl.program_id(0)
        valid = _scores_mask(n, blk, 1)
        qv = q_ref[...]
        kvo = kvo_ref[...]
        kvp = kvp_ref[...]
        for kh in range(n_kv):
            ks = slice(kh * HEAD_DIM, (kh + 1) * HEAD_DIM)
            vs = slice(kw + kh * HEAD_DIM, kw + (kh + 1) * HEAD_DIM)
            kb = jnp.concatenate([kvp[:, ks], kvo[:, ks]], axis=0)
            vb = jnp.concatenate([kvp[:, vs], kvo[:, vs]], axis=0)
            for g in range(q_per_kv):
                hd = kh * q_per_kv + g
                hs = slice(hd * HEAD_DIM, (hd + 1) * HEAD_DIM)
                sink = sink_ref[0, hd]
                s = lax.dot_general(qv[:, hs], kb, _DIMS["nt"], preferred_element_type=F32) * scale
                s = jnp.where(valid, s, -1e30)
                m = jnp.maximum(jnp.max(s, axis=-1, keepdims=True), sink)
                e = jnp.where(valid, jnp.exp(s - m), 0.0)
                denom = jnp.sum(e, axis=-1, keepdims=True) + jnp.exp(sink - m)
                p = (e / denom).astype(BF16)
                o_ref[:, hs] = jnp.dot(p, vb, preferred_element_type=F32).astype(BF16)

    return _rows(
        "swa_attn", t // blk,
        [(sinks, pl.BlockSpec(memory_space=pltpu.SMEM)), _row(q, blk), _row(kv, blk),
         (kv, (blk, kv.shape[1]), lambda i: (jnp.maximum(i - 1, 0), 0))],
        [_row_out((t, qw), BF16, blk)], body)[0]


def _attn_bwd(q, kv, do, sinks, blk):
    t, qw = q.shape
    kw = kv.shape[1] // 2
    n_kv = kw // HEAD_DIM
    q_per_kv = qw // kw
    scale = HEAD_DIM ** -0.5

    def body(sink_ref, q_ref, kvo_ref, kvp_ref, do_ref, dq_ref, dkvo_ref, dkvp_ref, dsink_ref):
        n = pl.program_id(0)
        _first(n, [dsink_ref])
        valid = _scores_mask(n, blk, q_per_kv)
        lane = lax.broadcasted_iota(jnp.int32, (1, LANES), 1)
        qv = q_ref[...]
        kvo = kvo_ref[...]
        kvp = kvp_ref[...]
        dov = do_ref[...]
        dsink = jnp.zeros((1, LANES), F32)
        for kh in range(n_kv):
            ks = slice(kh * HEAD_DIM, (kh + 1) * HEAD_DIM)
            vs = slice(kw + kh * HEAD_DIM, kw + (kh + 1) * HEAD_DIM)
            kb = jnp.concatenate([kvp[:, ks], kvo[:, ks]], axis=0)
            vb = jnp.concatenate([kvp[:, vs], kvo[:, vs]], axis=0)
            qg = _heads_to_rows(qv, kh * q_per_kv, q_per_kv)
            dog = _heads_to_rows(dov, kh * q_per_kv, q_per_kv)
            sink = _sink_column(sink_ref, kh * q_per_kv, q_per_kv, blk)
            s = lax.dot_general(qg, kb, _DIMS["nt"], preferred_element_type=F32) * scale
            s = jnp.where(valid, s, -1e30)
            m = jnp.maximum(jnp.max(s, axis=-1, keepdims=True), sink)
            e = jnp.where(valid, jnp.exp(s - m), 0.0)
            e_sink = jnp.exp(sink - m)
            inv = 1.0 / (jnp.sum(e, axis=-1, keepdims=True) + e_sink)
            p = e * inv
            p16 = p.astype(BF16)
            dp = lax.dot_general(dog, vb, _DIMS["nt"], preferred_element_type=F32)
            dot_pd = jnp.sum(p * dp, axis=-1, keepdims=True)
            ds16 = (p * (dp - dot_pd)).astype(BF16)
            d_sink_rows = e_sink * inv * dot_pd
            dqg = (jnp.dot(ds16, kb, preferred_element_type=F32) * scale).astype(BF16)
            for g in range(q_per_kv):
                hd = kh * q_per_kv + g
                rows = slice(g * blk, (g + 1) * blk)
                dsink = dsink + jnp.where(lane == hd, -jnp.sum(d_sink_rows[rows], axis=0, keepdims=True), 0.0)
                dq_ref[:, hd * HEAD_DIM:(hd + 1) * HEAD_DIM] = dqg[rows]
            dkb = lax.dot_general(ds16, qg, _DIMS["tn"], preferred_element_type=F32) * scale
            dvb = lax.dot_general(p16, dog, _DIMS["tn"], preferred_element_type=F32)
            dkvp_ref[:, ks] = dkb[:blk]
            dkvo_ref[:, ks] = dkb[blk:]
            dkvp_ref[:, vs] = dvb[:blk]
            dkvo_ref[:, vs] = dvb[blk:]
        dsink_ref[...] += dsink

    return _rows(
        "swa_attn_bwd", t // blk,
        [(sinks, pl.BlockSpec(memory_space=pltpu.SMEM)), _row(q, blk), _row(kv, blk),
         (kv, (blk, kv.shape[1]), lambda i: (jnp.maximum(i - 1, 0), 0)), _row(do, blk)],
        [_row_out((t, qw), BF16, blk), _row_out((t, 2 * kw), F32, blk), _row_out((t, 2 * kw), F32, blk),
         _acc_out((1, LANES))], body)


def _rope_bwd(dq_r, dkv_own, dkv_prev, tabs_q, tabs_kv, blk):
    t, qw = dq_r.shape
    kvw = dkv_own.shape[1]
    nb = t // blk

    def body(dq_ref, own_ref, nxt_ref, cq, snq, spq, ck, snk, spk, dqo_ref, dkvo_ref, dbq_ref, dbkv_ref):
        i = pl.program_id(0)
        _first(i, [dbq_ref, dbkv_ref])
        dq = _rope(dq_ref[...].astype(F32), cq[...], snq[...], spq[...], -1.0)
        dkv = own_ref[...] + jnp.where(i < nb - 1, nxt_ref[...], 0.0)
        dkv = _rope(dkv, ck[...], snk[...], spk[...], -1.0)
        dqo_ref[...] = dq.astype(BF16)
        dkvo_ref[...] = dkv.astype(BF16)
        dbq_ref[...] += jnp.sum(dq, axis=0, keepdims=True)
        dbkv_ref[...] += jnp.sum(dkv, axis=0, keepdims=True)

    return _rows(
        "swa_rope_bwd", nb,
        [_row(dq_r, blk), _row(dkv_own, blk), (dkv_prev, (blk, kvw), lambda i: (jnp.minimum(i + 1, nb - 1), 0))]
        + [_row(tab, blk) for tab in tabs_q] + [_row(tab, blk) for tab in tabs_kv],
        [_row_out((t, qw), BF16, blk), _row_out((t, kvw), BF16, blk), _acc_out((1, qw)), _acc_out((1, kvw))], body)


def _swa_fwd(x, gn, bq, bkv, bo, sinks, wq, wkv, wo, blk, tag):
    t, d = x.shape
    qw = bq.shape[1]
    kvw = bkv.shape[1]
    d4 = d // N_CHIPS
    tm = _tm(t)
    h = _rms_fwd(x, gn, f"mix_{tag}")
    tabs_q = _rope_tables(t, LANES, LANES // HEAD_DIM)
    tabs_kv = _rope_tables(t, kvw, kvw // 2 // HEAD_DIM)
    tn = min(qw, 1024)
    whole = lambda i, j, k: (0, 0, j)

    def ep_rope(accs, ex):
        return [_rope(accs[0] + ex[0], ex[1], ex[2], ex[3], 1.0)]

    q = _mm(
        f"swa_q_{tag}", (t // tm, qw // tn, 1),
        [(h, (tm, d), _ik, wq, (N_CHIPS, d4, tn), whole, "nn", 0, False)],
        [(bq, (1, tn), _0j)] + [(tab, (tm, LANES), _i0e) for tab in tabs_q],
        [((t, qw), BF16, (tm, tn), _ij)], ep_rope, [(tm, tn)], order="ji")[0]
    kv = _mm(
        f"swa_kv_{tag}", (t // tm, 1, 1),
        [(h, (tm, d), _ik, wkv, (N_CHIPS, d4, kvw), whole, "nn", 0, False)],
        [(bkv, (1, kvw), _0j)] + [(tab, (tm, kvw), _i0e) for tab in tabs_kv],
        [((t, kvw), BF16, (tm, kvw), _ij)], ep_rope, [(tm, kvw)])[0]
    o = _attn_fwd(q, kv, sinks, blk)
    tno = min(d, 1024)
    x_new = _mm(
        f"swa_out_{tag}", (t // tm, d // tno, 1),
        [(o, (tm, qw), _ik, wo, (N_CHIPS, qw // N_CHIPS, tno), whole, "nn", 0, False)],
        [(x, (tm, tno), _ij), (bo, (1, tno), _0j)], [((t, d), F32, (tm, tno), _ij)],
        lambda accs, ex: [ex[0] + accs[0] + ex[1]], [(tm, tno)], order="ji")[0]
    return x_new, (x, h, q, kv, o, tabs_q, tabs_kv)


def _swa_bwd(dx_out, saved, gn, sinks, wq, wkv, wo, blk, tag, dep=None):
    x, h, q, kv, o, tabs_q, tabs_kv = saved
    t, d = x.shape
    qw = q.shape[1]
    kvw = kv.shape[1]
    d4 = d // N_CHIPS
    qw4 = qw // N_CHIPS
    tm = _tm(t)
    whole = lambda i, j, k: (0, 0, 0)
    do = _mm(
        f"swa_do_{tag}", (t // tm, 1, 1),
        [(dx_out, (tm, d), _ik, wo, (N_CHIPS, qw4, d), whole, "nt", 0, False)],
        _dep_extra(dep), [((t, qw), BF16, (tm, qw), _ij)], lambda accs, ex: accs, [(tm, qw)])[0]
    tk = min(t, 2048)
    tn = min(d, 1024)
    dwo = _mm(
        f"swa_dwo_{tag}", (N_CHIPS, d // tn, t // tk),
        [(o, (tk, qw4), _ki, dx_out, (tk, tn), _kj, "tn", 0, False)],
        [], [((N_CHIPS, qw4, d), BF16, (None, qw4, tn), lambda i, j: (i, 0, j))], lambda accs, ex: accs, [(qw4, tn)])[0]
    dbo = _colsum(dx_out, f"bo_{tag}")
    dq_r, dkv_own, dkv_prev, dsink = _attn_bwd(q, kv, do, sinks, blk)
    dq, dkv, dbq, dbkv = _rope_bwd(dq_r, dkv_own, dkv_prev, tabs_q, tabs_kv, blk)
    tnq = min(qw, 1024)
    dwq = _mm(
        f"swa_dwq_{tag}", (N_CHIPS, qw // tnq, t // tk),
        [(h, (tk, d4), _ki, dq, (tk, tnq), _kj, "tn", 0, False)],
        [], [((N_CHIPS, d4, qw), BF16, (None, d4, tnq), lambda i, j: (i, 0, j))], lambda accs, ex: accs, [(d4, tnq)])[0]
    dwkv = _mm(
        f"swa_dwkv_{tag}", (N_CHIPS, 1, t // tk),
        [(h, (tk, d4), _ki, dkv, (tk, kvw), _kj, "tn", 0, False)],
        [], [((N_CHIPS, d4, kvw), BF16, (None, d4, kvw), lambda i, j: (i, 0, j))], lambda accs, ex: accs, [(d4, kvw)])[0]
    dh = _mm(
        f"swa_dh_{tag}", (t // tm, 1, 1),
        [(dq, (tm, qw), _ik, wq, (N_CHIPS, d4, qw), whole, "nt", 0, False),
         (dkv, (tm, kvw), _ik, wkv, (N_CHIPS, d4, kvw), whole, "nt", 0, False)],
        [], [((t, d), F32, (tm, d), _ij)], lambda accs, ex: accs, [(tm, d)])[0]
    dx, dgn = _rms_bwd(dh, x, gn, dx_out, f"mix_{tag}")
    n_heads = qw // HEAD_DIM
    return dx, dgn, dwq, dwkv, dwo, dbq, dbkv, dbo, dsink[:, :n_heads]


_HBM = pl.BlockSpec(memory_space=pl.ANY)
_CHIP_FLIPS = ((1, 0), (0, 1), (1, 1))


def _place():
    x, y, c = lax.axis_index("x"), lax.axis_index("y"), lax.axis_index("c")
    return x, y, c


def _flip(v, bit):
    return 1 - v if bit else v


def _exchange_small(v_ref, all_ref, send_sems, recv_sems):
    x, y, c = _place()
    me = 4 * x + 2 * y + c
    all_ref[me] = v_ref[...]
    copies = []
    for dlt in range(1, N_DEV):
        peer = (_flip(x, dlt & 4), _flip(y, dlt & 2), _flip(c, dlt & 1))
        copies.append(pltpu.make_async_remote_copy(
            src_ref=v_ref, dst_ref=all_ref.at[me], send_sem=send_sems.at[dlt - 1], recv_sem=recv_sems.at[dlt - 1],
            device_id=peer, device_id_type=MESH_ID))
    for cp in copies:
        cp.start()
    for cp in copies:
        cp.wait()


def _all_gather_small(v):
    r, cdim = v.shape

    def body(v_ref, out_ref, send_sems, recv_sems):
        _exchange_small(v_ref, out_ref, send_sems, recv_sems)

    return pl.pallas_call(
        body, name="all_gather_small", out_shape=jax.ShapeDtypeStruct((N_DEV, r, cdim), v.dtype),
        scratch_shapes=[pltpu.SemaphoreType.DMA((N_DEV - 1,)), pltpu.SemaphoreType.DMA((N_DEV - 1,))],
        compiler_params=pltpu.CompilerParams(vmem_limit_bytes=VMEM_LIMIT_BYTES),
    )(v)


def _all_sum_small(v):
    r, cdim = v.shape

    def body(v_ref, out_ref, all_ref, send_sems, recv_sems):
        _exchange_small(v_ref, all_ref, send_sems, recv_sems)
        acc = all_ref[0]
        for dv in range(1, N_DEV):
            acc = acc + all_ref[dv]
        out_ref[...] = acc

    return pl.pallas_call(
        body, name="all_sum_small", out_shape=jax.ShapeDtypeStruct((r, cdim), v.dtype),
        scratch_shapes=[pltpu.VMEM((N_DEV, r, cdim), v.dtype), pltpu.SemaphoreType.DMA((N_DEV - 1,)),
                        pltpu.SemaphoreType.DMA((N_DEV - 1,))],
        compiler_params=pltpu.CompilerParams(vmem_limit_bytes=VMEM_LIMIT_BYTES),
    )(v)


_HBM_SPEC = pl.BlockSpec(memory_space=pltpu.HBM)
_SEM_SPEC = pl.BlockSpec(memory_space=pltpu.SEMAPHORE)
_EFFECT = pltpu.SideEffectType.DATAFLOW_SIDE_EFFECTING
_TOKEN = jax.ShapeDtypeStruct((8, LANES), F32)


def _in_hbm(a):
    return pltpu.with_memory_space_constraint(a, pltpu.HBM)


def _hbm_like(a):
    return pltpu.HBM(a.shape, a.dtype)


def _ici_copies(shard, land, send_sems, recv_sems, base, x, y, c):
    half = shard.shape[0] // 2
    rows = pl.ds(c * half, half)
    return [pltpu.make_async_remote_copy(
        src_ref=shard.at[rows], dst_ref=land.at[2 * x + y, rows], send_sem=send_sems.at[base + j], recv_sem=recv_sems.at[base + j],
        device_id=(_flip(x, fx), _flip(y, fy), c), device_id_type=MESH_ID) for j, (fx, fy) in enumerate(_CHIP_FLIPS)]


def _d2d_copies(land, send_sems, recv_sems, base, x, y, c, c_rows):
    half = land.shape[1] // 2
    rows = pl.ds(c_rows * half, half)
    out = []
    for j, (fx, fy) in enumerate(_CHIP_FLIPS):
        piece = land.at[2 * _flip(x, fx) + _flip(y, fy), rows]
        out.append(pltpu.make_async_remote_copy(
            src_ref=piece, dst_ref=piece, send_sem=send_sems.at[base + j], recv_sem=recv_sems.at[base + j],
            device_id=(x, y, 1 - c), device_id_type=MESH_ID))
    return out


def _own_copy(shard, land, send_sems, recv_sems, s, x, y, c):
    return pltpu.make_async_remote_copy(
        src_ref=shard, dst_ref=land.at[2 * x + y], send_sem=send_sems.at[s], recv_sem=recv_sems.at[s],
        device_id=(x, y, 1 - c), device_id_type=MESH_ID)


def _gather_start(groups):
    flat = [s for grp in groups for s in grp]
    n, n_g = len(flat), len(groups)
    lands = [lax.empty((N_CHIPS,) + s.shape, s.dtype) for s in flat]

    def body(*refs):
        shards, land_refs = refs[:n], refs[n:2 * n]
        sems = refs[2 * n: 2 * n + 4 * n_g]
        token = refs[-1]
        x, y, c = _place()
        idx = 0
        for g, grp in enumerate(groups):
            ici_send, ici_recv, own_send, own_recv = sems[4 * g: 4 * g + 4]
            for s in range(len(grp)):
                _own_copy(shards[idx], land_refs[idx], own_send, own_recv, s, x, y, c).start()
                for cp in _ici_copies(shards[idx], land_refs[idx], ici_send, ici_recv, 3 * s, x, y, c):
                    cp.start()
                idx += 1
        token[...] = jnp.zeros(token.shape, F32)

    sem_shapes = []
    for grp in groups:
        k = len(grp)
        sem_shapes += [pltpu.SemaphoreType.DMA((3 * k,)), pltpu.SemaphoreType.DMA((3 * k,)),
                       pltpu.SemaphoreType.DMA((k,)), pltpu.SemaphoreType.DMA((k,))]
    res = pl.pallas_call(
        body, name="gather_start",
        out_shape=sem_shapes + [_hbm_like(s) for s in flat] + [_hbm_like(ld) for ld in lands] + [_TOKEN],
        in_specs=[_HBM_SPEC] * (2 * n),
        out_specs=[_SEM_SPEC] * (4 * n_g) + [_HBM_SPEC] * (2 * n) + [pl.BlockSpec(memory_space=pltpu.VMEM)],
        input_output_aliases={i: 4 * n_g + i for i in range(2 * n)},
        compiler_params=pltpu.CompilerParams(has_side_effects=_EFFECT),
    )(*[_in_hbm(s) for s in flat], *[_in_hbm(ld) for ld in lands])
    sems, thru, token = res[:4 * n_g], res[4 * n_g: 4 * n_g + 2 * n], res[-1]
    out, idx = [], 0
    for g, grp in enumerate(groups):
        k = len(grp)
        out.append(dict(sems=tuple(sems[4 * g: 4 * g + 4]), shards=list(thru[idx: idx + k]),
                        lands=list(thru[n + idx: n + idx + k])))
        idx += k
    return out, token


def _gather_step(name, landed, arriving, after):
    n_l = len(landed["lands"]) if landed else 0
    n_a = len(arriving["lands"]) if arriving else 0

    def body(*refs):
        pos = 0
        l_lands = refs[pos: pos + n_l]; pos += n_l
        l_sems = refs[pos: pos + (2 if landed else 0)]; pos += 2 if landed else 0
        a_shards = refs[pos: pos + n_a]; pos += n_a
        a_lands = refs[pos: pos + n_a]; pos += n_a
        a_sems = refs[pos: pos + (4 if arriving else 0)]; pos += 4 if arriving else 0
        pos += 1
        pos += n_l + n_a
        new_sems = refs[pos: pos + (2 if arriving else 0)]
        x, y, c = _place()
        if arriving:
            ici_send, ici_recv, own_send, own_recv = a_sems
            started = []
            for s in range(n_a):
                own = _own_copy(a_shards[s], a_lands[s], own_send, own_recv, s, x, y, c)
                own.wait_recv()
                for cp in _d2d_copies(a_lands[s], ici_send, ici_recv, 3 * s, x, y, c, c):
                    cp.wait_recv()
                for cp in _d2d_copies(a_lands[s], new_sems[0], new_sems[1], 3 * s, x, y, c, c):
                    cp.start()
                started.append(own)
                started += _ici_copies(a_shards[s], a_lands[s], ici_send, ici_recv, 3 * s, x, y, c)
            for cp in started:
                cp.wait_send()
        if landed:
            for s in range(n_l):
                for cp in _d2d_copies(l_lands[s], l_sems[0], l_sems[1], 3 * s, x, y, c, c):
                    cp.wait_send()
                for cp in _d2d_copies(l_lands[s], l_sems[0], l_sems[1], 3 * s, x, y, c, 1 - c):
                    cp.wait_recv()

    args, in_specs = [], []
    if landed:
        args += [_in_hbm(a) for a in landed["lands"]] + list(landed["d2d"])
        in_specs += [_HBM_SPEC] * n_l + [_SEM_SPEC] * 2
    if arriving:
        args += [_in_hbm(a) for a in arriving["shards"]] + [_in_hbm(a) for a in arriving["lands"]] + list(arriving["sems"])
        in_specs += [_HBM_SPEC] * (2 * n_a) + [_SEM_SPEC] * 4
    args.append(after)
    in_specs.append(pl.BlockSpec(memory_space=pl.ANY))
    out_shape, out_specs, aliases = [], [], {}
    if landed:
        out_shape += [_hbm_like(a) for a in landed["lands"]]
        for s in range(n_l):
            aliases[s] = s
    if arriving:
        first = (n_l + 2 if landed else 0) + n_a
        for s in range(n_a):
            aliases[first + s] = n_l + s
        out_shape += [_hbm_like(a) for a in arriving["lands"]]
    out_specs += [_HBM_SPEC] * (n_l + n_a)
    if arriving:
        out_shape += [pltpu.SemaphoreType.DMA((3 * n_a,)), pltpu.SemaphoreType.DMA((3 * n_a,))]
        out_specs += [_SEM_SPEC] * 2
    res = pl.pallas_call(
        body, name=name, out_shape=out_shape, in_specs=in_specs, out_specs=out_specs, input_output_aliases=aliases,
        compiler_params=pltpu.CompilerParams(has_side_effects=_EFFECT),
    )(*args)
    done = list(res[:n_l]) if landed else None
    nxt = None
    if arriving:
        nxt = dict(lands=list(res[n_l: n_l + n_a]), d2d=tuple(res[n_l + n_a: n_l + n_a + 2]))
    return done, nxt


def _swap_copies(grad, land, send_sems, recv_sems, base, x, y, c):
    half = grad.shape[1] // 2
    return [pltpu.make_async_remote_copy(
        src_ref=grad.at[:, pl.ds((1 - c) * half, half)], dst_ref=land, send_sem=send_sems.at[base], recv_sem=recv_sems.at[base],
        device_id=(x, y, 1 - c), device_id_type=MESH_ID)]


def _add_halves(g, r, c_idx, tag):
    _, rows, w = g.shape
    half = rows // 2
    tr = _row_tile(half, 1024)
    nb = half // tr

    def body(c_ref, g_ref, r_ref, o_ref):
        o_ref[...] = (g_ref[...].astype(F32) + r_ref[...].astype(F32)).astype(BF16)

    return pl.pallas_call(
        body, name=f"add_halves_{tag}",
        grid_spec=pltpu.PrefetchScalarGridSpec(
            num_scalar_prefetch=1, grid=(N_CHIPS, nb),
            in_specs=[pl.BlockSpec((None, tr, w), lambda k, i, c: (k, c[0] * nb + i, 0)),
                      pl.BlockSpec((None, tr, w), lambda k, i, c: (k, i, 0))],
            out_specs=pl.BlockSpec((None, tr, w), lambda k, i, c: (k, i, 0))),
        out_shape=jax.ShapeDtypeStruct((N_CHIPS, half, w), BF16), compiler_params=_params(2),
    )(c_idx, g, r)


def _partial_copies(part, land, send_sems, recv_sems, base, x, y, c):
    out = []
    for j, (fx, fy) in enumerate(_CHIP_FLIPS):
        px, py = _flip(x, fx), _flip(y, fy)
        out.append(pltpu.make_async_remote_copy(
            src_ref=part.at[2 * px + py], dst_ref=land.at[j], send_sem=send_sems.at[base + j], recv_sem=recv_sems.at[base + j],
            device_id=(px, py, c), device_id_type=MESH_ID))
    return out


def _split_start(name, srcs, land_shapes, copies_of, per):
    n = len(srcs)
    lands = [lax.empty(shape, s.dtype) for shape, s in zip(land_shapes, srcs)]

    def body(*refs):
        src_refs, land_refs = refs[:n], refs[n:2 * n]
        send_sems, recv_sems = refs[2 * n: 2 * n + 2]
        token = refs[-1]
        x, y, c = _place()
        for s in range(n):
            for cp in copies_of(src_refs[s], land_refs[s], send_sems, recv_sems, per * s, x, y, c):
                cp.start()
        token[...] = jnp.zeros(token.shape, F32)

    res = pl.pallas_call(
        body, name=name,
        out_shape=[pltpu.SemaphoreType.DMA((per * n,)), pltpu.SemaphoreType.DMA((per * n,))]
        + [_hbm_like(s) for s in srcs] + [_hbm_like(ld) for ld in lands] + [_TOKEN],
        in_specs=[_HBM_SPEC] * (2 * n),
        out_specs=[_SEM_SPEC] * 2 + [_HBM_SPEC] * (2 * n) + [pl.BlockSpec(memory_space=pltpu.VMEM)],
        input_output_aliases={i: 2 + i for i in range(2 * n)},
        compiler_params=pltpu.CompilerParams(has_side_effects=_EFFECT),
    )(*[_in_hbm(s) for s in srcs], *[_in_hbm(ld) for ld in lands])
    state = dict(sems=tuple(res[:2]), srcs=list(res[2: 2 + n]), lands=list(res[2 + n: 2 + 2 * n]))
    return state, res[-1]


def _split_wait(name, state, after, copies_of, per):
    n = len(state["srcs"])

    def body(*refs):
        src_refs, land_refs = refs[:n], refs[n:2 * n]
        send_sems, recv_sems = refs[2 * n: 2 * n + 2]
        x, y, c = _place()
        for s in range(n):
            for cp in copies_of(src_refs[s], land_refs[s], send_sems, recv_sems, per * s, x, y, c):
                cp.wait_send()
                cp.wait_recv()

    res = pl.pallas_call(
        body, name=name,
        out_shape=[_hbm_like(s) for s in state["srcs"]] + [_hbm_like(ld) for ld in state["lands"]],
        in_specs=[_HBM_SPEC] * (2 * n) + [_SEM_SPEC] * 2 + [pl.BlockSpec(memory_space=pl.ANY)],
        out_specs=[_HBM_SPEC] * (2 * n), input_output_aliases={i: i for i in range(2 * n)},
        compiler_params=pltpu.CompilerParams(has_side_effects=_EFFECT),
    )(*state["srcs"], *state["lands"], *state["sems"], after)
    return list(res[:n]), list(res[n:])


def _sum_partials(p, q, chip_idx, c_idx, tag):
    _, half, w = p.shape
    tr = _row_tile(half, 512)
    nb = half // tr

    def body(k_ref, c_ref, p_ref, q_ref, o_ref):
        acc = p_ref[...].astype(F32)
        for j in range(3):
            acc = acc + q_ref[j].astype(F32)
        o_ref[...] = acc

    return pl.pallas_call(
        body, name=f"sum_partials_{tag}",
        grid_spec=pltpu.PrefetchScalarGridSpec(
            num_scalar_prefetch=2, grid=(nb,),
            in_specs=[pl.BlockSpec((None, tr, w), lambda i, k, c: (k[0], i, 0)),
                      pl.BlockSpec((3, tr, w), lambda i, k, c: (0, i, 0))],
            out_specs=pl.BlockSpec((tr, w), lambda i, k, c: (c[0] * nb + i, 0))),
        out_shape=jax.ShapeDtypeStruct((2 * half, w), F32), compiler_params=_params(1),
    )(chip_idx, c_idx, p, q)


def _join_halves(tots, tag):
    n = len(tots)

    def body(*refs):
        bufs = refs[n:2 * n]
        send_sems, recv_sems = refs[2 * n:]
        x, y, c = _place()
        copies = []
        for s in range(n):
            half = bufs[s].shape[0] // 2
            mine = bufs[s].at[pl.ds(c * half, half)]
            cp = pltpu.make_async_remote_copy(
                src_ref=mine, dst_ref=mine, send_sem=send_sems.at[s], recv_sem=recv_sems.at[s],
                device_id=(x, y, 1 - c), device_id_type=MESH_ID)
            cp.start()
            copies.append(cp)
        for s in range(n):
            half = bufs[s].shape[0] // 2
            theirs = bufs[s].at[pl.ds((1 - c) * half, half)]
            pltpu.make_async_remote_copy(
                src_ref=theirs, dst_ref=theirs, send_sem=send_sems.at[s], recv_sem=recv_sems.at[s],
                device_id=(x, y, c), device_id_type=MESH_ID).wait_recv()
        for cp in copies:
            cp.wait_send()

    return pl.pallas_call(
        body, name=f"join_halves_{tag}",
        out_shape=[jax.ShapeDtypeStruct(tt.shape, tt.dtype) for tt in tots],
        in_specs=[_HBM] * n, out_specs=[_HBM] * n, input_output_aliases={s: s for s in range(n)},
        scratch_shapes=[pltpu.SemaphoreType.DMA((n,)), pltpu.SemaphoreType.DMA((n,))],
    )(*tots)


def _adamw_math(w, g, m, v):
    m = ADAM_B1 * m + (1.0 - ADAM_B1) * g
    v = ADAM_B2 * v + (1.0 - ADAM_B2) * jnp.square(g)
    m_hat = m / (1.0 - ADAM_B1 ** ADAM_STEP)
    v_hat = v / (1.0 - ADAM_B2 ** ADAM_STEP)
    delta = -ADAM_LR * (m_hat / (jnp.sqrt(v_hat) + ADAM_EPS) + ADAM_WD * w)
    return delta, m, v


def _adamw_big(g, col, w2, m2, v2, row0, prev, tag):
    rows = g.shape[0]
    rtot, wd = w2.shape
    tr = _row_tile(rows)
    assert row0 % tr == 0
    n_prev = 0 if prev is None else 4

    def body(*refs):
        g_ref, w_ref, m_ref, v_ref = refs[:4]
        go_ref, d_ref, mo_ref, vo_ref = refs[4 + n_prev:]
        gv = g_ref[...]
        delta, mn, vn = _adamw_math(w_ref[...], gv, m_ref[...], v_ref[...])
        go_ref[...] = gv
        d_ref[...] = delta
        mo_ref[...] = mn
        vo_ref[...] = vn

    at = lambda i: (row0 // tr + i, 0)
    return pl.pallas_call(
        body, name=f"adamw_{tag}", grid=(rows // tr,),
        in_specs=[pl.BlockSpec((tr, wd), lambda i: (i, col))] + [pl.BlockSpec((tr, wd), at)] * 3
        + [pl.BlockSpec(memory_space=pl.ANY)] * n_prev,
        out_specs=[pl.BlockSpec((tr, wd), at)] * 4, out_shape=[jax.ShapeDtypeStruct((rtot, wd), F32)] * 4,
        input_output_aliases={4 + k: k for k in range(n_prev)}, compiler_params=_params(1),
    )(g, w2, m2, v2, *(prev or ()))


def _adamw_small(items):
    n = len(items)

    def body(*refs):
        ins, outs = refs[:4 * n], refs[4 * n:]
        for k in range(n):
            g_ref, w_ref, m_ref, v_ref = ins[4 * k: 4 * k + 4]
            delta, mn, vn = _adamw_math(w_ref[...], g_ref[...], m_ref[...], v_ref[...])
            outs[3 * k][...] = delta
            outs[3 * k + 1][...] = mn
            outs[3 * k + 2][...] = vn

    flat = [a for it in items for a in it]
    out_shape = [jax.ShapeDtypeStruct(it[1].shape, F32) for it in items for _ in range(3)]
    res = pl.pallas_call(body, name="adamw_small", out_shape=out_shape,
                         compiler_params=pltpu.CompilerParams(vmem_limit_bytes=VMEM_LIMIT_BYTES))(*flat)
    return [tuple(res[3 * k: 3 * k + 3]) for k in range(n)]


def _pack(arrays, width):
    rows, layout, r = [], [], 0
    for a in arrays:
        flat = a.reshape(-1).astype(F32)
        nr = -(-flat.shape[0] // (8 * width)) * 8
        flat = jnp.pad(flat, (0, nr * width - flat.shape[0]))
        rows.append(flat.reshape(nr, width))
        layout.append((r, nr, a.shape))
        r += nr
    return jnp.concatenate(rows, axis=0), layout


def _unpack(packed, layout):
    out = []
    for r, nr, shape in layout:
        size = 1
        for s in shape:
            size *= s
        out.append(packed[r:r + nr].reshape(-1)[:size].reshape(shape))
    return out


_WEIGHTS = ['ffn1_norm', 'ffn1_w1', 'ffn1_w3', 'ffn1_w2', 'mix_norm', 'ffn2_norm', 'ffn2_w1', 'ffn2_w3', 'ffn2_w2',
            'ple_norm', 'ple_w_gate', 'ple_w_proj', 'gmlp_w_in', 'gmlp_ln_g', 'gmlp_ln_b', 'gmlp_w_s', 'gmlp_b_s',
            'gmlp_w_out', 'swa_wq', 'swa_bq', 'swa_wk', 'swa_bk', 'swa_wv', 'swa_bv', 'swa_sinks', 'swa_wo', 'swa_bo',
            'final_norm']
_REPLICATED = ['ffn1_norm', 'mix_norm', 'ffn2_norm', 'ple_norm', 'gmlp_ln_g', 'gmlp_ln_b', 'gmlp_w_s', 'gmlp_b_s',
               'swa_sinks', 'final_norm']
_BIASES = ['swa_bq', 'swa_bk', 'swa_bv', 'swa_bo']


def _as2d(a):
    if a.ndim == 1:
        return a.reshape(1, -1)
    return a.reshape(-1, a.shape[-1])


_GROUPS = [("f1l0", ("w1", "w3", "w2")), ("mix0", ("in", "out")), ("f2l0", ("w1", "w3", "w2")), ("ple0", ("gate", "proj")),
           ("f1l1", ("w1", "w3", "w2")), ("mix1", ("q", "kv", "o")), ("f2l1", ("w1", "w3", "w2")), ("ple1", ("gate", "proj"))]
_GATHER = []
for _g, _parts in _GROUPS:
    if _g.startswith("f"):
        _GATHER += [(_g + "a", _g, ("w1", "w3")), (_g + "b", _g, ("w2",))]
    else:
        _GATHER.append((_g, _g, _parts))


def _local_step(x, p, target, small, full_bias, get_group, put_group):
    t, d = x.shape
    n_layers = 2
    blk = small['gmlp_w_s'].shape[2]
    bq, bkv, bo = full_bias
    norm = lambda name, i: small[name][i:i + 1]
    ln_g, ln_b = small['gmlp_ln_g'], small['gmlp_ln_b']
    w_s = small['gmlp_w_s'][0]
    b_st = jnp.pad(small['gmlp_b_s'][0].T, ((0, 0), (0, LANES - small['gmlp_b_s'].shape[1])))
    sinks = small['swa_sinks']

    saved, wts = [], {}

    def ffn_fwd(x, name, norm_w):
        w = get_group(name + "a", x)
        x, s_ffn, w["w2"] = _ffn_fwd(x, norm_w, w["w1"], w["w3"], lambda after: get_group(name + "b", after)["w2"], name)
        wts[name] = w
        return x, s_ffn

    for i in range(n_layers):
        x, s_f1 = ffn_fwd(x, f"f1l{i}", norm('ffn1_norm', i))
        w = wts[f"mix{i}"] = get_group(f"mix{i}", x)
        if i == 0:
            x, s_mix = _gmlp_fwd(x, norm('mix_norm', i), ln_g, ln_b, w_s, b_st, w["in"], w["out"], f"l{i}")
        else:
            x, s_mix = _swa_fwd(x, norm('mix_norm', i), bq, bkv, bo, sinks, w["q"], w["kv"], w["o"], blk, f"l{i}")
        x, s_f2 = ffn_fwd(x, f"f2l{i}", norm('ffn2_norm', i))
        w = wts[f"ple{i}"] = get_group(f"ple{i}", x)
        x, s_ple = _ple_fwd(x, p[i], norm('ple_norm', i), w["gate"], w["proj"], f"l{i}")
        saved.append((s_f1, s_mix, s_f2, s_ple))

    dx, d_final, loss = _loss_head(x, small['final_norm'].reshape(1, d), target)

    gn = {k: [None] * n_layers for k in ('ffn1_norm', 'mix_norm', 'ffn2_norm', 'ple_norm')}
    dep = None
    for i in reversed(range(n_layers)):
        s_f1, s_mix, s_f2, s_ple = saved[i]
        w = wts[f"ple{i}"]
        dx, gn['ple_norm'][i], dwg, dwp = _ple_bwd(dx, s_ple, p[i], norm('ple_norm', i), w["gate"], f"l{i}", dep)
        dep = put_group(f"ple{i}", {"gate": dwg, "proj": dwp})
        w = wts[f"f2l{i}"]
        dx, gn['ffn2_norm'][i], dep = _ffn_bwd(
            dx, s_f2, norm('ffn2_norm', i), w["w1"], w["w3"], w["w2"], f"f2l{i}", dep,
            functools.partial(put_group, f"f2l{i}"))
        w = wts[f"mix{i}"]
        if i == 0:
            dx, gn['mix_norm'][i], dw_in, dw_out, d_ws, d_bs, d_lg, d_lb = _gmlp_bwd(
                dx, s_mix, norm('mix_norm', i), ln_g, ln_b, w_s, b_st, w["in"], w["out"], f"l{i}", dep)
            dep = put_group(f"mix{i}", {"in": dw_in, "out": dw_out})
        else:
            dx, gn['mix_norm'][i], dwq, dwkv, dwo, d_bq, d_bkv, d_bo, d_sink = _swa_bwd(
                dx, s_mix, norm('mix_norm', i), sinks, w["q"], w["kv"], w["o"], blk, f"l{i}", dep)
            dep = put_group(f"mix{i}", {"q": dwq, "kv": dwkv, "o": dwo})
        w = wts[f"f1l{i}"]
        dx, gn['ffn1_norm'][i], dep = _ffn_bwd(
            dx, s_f1, norm('ffn1_norm', i), w["w1"], w["w3"], w["w2"], f"f1l{i}", dep,
            functools.partial(put_group, f"f1l{i}"))

    kw = d_bkv.shape[1] // 2
    g_small = {
        'ffn1_norm': jnp.concatenate(gn['ffn1_norm'], axis=0), 'mix_norm': jnp.concatenate(gn['mix_norm'], axis=0),
        'ffn2_norm': jnp.concatenate(gn['ffn2_norm'], axis=0), 'ple_norm': jnp.concatenate(gn['ple_norm'], axis=0),
        'gmlp_ln_g': d_lg, 'gmlp_ln_b': d_lb, 'gmlp_w_s': d_ws[None], 'gmlp_b_s': d_bs[None], 'swa_sinks': d_sink,
        'final_norm': d_final.reshape(d), 'swa_bq': d_bq, 'swa_bk': d_bkv[:, :kw], 'swa_bv': d_bkv[:, kw:], 'swa_bo': d_bo,
    }
    return loss, dx, g_small


def _group_shards(a):
    bf = lambda w: w.astype(BF16)
    out = {}
    for i in range(2):
        for f in (1, 2):
            out[f"f{f}l{i}"] = {
                "w1": (bf(a[f"ffn{f}_w1"][i]), [(f"ffn{f}_w1", i, 0)]), "w3": (bf(a[f"ffn{f}_w3"][i]), [(f"ffn{f}_w3", i, 0)]),
                "w2": (bf(a[f"ffn{f}_w2"][i]), [(f"ffn{f}_w2", i, 0)])}
        out[f"ple{i}"] = {"gate": (bf(a["ple_w_gate"][i]), [("ple_w_gate", i, 0)]),
                          "proj": (bf(a["ple_w_proj"][i]), [("ple_w_proj", i, 0)])}
    out["mix0"] = {"in": (bf(a["gmlp_w_in"][0]), [("gmlp_w_in", 0, 0)]), "out": (bf(a["gmlp_w_out"][0]), [("gmlp_w_out", 0, 0)])}
    wkv = jnp.concatenate([bf(a["swa_wk"][0]), bf(a["swa_wv"][0])], axis=1)
    out["mix1"] = {"q": (bf(a["swa_wq"][0]), [("swa_wq", 0, 0)]), "kv": (wkv, [("swa_wk", 0, 0), ("swa_wv", 0, 1)]),
                   "o": (bf(a["swa_wo"][0]), [("swa_wo", 0, 0)])}
    return out


def kernel(x, p, ffn1_norm, ffn1_w1, ffn1_w3, ffn1_w2, mix_norm, ffn2_norm, ffn2_w1, ffn2_w3, ffn2_w2, ple_norm, ple_w_gate, ple_w_proj, gmlp_w_in, gmlp_ln_g, gmlp_ln_b, gmlp_w_s, gmlp_b_s, gmlp_w_out, swa_wq, swa_bq, swa_wk, swa_bk, swa_wv, swa_bv, swa_sinks, swa_wo, swa_bo, final_norm, loss_target, m_ffn1_norm, m_ffn1_w1, m_ffn1_w3, m_ffn1_w2, m_mix_norm, m_ffn2_norm, m_ffn2_w1, m_ffn2_w3, m_ffn2_w2, m_ple_norm, m_ple_w_gate, m_ple_w_proj, m_gmlp_w_in, m_gmlp_ln_g, m_gmlp_ln_b, m_gmlp_w_s, m_gmlp_b_s, m_gmlp_w_out, m_swa_wq, m_swa_bq, m_swa_wk, m_swa_bk, m_swa_wv, m_swa_bv, m_swa_sinks, m_swa_wo, m_swa_bo, m_final_norm, v_ffn1_norm, v_ffn1_w1, v_ffn1_w3, v_ffn1_w2, v_mix_norm, v_ffn2_norm, v_ffn2_w1, v_ffn2_w3, v_ffn2_w2, v_ple_norm, v_ple_w_gate, v_ple_w_proj, v_gmlp_w_in, v_gmlp_ln_g, v_gmlp_ln_b, v_gmlp_w_s, v_gmlp_b_s, v_gmlp_w_out, v_swa_wq, v_swa_bq, v_swa_wk, v_swa_bk, v_swa_wv, v_swa_bv, v_swa_sinks, v_swa_wo, v_swa_bo, v_final_norm):
    a = dict(locals())
    xi, yi, ci = _place()
    chip = 2 * xi + yi
    c_idx = ci.reshape(1).astype(jnp.int32)
    chip_idx = chip.reshape(1).astype(jnp.int32)
    d = x.shape[-1]
    d4 = d // N_CHIPS

    shards = _group_shards(a)
    started, token = _gather_start([[shards[g][part][0] for part in parts] for _, g, parts in _GATHER])
    bias_pack, bias_layout = _pack([a[n] for n in _BIASES], d4)
    bias_all = _all_gather_small(bias_pack)

    def full_bias(idx):
        return jnp.concatenate([_unpack(bias_all[2 * k], bias_layout)[idx] for k in range(N_CHIPS)], axis=1)

    bq, bk, bv, bo = (full_bias(i) for i in range(4))
    bkv = jnp.concatenate([bk, bv], axis=1)

    state = {"handed": _gather_step("gather_step_first", None, started[0], token)[1], "next": 1}

    def get_group(name, after):
        k = state["next"]
        assert _GATHER[k - 1][0] == name
        arriving = started[k] if k < len(_GATHER) else None
        done, state["handed"] = _gather_step(f"gather_step_{name}", state["handed"], arriving, after)
        state["next"] = k + 1
        return dict(zip(_GATHER[k - 1][2], done))

    pending, swapping = [], []

    def send_swapped(after):
        name, st = swapping.pop()
        parts = dict(_GROUPS)[name]
        glist, sib = _split_wait(f"swap_wait_{name}", st, after, _swap_copies, 1)
        partial = [_add_halves(g, r, c_idx, f"{name}_{part}") for part, g, r in zip(parts, glist, sib)]
        st, tok = _split_start(f"send_start_{name}", partial, [(3,) + pt.shape[1:] for pt in partial], _partial_copies, 3)
        pending.append((name, st))
        return tok

    def put_group(name, grads):
        glist = [grads[part] for part in dict(_GROUPS)[name]]
        st, tok = _split_start(f"swap_start_{name}", glist, [(N_CHIPS, g.shape[1] // 2, g.shape[2]) for g in glist],
                               _swap_copies, 1)
        if swapping:
            tok = send_swapped(tok)
        swapping.append((name, st))
        return tok

    small = {n: a[n] for n in _REPLICATED}
    loss, grad_x, g_small = _local_step(x[0], p[:, 0], loss_target[0], small, (bq, bkv, bo), get_group, put_group)
    last_sent = send_swapped(grad_x)

    names = _REPLICATED + _BIASES
    packed, layout = _pack([g_small[n] for n in names] + [loss[:, :1]], d4)
    summed = _unpack(_all_sum_small(packed), layout)
    g_sum = dict(zip(names, summed[:-1]))
    loss_out = summed[-1].reshape(())
    for n in _BIASES:
        width = a[n].shape[-1]
        g_sum[n] = lax.dynamic_slice_in_dim(g_sum[n], chip * width, width, axis=1)
    out, chain, last_done = {}, {}, grad_x
    for idx, (name, st) in enumerate(pending):
        mine, recv = _split_wait(f"send_wait_{name}", st, last_done if idx == len(pending) - 1 else last_sent,
                                 _partial_copies, 3)
        parts = dict(_GROUPS)[name]
        tots = _join_halves([_sum_partials(pm, q, chip_idx, c_idx, f"{name}_{part}")
                             for part, pm, q in zip(parts, mine, recv)], name)
        for part, g in zip(parts, tots):
            for n, layer, col in shards[name][part][1]:
                wd = a[n].shape[-1]
                rows = a[n].shape[-2]
                w2, m2, v2 = (a[pre + n].reshape(-1, wd) for pre in ('', 'm_', 'v_'))
                chain[n] = _adamw_big(g, col, w2, m2, v2, layer * rows, chain.get(n), f"{n}_{layer}")
                last_done = chain[n][1]
    for n, res in chain.items():
        out[n] = tuple(o.reshape(a[n].shape) for o in res)
    small_names = _REPLICATED + _BIASES
    items = [(_as2d(g_sum[n]), _as2d(a[n]), _as2d(a['m_' + n]), _as2d(a['v_' + n])) for n in small_names]
    for n, (delta, mn, vn) in zip(small_names, _adamw_small(items)):
        shape = a[n].shape
        out[n] = (g_sum[n].reshape(shape), delta.reshape(shape), mn.reshape(shape), vn.reshape(shape))

    return (loss_out, grad_x[None]) + tuple(out[n][j] for j in range(4) for n in _WEIGHTS)
```

```python
import functools

import jax
import jax.numpy as jnp
from jax import lax
from jax.experimental import pallas as pl
from jax.experimental.pallas import tpu as pltpu

F32 = jnp.float32
BF16 = jnp.bfloat16

RMS_EPS = 1e-6
LN_EPS = 1e-5
FFN_RESIDUAL_WEIGHT = 0.5
HEAD_DIM = 64
ROPE_DIM = 16
ROPE_THETA = 500000.0
ADAM_LR = 0.001
ADAM_B1 = 0.9
ADAM_B2 = 0.999
ADAM_EPS = 1e-08
ADAM_WD = 0.01
ADAM_STEP = 10
N_CHIPS = 4
N_DEV = 8
LANES = 128
VMEM_LIMIT_BYTES = 56 * 1024 * 1024
MESH_ID = pl.DeviceIdType.MESH

_DIMS = {
    "nn": (((1,), (0,)), ((), ())),
    "nt": (((1,), (1,)), ((), ())),
    "tn": (((0,), (0,)), ((), ())),
}


def _params(n_axes):
    return pltpu.CompilerParams(dimension_semantics=("arbitrary",) * n_axes, vmem_limit_bytes=VMEM_LIMIT_BYTES)


def _mm(name, grid, pairs, extras, outs, epilogue, acc_shapes, order="ij"):
    ni, nj, nk = grid
    if order == "ij":
        pgrid = (ni, nj, nk)
        ijk = lambda g0, g1, g2: (g0, g1, g2)
    else:
        pgrid = (nj, ni, nk)
        ijk = lambda g0, g1, g2: (g1, g0, g2)
    in_specs, args = [], []
    for a, ablk, amap, b, bblk, bmap, _, _, _ in pairs:
        in_specs.append(pl.BlockSpec(ablk, lambda g0, g1, g2, m=amap: m(*ijk(g0, g1, g2))))
        in_specs.append(pl.BlockSpec(bblk, lambda g0, g1, g2, m=bmap: m(*ijk(g0, g1, g2))))
        args += [a, b]
    for e, eblk, emap in extras:
        in_specs.append(pl.BlockSpec(eblk, lambda g0, g1, g2, m=emap: m(*ijk(g0, g1, g2)[:2])))
        args.append(e)
    out_specs = [pl.BlockSpec(oblk, lambda g0, g1, g2, m=omap: m(*ijk(g0, g1, g2)[:2])) for _, _, oblk, omap in outs]
    out_shape = [jax.ShapeDtypeStruct(s, d) for s, d, _, _ in outs]
    n_p, n_e, n_o = len(pairs), len(extras), len(outs)

    def body(*refs):
        p_refs = refs[: 2 * n_p]
        e_refs = refs[2 * n_p: 2 * n_p + n_e]
        o_refs = refs[2 * n_p + n_e: 2 * n_p + n_e + n_o]
        accs = refs[2 * n_p + n_e + n_o:]
        k = pl.program_id(2)

        def product(idx):
            a = p_refs[2 * idx][...].astype(BF16)
            b = p_refs[2 * idx + 1][...].astype(BF16)
            if b.ndim == 3:
                b = b.reshape(-1, b.shape[-1])
            return lax.dot_general(a, b, _DIMS[pairs[idx][6]], preferred_element_type=F32)

        def finish(sums):
            vals = epilogue(sums, [e[...] for e in e_refs])
            for o, v in zip(o_refs, vals):
                o[...] = v.astype(o.dtype)

        if nk == 1:
            sums = [None] * len(acc_shapes)
            for idx in range(n_p):
                ai = pairs[idx][7]
                sums[ai] = product(idx) if sums[ai] is None else sums[ai] + product(idx)
            finish(sums)
            return

        written = set()
        for idx in range(n_p):
            ai, k0 = pairs[idx][7], pairs[idx][8]
            if k0 or ai not in written:
                @pl.when(k == 0)
                def _(idx=idx, ai=ai, first=ai not in written):
                    if first:
                        accs[ai][...] = product(idx)
                    else:
                        accs[ai][...] += product(idx)
            if not k0:
                if ai in written:
                    accs[ai][...] += product(idx)
                else:
                    @pl.when(k > 0)
                    def _(idx=idx, ai=ai):
                        accs[ai][...] += product(idx)
            written.add(ai)

        @pl.when(k == nk - 1)
        def _():
            finish([acc[...] for acc in accs])

    res = pl.pallas_call(
        body, name=name, grid=pgrid, in_specs=in_specs, out_specs=out_specs, out_shape=out_shape,
        scratch_shapes=[] if nk == 1 else [pltpu.VMEM(s, F32) for s in acc_shapes], compiler_params=_params(3),
    )(*args)
    return res


def _col_nn(r0, tk, tn, wc):
    assert r0 % tk == 0 and wc % tn == 0
    npc = wc // tn
    return (None, tk, tn), lambda i, j, k: (j // npc, r0 // tk + k, j % npc)


def _row_nn(r0, tk, tn, rc):
    assert r0 % tk == 0 and rc % tk == 0
    kpc = rc // tk
    return (None, tk, tn), lambda i, j, k: (k // kpc, r0 // tk + k % kpc, j)


def _col_nt(r0, tn, tk, wc):
    assert r0 % tn == 0 and wc % tk == 0
    kpc = wc // tk
    return (None, tn, tk), lambda i, j, k: (k // kpc, r0 // tn + j, k % kpc)


def _row_nt(r0, tn, tk, rc):
    assert r0 % tn == 0 and rc % tn == 0
    npc = rc // tn
    return (None, tn, tk), lambda i, j, k: (j // npc, r0 // tn + j % npc, k)


def _out_col(tm, tn, wc):
    npc = wc // tn
    return (None, tm, tn), lambda i, j: (j // npc, i, j % npc)


def _out_row(tm, tn, rc):
    mpc = rc // tm
    return (None, tm, tn), lambda i, j: (i // mpc, i % mpc, j)


def _ik(i, j, k):
    return (i, k)


def _ki(i, j, k):
    return (k, i)


def _kj(i, j, k):
    return (k, j)


def _i0(i, j, k):
    return (i, 0)


def _ij(i, j):
    return (i, j)


def _0j(i, j):
    return (0, j)


def _i0e(i, j):
    return (i, 0)


def _rows(name, n, ins, outs, body, scratch=(), deps=()):
    in_specs, args = [], []
    for item in ins:
        if len(item) == 2:
            in_specs.append(item[1])
        else:
            in_specs.append(pl.BlockSpec(item[1], item[2]))
        args.append(item[0])
    n_in = len(args)
    for dep in deps:
        in_specs.append(pl.BlockSpec(dep.shape, lambda i: (0, 0)))
        args.append(dep)
    n_dep = len(deps)

    def call_body(*refs):
        body(*refs[:n_in], *refs[n_in + n_dep:])

    out_specs = [pl.BlockSpec(blk, m) for _, _, blk, m in outs]
    out_shape = [jax.ShapeDtypeStruct(s, d) for s, d, _, _ in outs]
    return pl.pallas_call(
        call_body, name=name, grid=(n,), in_specs=in_specs, out_specs=out_specs, out_shape=out_shape,
        scratch_shapes=list(scratch), compiler_params=_params(1),
    )(*args)


def _dep_extra(dep):
    return [] if dep is None else [(dep, dep.shape, lambda i, j: (0, 0))]


def _row(arr, tm):
    return (arr, (tm, arr.shape[1]), lambda i: (i, 0))


def _full(arr):
    nd = arr.ndim
    return (arr, arr.shape, lambda i: (0,) * nd)


def _row_out(shape, dtype, tm):
    return (shape, dtype, (tm, shape[1]), lambda i: (i, 0))


def _acc_out(shape):
    nd = len(shape)
    return (shape, F32, shape, lambda i: (0,) * nd)


def _tm(t):
    return 512 if t >= 1024 else t // 2


def _row_tile(rows, cap=256):
    best = max(tr for tr in range(16, min(rows, cap) + 1, 16) if rows % tr == 0)
    return best


def _first(i, refs):
    @pl.when(i == 0)
    def _():
        for r in refs:
            r[...] = jnp.zeros(r.shape, r.dtype)


def _rms_fwd(x, g, tag):
    t, d = x.shape
    tm = _tm(t)

    def body(x_ref, g_ref, h_ref):
        xv = x_ref[...]
        r = lax.rsqrt(jnp.mean(xv * xv, axis=-1, keepdims=True) + RMS_EPS)
        h_ref[...] = (xv * r * g_ref[...]).astype(BF16)

    return _rows(f"rms_fwd_{tag}", t // tm, [_row(x, tm), _full(g)], [_row_out((t, d), BF16, tm)], body)[0]


def _rms_bwd(dh, x, g, dx_out, tag):
    t, d = x.shape
    tm = _tm(t)

    def body(dh_ref, x_ref, g_ref, dxo_ref, dx_ref, dx16_ref, dg_ref):
        i = pl.program_id(0)
        _first(i, [dg_ref])
        xv = x_ref[...]
        r = lax.rsqrt(jnp.mean(xv * xv, axis=-1, keepdims=True) + RMS_EPS)
        xh = xv * r
        dhv = dh_ref[...]
        dxh = dhv * g_ref[...]
        dx = dxo_ref[...] + r * (dxh - xh * jnp.mean(dxh * xh, axis=-1, keepdims=True))
        dx_ref[...] = dx
        dx16_ref[...] = dx.astype(BF16)
        dg_ref[...] += jnp.sum(dhv * xh, axis=0, keepdims=True)

    return _rows(f"rms_bwd_{tag}", t // tm, [_row(dh, tm), _row(x, tm), _full(g), _row(dx_out, tm)],
                 [_row_out((t, d), F32, tm), _row_out((t, d), BF16, tm), _acc_out((1, d))], body)


def _loss_head(x, g, target):
    t, d = x.shape
    tm = _tm(t)

    def body(x_ref, g_ref, t_ref, dx_ref, dx16_ref, dg_ref, loss_ref):
        i = pl.program_id(0)
        _first(i, [dg_ref, loss_ref])
        xv = x_ref[...]
        gv = g_ref[...]
        r = lax.rsqrt(jnp.mean(xv * xv, axis=-1, keepdims=True) + RMS_EPS)
        xh = xv * r
        err = xh * gv - t_ref[...]
        loss_ref[...] += jnp.full((1, LANES), 0.5, F32) * jnp.sum(jnp.mean(err * err, axis=-1, keepdims=True))
        dy = err * (1.0 / d)
        dxh = dy * gv
        dx = r * (dxh - xh * jnp.mean(dxh * xh, axis=-1, keepdims=True))
        dx_ref[...] = dx
        dx16_ref[...] = dx.astype(BF16)
        dg_ref[...] += jnp.sum(dy * xh, axis=0, keepdims=True)

    return _rows("loss_head", t // tm, [_row(x, tm), _full(g), _row(target, tm)],
                 [_row_out((t, d), F32, tm), _row_out((t, d), BF16, tm), _acc_out((1, d)), _acc_out((1, LANES))], body)


def _colsum(arr, tag):
    t, w = arr.shape
    tm = _tm(t)

    def body(a_ref, o_ref):
        _first(pl.program_id(0), [o_ref])
        o_ref[...] += jnp.sum(a_ref[...].astype(F32), axis=0, keepdims=True)

    return _rows(f"colsum_{tag}", t // tm, [_row(arr, tm)], [_acc_out((1, w))], body)[0]


def _ffn_fwd(x, gn, w1, w3, get_w2, tag):
    t, d = x.shape
    fc = w1.shape[2]
    f = N_CHIPS * fc
    tm = _tm(t)
    h = _rms_fwd(x, gn, f"ffn_{tag}")
    w1blk, w1map = _col_nn(0, d, fc, fc)
    w3blk, w3map = _col_nn(0, d, fc, fc)
    tma = min(tm, 256)

    def ep_ab(accs, ex):
        a, b = accs
        sig = jax.nn.sigmoid(a)
        silu = a * sig
        return b * sig * (1.0 + a * (1.0 - sig)), silu, silu * b

    a, b, act = _mm(
        f"ffn_ab_{tag}", (t // tma, N_CHIPS, 1),
        [(h, (tma, d), _ik, w1, w1blk, w1map, "nn", 0, False), (h, (tma, d), _ik, w3, w3blk, w3map, "nn", 1, False)],
        [], [((t, f), BF16, (tma, fc), _ij)] * 3, ep_ab, [(tma, fc)] * 2, order="ji")
    w2 = get_w2(act)
    tn = min(d, 1024)
    tmo = min(t, 2 * tm)
    w2blk, w2map = _row_nn(0, fc, tn, fc)
    x_new = _mm(
        f"ffn_out_{tag}", (t // tmo, d // tn, N_CHIPS),
        [(act, (tmo, fc), _ik, w2, w2blk, w2map, "nn", 0, False)],
        [(x, (tmo, tn), _ij)], [((t, d), F32, (tmo, tn), _ij)],
        lambda accs, ex: [ex[0] + FFN_RESIDUAL_WEIGHT * accs[0]], [(tmo, tn)])[0]
    return x_new, (x, h, a, b, act), w2


def _ffn_bwd(dx_out, dx16, saved, gn, w1, w3, w2, tag, dep, put):
    x, h, a, b, act = saved
    t, d = x.shape
    fc = w1.shape[2]
    f = N_CHIPS * fc
    tm = _tm(t)
    tma = min(tm, 256)
    w2blk, w2map = _row_nt(0, fc, d, fc)

    def ep_dg(accs, ex):
        dg = FFN_RESIDUAL_WEIGHT * accs[0]
        return dg * ex[0].astype(F32), dg * ex[1].astype(F32)

    da, db = _mm(
        f"ffn_dg_{tag}", (t // tma, N_CHIPS, 1),
        [(dx16, (tma, d), _ik, w2, w2blk, w2map, "nt", 0, False)],
        [(a, (tma, fc), _ij), (b, (tma, fc), _ij)] + _dep_extra(dep), [((t, f), BF16, (tma, fc), _ij)] * 2, ep_dg, [(tma, fc)], order="ji")

    tmd = min(d, 512)
    tk = min(t, 2048)
    oblk, omap = _out_col(tmd, fc, fc)
    dw1, dw3 = _mm(
        f"ffn_dw13_{tag}", (d // tmd, N_CHIPS, t // tk),
        [(h, (tk, tmd), _ki, da, (tk, fc), _kj, "tn", 0, False), (h, (tk, tmd), _ki, db, (tk, fc), _kj, "tn", 1, False)],
        [], [((N_CHIPS, d, fc), BF16, oblk, omap)] * 2, lambda accs, ex: accs, [(tmd, fc)] * 2)
    tn = min(d, 1024)
    tk2 = min(t, 2048)
    dw2 = _mm(
        f"ffn_dw2_{tag}", (N_CHIPS, d // tn, t // tk2),
        [(act, (tk2, fc), _ki, dx16, (tk2, tn), _kj, "tn", 0, False)],
        [], [((N_CHIPS, fc, d), BF16, (None, fc, tn), lambda i, j: (i, 0, j))],
        lambda accs, ex: [FFN_RESIDUAL_WEIGHT * accs[0]], [(fc, tn)])[0]
    token = put({"w1": dw1, "w3": dw3, "w2": dw2})
    w1blk, w1map = _col_nt(0, tn, fc, fc)
    w3blk, w3map = _col_nt(0, tn, fc, fc)
    tmh = min(t, 2 * tm)
    dh = _mm(
        f"ffn_dh_{tag}", (t // tmh, d // tn, N_CHIPS),
        [(da, (tmh, fc), _ik, w1, w1blk, w1map, "nt", 0, False), (db, (tmh, fc), _ik, w3, w3blk, w3map, "nt", 0, False)],
        _dep_extra(token), [((t, d), F32, (tmh, tn), _ij)], lambda accs, ex: accs, [(tmh, tn)])[0]
    dx, dx16, dgn = _rms_bwd(dh, x, gn, dx_out, f"ffn_{tag}")
    return dx, dx16, dgn, token


def _ple_fwd(x, p_i, gn, wg, wp, tag):
    t, d = x.shape
    pdim = p_i.shape[1]
    d4 = d // N_CHIPS
    tm = _tm(t)
    hp = _rms_fwd(x, gn, f"ple_{tag}")
    tm = min(t, 2 * tm)

    def ep(accs, ex):
        s = jax.nn.sigmoid(accs[0])
        return ex[0] + s * accs[1], s, accs[1]

    x_new, s, pp = _mm(
        f"ple_fwd_{tag}", (t // tm, N_CHIPS, 1),
        [(hp, (tm, d), _i0, wg, (N_CHIPS, d4, d4), lambda i, j, k: (0, 0, j), "nn", 0, False),
         (p_i, (tm, pdim), _i0, wp, (None, pdim, d4), lambda i, j, k: (j, 0, 0), "nn", 1, False)],
        [(x, (tm, d4), _ij)], [((t, d), F32, (tm, d4), _ij), ((t, d), BF16, (tm, d4), _ij), ((t, d), BF16, (tm, d4), _ij)],
        ep, [(tm, d4)] * 2)
    return x_new, (x, hp, s, pp)


def _ple_bwd(dx_out, saved, p_i, gn, wg, tag, dep=None):
    x, hp, s, pp = saved
    t, d = x.shape
    pdim = p_i.shape[1]
    d4 = d // N_CHIPS
    tm = _tm(t)

    def body(dx_ref, s_ref, pp_ref, dpp_ref, dgp_ref):
        dxv = dx_ref[...]
        sv = s_ref[...].astype(F32)
        ppv = pp_ref[...].astype(F32)
        dpp_ref[...] = (dxv * sv).astype(BF16)
        dgp_ref[...] = (dxv * ppv * sv * (1.0 - sv)).astype(BF16)

    dpp, dgp = _rows(f"ple_ew_{tag}", t // tm, [_row(dx_out, tm), _row(s, tm), _row(pp, tm)],
                     [_row_out((t, d), BF16, tm)] * 2, body, deps=[] if dep is None else [dep])
    tk = min(t, 2048)
    dwp = _mm(
        f"ple_dwp_{tag}", (1, N_CHIPS, t // tk),
        [(p_i, (tk, pdim), lambda i, j, k: (k, 0), dpp, (tk, d4), _kj, "tn", 0, False)],
        [], [((N_CHIPS, pdim, d4), BF16, (None, pdim, d4), lambda i, j: (j, 0, 0))], lambda accs, ex: accs, [(pdim, d4)])[0]
    tn = min(d, 1024)
    dwg = _mm(
        f"ple_dwg_{tag}", (N_CHIPS, d // tn, t // tk),
        [(hp, (tk, d4), _ki, dgp, (tk, tn), _kj, "tn", 0, False)],
        [], [((N_CHIPS, d4, d), BF16, (None, d4, tn), lambda i, j: (i, 0, j))], lambda accs, ex: accs, [(d4, tn)])[0]
    dhp = _mm(
        f"ple_dh_{tag}", (t // tm, 1, 1),
        [(dgp, (tm, d), _ik, wg, (N_CHIPS, d4, d), lambda i, j, k: (0, 0, 0), "nt", 0, False)],
        [], [((t, d), F32, (tm, d), _ij)], lambda accs, ex: accs, [(tm, d)])[0]
    dx, dx16, dgn = _rms_bwd(dhp, x, gn, dx_out, f"ple_{tag}")
    return dx, dx16, dgn, dwg, dwp


_SQRT_HALF = 0.7071067811865476
_INV_SQRT_2PI = 0.3989422804014327


def _gelu(z):
    return z * (lax.erf(z * _SQRT_HALF) + 1.0) * 0.5


def _gelu_grad(z):
    return 0.5 * (1.0 + lax.erf(z * _SQRT_HALF)) + z * (_INV_SQRT_2PI * jnp.exp(-0.5 * z * z))


def _causal_bf16(w):
    c = w.shape[0]
    keep = lax.broadcasted_iota(jnp.int32, (c, c), 0) >= lax.broadcasted_iota(jnp.int32, (c, c), 1)
    return jnp.where(keep, w, 0.0).astype(BF16), keep


def _gmlp_gate_fwd(z_pre, ln_g, ln_b, w_s, b_st, tag):
    t, w2 = z_pre.shape
    gw = w2 // 2
    n_g, chunk, _ = w_s.shape
    gd = gw // n_g

    def body(z_ref, g_ref, b_ref, ws_ref, bs_ref, o_ref):
        z = z_ref[...]
        u = _gelu(z[:, :gw])
        zv = _gelu(z[:, gw:])
        mu = jnp.mean(zv, axis=-1, keepdims=True)
        cen = zv - mu
        rstd = lax.rsqrt(jnp.mean(cen * cen, axis=-1, keepdims=True) + LN_EPS)
        vln = (cen * rstd * g_ref[...] + b_ref[...]).astype(BF16)
        bst = bs_ref[...]
        for g in range(n_g):
            wm, _ = _causal_bf16(ws_ref[g])
            sl = slice(g * gd, (g + 1) * gd)
            s = jnp.dot(wm, vln[:, sl], preferred_element_type=F32) + bst[:, g:g + 1]
            o_ref[:, sl] = (u[:, sl] * s).astype(BF16)

    return _rows(f"gmlp_gate_{tag}", t // chunk, [_row(z_pre, chunk), _full(ln_g), _full(ln_b), _full(w_s), _full(b_st)],
                 [_row_out((t, gw), BF16, chunk)], body)[0]


def _gmlp_gate_bwd(z_pre, dgated, ln_g, ln_b, w_s, b_st, tag):
    t, w2 = z_pre.shape
    gw = w2 // 2
    n_g, chunk, _ = w_s.shape
    gd = gw // n_g

    def body(z_ref, dgt_ref, g_ref, b_ref, ws_ref, bs_ref, dz_ref, dws_ref, dbs_ref, dlg_ref, dlb_ref, dv_scr):
        _first(pl.program_id(0), [dws_ref, dbs_ref, dlg_ref, dlb_ref])
        z = z_ref[...]
        zu, zvp = z[:, :gw], z[:, gw:]
        u = _gelu(zu)
        zv = _gelu(zvp)
        mu = jnp.mean(zv, axis=-1, keepdims=True)
        cen = zv - mu
        rstd = lax.rsqrt(jnp.mean(cen * cen, axis=-1, keepdims=True) + LN_EPS)
        vn = cen * rstd
        lg = g_ref[...]
        vln = (vn * lg + b_ref[...]).astype(BF16)
        bst = bs_ref[...]
        lane = lax.broadcasted_iota(jnp.int32, (chunk, LANES), 1)
        dbs = jnp.zeros((chunk, LANES), F32)
        for g in range(n_g):
            wm, keep = _causal_bf16(ws_ref[g])
            sl = slice(g * gd, (g + 1) * gd)
            vg = vln[:, sl]
            s = jnp.dot(wm, vg, preferred_element_type=F32) + bst[:, g:g + 1]
            dgt = dgt_ref[:, sl].astype(F32)
            ds = dgt * u[:, sl]
            ds16 = ds.astype(BF16)
            dz_ref[:, sl] = (dgt * s * _gelu_grad(zu[:, sl])).astype(dz_ref.dtype)
            dv_scr[:, sl] = lax.dot_general(wm, ds16, _DIMS["tn"], preferred_element_type=F32)
            dw = lax.dot_general(ds16, vg, _DIMS["nt"], preferred_element_type=F32)
            dws_ref[g] += jnp.where(keep, dw, 0.0)
            dbs = dbs + jnp.where(lane == g, jnp.sum(ds, axis=-1, keepdims=True), 0.0)
        dbs_ref[...] += dbs
        dvln = dv_scr[...]
        dlg_ref[...] += jnp.sum(dvln * vn, axis=0, keepdims=True)
        dlb_ref[...] += jnp.sum(dvln, axis=0, keepdims=True)
        dvn = dvln * lg
        dzv = rstd * (dvn - jnp.mean(dvn, axis=-1, keepdims=True) - vn * jnp.mean(dvn * vn, axis=-1, keepdims=True))
        dz_ref[:, gw:] = (dzv * _gelu_grad(zvp)).astype(dz_ref.dtype)

    return _rows(
        f"gmlp_gate_bwd_{tag}", t // chunk,
        [_row(z_pre, chunk), _row(dgated, chunk), _full(ln_g), _full(ln_b), _full(w_s), _full(b_st)],
        [_row_out((t, w2), BF16, chunk), _acc_out((n_g, chunk, chunk)), _acc_out((chunk, LANES)), _acc_out((1, gw)),
         _acc_out((1, gw))], body, scratch=[pltpu.VMEM((chunk, gw), F32)])


def _gmlp_fwd(x, gn, ln_g, ln_b, w_s, b_st, w_in, w_out, tag):
    t, d = x.shape
    gw = ln_g.shape[1]
    tm = _tm(t)
    d4 = d // N_CHIPS
    h = _rms_fwd(x, gn, f"mix_{tag}")
    tn = min(d, 1024)
    iblk, imap = _col_nn(0, d, tn, d)
    z_pre = _mm(
        f"gmlp_in_{tag}", (t // tm, N_CHIPS * d // tn, 1),
        [(h, (tm, d), _ik, w_in, iblk, imap, "nn", 0, False)],
        [], [((t, 2 * gw), F32, (tm, tn), _ij)], lambda accs, ex: accs, [(tm, tn)], order="ji")[0]
    gated = _gmlp_gate_fwd(z_pre, ln_g, ln_b, w_s, b_st, tag)
    rc = gw // N_CHIPS
    tk = min(rc, 1024)
    tmo = min(t, 2 * tm)
    oblk, omap = _row_nn(0, tk, tn, rc)
    x_new = _mm(
        f"gmlp_out_{tag}", (t // tmo, d // tn, gw // tk),
        [(gated, (tmo, tk), _ik, w_out, oblk, omap, "nn", 0, False)],
        [(x, (tmo, tn), _ij)], [((t, d), F32, (tmo, tn), _ij)], lambda accs, ex: [ex[0] + accs[0]], [(tmo, tn)])[0]
    return x_new, (x, h, z_pre, gated)


def _gmlp_bwd(dx_out, dx16, saved, gn, ln_g, ln_b, w_s, b_st, w_in, w_out, tag, dep=None):
    x, h, z_pre, gated = saved
    t, d = x.shape
    gw = ln_g.shape[1]
    tm = _tm(t)
    d4 = d // N_CHIPS
    rc = gw // N_CHIPS
    tnr = min(rc, 1024)
    oblk, omap = _row_nt(0, tnr, d, rc)
    dgated = _mm(
        f"gmlp_dgated_{tag}", (t // tm, gw // tnr, 1),
        [(dx16, (tm, d), _ik, w_out, oblk, omap, "nt", 0, False)],
        _dep_extra(dep), [((t, gw), BF16, (tm, tnr), _ij)], lambda accs, ex: accs, [(tm, tnr)], order="ji")[0]
    tk = min(t, 2048)
    tn = min(d, 1024)
    rblk, rmap = _out_row(tnr, tn, rc)
    dw_out = _mm(
        f"gmlp_dwout_{tag}", (gw // tnr, d // tn, t // tk),
        [(gated, (tk, tnr), _ki, dx16, (tk, tn), _kj, "tn", 0, False)],
        [], [((N_CHIPS, rc, d), BF16, rblk, rmap)], lambda accs, ex: accs, [(tnr, tn)])[0]
    dz, dws, dbs, dlg, dlb = _gmlp_gate_bwd(z_pre, dgated, ln_g, ln_b, w_s, b_st, tag)
    tmd = min(d, 512)
    cblk, cmap = _out_col(tmd, tn, d)
    dw_in = _mm(
        f"gmlp_dwin_{tag}", (d // tmd, N_CHIPS * d // tn, t // tk),
        [(h, (tk, tmd), _ki, dz, (tk, tn), _kj, "tn", 0, False)],
        [], [((N_CHIPS, d, d), BF16, cblk, cmap)], lambda accs, ex: accs, [(tmd, tn)])[0]
    iblk, imap = _col_nt(0, tn, d, d)
    dh = _mm(
        f"gmlp_dh_{tag}", (t // tm, d // tn, N_CHIPS),
        [(dz, (tm, d), _ik, w_in, iblk, imap, "nt", 0, False)],
        [], [((t, d), F32, (tm, tn), _ij)], lambda accs, ex: accs, [(tm, tn)])[0]
    dx, dx16, dgn = _rms_bwd(dh, x, gn, dx_out, f"mix_{tag}")
    n_g = w_s.shape[0]
    return dx, dx16, dgn, dw_in, dw_out, dws, dbs[:, :n_g].T, dlg, dlb


def _rope_tables(t, width, n_rot_heads):
    half = ROPE_DIM // 2
    inv_freq = ROPE_THETA ** (-jnp.arange(0, ROPE_DIM, 2, dtype=F32) / ROPE_DIM)
    ang = jnp.arange(t, dtype=F32)[:, None] * inv_freq[None, :]
    cos, sin = jnp.cos(ang), jnp.sin(ang)
    ones = jnp.ones((t, HEAD_DIM - ROPE_DIM), F32)
    zeros = jnp.zeros((t, HEAD_DIM - ROPE_DIM), F32)
    zh = jnp.zeros((t, half), F32)
    c = jnp.concatenate([cos, cos, ones], axis=1)
    s_next = jnp.concatenate([-sin, zh, zeros], axis=1)
    s_prev = jnp.concatenate([zh, sin, zeros], axis=1)
    rest = width - n_rot_heads * HEAD_DIM

    def widen(tab, fill):
        parts = [jnp.tile(tab, (1, n_rot_heads))]
        if rest:
            parts.append(jnp.full((t, rest), fill, F32))
        return jnp.concatenate(parts, axis=1)

    return widen(c, 1.0), widen(s_next, 0.0), widen(s_prev, 0.0)


def _rope(v, c, s_next, s_prev, sign):
    w = v.shape[1]
    half = ROPE_DIM // 2
    reps = w // c.shape[1]
    if reps > 1:
        c, s_next, s_prev = (jnp.tile(tab, (1, reps)) for tab in (c, s_next, s_prev))
    return v * c + sign * (pltpu.roll(v, w - half, 1) * s_next + pltpu.roll(v, half, 1) * s_prev)


def _scores_mask(n, blk, heads):
    row = lax.broadcasted_iota(jnp.int32, (heads * blk, 2 * blk), 0)
    q_pos = lax.rem(row, blk) + blk
    k_pos = lax.broadcasted_iota(jnp.int32, (heads * blk, 2 * blk), 1)
    diff = q_pos - k_pos
    band = (diff >= 0) & (diff < blk)
    return band & ((k_pos >= blk) | (n > 0))


def _heads_to_rows(v, first, heads):
    return jnp.concatenate([v[:, (first + g) * HEAD_DIM:(first + g + 1) * HEAD_DIM] for g in range(heads)], axis=0)


def _sink_column(sink_ref, first, heads, blk):
    return jnp.concatenate([jnp.full((blk, 1), sink_ref[0, first + g], F32) for g in range(heads)], axis=0)


def _attn_fwd(q, kv, sinks, blk):
    t, qw = q.shape
    kw = kv.shape[1] // 2
    n_kv = kw // HEAD_DIM
    q_per_kv = qw // kw
    scale = HEAD_DIM ** -0.5

    def body(sink_ref, q_ref, kvo_ref, kvp_ref, o_ref):
        n = pl.program_id(0)
        valid = _scores_mask(n, blk, 1)
        qv = q_ref[...]
        kvo = kvo_ref[...]
        kvp = kvp_ref[...]
        for kh in range(n_kv):
            ks = slice(kh * HEAD_DIM, (kh + 1) * HEAD_DIM)
            vs = slice(kw + kh * HEAD_DIM, kw + (kh + 1) * HEAD_DIM)
            kb = jnp.concatenate([kvp[:, ks], kvo[:, ks]], axis=0)
            vb = jnp.concatenate([kvp[:, vs], kvo[:, vs]], axis=0)
            for g in range(q_per_kv):
                hd = kh * q_per_kv + g
                hs = slice(hd * HEAD_DIM, (hd + 1) * HEAD_DIM)
                sink = sink_ref[0, hd]
                s = lax.dot_general(qv[:, hs], kb, _DIMS["nt"], preferred_element_type=F32) * scale
                s = jnp.where(valid, s, -1e30)
                m = jnp.maximum(jnp.max(s, axis=-1, keepdims=True), sink)
                e = jnp.where(valid, jnp.exp(s - m), 0.0)
                denom = jnp.sum(e, axis=-1, keepdims=True) + jnp.exp(sink - m)
                p = (e / denom).astype(BF16)
                o_ref[:, hs] = jnp.dot(p, vb, preferred_element_type=F32).astype(BF16)

    return _rows(
        "swa_attn", t // blk,
        [(sinks, pl.BlockSpec(memory_space=pltpu.SMEM)), _row(q, blk), _row(kv, blk),
         (kv, (blk, kv.shape[1]), lambda i: (jnp.maximum(i - 1, 0), 0))],
        [_row_out((t, qw), BF16, blk)], body)[0]


def _attn_bwd(q, kv, do, sinks, blk):
    t, qw = q.shape
    kw = kv.shape[1] // 2
    n_kv = kw // HEAD_DIM
    q_per_kv = qw // kw
    scale = HEAD_DIM ** -0.5

    def body(sink_ref, q_ref, kvo_ref, kvp_ref, do_ref, dq_ref, dkvo_ref, dkvp_ref, dsink_ref):
        n = pl.program_id(0)
        _first(n, [dsink_ref])
        valid = _scores_mask(n, blk, q_per_kv)
        lane = lax.broadcasted_iota(jnp.int32, (1, LANES), 1)
        qv = q_ref[...]
        kvo = kvo_ref[...]
        kvp = kvp_ref[...]
        dov = do_ref[...]
        dsink = jnp.zeros((1, LANES), F32)
        for kh in range(n_kv):
            ks = slice(kh * HEAD_DIM, (kh + 1) * HEAD_DIM)
            vs = slice(kw + kh * HEAD_DIM, kw + (kh + 1) * HEAD_DIM)
            kb = jnp.concatenate([kvp[:, ks], kvo[:, ks]], axis=0)
            vb = jnp.concatenate([kvp[:, vs], kvo[:, vs]], axis=0)
            qg = _heads_to_rows(qv, kh * q_per_kv, q_per_kv)
            dog = _heads_to_rows(dov, kh * q_per_kv, q_per_kv)
            sink = _sink_column(sink_ref, kh * q_per_kv, q_per_kv, blk)
            s = lax.dot_general(qg, kb, _DIMS["nt"], preferred_element_type=F32) * scale
            s = jnp.where(valid, s, -1e30)
            m = jnp.maximum(jnp.max(s, axis=-1, keepdims=True), sink)
            e = jnp.where(valid, jnp.exp(s - m), 0.0)
            e_sink = jnp.exp(sink - m)
            inv = 1.0 / (jnp.sum(e, axis=-1, keepdims=True) + e_sink)
            p = e * inv
            p16 = p.astype(BF16)
            dp = lax.dot_general(dog, vb, _DIMS["nt"], preferred_element_type=F32)
            dot_pd = jnp.sum(p * dp, axis=-1, keepdims=True)
            ds16 = (p * (dp - dot_pd)).astype(BF16)
            d_sink_rows = e_sink * inv * dot_pd
            dqg = (jnp.dot(ds16, kb, preferred_element_type=F32) * scale).astype(BF16)
            for g in range(q_per_kv):
                hd = kh * q_per_kv + g
                rows = slice(g * blk, (g + 1) * blk)
                dsink = dsink + jnp.where(lane == hd, -jnp.sum(d_sink_rows[rows], axis=0, keepdims=True), 0.0)
                dq_ref[:, hd * HEAD_DIM:(hd + 1) * HEAD_DIM] = dqg[rows]
            dkb = lax.dot_general(ds16, qg, _DIMS["tn"], preferred_element_type=F32) * scale
            dvb = lax.dot_general(p16, dog, _DIMS["tn"], preferred_element_type=F32)
            dkvp_ref[:, ks] = dkb[:blk]
            dkvo_ref[:, ks] = dkb[blk:]
            dkvp_ref[:, vs] = dvb[:blk]
            dkvo_ref[:, vs] = dvb[blk:]
        dsink_ref[...] += dsink

    return _rows(
        "swa_attn_bwd", t // blk,
        [(sinks, pl.BlockSpec(memory_space=pltpu.SMEM)), _row(q, blk), _row(kv, blk),
         (kv, (blk, kv.shape[1]), lambda i: (jnp.maximum(i - 1, 0), 0)), _row(do, blk)],
        [_row_out((t, qw), BF16, blk), _row_out((t, 2 * kw), F32, blk), _row_out((t, 2 * kw), F32, blk),
         _acc_out((1, LANES))], body)


def _rope_bwd(dq_r, dkv_own, dkv_prev, tabs_q, tabs_kv, blk):
    t, qw = dq_r.shape
    kvw = dkv_own.shape[1]
    nb = t // blk

    def body(dq_ref, own_ref, nxt_ref, cq, snq, spq, ck, snk, spk, dqo_ref, dkvo_ref, dbq_ref, dbkv_ref):
        i = pl.program_id(0)
        _first(i, [dbq_ref, dbkv_ref])
        dq = _rope(dq_ref[...].astype(F32), cq[...], snq[...], spq[...], -1.0)
        dkv = own_ref[...] + jnp.where(i < nb - 1, nxt_ref[...], 0.0)
        dkv = _rope(dkv, ck[...], snk[...], spk[...], -1.0)
        dqo_ref[...] = dq.astype(BF16)
        dkvo_ref[...] = dkv.astype(BF16)
        dbq_ref[...] += jnp.sum(dq, axis=0, keepdims=True)
        dbkv_ref[...] += jnp.sum(dkv, axis=0, keepdims=True)

    return _rows(
        "swa_rope_bwd", nb,
        [_row(dq_r, blk), _row(dkv_own, blk), (dkv_prev, (blk, kvw), lambda i: (jnp.minimum(i + 1, nb - 1), 0))]
        + [_row(tab, blk) for tab in tabs_q] + [_row(tab, blk) for tab in tabs_kv],
        [_row_out((t, qw), BF16, blk), _row_out((t, kvw), BF16, blk), _acc_out((1, qw)), _acc_out((1, kvw))], body)


def _swa_fwd(x, gn, bq, bkv, bo, sinks, wq, wkv, wo, blk, tag):
    t, d = x.shape
    qw = bq.shape[1]
    kvw = bkv.shape[1]
    d4 = d // N_CHIPS
    tm = _tm(t)
    h = _rms_fwd(x, gn, f"mix_{tag}")
    tabs_q = _rope_tables(t, LANES, LANES // HEAD_DIM)
    tabs_kv = _rope_tables(t, kvw, kvw // 2 // HEAD_DIM)
    tn = min(qw, 1024)
    whole = lambda i, j, k: (0, 0, j)

    def ep_rope(accs, ex):
        return [_rope(accs[0] + ex[0], ex[1], ex[2], ex[3], 1.0)]

    q = _mm(
        f"swa_q_{tag}", (t // tm, qw // tn, 1),
        [(h, (tm, d), _ik, wq, (N_CHIPS, d4, tn), whole, "nn", 0, False)],
        [(bq, (1, tn), _0j)] + [(tab, (tm, LANES), _i0e) for tab in tabs_q],
        [((t, qw), BF16, (tm, tn), _ij)], ep_rope, [(tm, tn)], order="ji")[0]
    kv = _mm(
        f"swa_kv_{tag}", (t // tm, 1, 1),
        [(h, (tm, d), _ik, wkv, (N_CHIPS, d4, kvw), whole, "nn", 0, False)],
        [(bkv, (1, kvw), _0j)] + [(tab, (tm, kvw), _i0e) for tab in tabs_kv],
        [((t, kvw), BF16, (tm, kvw), _ij)], ep_rope, [(tm, kvw)])[0]
    o = _attn_fwd(q, kv, sinks, blk)
    tno = min(d, 1024)
    x_new = _mm(
        f"swa_out_{tag}", (t // tm, d // tno, 1),
        [(o, (tm, qw), _ik, wo, (N_CHIPS, qw // N_CHIPS, tno), whole, "nn", 0, False)],
        [(x, (tm, tno), _ij), (bo, (1, tno), _0j)], [((t, d), F32, (tm, tno), _ij)],
        lambda accs, ex: [ex[0] + accs[0] + ex[1]], [(tm, tno)], order="ji")[0]
    return x_new, (x, h, q, kv, o, tabs_q, tabs_kv)


def _swa_bwd(dx_out, dx16, saved, gn, sinks, wq, wkv, wo, blk, tag, dep=None):
    x, h, q, kv, o, tabs_q, tabs_kv = saved
    t, d = x.shape
    qw = q.shape[1]
    kvw = kv.shape[1]
    d4 = d // N_CHIPS
    qw4 = qw // N_CHIPS
    tm = _tm(t)
    whole = lambda i, j, k: (0, 0, 0)
    do = _mm(
        f"swa_do_{tag}", (t // tm, 1, 1),
        [(dx16, (tm, d), _ik, wo, (N_CHIPS, qw4, d), whole, "nt", 0, False)],
        _dep_extra(dep), [((t, qw), BF16, (tm, qw), _ij)], lambda accs, ex: accs, [(tm, qw)])[0]
    tk = min(t, 2048)
    tn = min(d, 1024)
    dwo = _mm(
        f"swa_dwo_{tag}", (N_CHIPS, d // tn, t // tk),
        [(o, (tk, qw4), _ki, dx16, (tk, tn), _kj, "tn", 0, False)],
        [], [((N_CHIPS, qw4, d), BF16, (None, qw4, tn), lambda i, j: (i, 0, j))], lambda accs, ex: accs, [(qw4, tn)])[0]
    dbo = _colsum(dx_out, f"bo_{tag}")
    dq_r, dkv_own, dkv_prev, dsink = _attn_bwd(q, kv, do, sinks, blk)
    dq, dkv, dbq, dbkv = _rope_bwd(dq_r, dkv_own, dkv_prev, tabs_q, tabs_kv, blk)
    tnq = min(qw, 1024)
    dwq = _mm(
        f"swa_dwq_{tag}", (N_CHIPS, qw // tnq, t // tk),
        [(h, (tk, d4), _ki, dq, (tk, tnq), _kj, "tn", 0, False)],
        [], [((N_CHIPS, d4, qw), BF16, (None, d4, tnq), lambda i, j: (i, 0, j))], lambda accs, ex: accs, [(d4, tnq)])[0]
    dwkv = _mm(
        f"swa_dwkv_{tag}", (N_CHIPS, 1, t // tk),
        [(h, (tk, d4), _ki, dkv, (tk, kvw), _kj, "tn", 0, False)],
        [], [((N_CHIPS, d4, kvw), BF16, (None, d4, kvw), lambda i, j: (i, 0, j))], lambda accs, ex: accs, [(d4, kvw)])[0]
    dh = _mm(
        f"swa_dh_{tag}", (t // tm, 1, 1),
        [(dq, (tm, qw), _ik, wq, (N_CHIPS, d4, qw), whole, "nt", 0, False),
         (dkv, (tm, kvw), _ik, wkv, (N_CHIPS, d4, kvw), whole, "nt", 0, False)],
        [], [((t, d), F32, (tm, d), _ij)], lambda accs, ex: accs, [(tm, d)])[0]
    dx, dx16, dgn = _rms_bwd(dh, x, gn, dx_out, f"mix_{tag}")
    n_heads = qw // HEAD_DIM
    return dx, dx16, dgn, dwq, dwkv, dwo, dbq, dbkv, dbo, dsink[:, :n_heads]


_HBM = pl.BlockSpec(memory_space=pl.ANY)
_CHIP_FLIPS = ((1, 0), (0, 1), (1, 1))


def _place():
    x, y, c = lax.axis_index("x"), lax.axis_index("y"), lax.axis_index("c")
    return x, y, c


def _flip(v, bit):
    return 1 - v if bit else v


def _exchange_small(v_ref, all_ref, send_sems, recv_sems):
    x, y, c = _place()
    me = 4 * x + 2 * y + c
    all_ref[me] = v_ref[...]
    copies = []
    for dlt in range(1, N_DEV):
        peer = (_flip(x, dlt & 4), _flip(y, dlt & 2), _flip(c, dlt & 1))
        copies.append(pltpu.make_async_remote_copy(
            src_ref=v_ref, dst_ref=all_ref.at[me], send_sem=send_sems.at[dlt - 1], recv_sem=recv_sems.at[dlt - 1],
            device_id=peer, device_id_type=MESH_ID))
    for cp in copies:
        cp.start()
    for cp in copies:
        cp.wait()


def _all_gather_small(v):
    r, cdim = v.shape

    def body(v_ref, out_ref, send_sems, recv_sems):
        _exchange_small(v_ref, out_ref, send_sems, recv_sems)

    return pl.pallas_call(
        body, name="all_gather_small", out_shape=jax.ShapeDtypeStruct((N_DEV, r, cdim), v.dtype),
        scratch_shapes=[pltpu.SemaphoreType.DMA((N_DEV - 1,)), pltpu.SemaphoreType.DMA((N_DEV - 1,))],
        compiler_params=pltpu.CompilerParams(vmem_limit_bytes=VMEM_LIMIT_BYTES),
    )(v)


def _all_sum_small(v):
    r, cdim = v.shape

    def body(v_ref, out_ref, all_ref, send_sems, recv_sems):
        _exchange_small(v_ref, all_ref, send_sems, recv_sems)
        acc = all_ref[0]
        for dv in range(1, N_DEV):
            acc = acc + all_ref[dv]
        out_ref[...] = acc

    return pl.pallas_call(
        body, name="all_sum_small", out_shape=jax.ShapeDtypeStruct((r, cdim), v.dtype),
        scratch_shapes=[pltpu.VMEM((N_DEV, r, cdim), v.dtype), pltpu.SemaphoreType.DMA((N_DEV - 1,)),
                        pltpu.SemaphoreType.DMA((N_DEV - 1,))],
        compiler_params=pltpu.CompilerParams(vmem_limit_bytes=VMEM_LIMIT_BYTES),
    )(v)


_HBM_SPEC = pl.BlockSpec(memory_space=pltpu.HBM)
_SEM_SPEC = pl.BlockSpec(memory_space=pltpu.SEMAPHORE)
_EFFECT = pltpu.SideEffectType.DATAFLOW_SIDE_EFFECTING
_TOKEN = jax.ShapeDtypeStruct((8, LANES), F32)


def _in_hbm(a):
    return pltpu.with_memory_space_constraint(a, pltpu.HBM)


def _hbm_like(a):
    return pltpu.HBM(a.shape, a.dtype)


def _ici_copies(shard, land, send_sems, recv_sems, base, x, y, c):
    half = shard.shape[0] // 2
    rows = pl.ds(c * half, half)
    return [pltpu.make_async_remote_copy(
        src_ref=shard.at[rows], dst_ref=land.at[2 * x + y, rows], send_sem=send_sems.at[base + j], recv_sem=recv_sems.at[base + j],
        device_id=(_flip(x, fx), _flip(y, fy), c), device_id_type=MESH_ID) for j, (fx, fy) in enumerate(_CHIP_FLIPS)]


def _d2d_copies(land, send_sems, recv_sems, base, x, y, c, c_rows):
    half = land.shape[1] // 2
    rows = pl.ds(c_rows * half, half)
    out = []
    for j, (fx, fy) in enumerate(_CHIP_FLIPS):
        piece = land.at[2 * _flip(x, fx) + _flip(y, fy), rows]
        out.append(pltpu.make_async_remote_copy(
            src_ref=piece, dst_ref=piece, send_sem=send_sems.at[base + j], recv_sem=recv_sems.at[base + j],
            device_id=(x, y, 1 - c), device_id_type=MESH_ID))
    return out


def _own_copy(shard, land, send_sems, recv_sems, s, x, y, c):
    return pltpu.make_async_remote_copy(
        src_ref=shard, dst_ref=land.at[2 * x + y], send_sem=send_sems.at[s], recv_sem=recv_sems.at[s],
        device_id=(x, y, 1 - c), device_id_type=MESH_ID)


def _gather_start(groups):
    flat = [s for grp in groups for s in grp]
    n, n_g = len(flat), len(groups)
    lands = [lax.empty((N_CHIPS,) + s.shape, s.dtype) for s in flat]

    def body(*refs):
        shards, land_refs = refs[:n], refs[n:2 * n]
        sems = refs[2 * n: 2 * n + 4 * n_g]
        token = refs[-1]
        x, y, c = _place()
        idx = 0
        for g, grp in enumerate(groups):
            ici_send, ici_recv, own_send, own_recv = sems[4 * g: 4 * g + 4]
            for s in range(len(grp)):
                _own_copy(shards[idx], land_refs[idx], own_send, own_recv, s, x, y, c).start()
                for cp in _ici_copies(shards[idx], land_refs[idx], ici_send, ici_recv, 3 * s, x, y, c):
                    cp.start()
                idx += 1
        token[...] = jnp.zeros(token.shape, F32)

    sem_shapes = []
    for grp in groups:
        k = len(grp)
        sem_shapes += [pltpu.SemaphoreType.DMA((3 * k,)), pltpu.SemaphoreType.DMA((3 * k,)),
                       pltpu.SemaphoreType.DMA((k,)), pltpu.SemaphoreType.DMA((k,))]
    res = pl.pallas_call(
        body, name="gather_start",
        out_shape=sem_shapes + [_hbm_like(s) for s in flat] + [_hbm_like(ld) for ld in lands] + [_TOKEN],
        in_specs=[_HBM_SPEC] * (2 * n),
        out_specs=[_SEM_SPEC] * (4 * n_g) + [_HBM_SPEC] * (2 * n) + [pl.BlockSpec(memory_space=pltpu.VMEM)],
        input_output_aliases={i: 4 * n_g + i for i in range(2 * n)},
        compiler_params=pltpu.CompilerParams(has_side_effects=_EFFECT),
    )(*[_in_hbm(s) for s in flat], *[_in_hbm(ld) for ld in lands])
    sems, thru, token = res[:4 * n_g], res[4 * n_g: 4 * n_g + 2 * n], res[-1]
    out, idx = [], 0
    for g, grp in enumerate(groups):
        k = len(grp)
        out.append(dict(sems=tuple(sems[4 * g: 4 * g + 4]), shards=list(thru[idx: idx + k]),
                        lands=list(thru[n + idx: n + idx + k])))
        idx += k
    return out, token


def _gather_step(name, landed, arriving, after):
    n_l = len(landed["lands"]) if landed else 0
    n_a = len(arriving["lands"]) if arriving else 0

    def body(*refs):
        pos = 0
        l_lands = refs[pos: pos + n_l]; pos += n_l
        l_sems = refs[pos: pos + (2 if landed else 0)]; pos += 2 if landed else 0
        a_shards = refs[pos: pos + n_a]; pos += n_a
        a_lands = refs[pos: pos + n_a]; pos += n_a
        a_sems = refs[pos: pos + (4 if arriving else 0)]; pos += 4 if arriving else 0
        pos += 1
        pos += n_l + n_a
        new_sems = refs[pos: pos + (2 if arriving else 0)]
        x, y, c = _place()
        if arriving:
            ici_send, ici_recv, own_send, own_recv = a_sems
            started = []
            for s in range(n_a):
                own = _own_copy(a_shards[s], a_lands[s], own_send, own_recv, s, x, y, c)
                own.wait_recv()
                for cp in _d2d_copies(a_lands[s], ici_send, ici_recv, 3 * s, x, y, c, c):
                    cp.wait_recv()
                for cp in _d2d_copies(a_lands[s], new_sems[0], new_sems[1], 3 * s, x, y, c, c):
                    cp.start()
                started.append(own)
                started += _ici_copies(a_shards[s], a_lands[s], ici_send, ici_recv, 3 * s, x, y, c)
            for cp in started:
                cp.wait_send()
        if landed:
            for s in range(n_l):
                for cp in _d2d_copies(l_lands[s], l_sems[0], l_sems[1], 3 * s, x, y, c, c):
                    cp.wait_send()
                for cp in _d2d_copies(l_lands[s], l_sems[0], l_sems[1], 3 * s, x, y, c, 1 - c):
                    cp.wait_recv()

    args, in_specs = [], []
    if landed:
        args += [_in_hbm(a) for a in landed["lands"]] + list(landed["d2d"])
        in_specs += [_HBM_SPEC] * n_l + [_SEM_SPEC] * 2
    if arriving:
        args += [_in_hbm(a) for a in arriving["shards"]] + [_in_hbm(a) for a in arriving["lands"]] + list(arriving["sems"])
        in_specs += [_HBM_SPEC] * (2 * n_a) + [_SEM_SPEC] * 4
    args.append(after)
    in_specs.append(pl.BlockSpec(memory_space=pl.ANY))
    out_shape, out_specs, aliases = [], [], {}
    if landed:
        out_shape += [_hbm_like(a) for a in landed["lands"]]
        for s in range(n_l):
            aliases[s] = s
    if arriving:
        first = (n_l + 2 if landed else 0) + n_a
        for s in range(n_a):
            aliases[first + s] = n_l + s
        out_shape += [_hbm_like(a) for a in arriving["lands"]]
    out_specs += [_HBM_SPEC] * (n_l + n_a)
    if arriving:
        out_shape += [pltpu.SemaphoreType.DMA((3 * n_a,)), pltpu.SemaphoreType.DMA((3 * n_a,))]
        out_specs += [_SEM_SPEC] * 2
    res = pl.pallas_call(
        body, name=name, out_shape=out_shape, in_specs=in_specs, out_specs=out_specs, input_output_aliases=aliases,
        compiler_params=pltpu.CompilerParams(has_side_effects=_EFFECT),
    )(*args)
    done = list(res[:n_l]) if landed else None
    nxt = None
    if arriving:
        nxt = dict(lands=list(res[n_l: n_l + n_a]), d2d=tuple(res[n_l + n_a: n_l + n_a + 2]))
    return done, nxt


def _swap_copies(grad, land, send_sems, recv_sems, base, x, y, c):
    half = grad.shape[1] // 2
    return [pltpu.make_async_remote_copy(
        src_ref=grad.at[:, pl.ds((1 - c) * half, half)], dst_ref=land, send_sem=send_sems.at[base], recv_sem=recv_sems.at[base],
        device_id=(x, y, 1 - c), device_id_type=MESH_ID)]


def _add_halves(g, r, c_idx, tag):
    _, rows, w = g.shape
    half = rows // 2
    tr = _row_tile(half, 1024)
    nb = half // tr

    def body(c_ref, g_ref, r_ref, o_ref):
        o_ref[...] = (g_ref[...].astype(F32) + r_ref[...].astype(F32)).astype(BF16)

    return pl.pallas_call(
        body, name=f"add_halves_{tag}",
        grid_spec=pltpu.PrefetchScalarGridSpec(
            num_scalar_prefetch=1, grid=(N_CHIPS, nb),
            in_specs=[pl.BlockSpec((None, tr, w), lambda k, i, c: (k, c[0] * nb + i, 0)),
                      pl.BlockSpec((None, tr, w), lambda k, i, c: (k, i, 0))],
            out_specs=pl.BlockSpec((None, tr, w), lambda k, i, c: (k, i, 0))),
        out_shape=jax.ShapeDtypeStruct((N_CHIPS, half, w), BF16), compiler_params=_params(2),
    )(c_idx, g, r)


def _partial_copies(part, land, send_sems, recv_sems, base, x, y, c):
    out = []
    for j, (fx, fy) in enumerate(_CHIP_FLIPS):
        px, py = _flip(x, fx), _flip(y, fy)
        out.append(pltpu.make_async_remote_copy(
            src_ref=part.at[2 * px + py], dst_ref=land.at[j], send_sem=send_sems.at[base + j], recv_sem=recv_sems.at[base + j],
            device_id=(px, py, c), device_id_type=MESH_ID))
    return out


def _split_start(name, srcs, land_shapes, copies_of, per):
    n = len(srcs)
    lands = [lax.empty(shape, s.dtype) for shape, s in zip(land_shapes, srcs)]

    def body(*refs):
        src_refs, land_refs = refs[:n], refs[n:2 * n]
        send_sems, recv_sems = refs[2 * n: 2 * n + 2]
        token = refs[-1]
        x, y, c = _place()
        for s in range(n):
            for cp in copies_of(src_refs[s], land_refs[s], send_sems, recv_sems, per * s, x, y, c):
                cp.start()
        token[...] = jnp.zeros(token.shape, F32)

    res = pl.pallas_call(
        body, name=name,
        out_shape=[pltpu.SemaphoreType.DMA((per * n,)), pltpu.SemaphoreType.DMA((per * n,))]
        + [_hbm_like(s) for s in srcs] + [_hbm_like(ld) for ld in lands] + [_TOKEN],
        in_specs=[_HBM_SPEC] * (2 * n),
        out_specs=[_SEM_SPEC] * 2 + [_HBM_SPEC] * (2 * n) + [pl.BlockSpec(memory_space=pltpu.VMEM)],
        input_output_aliases={i: 2 + i for i in range(2 * n)},
        compiler_params=pltpu.CompilerParams(has_side_effects=_EFFECT),
    )(*[_in_hbm(s) for s in srcs], *[_in_hbm(ld) for ld in lands])
    state = dict(sems=tuple(res[:2]), srcs=list(res[2: 2 + n]), lands=list(res[2 + n: 2 + 2 * n]))
    return state, res[-1]


def _split_wait(name, state, after, copies_of, per):
    n = len(state["srcs"])

    def body(*refs):
        src_refs, land_refs = refs[:n], refs[n:2 * n]
        send_sems, recv_sems = refs[2 * n: 2 * n + 2]
        x, y, c = _place()
        for s in range(n):
            for cp in copies_of(src_refs[s], land_refs[s], send_sems, recv_sems, per * s, x, y, c):
                cp.wait_send()
                cp.wait_recv()

    res = pl.pallas_call(
        body, name=name,
        out_shape=[_hbm_like(s) for s in state["srcs"]] + [_hbm_like(ld) for ld in state["lands"]],
        in_specs=[_HBM_SPEC] * (2 * n) + [_SEM_SPEC] * 2 + [pl.BlockSpec(memory_space=pl.ANY)],
        out_specs=[_HBM_SPEC] * (2 * n), input_output_aliases={i: i for i in range(2 * n)},
        compiler_params=pltpu.CompilerParams(has_side_effects=_EFFECT),
    )(*state["srcs"], *state["lands"], *state["sems"], after)
    return list(res[:n]), list(res[n:])


def _sum_partials(p, q, chip_idx, c_idx, tag):
    _, half, w = p.shape
    tr = _row_tile(half, 512)
    nb = half // tr

    def body(k_ref, c_ref, p_ref, q_ref, o_ref):
        acc = p_ref[...].astype(F32)
        for j in range(3):
            acc = acc + q_ref[j].astype(F32)
        o_ref[...] = acc

    return pl.pallas_call(
        body, name=f"sum_partials_{tag}",
        grid_spec=pltpu.PrefetchScalarGridSpec(
            num_scalar_prefetch=2, grid=(nb,),
            in_specs=[pl.BlockSpec((None, tr, w), lambda i, k, c: (k[0], i, 0)),
                      pl.BlockSpec((3, tr, w), lambda i, k, c: (0, i, 0))],
            out_specs=pl.BlockSpec((tr, w), lambda i, k, c: (c[0] * nb + i, 0))),
        out_shape=jax.ShapeDtypeStruct((2 * half, w), F32), compiler_params=_params(1),
    )(chip_idx, c_idx, p, q)


def _join_halves(tots, tag):
    n = len(tots)

    def body(*refs):
        bufs = refs[n:2 * n]
        send_sems, recv_sems = refs[2 * n:]
        x, y, c = _place()
        copies = []
        for s in range(n):
            half = bufs[s].shape[0] // 2
            mine = bufs[s].at[pl.ds(c * half, half)]
            cp = pltpu.make_async_remote_copy(
                src_ref=mine, dst_ref=mine, send_sem=send_sems.at[s], recv_sem=recv_sems.at[s],
                device_id=(x, y, 1 - c), device_id_type=MESH_ID)
            cp.start()
            copies.append(cp)
        for s in range(n):
            half = bufs[s].shape[0] // 2
            theirs = bufs[s].at[pl.ds((1 - c) * half, half)]
            pltpu.make_async_remote_copy(
                src_ref=theirs, dst_ref=theirs, send_sem=send_sems.at[s], recv_sem=recv_sems.at[s],
                device_id=(x, y, c), device_id_type=MESH_ID).wait_recv()
        for cp in copies:
            cp.wait_send()

    return pl.pallas_call(
        body, name=f"join_halves_{tag}",
        out_shape=[jax.ShapeDtypeStruct(tt.shape, tt.dtype) for tt in tots],
        in_specs=[_HBM] * n, out_specs=[_HBM] * n, input_output_aliases={s: s for s in range(n)},
        scratch_shapes=[pltpu.SemaphoreType.DMA((n,)), pltpu.SemaphoreType.DMA((n,))],
    )(*tots)


def _adamw_math(w, g, m, v):
    m = ADAM_B1 * m + (1.0 - ADAM_B1) * g
    v = ADAM_B2 * v + (1.0 - ADAM_B2) * jnp.square(g)
    m_hat = m / (1.0 - ADAM_B1 ** ADAM_STEP)
    v_hat = v / (1.0 - ADAM_B2 ** ADAM_STEP)
    delta = -ADAM_LR * (m_hat / (jnp.sqrt(v_hat) + ADAM_EPS) + ADAM_WD * w)
    return delta, m, v


def _adamw_big(g, col, w2, m2, v2, row0, prev, tag):
    rows = g.shape[0]
    rtot, wd = w2.shape
    tr = _row_tile(rows)
    assert row0 % tr == 0
    n_prev = 0 if prev is None else 4

    def body(*refs):
        g_ref, w_ref, m_ref, v_ref = refs[:4]
        go_ref, d_ref, mo_ref, vo_ref = refs[4 + n_prev:]
        gv = g_ref[...]
        delta, mn, vn = _adamw_math(w_ref[...], gv, m_ref[...], v_ref[...])
        go_ref[...] = gv
        d_ref[...] = delta
        mo_ref[...] = mn
        vo_ref[...] = vn

    at = lambda i: (row0 // tr + i, 0)
    return pl.pallas_call(
        body, name=f"adamw_{tag}", grid=(rows // tr,),
        in_specs=[pl.BlockSpec((tr, wd), lambda i: (i, col))] + [pl.BlockSpec((tr, wd), at)] * 3
        + [pl.BlockSpec(memory_space=pl.ANY)] * n_prev,
        out_specs=[pl.BlockSpec((tr, wd), at)] * 4, out_shape=[jax.ShapeDtypeStruct((rtot, wd), F32)] * 4,
        input_output_aliases={4 + k: k for k in range(n_prev)}, compiler_params=_params(1),
    )(g, w2, m2, v2, *(prev or ()))


def _adamw_small(items):
    n = len(items)

    def body(*refs):
        ins, outs = refs[:4 * n], refs[4 * n:]
        for k in range(n):
            g_ref, w_ref, m_ref, v_ref = ins[4 * k: 4 * k + 4]
            delta, mn, vn = _adamw_math(w_ref[...], g_ref[...], m_ref[...], v_ref[...])
            outs[3 * k][...] = delta
            outs[3 * k + 1][...] = mn
            outs[3 * k + 2][...] = vn

    flat = [a for it in items for a in it]
    out_shape = [jax.ShapeDtypeStruct(it[1].shape, F32) for it in items for _ in range(3)]
    res = pl.pallas_call(body, name="adamw_small", out_shape=out_shape,
                         compiler_params=pltpu.CompilerParams(vmem_limit_bytes=VMEM_LIMIT_BYTES))(*flat)
    return [tuple(res[3 * k: 3 * k + 3]) for k in range(n)]


def _pack(arrays, width):
    rows, layout, r = [], [], 0
    for a in arrays:
        flat = a.reshape(-1).astype(F32)
        nr = -(-flat.shape[0] // (8 * width)) * 8
        flat = jnp.pad(flat, (0, nr * width - flat.shape[0]))
        rows.append(flat.reshape(nr, width))
        layout.append((r, nr, a.shape))
        r += nr
    return jnp.concatenate(rows, axis=0), layout


def _unpack(packed, layout):
    out = []
    for r, nr, shape in layout:
        size = 1
        for s in shape:
            size *= s
        out.append(packed[r:r + nr].reshape(-1)[:size].reshape(shape))
    return out


_WEIGHTS = ['ffn1_norm', 'ffn1_w1', 'ffn1_w3', 'ffn1_w2', 'mix_norm', 'ffn2_norm', 'ffn2_w1', 'ffn2_w3', 'ffn2_w2',
            'ple_norm', 'ple_w_gate', 'ple_w_proj', 'gmlp_w_in', 'gmlp_ln_g', 'gmlp_ln_b', 'gmlp_w_s', 'gmlp_b_s',
            'gmlp_w_out', 'swa_wq', 'swa_bq', 'swa_wk', 'swa_bk', 'swa_wv', 'swa_bv', 'swa_sinks', 'swa_wo', 'swa_bo',
            'final_norm']
_REPLICATED = ['ffn1_norm', 'mix_norm', 'ffn2_norm', 'ple_norm', 'gmlp_ln_g', 'gmlp_ln_b', 'gmlp_w_s', 'gmlp_b_s',
               'swa_sinks', 'final_norm']
_BIASES = ['swa_bq', 'swa_bk', 'swa_bv', 'swa_bo']


def _as2d(a):
    if a.ndim == 1:
        return a.reshape(1, -1)
    return a.reshape(-1, a.shape[-1])


_GROUPS = [("f1l0", ("w1", "w3", "w2")), ("mix0", ("in", "out")), ("f2l0", ("w1", "w3", "w2")), ("ple0", ("gate", "proj")),
           ("f1l1", ("w1", "w3", "w2")), ("mix1", ("q", "kv", "o")), ("f2l1", ("w1", "w3", "w2")), ("ple1", ("gate", "proj"))]
_GATHER = []
for _g, _parts in _GROUPS:
    if _g.startswith("f"):
        _GATHER += [(_g + "a", _g, ("w1", "w3")), (_g + "b", _g, ("w2",))]
    else:
        _GATHER.append((_g, _g, _parts))


def _local_step(x, p, target, small, full_bias, get_group, put_group):
    t, d = x.shape
    n_layers = 2
    blk = small['gmlp_w_s'].shape[2]
    bq, bkv, bo = full_bias
    norm = lambda name, i: small[name][i:i + 1]
    ln_g, ln_b = small['gmlp_ln_g'], small['gmlp_ln_b']
    w_s = small['gmlp_w_s'][0]
    b_st = jnp.pad(small['gmlp_b_s'][0].T, ((0, 0), (0, LANES - small['gmlp_b_s'].shape[1])))
    sinks = small['swa_sinks']

    saved, wts = [], {}

    def ffn_fwd(x, name, norm_w):
        w = get_group(name + "a", x)
        x, s_ffn, w["w2"] = _ffn_fwd(x, norm_w, w["w1"], w["w3"], lambda after: get_group(name + "b", after)["w2"], name)
        wts[name] = w
        return x, s_ffn

    for i in range(n_layers):
        x, s_f1 = ffn_fwd(x, f"f1l{i}", norm('ffn1_norm', i))
        w = wts[f"mix{i}"] = get_group(f"mix{i}", x)
        if i == 0:
            x, s_mix = _gmlp_fwd(x, norm('mix_norm', i), ln_g, ln_b, w_s, b_st, w["in"], w["out"], f"l{i}")
        else:
            x, s_mix = _swa_fwd(x, norm('mix_norm', i), bq, bkv, bo, sinks, w["q"], w["kv"], w["o"], blk, f"l{i}")
        x, s_f2 = ffn_fwd(x, f"f2l{i}", norm('ffn2_norm', i))
        w = wts[f"ple{i}"] = get_group(f"ple{i}", x)
        x, s_ple = _ple_fwd(x, p[i], norm('ple_norm', i), w["gate"], w["proj"], f"l{i}")
        saved.append((s_f1, s_mix, s_f2, s_ple))

    dx, dx16, d_final, loss = _loss_head(x, small['final_norm'].reshape(1, d), target)

    gn = {k: [None] * n_layers for k in ('ffn1_norm', 'mix_norm', 'ffn2_norm', 'ple_norm')}
    dep = None
    for i in reversed(range(n_layers)):
        s_f1, s_mix, s_f2, s_ple = saved[i]
        w = wts[f"ple{i}"]
        dx, dx16, gn['ple_norm'][i], dwg, dwp = _ple_bwd(dx, s_ple, p[i], norm('ple_norm', i), w["gate"], f"l{i}", dep)
        dep = put_group(f"ple{i}", {"gate": dwg, "proj": dwp})
        w = wts[f"f2l{i}"]
        dx, dx16, gn['ffn2_norm'][i], dep = _ffn_bwd(
            dx, dx16, s_f2, norm('ffn2_norm', i), w["w1"], w["w3"], w["w2"], f"f2l{i}", dep,
            functools.partial(put_group, f"f2l{i}"))
        w = wts[f"mix{i}"]
        if i == 0:
            dx, dx16, gn['mix_norm'][i], dw_in, dw_out, d_ws, d_bs, d_lg, d_lb = _gmlp_bwd(
                dx, dx16, s_mix, norm('mix_norm', i), ln_g, ln_b, w_s, b_st, w["in"], w["out"], f"l{i}", dep)
            dep = put_group(f"mix{i}", {"in": dw_in, "out": dw_out})
        else:
            dx, dx16, gn['mix_norm'][i], dwq, dwkv, dwo, d_bq, d_bkv, d_bo, d_sink = _swa_bwd(
                dx, dx16, s_mix, norm('mix_norm', i), sinks, w["q"], w["kv"], w["o"], blk, f"l{i}", dep)
            dep = put_group(f"mix{i}", {"q": dwq, "kv": dwkv, "o": dwo})
        w = wts[f"f1l{i}"]
        dx, dx16, gn['ffn1_norm'][i], dep = _ffn_bwd(
            dx, dx16, s_f1, norm('ffn1_norm', i), w["w1"], w["w3"], w["w2"], f"f1l{i}", dep,
            functools.partial(put_group, f"f1l{i}"))

    kw = d_bkv.shape[1] // 2
    g_small = {
        'ffn1_norm': jnp.concatenate(gn['ffn1_norm'], axis=0), 'mix_norm': jnp.concatenate(gn['mix_norm'], axis=0),
        'ffn2_norm': jnp.concatenate(gn['ffn2_norm'], axis=0), 'ple_norm': jnp.concatenate(gn['ple_norm'], axis=0),
        'gmlp_ln_g': d_lg, 'gmlp_ln_b': d_lb, 'gmlp_w_s': d_ws[None], 'gmlp_b_s': d_bs[None], 'swa_sinks': d_sink,
        'final_norm': d_final.reshape(d), 'swa_bq': d_bq, 'swa_bk': d_bkv[:, :kw], 'swa_bv': d_bkv[:, kw:], 'swa_bo': d_bo,
    }
    return loss, dx, g_small


def _group_shards(a):
    bf = lambda w: w.astype(BF16)
    out = {}
    for i in range(2):
        for f in (1, 2):
            out[f"f{f}l{i}"] = {
                "w1": (bf(a[f"ffn{f}_w1"][i]), [(f"ffn{f}_w1", i, 0)]), "w3": (bf(a[f"ffn{f}_w3"][i]), [(f"ffn{f}_w3", i, 0)]),
                "w2": (bf(a[f"ffn{f}_w2"][i]), [(f"ffn{f}_w2", i, 0)])}
        out[f"ple{i}"] = {"gate": (bf(a["ple_w_gate"][i]), [("ple_w_gate", i, 0)]),
                          "proj": (bf(a["ple_w_proj"][i]), [("ple_w_proj", i, 0)])}
    out["mix0"] = {"in": (bf(a["gmlp_w_in"][0]), [("gmlp_w_in", 0, 0)]), "out": (bf(a["gmlp_w_out"][0]), [("gmlp_w_out", 0, 0)])}
    wkv = jnp.concatenate([bf(a["swa_wk"][0]), bf(a["swa_wv"][0])], axis=1)
    out["mix1"] = {"q": (bf(a["swa_wq"][0]), [("swa_wq", 0, 0)]), "kv": (wkv, [("swa_wk", 0, 0), ("swa_wv", 0, 1)]),
                   "o": (bf(a["swa_wo"][0]), [("swa_wo", 0, 0)])}
    return out


def kernel(x, p, ffn1_norm, ffn1_w1, ffn1_w3, ffn1_w2, mix_norm, ffn2_norm, ffn2_w1, ffn2_w3, ffn2_w2, ple_norm, ple_w_gate, ple_w_proj, gmlp_w_in, gmlp_ln_g, gmlp_ln_b, gmlp_w_s, gmlp_b_s, gmlp_w_out, swa_wq, swa_bq, swa_wk, swa_bk, swa_wv, swa_bv, swa_sinks, swa_wo, swa_bo, final_norm, loss_target, m_ffn1_norm, m_ffn1_w1, m_ffn1_w3, m_ffn1_w2, m_mix_norm, m_ffn2_norm, m_ffn2_w1, m_ffn2_w3, m_ffn2_w2, m_ple_norm, m_ple_w_gate, m_ple_w_proj, m_gmlp_w_in, m_gmlp_ln_g, m_gmlp_ln_b, m_gmlp_w_s, m_gmlp_b_s, m_gmlp_w_out, m_swa_wq, m_swa_bq, m_swa_wk, m_swa_bk, m_swa_wv, m_swa_bv, m_swa_sinks, m_swa_wo, m_swa_bo, m_final_norm, v_ffn1_norm, v_ffn1_w1, v_ffn1_w3, v_ffn1_w2, v_mix_norm, v_ffn2_norm, v_ffn2_w1, v_ffn2_w3, v_ffn2_w2, v_ple_norm, v_ple_w_gate, v_ple_w_proj, v_gmlp_w_in, v_gmlp_ln_g, v_gmlp_ln_b, v_gmlp_w_s, v_gmlp_b_s, v_gmlp_w_out, v_swa_wq, v_swa_bq, v_swa_wk, v_swa_bk, v_swa_wv, v_swa_bv, v_swa_sinks, v_swa_wo, v_swa_bo, v_final_norm):
    a = dict(locals())
    xi, yi, ci = _place()
    chip = 2 * xi + yi
    c_idx = ci.reshape(1).astype(jnp.int32)
    chip_idx = chip.reshape(1).astype(jnp.int32)
    d = x.shape[-1]
    d4 = d // N_CHIPS

    shards = _group_shards(a)
    started, token = _gather_start([[shards[g][part][0] for part in parts] for _, g, parts in _GATHER])
    bias_pack, bias_layout = _pack([a[n] for n in _BIASES], d4)
    bias_all = _all_gather_small(bias_pack)

    def full_bias(idx):
        return jnp.concatenate([_unpack(bias_all[2 * k], bias_layout)[idx] for k in range(N_CHIPS)], axis=1)

    bq, bk, bv, bo = (full_bias(i) for i in range(4))
    bkv = jnp.concatenate([bk, bv], axis=1)

    state = {"handed": _gather_step("gather_step_first", None, started[0], token)[1], "next": 1}

    def get_group(name, after):
        k = state["next"]
        assert _GATHER[k - 1][0] == name
        arriving = started[k] if k < len(_GATHER) else None
        done, state["handed"] = _gather_step(f"gather_step_{name}", state["handed"], arriving, after)
        state["next"] = k + 1
        return dict(zip(_GATHER[k - 1][2], done))

    pending, swapping = [], []

    def send_swapped(after):
        name, st = swapping.pop()
        parts = dict(_GROUPS)[name]
        glist, sib = _split_wait(f"swap_wait_{name}", st, after, _swap_copies, 1)
        partial = [_add_halves(g, r, c_idx, f"{name}_{part}") for part, g, r in zip(parts, glist, sib)]
        st, tok = _split_start(f"send_start_{name}", partial, [(3,) + pt.shape[1:] for pt in partial], _partial_copies, 3)
        pending.append((name, st))
        return tok

    def put_group(name, grads):
        glist = [grads[part] for part in dict(_GROUPS)[name]]
        st, tok = _split_start(f"swap_start_{name}", glist, [(N_CHIPS, g.shape[1] // 2, g.shape[2]) for g in glist],
                               _swap_copies, 1)
        if swapping:
            tok = send_swapped(tok)
        swapping.append((name, st))
        return tok

    small = {n: a[n] for n in _REPLICATED}
    loss, grad_x, g_small = _local_step(x[0], p[:, 0], loss_target[0], small, (bq, bkv, bo), get_group, put_group)
    last_sent = send_swapped(grad_x)

    names = _REPLICATED + _BIASES
    packed, layout = _pack([g_small[n] for n in names] + [loss[:, :1]], d4)
    summed = _unpack(_all_sum_small(packed), layout)
    g_sum = dict(zip(names, summed[:-1]))
    loss_out = summed[-1].reshape(())
    for n in _BIASES:
        width = a[n].shape[-1]
        g_sum[n] = lax.dynamic_slice_in_dim(g_sum[n], chip * width, width, axis=1)
    out, chain, last_done = {}, {}, grad_x
    for idx, (name, st) in enumerate(pending):
        mine, recv = _split_wait(f"send_wait_{name}", st, last_done if idx == len(pending) - 1 else last_sent,
                                 _partial_copies, 3)
        parts = dict(_GROUPS)[name]
        tots = _join_halves([_sum_partials(pm, q, chip_idx, c_idx, f"{name}_{part}")
                             for part, pm, q in zip(parts, mine, recv)], name)
        for part, g in zip(parts, tots):
            for n, layer, col in shards[name][part][1]:
                wd = a[n].shape[-1]
                rows = a[n].shape[-2]
                w2, m2, v2 = (a[pre + n].reshape(-1, wd) for pre in ('', 'm_', 'v_'))
                chain[n] = _adamw_big(g, col, w2, m2, v2, layer * rows, chain.get(n), f"{n}_{layer}")
                last_done = chain[n][1]
    for n, res in chain.items():
        out[n] = tuple(o.reshape(a[n].shape) for o in res)
    small_names = _REPLICATED + _BIASES
    items = [(_as2d(g_sum[n]), _as2d(a[n]), _as2d(a['m_' + n]), _as2d(a['v_' + n])) for n in small_names]
    for n, (delta, mn, vn) in zip(small_names, _adamw_small(items)):
        shape = a[n].shape
        out[n] = (g_sum[n].reshape(shape), delta.reshape(shape), mn.reshape(shape), vn.reshape(shape))

    return (loss_out, grad_x[None]) + tuple(out[n][j] for j in range(4) for n in _WEIGHTS)
```

```python
import functools

import jax
import jax.numpy as jnp
from jax import lax
from jax.experimental import pallas as pl
from jax.experimental.pallas import tpu as pltpu

F32 = jnp.float32
BF16 = jnp.bfloat16

RMS_EPS = 1e-6
LN_EPS = 1e-5
FFN_RESIDUAL_WEIGHT = 0.5
HEAD_DIM = 64
ROPE_DIM = 16
ROPE_THETA = 500000.0
ADAM_LR = 0.001
ADAM_B1 = 0.9
ADAM_B2 = 0.999
ADAM_EPS = 1e-08
ADAM_WD = 0.01
ADAM_STEP = 10
N_CHIPS = 4
N_DEV = 8
LANES = 128
VMEM_LIMIT_BYTES = 56 * 1024 * 1024
MESH_ID = pl.DeviceIdType.MESH

_DIMS = {
    "nn": (((1,), (0,)), ((), ())),
    "nt": (((1,), (1,)), ((), ())),
    "tn": (((0,), (0,)), ((), ())),
}


def _params(n_axes):
    return pltpu.CompilerParams(dimension_semantics=("arbitrary",) * n_axes, vmem_limit_bytes=VMEM_LIMIT_BYTES)


def _mm(name, grid, pairs, extras, outs, epilogue, acc_shapes, order="ij"):
    ni, nj, nk = grid
    if order == "ij":
        pgrid = (ni, nj, nk)
        ijk = lambda g0, g1, g2: (g0, g1, g2)
    else:
        pgrid = (nj, ni, nk)
        ijk = lambda g0, g1, g2: (g1, g0, g2)
    in_specs, args = [], []
    for a, ablk, amap, b, bblk, bmap, _, _, _ in pairs:
        in_specs.append(pl.BlockSpec(ablk, lambda g0, g1, g2, m=amap: m(*ijk(g0, g1, g2))))
        in_specs.append(pl.BlockSpec(bblk, lambda g0, g1, g2, m=bmap: m(*ijk(g0, g1, g2))))
        args += [a, b]
    for e, eblk, emap in extras:
        in_specs.append(pl.BlockSpec(eblk, lambda g0, g1, g2, m=emap: m(*ijk(g0, g1, g2)[:2])))
        args.append(e)
    out_specs = [pl.BlockSpec(oblk, lambda g0, g1, g2, m=omap: m(*ijk(g0, g1, g2)[:2])) for _, _, oblk, omap in outs]
    out_shape = [jax.ShapeDtypeStruct(s, d) for s, d, _, _ in outs]
    n_p, n_e, n_o = len(pairs), len(extras), len(outs)

    def body(*refs):
        p_refs = refs[: 2 * n_p]
        e_refs = refs[2 * n_p: 2 * n_p + n_e]
        o_refs = refs[2 * n_p + n_e: 2 * n_p + n_e + n_o]
        accs = refs[2 * n_p + n_e + n_o:]
        k = pl.program_id(2)

        def product(idx):
            a = p_refs[2 * idx][...].astype(BF16)
            b = p_refs[2 * idx + 1][...].astype(BF16)
            if b.ndim == 3:
                b = b.reshape(-1, b.shape[-1])
            return lax.dot_general(a, b, _DIMS[pairs[idx][6]], preferred_element_type=F32)

        def finish(sums):
            vals = epilogue(sums, [e[...] for e in e_refs])
            for o, v in zip(o_refs, vals):
                o[...] = v.astype(o.dtype)

        if nk == 1:
            sums = [None] * len(acc_shapes)
            for idx in range(n_p):
                ai = pairs[idx][7]
                sums[ai] = product(idx) if sums[ai] is None else sums[ai] + product(idx)
            finish(sums)
            return

        written = set()
        for idx in range(n_p):
            ai, k0 = pairs[idx][7], pairs[idx][8]
            if k0 or ai not in written:
                @pl.when(k == 0)
                def _(idx=idx, ai=ai, first=ai not in written):
                    if first:
                        accs[ai][...] = product(idx)
                    else:
                        accs[ai][...] += product(idx)
            if not k0:
                if ai in written:
                    accs[ai][...] += product(idx)
                else:
                    @pl.when(k > 0)
                    def _(idx=idx, ai=ai):
                        accs[ai][...] += product(idx)
            written.add(ai)

        @pl.when(k == nk - 1)
        def _():
            finish([acc[...] for acc in accs])

    res = pl.pallas_call(
        body, name=name, grid=pgrid, in_specs=in_specs, out_specs=out_specs, out_shape=out_shape,
        scratch_shapes=[] if nk == 1 else [pltpu.VMEM(s, F32) for s in acc_shapes], compiler_params=_params(3),
    )(*args)
    return res


def _col_nn(r0, tk, tn, wc):
    assert r0 % tk == 0 and wc % tn == 0
    npc = wc // tn
    return (None, tk, tn), lambda i, j, k: (j // npc, r0 // tk + k, j % npc)


def _row_nn(r0, tk, tn, rc):
    assert r0 % tk == 0 and rc % tk == 0
    kpc = rc // tk
    return (None, tk, tn), lambda i, j, k: (k // kpc, r0 // tk + k % kpc, j)


def _col_nt(r0, tn, tk, wc):
    assert r0 % tn == 0 and wc % tk == 0
    kpc = wc // tk
    return (None, tn, tk), lambda i, j, k: (k // kpc, r0 // tn + j, k % kpc)


def _row_nt(r0, tn, tk, rc):
    assert r0 % tn == 0 and rc % tn == 0
    npc = rc // tn
    return (None, tn, tk), lambda i, j, k: (j // npc, r0 // tn + j % npc, k)


def _out_col(tm, tn, wc):
    npc = wc // tn
    return (None, tm, tn), lambda i, j: (j // npc, i, j % npc)


def _out_row(tm, tn, rc):
    mpc = rc // tm
    return (None, tm, tn), lambda i, j: (i // mpc, i % mpc, j)


def _ik(i, j, k):
    return (i, k)


def _ki(i, j, k):
    return (k, i)


def _kj(i, j, k):
    return (k, j)


def _i0(i, j, k):
    return (i, 0)


def _ij(i, j):
    return (i, j)


def _0j(i, j):
    return (0, j)


def _i0e(i, j):
    return (i, 0)


def _rows(name, n, ins, outs, body, scratch=(), deps=()):
    in_specs, args = [], []
    for item in ins:
        if len(item) == 2:
            in_specs.append(item[1])
        else:
            in_specs.append(pl.BlockSpec(item[1], item[2]))
        args.append(item[0])
    n_in = len(args)
    for dep in deps:
        in_specs.append(pl.BlockSpec(dep.shape, lambda i: (0, 0)))
        args.append(dep)
    n_dep = len(deps)

    def call_body(*refs):
        body(*refs[:n_in], *refs[n_in + n_dep:])

    out_specs = [pl.BlockSpec(blk, m) for _, _, blk, m in outs]
    out_shape = [jax.ShapeDtypeStruct(s, d) for s, d, _, _ in outs]
    return pl.pallas_call(
        call_body, name=name, grid=(n,), in_specs=in_specs, out_specs=out_specs, out_shape=out_shape,
        scratch_shapes=list(scratch), compiler_params=_params(1),
    )(*args)


def _dep_extra(dep):
    return [] if dep is None else [(dep, dep.shape, lambda i, j: (0, 0))]


def _row(arr, tm):
    return (arr, (tm, arr.shape[1]), lambda i: (i, 0))


def _full(arr):
    nd = arr.ndim
    return (arr, arr.shape, lambda i: (0,) * nd)


def _row_out(shape, dtype, tm):
    return (shape, dtype, (tm, shape[1]), lambda i: (i, 0))


def _acc_out(shape):
    nd = len(shape)
    return (shape, F32, shape, lambda i: (0,) * nd)


def _tm(t):
    return 512 if t >= 1024 else t // 2


def _row_tile(rows, cap=256):
    best = max(tr for tr in range(16, min(rows, cap) + 1, 16) if rows % tr == 0)
    return best


def _first(i, refs):
    @pl.when(i == 0)
    def _():
        for r in refs:
            r[...] = jnp.zeros(r.shape, r.dtype)


def _rms_fwd(x, g, tag):
    t, d = x.shape
    tm = _tm(t)

    def body(x_ref, g_ref, h_ref):
        xv = x_ref[...]
        r = lax.rsqrt(jnp.mean(xv * xv, axis=-1, keepdims=True) + RMS_EPS)
        h_ref[...] = (xv * r * g_ref[...]).astype(BF16)

    return _rows(f"rms_fwd_{tag}", t // tm, [_row(x, tm), _full(g)], [_row_out((t, d), BF16, tm)], body)[0]


def _rms_bwd(dh, x, g, dx_out, tag):
    t, d = x.shape
    tm = _tm(t)

    def body(dh_ref, x_ref, g_ref, dxo_ref, dx_ref, dx16_ref, dg_ref):
        i = pl.program_id(0)
        _first(i, [dg_ref])
        xv = x_ref[...]
        r = lax.rsqrt(jnp.mean(xv * xv, axis=-1, keepdims=True) + RMS_EPS)
        xh = xv * r
        dhv = dh_ref[...]
        dxh = dhv * g_ref[...]
        dx = dxo_ref[...] + r * (dxh - xh * jnp.mean(dxh * xh, axis=-1, keepdims=True))
        dx_ref[...] = dx
        dx16_ref[...] = dx.astype(BF16)
        dg_ref[...] += jnp.sum(dhv * xh, axis=0, keepdims=True)

    return _rows(f"rms_bwd_{tag}", t // tm, [_row(dh, tm), _row(x, tm), _full(g), _row(dx_out, tm)],
                 [_row_out((t, d), F32, tm), _row_out((t, d), BF16, tm), _acc_out((1, d))], body)


def _loss_head(x, g, target):
    t, d = x.shape
    tm = _tm(t)

    def body(x_ref, g_ref, t_ref, dx_ref, dx16_ref, dg_ref, loss_ref):
        i = pl.program_id(0)
        _first(i, [dg_ref, loss_ref])
        xv = x_ref[...]
        gv = g_ref[...]
        r = lax.rsqrt(jnp.mean(xv * xv, axis=-1, keepdims=True) + RMS_EPS)
        xh = xv * r
        err = xh * gv - t_ref[...]
        loss_ref[...] += jnp.full((1, LANES), 0.5, F32) * jnp.sum(jnp.mean(err * err, axis=-1, keepdims=True))
        dy = err * (1.0 / d)
        dxh = dy * gv
        dx = r * (dxh - xh * jnp.mean(dxh * xh, axis=-1, keepdims=True))
        dx_ref[...] = dx
        dx16_ref[...] = dx.astype(BF16)
        dg_ref[...] += jnp.sum(dy * xh, axis=0, keepdims=True)

    return _rows("loss_head", t // tm, [_row(x, tm), _full(g), _row(target, tm)],
                 [_row_out((t, d), F32, tm), _row_out((t, d), BF16, tm), _acc_out((1, d)), _acc_out((1, LANES))], body)


def _colsum(arr, tag):
    t, w = arr.shape
    tm = _tm(t)

    def body(a_ref, o_ref):
        _first(pl.program_id(0), [o_ref])
        o_ref[...] += jnp.sum(a_ref[...].astype(F32), axis=0, keepdims=True)

    return _rows(f"colsum_{tag}", t // tm, [_row(arr, tm)], [_acc_out((1, w))], body)[0]


def _ffn_fwd(x, gn, w1, w3, get_w2, tag):
    t, d = x.shape
    fc = w1.shape[2]
    f = N_CHIPS * fc
    tm = _tm(t)
    h = _rms_fwd(x, gn, f"ffn_{tag}")
    w1blk, w1map = _col_nn(0, d, fc, fc)
    w3blk, w3map = _col_nn(0, d, fc, fc)
    tma = tm

    def ep_ab(accs, ex):
        a, b = accs
        sig = jax.nn.sigmoid(a)
        silu = a * sig
        return b * sig * (1.0 + a * (1.0 - sig)), silu, silu * b

    a, b, act = _mm(
        f"ffn_ab_{tag}", (t // tma, N_CHIPS, 1),
        [(h, (tma, d), _ik, w1, w1blk, w1map, "nn", 0, False), (h, (tma, d), _ik, w3, w3blk, w3map, "nn", 1, False)],
        [], [((t, f), BF16, (tma, fc), _ij)] * 3, ep_ab, [(tma, fc)] * 2, order="ji")
    w2 = get_w2(act)
    tn = min(d, 1024)
    tmo = min(t, 2 * tm)
    w2blk, w2map = _row_nn(0, fc, tn, fc)
    x_new = _mm(
        f"ffn_out_{tag}", (t // tmo, d // tn, N_CHIPS),
        [(act, (tmo, fc), _ik, w2, w2blk, w2map, "nn", 0, False)],
        [(x, (tmo, tn), _ij)], [((t, d), F32, (tmo, tn), _ij)],
        lambda accs, ex: [ex[0] + FFN_RESIDUAL_WEIGHT * accs[0]], [(tmo, tn)])[0]
    return x_new, (x, h, a, b, act), w2


def _ffn_bwd(dx_out, dx16, saved, gn, w1, w3, w2, tag, dep, put):
    x, h, a, b, act = saved
    t, d = x.shape
    fc = w1.shape[2]
    f = N_CHIPS * fc
    tm = _tm(t)
    tma = min(tm, 256)
    w2blk, w2map = _row_nt(0, fc, d, fc)

    def ep_dg(accs, ex):
        dg = FFN_RESIDUAL_WEIGHT * accs[0]
        return dg * ex[0].astype(F32), dg * ex[1].astype(F32)

    da, db = _mm(
        f"ffn_dg_{tag}", (t // tma, N_CHIPS, 1),
        [(dx16, (tma, d), _ik, w2, w2blk, w2map, "nt", 0, False)],
        [(a, (tma, fc), _ij), (b, (tma, fc), _ij)] + _dep_extra(dep), [((t, f), BF16, (tma, fc), _ij)] * 2, ep_dg, [(tma, fc)], order="ji")

    tmd = min(d, 512)
    tk = min(t, 2048)
    oblk, omap = _out_col(tmd, fc, fc)
    dw1, dw3 = _mm(
        f"ffn_dw13_{tag}", (d // tmd, N_CHIPS, t // tk),
        [(h, (tk, tmd), _ki, da, (tk, fc), _kj, "tn", 0, False), (h, (tk, tmd), _ki, db, (tk, fc), _kj, "tn", 1, False)],
        [], [((N_CHIPS, d, fc), BF16, oblk, omap)] * 2, lambda accs, ex: accs, [(tmd, fc)] * 2)
    tn = min(d, 1024)
    tk2 = min(t, 2048)
    dw2 = _mm(
        f"ffn_dw2_{tag}", (N_CHIPS, d // tn, t // tk2),
        [(act, (tk2, fc), _ki, dx16, (tk2, tn), _kj, "tn", 0, False)],
        [], [((N_CHIPS, fc, d), BF16, (None, fc, tn), lambda i, j: (i, 0, j))],
        lambda accs, ex: [FFN_RESIDUAL_WEIGHT * accs[0]], [(fc, tn)])[0]
    token = put({"w1": dw1, "w3": dw3, "w2": dw2})
    w1blk, w1map = _col_nt(0, tn, fc, fc)
    w3blk, w3map = _col_nt(0, tn, fc, fc)
    tmh = min(t, 2 * tm)
    dh = _mm(
        f"ffn_dh_{tag}", (t // tmh, d // tn, N_CHIPS),
        [(da, (tmh, fc), _ik, w1, w1blk, w1map, "nt", 0, False), (db, (tmh, fc), _ik, w3, w3blk, w3map, "nt", 0, False)],
        _dep_extra(token), [((t, d), F32, (tmh, tn), _ij)], lambda accs, ex: accs, [(tmh, tn)])[0]
    dx, dx16, dgn = _rms_bwd(dh, x, gn, dx_out, f"ffn_{tag}")
    return dx, dx16, dgn, token


def _ple_fwd(x, p_i, gn, wg, wp, tag):
    t, d = x.shape
    pdim = p_i.shape[1]
    d4 = d // N_CHIPS
    tm = _tm(t)
    hp = _rms_fwd(x, gn, f"ple_{tag}")
    tm = min(t, 2 * tm)

    def ep(accs, ex):
        s = jax.nn.sigmoid(accs[0])
        return ex[0] + s * accs[1], s, accs[1]

    x_new, s, pp = _mm(
        f"ple_fwd_{tag}", (t // tm, N_CHIPS, 1),
        [(hp, (tm, d), _i0, wg, (N_CHIPS, d4, d4), lambda i, j, k: (0, 0, j), "nn", 0, False),
         (p_i, (tm, pdim), _i0, wp, (None, pdim, d4), lambda i, j, k: (j, 0, 0), "nn", 1, False)],
        [(x, (tm, d4), _ij)], [((t, d), F32, (tm, d4), _ij), ((t, d), BF16, (tm, d4), _ij), ((t, d), BF16, (tm, d4), _ij)],
        ep, [(tm, d4)] * 2)
    return x_new, (x, hp, s, pp)


def _ple_bwd(dx_out, saved, p_i, gn, wg, tag, dep=None):
    x, hp, s, pp = saved
    t, d = x.shape
    pdim = p_i.shape[1]
    d4 = d // N_CHIPS
    tm = _tm(t)

    def body(dx_ref, s_ref, pp_ref, dpp_ref, dgp_ref):
        dxv = dx_ref[...]
        sv = s_ref[...].astype(F32)
        ppv = pp_ref[...].astype(F32)
        dpp_ref[...] = (dxv * sv).astype(BF16)
        dgp_ref[...] = (dxv * ppv * sv * (1.0 - sv)).astype(BF16)

    dpp, dgp = _rows(f"ple_ew_{tag}", t // tm, [_row(dx_out, tm), _row(s, tm), _row(pp, tm)],
                     [_row_out((t, d), BF16, tm)] * 2, body, deps=[] if dep is None else [dep])
    tk = min(t, 2048)
    dwp = _mm(
        f"ple_dwp_{tag}", (1, N_CHIPS, t // tk),
        [(p_i, (tk, pdim), lambda i, j, k: (k, 0), dpp, (tk, d4), _kj, "tn", 0, False)],
        [], [((N_CHIPS, pdim, d4), BF16, (None, pdim, d4), lambda i, j: (j, 0, 0))], lambda accs, ex: accs, [(pdim, d4)])[0]
    tn = min(d, 1024)
    dwg = _mm(
        f"ple_dwg_{tag}", (N_CHIPS, d // tn, t // tk),
        [(hp, (tk, d4), _ki, dgp, (tk, tn), _kj, "tn", 0, False)],
        [], [((N_CHIPS, d4, d), BF16, (None, d4, tn), lambda i, j: (i, 0, j))], lambda accs, ex: accs, [(d4, tn)])[0]
    dhp = _mm(
        f"ple_dh_{tag}", (t // tm, 1, 1),
        [(dgp, (tm, d), _ik, wg, (N_CHIPS, d4, d), lambda i, j, k: (0, 0, 0), "nt", 0, False)],
        [], [((t, d), F32, (tm, d), _ij)], lambda accs, ex: accs, [(tm, d)])[0]
    dx, dx16, dgn = _rms_bwd(dhp, x, gn, dx_out, f"ple_{tag}")
    return dx, dx16, dgn, dwg, dwp


_SQRT_HALF = 0.7071067811865476
_INV_SQRT_2PI = 0.3989422804014327


def _gelu(z):
    return z * (lax.erf(z * _SQRT_HALF) + 1.0) * 0.5


def _gelu_grad(z):
    return 0.5 * (1.0 + lax.erf(z * _SQRT_HALF)) + z * (_INV_SQRT_2PI * jnp.exp(-0.5 * z * z))


def _causal_bf16(w):
    c = w.shape[0]
    keep = lax.broadcasted_iota(jnp.int32, (c, c), 0) >= lax.broadcasted_iota(jnp.int32, (c, c), 1)
    return jnp.where(keep, w, 0.0).astype(BF16), keep


def _gmlp_gate_fwd(z_pre, ln_g, ln_b, w_s, b_st, tag):
    t, w2 = z_pre.shape
    gw = w2 // 2
    n_g, chunk, _ = w_s.shape
    gd = gw // n_g

    def body(z_ref, g_ref, b_ref, ws_ref, bs_ref, o_ref):
        z = z_ref[...]
        u = _gelu(z[:, :gw])
        zv = _gelu(z[:, gw:])
        mu = jnp.mean(zv, axis=-1, keepdims=True)
        cen = zv - mu
        rstd = lax.rsqrt(jnp.mean(cen * cen, axis=-1, keepdims=True) + LN_EPS)
        vln = (cen * rstd * g_ref[...] + b_ref[...]).astype(BF16)
        bst = bs_ref[...]
        for g in range(n_g):
            wm, _ = _causal_bf16(ws_ref[g])
            sl = slice(g * gd, (g + 1) * gd)
            s = jnp.dot(wm, vln[:, sl], preferred_element_type=F32) + bst[:, g:g + 1]
            o_ref[:, sl] = (u[:, sl] * s).astype(BF16)

    return _rows(f"gmlp_gate_{tag}", t // chunk, [_row(z_pre, chunk), _full(ln_g), _full(ln_b), _full(w_s), _full(b_st)],
                 [_row_out((t, gw), BF16, chunk)], body)[0]


def _gmlp_gate_bwd(z_pre, dgated, ln_g, ln_b, w_s, b_st, tag):
    t, w2 = z_pre.shape
    gw = w2 // 2
    n_g, chunk, _ = w_s.shape
    gd = gw // n_g

    def body(z_ref, dgt_ref, g_ref, b_ref, ws_ref, bs_ref, dz_ref, dws_ref, dbs_ref, dlg_ref, dlb_ref, dv_scr):
        _first(pl.program_id(0), [dws_ref, dbs_ref, dlg_ref, dlb_ref])
        z = z_ref[...]
        zu, zvp = z[:, :gw], z[:, gw:]
        u = _gelu(zu)
        zv = _gelu(zvp)
        mu = jnp.mean(zv, axis=-1, keepdims=True)
        cen = zv - mu
        rstd = lax.rsqrt(jnp.mean(cen * cen, axis=-1, keepdims=True) + LN_EPS)
        vn = cen * rstd
        lg = g_ref[...]
        vln = (vn * lg + b_ref[...]).astype(BF16)
        bst = bs_ref[...]
        lane = lax.broadcasted_iota(jnp.int32, (chunk, LANES), 1)
        dbs = jnp.zeros((chunk, LANES), F32)
        for g in range(n_g):
            wm, keep = _causal_bf16(ws_ref[g])
            sl = slice(g * gd, (g + 1) * gd)
            vg = vln[:, sl]
            s = jnp.dot(wm, vg, preferred_element_type=F32) + bst[:, g:g + 1]
            dgt = dgt_ref[:, sl].astype(F32)
            ds = dgt * u[:, sl]
            ds16 = ds.astype(BF16)
            dz_ref[:, sl] = (dgt * s * _gelu_grad(zu[:, sl])).astype(dz_ref.dtype)
            dv_scr[:, sl] = lax.dot_general(wm, ds16, _DIMS["tn"], preferred_element_type=F32)
            dw = lax.dot_general(ds16, vg, _DIMS["nt"], preferred_element_type=F32)
            dws_ref[g] += jnp.where(keep, dw, 0.0)
            dbs = dbs + jnp.where(lane == g, jnp.sum(ds, axis=-1, keepdims=True), 0.0)
        dbs_ref[...] += dbs
        dvln = dv_scr[...]
        dlg_ref[...] += jnp.sum(dvln * vn, axis=0, keepdims=True)
        dlb_ref[...] += jnp.sum(dvln, axis=0, keepdims=True)
        dvn = dvln * lg
        dzv = rstd * (dvn - jnp.mean(dvn, axis=-1, keepdims=True) - vn * jnp.mean(dvn * vn, axis=-1, keepdims=True))
        dz_ref[:, gw:] = (dzv * _gelu_grad(zvp)).astype(dz_ref.dtype)

    return _rows(
        f"gmlp_gate_bwd_{tag}", t // chunk,
        [_row(z_pre, chunk), _row(dgated, chunk), _full(ln_g), _full(ln_b), _full(w_s), _full(b_st)],
        [_row_out((t, w2), BF16, chunk), _acc_out((n_g, chunk, chunk)), _acc_out((chunk, LANES)), _acc_out((1, gw)),
         _acc_out((1, gw))], body, scratch=[pltpu.VMEM((chunk, gw), F32)])


def _gmlp_fwd(x, gn, ln_g, ln_b, w_s, b_st, w_in, w_out, tag):
    t, d = x.shape
    gw = ln_g.shape[1]
    tm = _tm(t)
    d4 = d // N_CHIPS
    h = _rms_fwd(x, gn, f"mix_{tag}")
    tn = min(d, 1024)
    iblk, imap = _col_nn(0, d, tn, d)
    z_pre = _mm(
        f"gmlp_in_{tag}", (t // tm, N_CHIPS * d // tn, 1),
        [(h, (tm, d), _ik, w_in, iblk, imap, "nn", 0, False)],
        [], [((t, 2 * gw), F32, (tm, tn), _ij)], lambda accs, ex: accs, [(tm, tn)], order="ji")[0]
    gated = _gmlp_gate_fwd(z_pre, ln_g, ln_b, w_s, b_st, tag)
    rc = gw // N_CHIPS
    tk = min(rc, 1024)
    tmo = min(t, 2 * tm)
    oblk, omap = _row_nn(0, tk, tn, rc)
    x_new = _mm(
        f"gmlp_out_{tag}", (t // tmo, d // tn, gw // tk),
        [(gated, (tmo, tk), _ik, w_out, oblk, omap, "nn", 0, False)],
        [(x, (tmo, tn), _ij)], [((t, d), F32, (tmo, tn), _ij)], lambda accs, ex: [ex[0] + accs[0]], [(tmo, tn)])[0]
    return x_new, (x, h, z_pre, gated)


def _gmlp_bwd(dx_out, dx16, saved, gn, ln_g, ln_b, w_s, b_st, w_in, w_out, tag, dep=None):
    x, h, z_pre, gated = saved
    t, d = x.shape
    gw = ln_g.shape[1]
    tm = _tm(t)
    d4 = d // N_CHIPS
    rc = gw // N_CHIPS
    tnr = min(rc, 1024)
    oblk, omap = _row_nt(0, tnr, d, rc)
    dgated = _mm(
        f"gmlp_dgated_{tag}", (t // tm, gw // tnr, 1),
        [(dx16, (tm, d), _ik, w_out, oblk, omap, "nt", 0, False)],
        _dep_extra(dep), [((t, gw), BF16, (tm, tnr), _ij)], lambda accs, ex: accs, [(tm, tnr)], order="ji")[0]
    tk = min(t, 2048)
    tn = min(d, 1024)
    rblk, rmap = _out_row(tnr, tn, rc)
    dw_out = _mm(
        f"gmlp_dwout_{tag}", (gw // tnr, d // tn, t // tk),
        [(gated, (tk, tnr), _ki, dx16, (tk, tn), _kj, "tn", 0, False)],
        [], [((N_CHIPS, rc, d), BF16, rblk, rmap)], lambda accs, ex: accs, [(tnr, tn)])[0]
    dz, dws, dbs, dlg, dlb = _gmlp_gate_bwd(z_pre, dgated, ln_g, ln_b, w_s, b_st, tag)
    tmd = min(d, 512)
    cblk, cmap = _out_col(tmd, tn, d)
    dw_in = _mm(
        f"gmlp_dwin_{tag}", (d // tmd, N_CHIPS * d // tn, t // tk),
        [(h, (tk, tmd), _ki, dz, (tk, tn), _kj, "tn", 0, False)],
        [], [((N_CHIPS, d, d), BF16, cblk, cmap)], lambda accs, ex: accs, [(tmd, tn)])[0]
    iblk, imap = _col_nt(0, tn, d, d)
    dh = _mm(
        f"gmlp_dh_{tag}", (t // tm, d // tn, N_CHIPS),
        [(dz, (tm, d), _ik, w_in, iblk, imap, "nt", 0, False)],
        [], [((t, d), F32, (tm, tn), _ij)], lambda accs, ex: accs, [(tm, tn)])[0]
    dx, dx16, dgn = _rms_bwd(dh, x, gn, dx_out, f"mix_{tag}")
    n_g = w_s.shape[0]
    return dx, dx16, dgn, dw_in, dw_out, dws, dbs[:, :n_g].T, dlg, dlb


def _rope_tables(t, width, n_rot_heads):
    half = ROPE_DIM // 2
    inv_freq = ROPE_THETA ** (-jnp.arange(0, ROPE_DIM, 2, dtype=F32) / ROPE_DIM)
    ang = jnp.arange(t, dtype=F32)[:, None] * inv_freq[None, :]
    cos, sin = jnp.cos(ang), jnp.sin(ang)
    ones = jnp.ones((t, HEAD_DIM - ROPE_DIM), F32)
    zeros = jnp.zeros((t, HEAD_DIM - ROPE_DIM), F32)
    zh = jnp.zeros((t, half), F32)
    c = jnp.concatenate([cos, cos, ones], axis=1)
    s_next = jnp.concatenate([-sin, zh, zeros], axis=1)
    s_prev = jnp.concatenate([zh, sin, zeros], axis=1)
    rest = width - n_rot_heads * HEAD_DIM

    def widen(tab, fill):
        parts = [jnp.tile(tab, (1, n_rot_heads))]
        if rest:
            parts.append(jnp.full((t, rest), fill, F32))
        return jnp.concatenate(parts, axis=1)

    return widen(c, 1.0), widen(s_next, 0.0), widen(s_prev, 0.0)


def _rope(v, c, s_next, s_prev, sign):
    w = v.shape[1]
    half = ROPE_DIM // 2
    reps = w // c.shape[1]
    if reps > 1:
        c, s_next, s_prev = (jnp.tile(tab, (1, reps)) for tab in (c, s_next, s_prev))
    return v * c + sign * (pltpu.roll(v, w - half, 1) * s_next + pltpu.roll(v, half, 1) * s_prev)


def _scores_mask(n, blk, heads):
    row = lax.broadcasted_iota(jnp.int32, (heads * blk, 2 * blk), 0)
    q_pos = lax.rem(row, blk) + blk
    k_pos = lax.broadcasted_iota(jnp.int32, (heads * blk, 2 * blk), 1)
    diff = q_pos - k_pos
    band = (diff >= 0) & (diff < blk)
    return band & ((k_pos >= blk) | (n > 0))


def _heads_to_rows(v, first, heads):
    return jnp.concatenate([v[:, (first + g) * HEAD_DIM:(first + g + 1) * HEAD_DIM] for g in range(heads)], axis=0)


def _sink_column(sink_ref, first, heads, blk):
    return jnp.concatenate([jnp.full((blk, 1), sink_ref[0, first + g], F32) for g in range(heads)], axis=0)


def _attn_fwd(q, kv, sinks, blk):
    t, qw = q.shape
    kw = kv.shape[1] // 2
    n_kv = kw // HEAD_DIM
    q_per_kv = qw // kw
    scale = HEAD_DIM ** -0.5

    def body(sink_ref, q_ref, kvo_ref, kvp_ref, o_ref):
        n = pl.program_id(0)
        valid = _scores_mask(n, blk, 1)
        qv = q_ref[...]
        kvo = kvo_ref[...]
        kvp = kvp_ref[...]
        for kh in range(n_kv):
            ks = slice(kh * HEAD_DIM, (kh + 1) * HEAD_DIM)
            vs = slice(kw + kh * HEAD_DIM, kw + (kh + 1) * HEAD_DIM)
            kb = jnp.concatenate([kvp[:, ks], kvo[:, ks]], axis=0)
            vb = jnp.concatenate([kvp[:, vs], kvo[:, vs]], axis=0)
            for g in range(q_per_kv):
                hd = kh * q_per_kv + g
                hs = slice(hd * HEAD_DIM, (hd + 1) * HEAD_DIM)
                sink = sink_ref[0, hd]
                s = lax.dot_general(qv[:, hs], kb, _DIMS["nt"], preferred_element_type=F32) * scale
                s = jnp.where(valid, s, -1e30)
                m = jnp.maximum(jnp.max(s, axis=-1, keepdims=True), sink)
                e = jnp.where(valid, jnp.exp(s - m), 0.0)
                denom = jnp.sum(e, axis=-1, keepdims=True) + jnp.exp(sink - m)
                p = (e / denom).astype(BF16)
                o_ref[:, hs] = jnp.dot(p, vb, preferred_element_type=F32).astype(BF16)

    return _rows(
        "swa_attn", t // blk,
        [(sinks, pl.BlockSpec(memory_space=pltpu.SMEM)), _row(q, blk), _row(kv, blk),
         (kv, (blk, kv.shape[1]), lambda i: (jnp.maximum(i - 1, 0), 0))],
        [_row_out((t, qw), BF16, blk)], body)[0]


def _attn_bwd(q, kv, do, sinks, blk):
    t, qw = q.shape
    kw = kv.shape[1] // 2
    n_kv = kw // HEAD_DIM
    q_per_kv = qw // kw
    scale = HEAD_DIM ** -0.5

    def body(sink_ref, q_ref, kvo_ref, kvp_ref, do_ref, dq_ref, dkvo_ref, dkvp_ref, dsink_ref):
        n = pl.program_id(0)
        _first(n, [dsink_ref])
        valid = _scores_mask(n, blk, q_per_kv)
        lane = lax.broadcasted_iota(jnp.int32, (1, LANES), 1)
        qv = q_ref[...]
        kvo = kvo_ref[...]
        kvp = kvp_ref[...]
        dov = do_ref[...]
        dsink = jnp.zeros((1, LANES), F32)
        for kh in range(n_kv):
            ks = slice(kh * HEAD_DIM, (kh + 1) * HEAD_DIM)
            vs = slice(kw + kh * HEAD_DIM, kw + (kh + 1) * HEAD_DIM)
            kb = jnp.concatenate([kvp[:, ks], kvo[:, ks]], axis=0)
            vb = jnp.concatenate([kvp[:, vs], kvo[:, vs]], axis=0)
            qg = _heads_to_rows(qv, kh * q_per_kv, q_per_kv)
            dog = _heads_to_rows(dov, kh * q_per_kv, q_per_kv)
            sink = _sink_column(sink_ref, kh * q_per_kv, q_per_kv, blk)
            s = lax.dot_general(qg, kb, _DIMS["nt"], preferred_element_type=F32) * scale
            s = jnp.where(valid, s, -1e30)
            m = jnp.maximum(jnp.max(s, axis=-1, keepdims=True), sink)
            e = jnp.where(valid, jnp.exp(s - m), 0.0)
            e_sink = jnp.exp(sink - m)
            inv = 1.0 / (jnp.sum(e, axis=-1, keepdims=True) + e_sink)
            p = e * inv
            p16 = p.astype(BF16)
            dp = lax.dot_general(dog, vb, _DIMS["nt"], preferred_element_type=F32)
            dot_pd = jnp.sum(p * dp, axis=-1, keepdims=True)
            ds16 = (p * (dp - dot_pd)).astype(BF16)
            d_sink_rows = e_sink * inv * dot_pd
            dqg = (jnp.dot(ds16, kb, preferred_element_type=F32) * scale).astype(BF16)
            for g in range(q_per_kv):
                hd = kh * q_per_kv + g
                rows = slice(g * blk, (g + 1) * blk)
                dsink = dsink + jnp.where(lane == hd, -jnp.sum(d_sink_rows[rows], axis=0, keepdims=True), 0.0)
                dq_ref[:, hd * HEAD_DIM:(hd + 1) * HEAD_DIM] = dqg[rows]
            dkb = lax.dot_general(ds16, qg, _DIMS["tn"], preferred_element_type=F32) * scale
            dvb = lax.dot_general(p16, dog, _DIMS["tn"], preferred_element_type=F32)
            dkvp_ref[:, ks] = dkb[:blk]
            dkvo_ref[:, ks] = dkb[blk:]
            dkvp_ref[:, vs] = dvb[:blk]
            dkvo_ref[:, vs] = dvb[blk:]
        dsink_ref[...] += dsink

    return _rows(
        "swa_attn_bwd", t // blk,
        [(sinks, pl.BlockSpec(memory_space=pltpu.SMEM)), _row(q, blk), _row(kv, blk),
         (kv, (blk, kv.shape[1]), lambda i: (jnp.maximum(i - 1, 0), 0)), _row(do, blk)],
        [_row_out((t, qw), BF16, blk), _row_out((t, 2 * kw), F32, blk), _row_out((t, 2 * kw), F32, blk),
         _acc_out((1, LANES))], body)


def _rope_bwd(dq_r, dkv_own, dkv_prev, tabs_q, tabs_kv, blk):
    t, qw = dq_r.shape
    kvw = dkv_own.shape[1]
    nb = t // blk

    def body(dq_ref, own_ref, nxt_ref, cq, snq, spq, ck, snk, spk, dqo_ref, dkvo_ref, dbq_ref, dbkv_ref):
        i = pl.program_id(0)
        _first(i, [dbq_ref, dbkv_ref])
        dq = _rope(dq_ref[...].astype(F32), cq[...], snq[...], spq[...], -1.0)
        dkv = own_ref[...] + jnp.where(i < nb - 1, nxt_ref[...], 0.0)
        dkv = _rope(dkv, ck[...], snk[...], spk[...], -1.0)
        dqo_ref[...] = dq.astype(BF16)
        dkvo_ref[...] = dkv.astype(BF16)
        dbq_ref[...] += jnp.sum(dq, axis=0, keepdims=True)
        dbkv_ref[...] += jnp.sum(dkv, axis=0, keepdims=True)

    return _rows(
        "swa_rope_bwd", nb,
        [_row(dq_r, blk), _row(dkv_own, blk), (dkv_prev, (blk, kvw), lambda i: (jnp.minimum(i + 1, nb - 1), 0))]
        + [_row(tab, blk) for tab in tabs_q] + [_row(tab, blk) for tab in tabs_kv],
        [_row_out((t, qw), BF16, blk), _row_out((t, kvw), BF16, blk), _acc_out((1, qw)), _acc_out((1, kvw))], body)


def _swa_fwd(x, gn, bq, bkv, bo, sinks, wq, wkv, wo, blk, tag):
    t, d = x.shape
    qw = bq.shape[1]
    kvw = bkv.shape[1]
    d4 = d // N_CHIPS
    tm = _tm(t)
    h = _rms_fwd(x, gn, f"mix_{tag}")
    tabs_q = _rope_tables(t, LANES, LANES // HEAD_DIM)
    tabs_kv = _rope_tables(t, kvw, kvw // 2 // HEAD_DIM)
    tn = min(qw, 1024)
    whole = lambda i, j, k: (0, 0, j)

    def ep_rope(accs, ex):
        return [_rope(accs[0] + ex[0], ex[1], ex[2], ex[3], 1.0)]

    q = _mm(
        f"swa_q_{tag}", (t // tm, qw // tn, 1),
        [(h, (tm, d), _ik, wq, (N_CHIPS, d4, tn), whole, "nn", 0, False)],
        [(bq, (1, tn), _0j)] + [(tab, (tm, LANES), _i0e) for tab in tabs_q],
        [((t, qw), BF16, (tm, tn), _ij)], ep_rope, [(tm, tn)], order="ji")[0]
    kv = _mm(
        f"swa_kv_{tag}", (t // tm, 1, 1),
        [(h, (tm, d), _ik, wkv, (N_CHIPS, d4, kvw), whole, "nn", 0, False)],
        [(bkv, (1, kvw), _0j)] + [(tab, (tm, kvw), _i0e) for tab in tabs_kv],
        [((t, kvw), BF16, (tm, kvw), _ij)], ep_rope, [(tm, kvw)])[0]
    o = _attn_fwd(q, kv, sinks, blk)
    tno = min(d, 1024)
    x_new = _mm(
        f"swa_out_{tag}", (t // tm, d // tno, 1),
        [(o, (tm, qw), _ik, wo, (N_CHIPS, qw // N_CHIPS, tno), whole, "nn", 0, False)],
        [(x, (tm, tno), _ij), (bo, (1, tno), _0j)], [((t, d), F32, (tm, tno), _ij)],
        lambda accs, ex: [ex[0] + accs[0] + ex[1]], [(tm, tno)], order="ji")[0]
    return x_new, (x, h, q, kv, o, tabs_q, tabs_kv)


def _swa_bwd(dx_out, dx16, saved, gn, sinks, wq, wkv, wo, blk, tag, dep=None):
    x, h, q, kv, o, tabs_q, tabs_kv = saved
    t, d = x.shape
    qw = q.shape[1]
    kvw = kv.shape[1]
    d4 = d // N_CHIPS
    qw4 = qw // N_CHIPS
    tm = _tm(t)
    whole = lambda i, j, k: (0, 0, 0)
    do = _mm(
        f"swa_do_{tag}", (t // tm, 1, 1),
        [(dx16, (tm, d), _ik, wo, (N_CHIPS, qw4, d), whole, "nt", 0, False)],
        _dep_extra(dep), [((t, qw), BF16, (tm, qw), _ij)], lambda accs, ex: accs, [(tm, qw)])[0]
    tk = min(t, 2048)
    tn = min(d, 1024)
    dwo = _mm(
        f"swa_dwo_{tag}", (N_CHIPS, d // tn, t // tk),
        [(o, (tk, qw4), _ki, dx16, (tk, tn), _kj, "tn", 0, False)],
        [], [((N_CHIPS, qw4, d), BF16, (None, qw4, tn), lambda i, j: (i, 0, j))], lambda accs, ex: accs, [(qw4, tn)])[0]
    dbo = _colsum(dx_out, f"bo_{tag}")
    dq_r, dkv_own, dkv_prev, dsink = _attn_bwd(q, kv, do, sinks, blk)
    dq, dkv, dbq, dbkv = _rope_bwd(dq_r, dkv_own, dkv_prev, tabs_q, tabs_kv, blk)
    tnq = min(qw, 1024)
    dwq = _mm(
        f"swa_dwq_{tag}", (N_CHIPS, qw // tnq, t // tk),
        [(h, (tk, d4), _ki, dq, (tk, tnq), _kj, "tn", 0, False)],
        [], [((N_CHIPS, d4, qw), BF16, (None, d4, tnq), lambda i, j: (i, 0, j))], lambda accs, ex: accs, [(d4, tnq)])[0]
    dwkv = _mm(
        f"swa_dwkv_{tag}", (N_CHIPS, 1, t // tk),
        [(h, (tk, d4), _ki, dkv, (tk, kvw), _kj, "tn", 0, False)],
        [], [((N_CHIPS, d4, kvw), BF16, (None, d4, kvw), lambda i, j: (i, 0, j))], lambda accs, ex: accs, [(d4, kvw)])[0]
    dh = _mm(
        f"swa_dh_{tag}", (t // tm, 1, 1),
        [(dq, (tm, qw), _ik, wq, (N_CHIPS, d4, qw), whole, "nt", 0, False),
         (dkv, (tm, kvw), _ik, wkv, (N_CHIPS, d4, kvw), whole, "nt", 0, False)],
        [], [((t, d), F32, (tm, d), _ij)], lambda accs, ex: accs, [(tm, d)])[0]
    dx, dx16, dgn = _rms_bwd(dh, x, gn, dx_out, f"mix_{tag}")
    n_heads = qw // HEAD_DIM
    return dx, dx16, dgn, dwq, dwkv, dwo, dbq, dbkv, dbo, dsink[:, :n_heads]


_HBM = pl.BlockSpec(memory_space=pl.ANY)
_CHIP_FLIPS = ((1, 0), (0, 1), (1, 1))


def _place():
    x, y, c = lax.axis_index("x"), lax.axis_index("y"), lax.axis_index("c")
    return x, y, c


def _flip(v, bit):
    return 1 - v if bit else v


def _exchange_small(v_ref, all_ref, send_sems, recv_sems):
    x, y, c = _place()
    me = 4 * x + 2 * y + c
    all_ref[me] = v_ref[...]
    copies = []
    for dlt in range(1, N_DEV):
        peer = (_flip(x, dlt & 4), _flip(y, dlt & 2), _flip(c, dlt & 1))
        copies.append(pltpu.make_async_remote_copy(
            src_ref=v_ref, dst_ref=all_ref.at[me], send_sem=send_sems.at[dlt - 1], recv_sem=recv_sems.at[dlt - 1],
            device_id=peer, device_id_type=MESH_ID))
    for cp in copies:
        cp.start()
    for cp in copies:
        cp.wait()


def _all_gather_small(v):
    r, cdim = v.shape

    def body(v_ref, out_ref, send_sems, recv_sems):
        _exchange_small(v_ref, out_ref, send_sems, recv_sems)

    return pl.pallas_call(
        body, name="all_gather_small", out_shape=jax.ShapeDtypeStruct((N_DEV, r, cdim), v.dtype),
        scratch_shapes=[pltpu.SemaphoreType.DMA((N_DEV - 1,)), pltpu.SemaphoreType.DMA((N_DEV - 1,))],
        compiler_params=pltpu.CompilerParams(vmem_limit_bytes=VMEM_LIMIT_BYTES),
    )(v)


def _all_sum_small(v):
    r, cdim = v.shape

    def body(v_ref, out_ref, all_ref, send_sems, recv_sems):
        _exchange_small(v_ref, all_ref, send_sems, recv_sems)
        acc = all_ref[0]
        for dv in range(1, N_DEV):
            acc = acc + all_ref[dv]
        out_ref[...] = acc

    return pl.pallas_call(
        body, name="all_sum_small", out_shape=jax.ShapeDtypeStruct((r, cdim), v.dtype),
        scratch_shapes=[pltpu.VMEM((N_DEV, r, cdim), v.dtype), pltpu.SemaphoreType.DMA((N_DEV - 1,)),
                        pltpu.SemaphoreType.DMA((N_DEV - 1,))],
        compiler_params=pltpu.CompilerParams(vmem_limit_bytes=VMEM_LIMIT_BYTES),
    )(v)


_HBM_SPEC = pl.BlockSpec(memory_space=pltpu.HBM)
_SEM_SPEC = pl.BlockSpec(memory_space=pltpu.SEMAPHORE)
_EFFECT = pltpu.SideEffectType.DATAFLOW_SIDE_EFFECTING
_TOKEN = jax.ShapeDtypeStruct((8, LANES), F32)


def _in_hbm(a):
    return pltpu.with_memory_space_constraint(a, pltpu.HBM)


def _hbm_like(a):
    return pltpu.HBM(a.shape, a.dtype)


def _ici_copies(shard, land, send_sems, recv_sems, base, x, y, c):
    half = shard.shape[0] // 2
    rows = pl.ds(c * half, half)
    return [pltpu.make_async_remote_copy(
        src_ref=shard.at[rows], dst_ref=land.at[2 * x + y, rows], send_sem=send_sems.at[base + j], recv_sem=recv_sems.at[base + j],
        device_id=(_flip(x, fx), _flip(y, fy), c), device_id_type=MESH_ID) for j, (fx, fy) in enumerate(_CHIP_FLIPS)]


def _d2d_copies(land, send_sems, recv_sems, base, x, y, c, c_rows):
    half = land.shape[1] // 2
    rows = pl.ds(c_rows * half, half)
    out = []
    for j, (fx, fy) in enumerate(_CHIP_FLIPS):
        piece = land.at[2 * _flip(x, fx) + _flip(y, fy), rows]
        out.append(pltpu.make_async_remote_copy(
            src_ref=piece, dst_ref=piece, send_sem=send_sems.at[base + j], recv_sem=recv_sems.at[base + j],
            device_id=(x, y, 1 - c), device_id_type=MESH_ID))
    return out


def _own_copy(shard, land, send_sems, recv_sems, s, x, y, c):
    return pltpu.make_async_remote_copy(
        src_ref=shard, dst_ref=land.at[2 * x + y], send_sem=send_sems.at[s], recv_sem=recv_sems.at[s],
        device_id=(x, y, 1 - c), device_id_type=MESH_ID)


def _gather_start(groups):
    flat = [s for grp in groups for s in grp]
    n, n_g = len(flat), len(groups)
    lands = [lax.empty((N_CHIPS,) + s.shape, s.dtype) for s in flat]

    def body(*refs):
        shards, land_refs = refs[:n], refs[n:2 * n]
        sems = refs[2 * n: 2 * n + 4 * n_g]
        token = refs[-1]
        x, y, c = _place()
        idx = 0
        for g, grp in enumerate(groups):
            ici_send, ici_recv, own_send, own_recv = sems[4 * g: 4 * g + 4]
            for s in range(len(grp)):
                _own_copy(shards[idx], land_refs[idx], own_send, own_recv, s, x, y, c).start()
                for cp in _ici_copies(shards[idx], land_refs[idx], ici_send, ici_recv, 3 * s, x, y, c):
                    cp.start()
                idx += 1
        token[...] = jnp.zeros(token.shape, F32)

    sem_shapes = []
    for grp in groups:
        k = len(grp)
        sem_shapes += [pltpu.SemaphoreType.DMA((3 * k,)), pltpu.SemaphoreType.DMA((3 * k,)),
                       pltpu.SemaphoreType.DMA((k,)), pltpu.SemaphoreType.DMA((k,))]
    res = pl.pallas_call(
        body, name="gather_start",
        out_shape=sem_shapes + [_hbm_like(s) for s in flat] + [_hbm_like(ld) for ld in lands] + [_TOKEN],
        in_specs=[_HBM_SPEC] * (2 * n),
        out_specs=[_SEM_SPEC] * (4 * n_g) + [_HBM_SPEC] * (2 * n) + [pl.BlockSpec(memory_space=pltpu.VMEM)],
        input_output_aliases={i: 4 * n_g + i for i in range(2 * n)},
        compiler_params=pltpu.CompilerParams(has_side_effects=_EFFECT),
    )(*[_in_hbm(s) for s in flat], *[_in_hbm(ld) for ld in lands])
    sems, thru, token = res[:4 * n_g], res[4 * n_g: 4 * n_g + 2 * n], res[-1]
    out, idx = [], 0
    for g, grp in enumerate(groups):
        k = len(grp)
        out.append(dict(sems=tuple(sems[4 * g: 4 * g + 4]), shards=list(thru[idx: idx + k]),
                        lands=list(thru[n + idx: n + idx + k])))
        idx += k
    return out, token


def _gather_step(name, landed, arriving, after):
    n_l = len(landed["lands"]) if landed else 0
    n_a = len(arriving["lands"]) if arriving else 0

    def body(*refs):
        pos = 0
        l_lands = refs[pos: pos + n_l]; pos += n_l
        l_sems = refs[pos: pos + (2 if landed else 0)]; pos += 2 if landed else 0
        a_shards = refs[pos: pos + n_a]; pos += n_a
        a_lands = refs[pos: pos + n_a]; pos += n_a
        a_sems = refs[pos: pos + (4 if arriving else 0)]; pos += 4 if arriving else 0
        pos += 1
        pos += n_l + n_a
        new_sems = refs[pos: pos + (2 if arriving else 0)]
        x, y, c = _place()
        if arriving:
            ici_send, ici_recv, own_send, own_recv = a_sems
            started = []
            for s in range(n_a):
                own = _own_copy(a_shards[s], a_lands[s], own_send, own_recv, s, x, y, c)
                own.wait_recv()
                for cp in _d2d_copies(a_lands[s], ici_send, ici_recv, 3 * s, x, y, c, c):
                    cp.wait_recv()
                for cp in _d2d_copies(a_lands[s], new_sems[0], new_sems[1], 3 * s, x, y, c, c):
                    cp.start()
                started.append(own)
                started += _ici_copies(a_shards[s], a_lands[s], ici_send, ici_recv, 3 * s, x, y, c)
            for cp in started:
                cp.wait_send()
        if landed:
            for s in range(n_l):
                for cp in _d2d_copies(l_lands[s], l_sems[0], l_sems[1], 3 * s, x, y, c, c):
                    cp.wait_send()
                for cp in _d2d_copies(l_lands[s], l_sems[0], l_sems[1], 3 * s, x, y, c, 1 - c):
                    cp.wait_recv()

    args, in_specs = [], []
    if landed:
        args += [_in_hbm(a) for a in landed["lands"]] + list(landed["d2d"])
        in_specs += [_HBM_SPEC] * n_l + [_SEM_SPEC] * 2
    if arriving:
        args += [_in_hbm(a) for a in arriving["shards"]] + [_in_hbm(a) for a in arriving["lands"]] + list(arriving["sems"])
        in_specs += [_HBM_SPEC] * (2 * n_a) + [_SEM_SPEC] * 4
    args.append(after)
    in_specs.append(pl.BlockSpec(memory_space=pl.ANY))
    out_shape, out_specs, aliases = [], [], {}
    if landed:
        out_shape += [_hbm_like(a) for a in landed["lands"]]
        for s in range(n_l):
            aliases[s] = s
    if arriving:
        first = (n_l + 2 if landed else 0) + n_a
        for s in range(n_a):
            aliases[first + s] = n_l + s
        out_shape += [_hbm_like(a) for a in arriving["lands"]]
    out_specs += [_HBM_SPEC] * (n_l + n_a)
    if arriving:
        out_shape += [pltpu.SemaphoreType.DMA((3 * n_a,)), pltpu.SemaphoreType.DMA((3 * n_a,))]
        out_specs += [_SEM_SPEC] * 2
    res = pl.pallas_call(
        body, name=name, out_shape=out_shape, in_specs=in_specs, out_specs=out_specs, input_output_aliases=aliases,
        compiler_params=pltpu.CompilerParams(has_side_effects=_EFFECT),
    )(*args)
    done = list(res[:n_l]) if landed else None
    nxt = None
    if arriving:
        nxt = dict(lands=list(res[n_l: n_l + n_a]), d2d=tuple(res[n_l + n_a: n_l + n_a + 2]))
    return done, nxt


def _swap_copies(grad, land, send_sems, recv_sems, base, x, y, c):
    half = grad.shape[1] // 2
    return [pltpu.make_async_remote_copy(
        src_ref=grad.at[:, pl.ds((1 - c) * half, half)], dst_ref=land, send_sem=send_sems.at[base], recv_sem=recv_sems.at[base],
        device_id=(x, y, 1 - c), device_id_type=MESH_ID)]


def _add_halves(g, r, c_idx, tag):
    _, rows, w = g.shape
    half = rows // 2
    tr = _row_tile(half, 1024)
    nb = half // tr

    def body(c_ref, g_ref, r_ref, o_ref):
        o_ref[...] = (g_ref[...].astype(F32) + r_ref[...].astype(F32)).astype(BF16)

    return pl.pallas_call(
        body, name=f"add_halves_{tag}",
        grid_spec=pltpu.PrefetchScalarGridSpec(
            num_scalar_prefetch=1, grid=(N_CHIPS, nb),
            in_specs=[pl.BlockSpec((None, tr, w), lambda k, i, c: (k, c[0] * nb + i, 0)),
                      pl.BlockSpec((None, tr, w), lambda k, i, c: (k, i, 0))],
            out_specs=pl.BlockSpec((None, tr, w), lambda k, i, c: (k, i, 0))),
        out_shape=jax.ShapeDtypeStruct((N_CHIPS, half, w), BF16), compiler_params=_params(2),
    )(c_idx, g, r)


def _partial_copies(part, land, send_sems, recv_sems, base, x, y, c):
    out = []
    for j, (fx, fy) in enumerate(_CHIP_FLIPS):
        px, py = _flip(x, fx), _flip(y, fy)
        out.append(pltpu.make_async_remote_copy(
            src_ref=part.at[2 * px + py], dst_ref=land.at[j], send_sem=send_sems.at[base + j], recv_sem=recv_sems.at[base + j],
            device_id=(px, py, c), device_id_type=MESH_ID))
    return out


def _split_start(name, srcs, land_shapes, copies_of, per):
    n = len(srcs)
    lands = [lax.empty(shape, s.dtype) for shape, s in zip(land_shapes, srcs)]

    def body(*refs):
        src_refs, land_refs = refs[:n], refs[n:2 * n]
        send_sems, recv_sems = refs[2 * n: 2 * n + 2]
        token = refs[-1]
        x, y, c = _place()
        for s in range(n):
            for cp in copies_of(src_refs[s], land_refs[s], send_sems, recv_sems, per * s, x, y, c):
                cp.start()
        token[...] = jnp.zeros(token.shape, F32)

    res = pl.pallas_call(
        body, name=name,
        out_shape=[pltpu.SemaphoreType.DMA((per * n,)), pltpu.SemaphoreType.DMA((per * n,))]
        + [_hbm_like(s) for s in srcs] + [_hbm_like(ld) for ld in lands] + [_TOKEN],
        in_specs=[_HBM_SPEC] * (2 * n),
        out_specs=[_SEM_SPEC] * 2 + [_HBM_SPEC] * (2 * n) + [pl.BlockSpec(memory_space=pltpu.VMEM)],
        input_output_aliases={i: 2 + i for i in range(2 * n)},
        compiler_params=pltpu.CompilerParams(has_side_effects=_EFFECT),
    )(*[_in_hbm(s) for s in srcs], *[_in_hbm(ld) for ld in lands])
    state = dict(sems=tuple(res[:2]), srcs=list(res[2: 2 + n]), lands=list(res[2 + n: 2 + 2 * n]))
    return state, res[-1]


def _split_wait(name, state, after, copies_of, per):
    n = len(state["srcs"])

    def body(*refs):
        src_refs, land_refs = refs[:n], refs[n:2 * n]
        send_sems, recv_sems = refs[2 * n: 2 * n + 2]
        x, y, c = _place()
        for s in range(n):
            for cp in copies_of(src_refs[s], land_refs[s], send_sems, recv_sems, per * s, x, y, c):
                cp.wait_send()
                cp.wait_recv()

    res = pl.pallas_call(
        body, name=name,
        out_shape=[_hbm_like(s) for s in state["srcs"]] + [_hbm_like(ld) for ld in state["lands"]],
        in_specs=[_HBM_SPEC] * (2 * n) + [_SEM_SPEC] * 2 + [pl.BlockSpec(memory_space=pl.ANY)],
        out_specs=[_HBM_SPEC] * (2 * n), input_output_aliases={i: i for i in range(2 * n)},
        compiler_params=pltpu.CompilerParams(has_side_effects=_EFFECT),
    )(*state["srcs"], *state["lands"], *state["sems"], after)
    return list(res[:n]), list(res[n:])


def _sum_partials(p, q, chip_idx, c_idx, tag):
    _, half, w = p.shape
    tr = _row_tile(half, 512)
    nb = half // tr

    def body(k_ref, c_ref, p_ref, q_ref, o_ref):
        acc = p_ref[...].astype(F32)
        for j in range(3):
            acc = acc + q_ref[j].astype(F32)
        o_ref[...] = acc

    return pl.pallas_call(
        body, name=f"sum_partials_{tag}",
        grid_spec=pltpu.PrefetchScalarGridSpec(
            num_scalar_prefetch=2, grid=(nb,),
            in_specs=[pl.BlockSpec((None, tr, w), lambda i, k, c: (k[0], i, 0)),
                      pl.BlockSpec((3, tr, w), lambda i, k, c: (0, i, 0))],
            out_specs=pl.BlockSpec((tr, w), lambda i, k, c: (c[0] * nb + i, 0))),
        out_shape=jax.ShapeDtypeStruct((2 * half, w), F32), compiler_params=_params(1),
    )(chip_idx, c_idx, p, q)


def _join_halves(tots, tag):
    n = len(tots)

    def body(*refs):
        bufs = refs[n:2 * n]
        send_sems, recv_sems = refs[2 * n:]
        x, y, c = _place()
        copies = []
        for s in range(n):
            half = bufs[s].shape[0] // 2
            mine = bufs[s].at[pl.ds(c * half, half)]
            cp = pltpu.make_async_remote_copy(
                src_ref=mine, dst_ref=mine, send_sem=send_sems.at[s], recv_sem=recv_sems.at[s],
                device_id=(x, y, 1 - c), device_id_type=MESH_ID)
            cp.start()
            copies.append(cp)
        for s in range(n):
            half = bufs[s].shape[0] // 2
            theirs = bufs[s].at[pl.ds((1 - c) * half, half)]
            pltpu.make_async_remote_copy(
                src_ref=theirs, dst_ref=theirs, send_sem=send_sems.at[s], recv_sem=recv_sems.at[s],
                device_id=(x, y, c), device_id_type=MESH_ID).wait_recv()
        for cp in copies:
            cp.wait_send()

    return pl.pallas_call(
        body, name=f"join_halves_{tag}",
        out_shape=[jax.ShapeDtypeStruct(tt.shape, tt.dtype) for tt in tots],
        in_specs=[_HBM] * n, out_specs=[_HBM] * n, input_output_aliases={s: s for s in range(n)},
        scratch_shapes=[pltpu.SemaphoreType.DMA((n,)), pltpu.SemaphoreType.DMA((n,))],
    )(*tots)


def _adamw_math(w, g, m, v):
    m = ADAM_B1 * m + (1.0 - ADAM_B1) * g
    v = ADAM_B2 * v + (1.0 - ADAM_B2) * jnp.square(g)
    m_hat = m / (1.0 - ADAM_B1 ** ADAM_STEP)
    v_hat = v / (1.0 - ADAM_B2 ** ADAM_STEP)
    delta = -ADAM_LR * (m_hat / (jnp.sqrt(v_hat) + ADAM_EPS) + ADAM_WD * w)
    return delta, m, v


def _adamw_big(g, col, w2, m2, v2, row0, prev, tag):
    rows = g.shape[0]
    rtot, wd = w2.shape
    tr = _row_tile(rows)
    assert row0 % tr == 0
    n_prev = 0 if prev is None else 4

    def body(*refs):
        g_ref, w_ref, m_ref, v_ref = refs[:4]
        go_ref, d_ref, mo_ref, vo_ref = refs[4 + n_prev:]
        gv = g_ref[...]
        delta, mn, vn = _adamw_math(w_ref[...], gv, m_ref[...], v_ref[...])
        go_ref[...] = gv
        d_ref[...] = delta
        mo_ref[...] = mn
        vo_ref[...] = vn

    at = lambda i: (row0 // tr + i, 0)
    return pl.pallas_call(
        body, name=f"adamw_{tag}", grid=(rows // tr,),
        in_specs=[pl.BlockSpec((tr, wd), lambda i: (i, col))] + [pl.BlockSpec((tr, wd), at)] * 3
        + [pl.BlockSpec(memory_space=pl.ANY)] * n_prev,
        out_specs=[pl.BlockSpec((tr, wd), at)] * 4, out_shape=[jax.ShapeDtypeStruct((rtot, wd), F32)] * 4,
        input_output_aliases={4 + k: k for k in range(n_prev)}, compiler_params=_params(1),
    )(g, w2, m2, v2, *(prev or ()))


def _adamw_small(items):
    n = len(items)

    def body(*refs):
        ins, outs = refs[:4 * n], refs[4 * n:]
        for k in range(n):
            g_ref, w_ref, m_ref, v_ref = ins[4 * k: 4 * k + 4]
            delta, mn, vn = _adamw_math(w_ref[...], g_ref[...], m_ref[...], v_ref[...])
            outs[3 * k][...] = delta
            outs[3 * k + 1][...] = mn
            outs[3 * k + 2][...] = vn

    flat = [a for it in items for a in it]
    out_shape = [jax.ShapeDtypeStruct(it[1].shape, F32) for it in items for _ in range(3)]
    res = pl.pallas_call(body, name="adamw_small", out_shape=out_shape,
                         compiler_params=pltpu.CompilerParams(vmem_limit_bytes=VMEM_LIMIT_BYTES))(*flat)
    return [tuple(res[3 * k: 3 * k + 3]) for k in range(n)]


def _pack(arrays, width):
    rows, layout, r = [], [], 0
    for a in arrays:
        flat = a.reshape(-1).astype(F32)
        nr = -(-flat.shape[0] // (8 * width)) * 8
        flat = jnp.pad(flat, (0, nr * width - flat.shape[0]))
        rows.append(flat.reshape(nr, width))
        layout.append((r, nr, a.shape))
        r += nr
    return jnp.concatenate(rows, axis=0), layout


def _unpack(packed, layout):
    out = []
    for r, nr, shape in layout:
        size = 1
        for s in shape:
            size *= s
        out.append(packed[r:r + nr].reshape(-1)[:size].reshape(shape))
    return out


_WEIGHTS = ['ffn1_norm', 'ffn1_w1', 'ffn1_w3', 'ffn1_w2', 'mix_norm', 'ffn2_norm', 'ffn2_w1', 'ffn2_w3', 'ffn2_w2',
            'ple_norm', 'ple_w_gate', 'ple_w_proj', 'gmlp_w_in', 'gmlp_ln_g', 'gmlp_ln_b', 'gmlp_w_s', 'gmlp_b_s',
            'gmlp_w_out', 'swa_wq', 'swa_bq', 'swa_wk', 'swa_bk', 'swa_wv', 'swa_bv', 'swa_sinks', 'swa_wo', 'swa_bo',
            'final_norm']
_REPLICATED = ['ffn1_norm', 'mix_norm', 'ffn2_norm', 'ple_norm', 'gmlp_ln_g', 'gmlp_ln_b', 'gmlp_w_s', 'gmlp_b_s',
               'swa_sinks', 'final_norm']
_BIASES = ['swa_bq', 'swa_bk', 'swa_bv', 'swa_bo']


def _as2d(a):
    if a.ndim == 1:
        return a.reshape(1, -1)
    return a.reshape(-1, a.shape[-1])


_GROUPS = [("f1l0", ("w1", "w3", "w2")), ("mix0", ("in", "out")), ("f2l0", ("w1", "w3", "w2")), ("ple0", ("gate", "proj")),
           ("f1l1", ("w1", "w3", "w2")), ("mix1", ("q", "kv", "o")), ("f2l1", ("w1", "w3", "w2")), ("ple1", ("gate", "proj"))]
_GATHER = []
for _g, _parts in _GROUPS:
    if _g.startswith("f"):
        _GATHER += [(_g + "a", _g, ("w1", "w3")), (_g + "b", _g, ("w2",))]
    else:
        _GATHER.append((_g, _g, _parts))


def _local_step(x, p, target, small, full_bias, get_group, put_group):
    t, d = x.shape
    n_layers = 2
    blk = small['gmlp_w_s'].shape[2]
    bq, bkv, bo = full_bias
    norm = lambda name, i: small[name][i:i + 1]
    ln_g, ln_b = small['gmlp_ln_g'], small['gmlp_ln_b']
    w_s = small['gmlp_w_s'][0]
    b_st = jnp.pad(small['gmlp_b_s'][0].T, ((0, 0), (0, LANES - small['gmlp_b_s'].shape[1])))
    sinks = small['swa_sinks']

    saved, wts = [], {}

    def ffn_fwd(x, name, norm_w):
        w = get_group(name + "a", x)
        x, s_ffn, w["w2"] = _ffn_fwd(x, norm_w, w["w1"], w["w3"], lambda after: get_group(name + "b", after)["w2"], name)
        wts[name] = w
        return x, s_ffn

    for i in range(n_layers):
        x, s_f1 = ffn_fwd(x, f"f1l{i}", norm('ffn1_norm', i))
        w = wts[f"mix{i}"] = get_group(f"mix{i}", x)
        if i == 0:
            x, s_mix = _gmlp_fwd(x, norm('mix_norm', i), ln_g, ln_b, w_s, b_st, w["in"], w["out"], f"l{i}")
        else:
            x, s_mix = _swa_fwd(x, norm('mix_norm', i), bq, bkv, bo, sinks, w["q"], w["kv"], w["o"], blk, f"l{i}")
        x, s_f2 = ffn_fwd(x, f"f2l{i}", norm('ffn2_norm', i))
        w = wts[f"ple{i}"] = get_group(f"ple{i}", x)
        x, s_ple = _ple_fwd(x, p[i], norm('ple_norm', i), w["gate"], w["proj"], f"l{i}")
        saved.append((s_f1, s_mix, s_f2, s_ple))

    dx, dx16, d_final, loss = _loss_head(x, small['final_norm'].reshape(1, d), target)

    gn = {k: [None] * n_layers for k in ('ffn1_norm', 'mix_norm', 'ffn2_norm', 'ple_norm')}
    dep = None
    for i in reversed(range(n_layers)):
        s_f1, s_mix, s_f2, s_ple = saved[i]
        w = wts[f"ple{i}"]
        dx, dx16, gn['ple_norm'][i], dwg, dwp = _ple_bwd(dx, s_ple, p[i], norm('ple_norm', i), w["gate"], f"l{i}", dep)
        dep = put_group(f"ple{i}", {"gate": dwg, "proj": dwp})
        w = wts[f"f2l{i}"]
        dx, dx16, gn['ffn2_norm'][i], dep = _ffn_bwd(
            dx, dx16, s_f2, norm('ffn2_norm', i), w["w1"], w["w3"], w["w2"], f"f2l{i}", dep,
            functools.partial(put_group, f"f2l{i}"))
        w = wts[f"mix{i}"]
        if i == 0:
            dx, dx16, gn['mix_norm'][i], dw_in, dw_out, d_ws, d_bs, d_lg, d_lb = _gmlp_bwd(
                dx, dx16, s_mix, norm('mix_norm', i), ln_g, ln_b, w_s, b_st, w["in"], w["out"], f"l{i}", dep)
            dep = put_group(f"mix{i}", {"in": dw_in, "out": dw_out})
        else:
            dx, dx16, gn['mix_norm'][i], dwq, dwkv, dwo, d_bq, d_bkv, d_bo, d_sink = _swa_bwd(
                dx, dx16, s_mix, norm('mix_norm', i), sinks, w["q"], w["kv"], w["o"], blk, f"l{i}", dep)
            dep = put_group(f"mix{i}", {"q": dwq, "kv": dwkv, "o": dwo})
        w = wts[f"f1l{i}"]
        dx, dx16, gn['ffn1_norm'][i], dep = _ffn_bwd(
            dx, dx16, s_f1, norm('ffn1_norm', i), w["w1"], w["w3"], w["w2"], f"f1l{i}", dep,
            functools.partial(put_group, f"f1l{i}"))

    kw = d_bkv.shape[1] // 2
    g_small = {
        'ffn1_norm': jnp.concatenate(gn['ffn1_norm'], axis=0), 'mix_norm': jnp.concatenate(gn['mix_norm'], axis=0),
        'ffn2_norm': jnp.concatenate(gn['ffn2_norm'], axis=0), 'ple_norm': jnp.concatenate(gn['ple_norm'], axis=0),
        'gmlp_ln_g': d_lg, 'gmlp_ln_b': d_lb, 'gmlp_w_s': d_ws[None], 'gmlp_b_s': d_bs[None], 'swa_sinks': d_sink,
        'final_norm': d_final.reshape(d), 'swa_bq': d_bq, 'swa_bk': d_bkv[:, :kw], 'swa_bv': d_bkv[:, kw:], 'swa_bo': d_bo,
    }
    return loss, dx, g_small


def _group_shards(a):
    bf = lambda w: w.astype(BF16)
    out = {}
    for i in range(2):
        for f in (1, 2):
            out[f"f{f}l{i}"] = {
                "w1": (bf(a[f"ffn{f}_w1"][i]), [(f"ffn{f}_w1", i, 0)]), "w3": (bf(a[f"ffn{f}_w3"][i]), [(f"ffn{f}_w3", i, 0)]),
                "w2": (bf(a[f"ffn{f}_w2"][i]), [(f"ffn{f}_w2", i, 0)])}
        out[f"ple{i}"] = {"gate": (bf(a["ple_w_gate"][i]), [("ple_w_gate", i, 0)]),
                          "proj": (bf(a["ple_w_proj"][i]), [("ple_w_proj", i, 0)])}
    out["mix0"] = {"in": (bf(a["gmlp_w_in"][0]), [("gmlp_w_in", 0, 0)]), "out": (bf(a["gmlp_w_out"][0]), [("gmlp_w_out", 0, 0)])}
    wkv = jnp.concatenate([bf(a["swa_wk"][0]), bf(a["swa_wv"][0])], axis=1)
    out["mix1"] = {"q": (bf(a["swa_wq"][0]), [("swa_wq", 0, 0)]), "kv": (wkv, [("swa_wk", 0, 0), ("swa_wv", 0, 1)]),
                   "o": (bf(a["swa_wo"][0]), [("swa_wo", 0, 0)])}
    return out


def kernel(x, p, ffn1_norm, ffn1_w1, ffn1_w3, ffn1_w2, mix_norm, ffn2_norm, ffn2_w1, ffn2_w3, ffn2_w2, ple_norm, ple_w_gate, ple_w_proj, gmlp_w_in, gmlp_ln_g, gmlp_ln_b, gmlp_w_s, gmlp_b_s, gmlp_w_out, swa_wq, swa_bq, swa_wk, swa_bk, swa_wv, swa_bv, swa_sinks, swa_wo, swa_bo, final_norm, loss_target, m_ffn1_norm, m_ffn1_w1, m_ffn1_w3, m_ffn1_w2, m_mix_norm, m_ffn2_norm, m_ffn2_w1, m_ffn2_w3, m_ffn2_w2, m_ple_norm, m_ple_w_gate, m_ple_w_proj, m_gmlp_w_in, m_gmlp_ln_g, m_gmlp_ln_b, m_gmlp_w_s, m_gmlp_b_s, m_gmlp_w_out, m_swa_wq, m_swa_bq, m_swa_wk, m_swa_bk, m_swa_wv, m_swa_bv, m_swa_sinks, m_swa_wo, m_swa_bo, m_final_norm, v_ffn1_norm, v_ffn1_w1, v_ffn1_w3, v_ffn1_w2, v_mix_norm, v_ffn2_norm, v_ffn2_w1, v_ffn2_w3, v_ffn2_w2, v_ple_norm, v_ple_w_gate, v_ple_w_proj, v_gmlp_w_in, v_gmlp_ln_g, v_gmlp_ln_b, v_gmlp_w_s, v_gmlp_b_s, v_gmlp_w_out, v_swa_wq, v_swa_bq, v_swa_wk, v_swa_bk, v_swa_wv, v_swa_bv, v_swa_sinks, v_swa_wo, v_swa_bo, v_final_norm):
    a = dict(locals())
    xi, yi, ci = _place()
    chip = 2 * xi + yi
    c_idx = ci.reshape(1).astype(jnp.int32)
    chip_idx = chip.reshape(1).astype(jnp.int32)
    d = x.shape[-1]
    d4 = d // N_CHIPS

    shards = _group_shards(a)
    started, token = _gather_start([[shards[g][part][0] for part in parts] for _, g, parts in _GATHER])
    bias_pack, bias_layout = _pack([a[n] for n in _BIASES], d4)
    bias_all = _all_gather_small(bias_pack)

    def full_bias(idx):
        return jnp.concatenate([_unpack(bias_all[2 * k], bias_layout)[idx] for k in range(N_CHIPS)], axis=1)

    bq, bk, bv, bo = (full_bias(i) for i in range(4))
    bkv = jnp.concatenate([bk, bv], axis=1)

    state = {"handed": _gather_step("gather_step_first", None, started[0], token)[1], "next": 1}

    def get_group(name, after):
        k = state["next"]
        assert _GATHER[k - 1][0] == name
        arriving = started[k] if k < len(_GATHER) else None
        done, state["handed"] = _gather_step(f"gather_step_{name}", state["handed"], arriving, after)
        state["next"] = k + 1
        return dict(zip(_GATHER[k - 1][2], done))

    pending, swapping = [], []

    def send_swapped(after):
        name, st = swapping.pop()
        parts = dict(_GROUPS)[name]
        glist, sib = _split_wait(f"swap_wait_{name}", st, after, _swap_copies, 1)
        partial = [_add_halves(g, r, c_idx, f"{name}_{part}") for part, g, r in zip(parts, glist, sib)]
        st, tok = _split_start(f"send_start_{name}", partial, [(3,) + pt.shape[1:] for pt in partial], _partial_copies, 3)
        pending.append((name, st))
        return tok

    def put_group(name, grads):
        glist = [grads[part] for part in dict(_GROUPS)[name]]
        st, tok = _split_start(f"swap_start_{name}", glist, [(N_CHIPS, g.shape[1] // 2, g.shape[2]) for g in glist],
                               _swap_copies, 1)
        if swapping:
            tok = send_swapped(tok)
        swapping.append((name, st))
        return tok

    small = {n: a[n] for n in _REPLICATED}
    loss, grad_x, g_small = _local_step(x[0], p[:, 0], loss_target[0], small, (bq, bkv, bo), get_group, put_group)
    last_sent = send_swapped(grad_x)

    names = _REPLICATED + _BIASES
    packed, layout = _pack([g_small[n] for n in names] + [loss[:, :1]], d4)
    summed = _unpack(_all_sum_small(packed), layout)
    g_sum = dict(zip(names, summed[:-1]))
    loss_out = summed[-1].reshape(())
    for n in _BIASES:
        width = a[n].shape[-1]
        g_sum[n] = lax.dynamic_slice_in_dim(g_sum[n], chip * width, width, axis=1)
    out, chain, last_done = {}, {}, grad_x
    for idx, (name, st) in enumerate(pending):
        mine, recv = _split_wait(f"send_wait_{name}", st, last_done if idx == len(pending) - 1 else last_sent,
                                 _partial_copies, 3)
        parts = dict(_GROUPS)[name]
        tots = _join_halves([_sum_partials(pm, q, chip_idx, c_idx, f"{name}_{part}")
                             for part, pm, q in zip(parts, mine, recv)], name)
        for part, g in zip(parts, tots):
            for n, layer, col in shards[name][part][1]:
                wd = a[n].shape[-1]
                rows = a[n].shape[-2]
                w2, m2, v2 = (a[pre + n].reshape(-1, wd) for pre in ('', 'm_', 'v_'))
                chain[n] = _adamw_big(g, col, w2, m2, v2, layer * rows, chain.get(n), f"{n}_{layer}")
                last_done = chain[n][1]
    for n, res in chain.items():
        out[n] = tuple(o.reshape(a[n].shape) for o in res)
    small_names = _REPLICATED + _BIASES
    items = [(_as2d(g_sum[n]), _as2d(a[n]), _as2d(a['m_' + n]), _as2d(a['v_' + n])) for n in small_names]
    for n, (delta, mn, vn) in zip(small_names, _adamw_small(items)):
        shape = a[n].shape
        out[n] = (g_sum[n].reshape(shape), delta.reshape(shape), mn.reshape(shape), vn.reshape(shape))

    return (loss_out, grad_x[None]) + tuple(out[n][j] for j in range(4) for n in _WEIGHTS)
```
